```python
import numpy as np
import jax
import jax.numpy as jnp
from jax import lax

D_MODEL = 2048
BATCH = 4
SEQ = 4096
DEPTH = 1

D_MIX = D_MODEL
NSA_HEADS = 16
NSA_KV_GROUPS = 4
NSA_REP = NSA_HEADS // NSA_KV_GROUPS
HEAD_DIM = 64
NSA_WIDTH = NSA_HEADS * HEAD_DIM
GMLP_WIDTH = D_MIX - NSA_WIDTH
GMLP_GROUPS = 8
GMLP_GROUP_DIM = GMLP_WIDTH // GMLP_GROUPS
GMLP_CHUNK = 128
CMP_BLOCK = 32
CMP_STRIDE = 16
CMP_HIDDEN = 256
SEL_BLOCK = 64
SEL_TOPK = 16
WINDOW = 512
Q_BLOCK = 128
N_BRANCH = 3
Q_COLS = NSA_WIDTH
KV_COLS = NSA_KV_GROUPS * HEAD_DIM
GATE_COLS = NSA_HEADS * N_BRANCH
IN_COLS = Q_COLS + 6 * KV_COLS + GATE_COLS + 2 * GMLP_WIDTH
D_FF = 5632
CONV_WIDTH = 3
NORM_EPS = 1e-6
LN_EPS = 1e-5
NEG_BIG = -1e30
SEL_BIG = 1e9

kernel_name = "hybrid_nsa_gmlp_convffn"


def _rms_norm(x, w):
    xf = x.astype(jnp.float32)
    y = xf * lax.rsqrt(jnp.mean(xf * xf, axis=-1, keepdims=True) + NORM_EPS)
    return (y * w.astype(jnp.float32)).astype(x.dtype)


def _layer_norm(x, w, b):
    xf = x.astype(jnp.float32)
    mu = jnp.mean(xf, axis=-1, keepdims=True)
    var = jnp.mean(jnp.square(xf - mu), axis=-1, keepdims=True)
    y = (xf - mu) * lax.rsqrt(var + LN_EPS)
    return (y * w.astype(jnp.float32) + b.astype(jnp.float32)).astype(x.dtype)


def _masked_softmax(s, mask):
    s = jnp.where(mask, s, NEG_BIG)
    m = jnp.max(s, axis=-1, keepdims=True)
    e = jnp.where(mask, jnp.exp(s - m), 0.0)
    return e / jnp.maximum(jnp.sum(e, axis=-1, keepdims=True), 1e-30)


def _alibi_slopes(n):
    return np.power(2.0, -8.0 * np.arange(1, n + 1) / n).astype(np.float32)


def _cmp_sel_overlap(n_cmp, n_sel):
    start = np.arange(n_cmp)[:, None] * CMP_STRIDE
    s0 = np.arange(n_sel)[None, :] * SEL_BLOCK
    return ((start < s0 + SEL_BLOCK) & (start + CMP_BLOCK > s0)).astype(np.float32)


def _compress(raw, idx, pos, w1, w2):
    blk = raw[:, idx] + pos[None, None, :, None, :]
    b, n, l, g, d = blk.shape
    blk = blk.transpose(0, 3, 1, 2, 4).reshape(b, g, n, l * d)
    return jax.nn.gelu(blk @ w1) @ w2


def _nsa(q, kc_raw, vc_raw, ks, vs, kw, vw, gate_logits, q_norm_w, k_norm_w, cmp_pos, cmp_w1, cmp_w2):
    B, T, _ = q.shape
    G, R, Dh = NSA_KV_GROUPS, NSA_REP, HEAD_DIM
    n_cmp = (T - CMP_BLOCK) // CMP_STRIDE + 1
    n_sel = T // SEL_BLOCK
    k_sel = min(SEL_TOPK, n_sel)
    n_qb = T // Q_BLOCK

    q = _rms_norm(q.reshape(B, T, NSA_HEADS, Dh), q_norm_w) * (Dh ** -0.5)
    q = q.reshape(B, T, G, R, Dh).transpose(0, 2, 3, 1, 4)
    gates = jax.nn.sigmoid(gate_logits.astype(jnp.float32))
    gates = gates.reshape(B, T, G, R, N_BRANCH).transpose(0, 2, 3, 1, 4)

    def kv_heads(a):
        return a.reshape(B, T, G, Dh)

    idx_c = np.arange(n_cmp)[:, None] * CMP_STRIDE + np.arange(CMP_BLOCK)[None, :]
    kc = _rms_norm(_compress(kv_heads(kc_raw), idx_c, cmp_pos[0], cmp_w1[0], cmp_w2[0]), k_norm_w[0])
    vc = _compress(kv_heads(vc_raw), idx_c, cmp_pos[1], cmp_w1[1], cmp_w2[1])
    cmp_end = jnp.asarray(idx_c[:, -1].astype(np.int32))
    cmp_center = jnp.asarray(idx_c.mean(axis=1).astype(np.float32))
    overlap = jnp.asarray(_cmp_sel_overlap(n_cmp, n_sel))

    ks = _rms_norm(kv_heads(ks), k_norm_w[1]).transpose(0, 2, 1, 3).reshape(B, G, n_sel, SEL_BLOCK, Dh)
    vs = kv_heads(vs).transpose(0, 2, 1, 3).reshape(B, G, n_sel, SEL_BLOCK, Dh)

    pad = ((0, 0), (0, 0), (WINDOW, 0), (0, 0))
    kw = jnp.pad(_rms_norm(kv_heads(kw), k_norm_w[2]).transpose(0, 2, 1, 3), pad)
    vw = jnp.pad(kv_heads(vw).transpose(0, 2, 1, 3), pad)

    slopes = jnp.asarray(_alibi_slopes(NSA_HEADS).reshape(G, R))[None, :, :, None, None]
    b_idx = jnp.arange(B)[:, None, None, None]
    g_idx = jnp.arange(G)[None, :, None, None]
    sel_off = jnp.arange(SEL_BLOCK, dtype=jnp.int32)
    win_off = jnp.arange(WINDOW + Q_BLOCK, dtype=jnp.int32)
    blk_ids = jnp.arange(n_sel, dtype=jnp.int32)

    def attend_block(qb):
        q0 = qb * Q_BLOCK
        qblk = lax.dynamic_slice_in_dim(q, q0, Q_BLOCK, axis=3)
        gblk = lax.dynamic_slice_in_dim(gates, q0, Q_BLOCK, axis=3)
        t = q0 + jnp.arange(Q_BLOCK, dtype=jnp.int32)
        tf = t.astype(jnp.float32)

        s_c = jnp.einsum('bgrqd,bgnd->bgrqn', qblk, kc).astype(jnp.float32)
        s_c = s_c - slopes * (tf[:, None] - cmp_center[None, :])
        p_c = _masked_softmax(s_c, cmp_end[None, :] <= t[:, None])
        o_c = jnp.einsum('bgrqn,bgnd->bgrqd', p_c.astype(vc.dtype), vc)

        imp = jnp.einsum('bgrqn,nj->bgqj', p_c, overlap)
        cur = t[:, None] // SEL_BLOCK
        valid = blk_ids[None, :] <= cur
        forced = (blk_ids[None, :] == 0) | (blk_ids[None, :] == cur) | (blk_ids[None, :] == cur - 1)
        score = jnp.where(valid, jnp.where(forced, SEL_BIG, imp), -SEL_BIG)
        top_val, top_idx = lax.top_k(score, k_sel)
        kg = ks[b_idx, g_idx, top_idx]
        vg = vs[b_idx, g_idx, top_idx]
        pos = top_idx[..., None] * SEL_BLOCK + sel_off
        m_s = (top_val > -0.5 * SEL_BIG)[..., None] & (pos <= t[:, None, None])
        s_s = jnp.einsum('bgrqd,bgqkld->bgrqkl', qblk, kg).astype(jnp.float32)
        s_s = s_s - slopes[..., None] * (tf[:, None, None] - pos.astype(jnp.float32))[:, :, None]
        n_keys = k_sel * SEL_BLOCK
        p_s = _masked_softmax(s_s.reshape(B, G, R, Q_BLOCK, n_keys),
                              m_s.reshape(B, G, 1, Q_BLOCK, n_keys)).reshape(s_s.shape)
        o_s = jnp.einsum('bgrqkl,bgqkld->bgrqd', p_s.astype(vg.dtype), vg)

        kwb = lax.dynamic_slice_in_dim(kw, q0, WINDOW + Q_BLOCK, axis=2)
        vwb = lax.dynamic_slice_in_dim(vw, q0, WINDOW + Q_BLOCK, axis=2)
        posw = q0 - WINDOW + win_off
        dist = t[:, None] - posw[None, :]
        m_w = (posw[None, :] >= 0) & (dist >= 0) & (dist < WINDOW)
        s_w = jnp.einsum('bgrqd,bgkd->bgrqk', qblk, kwb).astype(jnp.float32)
        s_w = s_w - slopes * dist.astype(jnp.float32)
        p_w = _masked_softmax(s_w, m_w)
        o_w = jnp.einsum('bgrqk,bgkd->bgrqd', p_w.astype(vwb.dtype), vwb)

        o = gblk[..., 0:1] * o_c + gblk[..., 1:2] * o_s + gblk[..., 2:3] * o_w
        return o.astype(q.dtype)

    out = lax.map(attend_block, jnp.arange(n_qb, dtype=jnp.int32))
    return out.transpose(1, 0, 4, 2, 3, 5).reshape(B, T, NSA_WIDTH)


def _spatial_gating(z, ln_w, ln_b, sw, sb):
    B, T, _ = z.shape
    u, v = jnp.split(jax.nn.gelu(z), 2, axis=-1)
    v = _layer_norm(v, ln_w, ln_b)
    v = v.reshape(B, T // GMLP_CHUNK, GMLP_CHUNK, GMLP_GROUPS, GMLP_GROUP_DIM)
    causal = jnp.tril(jnp.ones((GMLP_CHUNK, GMLP_CHUNK), dtype=bool))
    w = jnp.where(causal[None], sw, jnp.zeros_like(sw))
    v_mix = jnp.einsum('gts,bnsgc->bntgc', w, v) + sb.T[None, None, :, :, None]
    return u * v_mix.reshape(B, T, GMLP_WIDTH)


def _conv_ffn(x, w_up, conv_w, conv_b, w_down):
    h = x @ w_up
    T = h.shape[1]
    hp = jnp.pad(h, ((0, 0), (CONV_WIDTH - 1, 0), (0, 0)))
    h = conv_b + sum(conv_w[k] * hp[:, k:k + T] for k in range(CONV_WIDTH))
    gate, up = jnp.split(h, 2, axis=-1)
    return (jax.nn.silu(gate) * up) @ w_down


def setup_inputs(seed: int = 0) -> dict:
    key = jax.random.key(seed)
    ks = jax.random.split(key, 18)
    f32 = jnp.float32
    nrm = lambda k, shape, s: jax.random.normal(k, shape, f32) * s
    return {
        "x": nrm(ks[0], (BATCH, SEQ, D_MODEL), 1.0),
        "attn_norm_w": 1.0 + nrm(ks[1], (D_MODEL,), 0.02),
        "w_in": nrm(ks[2], (D_MODEL, IN_COLS), D_MODEL ** -0.5),
        "q_norm_w": 1.0 + nrm(ks[3], (HEAD_DIM,), 0.02),
        "k_norm_w": 1.0 + nrm(ks[4], (N_BRANCH, HEAD_DIM), 0.02),
        "cmp_pos": nrm(ks[5], (2, CMP_BLOCK, HEAD_DIM), 0.1),
        "cmp_w1": nrm(ks[6], (2, CMP_BLOCK * HEAD_DIM, CMP_HIDDEN), (CMP_BLOCK * HEAD_DIM) ** -0.5),
        "cmp_w2": nrm(ks[7], (2, CMP_HIDDEN, HEAD_DIM), CMP_HIDDEN ** -0.5),
        "gmlp_ln_w": 1.0 + nrm(ks[8], (GMLP_WIDTH,), 0.02),
        "gmlp_ln_b": nrm(ks[9], (GMLP_WIDTH,), 0.02),
        "spatial_w": nrm(ks[10], (GMLP_GROUPS, GMLP_CHUNK, GMLP_CHUNK), 0.5 * GMLP_CHUNK ** -0.5),
        "spatial_b": 1.0 + nrm(ks[11], (GMLP_GROUPS, GMLP_CHUNK), 0.1),
        "w_out": nrm(ks[12], (D_MIX, D_MODEL), D_MIX ** -0.5),
        "ffn_norm_w": 1.0 + nrm(ks[13], (D_MODEL,), 0.02),
        "w_up": nrm(ks[14], (D_MODEL, 2 * D_FF), D_MODEL ** -0.5),
        "conv_w": nrm(ks[15], (CONV_WIDTH, 2 * D_FF), CONV_WIDTH ** -0.5),
        "conv_b": nrm(ks[16], (2 * D_FF,), 0.02),
        "w_down": nrm(ks[17], (D_FF, D_MODEL), D_FF ** -0.5),
    }


def reference(x, attn_norm_w, w_in, q_norm_w, k_norm_w, cmp_pos, cmp_w1, cmp_w2, gmlp_ln_w, gmlp_ln_b,
              spatial_w, spatial_b, w_out, ffn_norm_w, w_up, conv_w, conv_b, w_down):
    splits = [int(c) for c in np.cumsum([Q_COLS] + [KV_COLS] * 6 + [GATE_COLS])]
    for _ in range(DEPTH):
        h = _rms_norm(x, attn_norm_w)
        proj = h @ w_in
        q, kc, vc, ksl, vsl, kwn, vwn, gl, z = jnp.split(proj, splits, axis=-1)
        a = _nsa(q, kc, vc, ksl, vsl, kwn, vwn, gl, q_norm_w, k_norm_w, cmp_pos, cmp_w1, cmp_w2)
        b = _spatial_gating(z, gmlp_ln_w, gmlp_ln_b, spatial_w, spatial_b)
        x = x + jnp.concatenate([a, b], axis=-1) @ w_out
        x = x + _conv_ffn(_rms_norm(x, ffn_norm_w), w_up, conv_w, conv_b, w_down)
    return x
```

```python
import numpy as np
import jax
import jax.numpy as jnp
from jax import lax
from jax.experimental import pallas as pl
from jax.experimental.pallas import tpu as pltpu

F32 = jnp.float32
BF16 = jnp.bfloat16

D_MODEL = 2048
BATCH = 4
SEQ = 4096
N_TOK = BATCH * SEQ
NSA_HEADS = 16
KV_GROUPS = 4
REP = NSA_HEADS // KV_GROUPS
HEAD_DIM = 64
NSA_WIDTH = NSA_HEADS * HEAD_DIM
GMLP_WIDTH = D_MODEL - NSA_WIDTH
GMLP_GROUPS = 8
GMLP_CHUNK = 128
CMP_BLOCK = 32
CMP_STRIDE = 16
CMP_HIDDEN = 256
N_CMP_PAD = SEQ // CMP_STRIDE
SEL_BLOCK = 64
N_SEL = SEQ // SEL_BLOCK
SEL_TOPK = 16
WINDOW = 512
Q_BLOCK = 128
N_QB = SEQ // Q_BLOCK
N_BRANCH = 3
KV_COLS = KV_GROUPS * HEAD_DIM
D_FF = 5632
CONV_WIDTH = 3
NORM_EPS = 1e-6
LN_EPS = 1e-5
NEG_BIG = -1e30
SEL_BIG = 1e9

LANES = 128
VMEM_LIMIT_BYTES = 56 * 1024 * 1024

Z_COLS = 2 * GMLP_WIDTH
PROJ_COLS = Z_COLS + NSA_WIDTH + 6 * KV_COLS
GATE_PAD = LANES
GATE_ROWS_PER_GROUP = 16

PROJ_TM = 1024
PROJ_TN = 512
GMLP_TM = 512
OPROJ_TM = 256
FFN_TM = 512
FFN_TF = 512
SEL_TILE_BLOCKS = 4
WIN_BLOCKS = WINDOW // Q_BLOCK + 1


def _cparams(*sem):
    return pltpu.CompilerParams(dimension_semantics=sem, vmem_limit_bytes=VMEM_LIMIT_BYTES)


def _dot(a, b):
    return jnp.dot(a, b, preferred_element_type=F32)


def _proj_kernel(x_ref, nw_ref, w_ref, wg_ref, o_ref, og_ref, h_ref):
    @pl.when(pl.program_id(1) == 0)
    def _():
        x = x_ref[...]
        ms = jnp.mean(x * x, axis=-1, keepdims=True)
        h = (x * lax.rsqrt(ms + NORM_EPS) * nw_ref[...]).astype(BF16)
        h_ref[...] = h
        og_ref[...] = _dot(h, wg_ref[...])

    o_ref[...] = _dot(h_ref[...], w_ref[...])


def _proj(x2, nw, w_main, w_gate):
    return pl.pallas_call(
        _proj_kernel,
        grid=(N_TOK // PROJ_TM, PROJ_COLS // PROJ_TN),
        in_specs=[
            pl.BlockSpec((PROJ_TM, D_MODEL), lambda i, j: (i, 0)),
            pl.BlockSpec((1, D_MODEL), lambda i, j: (0, 0)),
            pl.BlockSpec((D_MODEL, PROJ_TN), lambda i, j: (0, j)),
            pl.BlockSpec((D_MODEL, GATE_PAD), lambda i, j: (0, 0)),
        ],
        out_specs=[
            pl.BlockSpec((PROJ_TM, PROJ_TN), lambda i, j: (i, j)),
            pl.BlockSpec((PROJ_TM, GATE_PAD), lambda i, j: (i, 0)),
        ],
        out_shape=[
            jax.ShapeDtypeStruct((N_TOK, PROJ_COLS), F32),
            jax.ShapeDtypeStruct((N_TOK, GATE_PAD), F32),
        ],
        scratch_shapes=[pltpu.VMEM((PROJ_TM, D_MODEL), BF16)],
        compiler_params=_cparams("arbitrary", "arbitrary"),
        name="proj",
    )(x2, nw, w_main, w_gate)


def _kprep_kernel(s0_ref, s1_ref, s2_ref, gl_ref, knw_ref, kcw_ref, pos_ref, w1_ref, w2_ref,
                  k2_ref, vt2_ref, kc_ref, vct_ref, gt_ref, ha_ref, hb_ref):
    lane = lax.broadcasted_iota(jnp.int32, (1, LANES), 1)
    lo = lane < HEAD_DIM
    inv_d = 1.0 / HEAD_DIM

    def chunk(c, carry):
        r0 = pl.multiple_of(c * Q_BLOCK, Q_BLOCK)
        x = s0_ref[pl.ds(r0, Q_BLOCK), :]
        sq = x * x
        s_lo = jnp.sum(jnp.where(lo, sq, 0.0), axis=1, keepdims=True)
        s_hi = jnp.sum(jnp.where(lo, 0.0, sq), axis=1, keepdims=True)
        ms = jnp.where(lo, s_lo, s_hi) * inv_d
        k2_ref[c] = (x * lax.rsqrt(ms + NORM_EPS) * knw_ref[...]).astype(BF16)
        vt2_ref[c] = s2_ref[pl.ds(r0, Q_BLOCK), :].T.astype(BF16)
        return carry

    lax.fori_loop(0, N_QB, chunk, 0)

    @pl.when(pl.program_id(1) == 0)
    def _():
        def gchunk(c, carry):
            r0 = pl.multiple_of(c * Q_BLOCK, Q_BLOCK)
            gt_ref[c] = jax.nn.sigmoid(gl_ref[pl.ds(r0, Q_BLOCK), :]).T
            return carry

        lax.fori_loop(0, N_QB, gchunk, 0)

    ha_ref[...] = jnp.zeros_like(ha_ref)
    hb_ref[...] = jnp.zeros_like(hb_ref)
    for tt in range(CMP_STRIDE):
        rows = s1_ref[pl.ds(tt, N_CMP_PAD, stride=CMP_STRIDE), :]
        xa = (rows + pos_ref[tt:tt + 1, :]).astype(BF16)
        xb = (rows + pos_ref[CMP_STRIDE + tt:CMP_STRIDE + tt + 1, :]).astype(BF16)
        ha_ref[...] += _dot(xa, w1_ref[tt])
        hb_ref[0:N_CMP_PAD, :] += _dot(xb, w1_ref[CMP_STRIDE + tt])
    hid = jax.nn.gelu(ha_ref[...] + hb_ref[1:N_CMP_PAD + 1, :]).astype(BF16)
    kc = _dot(hid[:, :CMP_HIDDEN], w2_ref[0])
    vc = _dot(hid[:, CMP_HIDDEN:], w2_ref[1])
    ms = jnp.sum(kc * kc, axis=1, keepdims=True) * inv_d
    kc_ref[...] = (kc * lax.rsqrt(ms + NORM_EPS) * kcw_ref[...]).astype(BF16)
    vct_ref[...] = vc.T[0:HEAD_DIM, :].astype(BF16)


def _kprep(proj, gl, knw2, kcw, pos2, w1p, w2p):
    kv0 = (Z_COLS + NSA_WIDTH) // LANES
    slab = lambda s: pl.BlockSpec((SEQ, LANES), lambda b, g, s=s: (b, kv0 + 3 * g + s))
    full = lambda shape: pl.BlockSpec(shape, lambda b, g: (0,) * len(shape))
    return pl.pallas_call(
        _kprep_kernel,
        grid=(BATCH, KV_GROUPS),
        in_specs=[
            slab(0), slab(1), slab(2),
            pl.BlockSpec((SEQ, GATE_PAD), lambda b, g: (b, 0)),
            full((1, LANES)), full((1, LANES)), full((CMP_BLOCK, LANES)),
            full((CMP_BLOCK, LANES, 2 * CMP_HIDDEN)), full((2, CMP_HIDDEN, LANES)),
        ],
        out_specs=[
            pl.BlockSpec((None, None, N_QB, Q_BLOCK, LANES), lambda b, g: (b, g, 0, 0, 0)),
            pl.BlockSpec((None, None, N_QB, LANES, Q_BLOCK), lambda b, g: (b, g, 0, 0, 0)),
            pl.BlockSpec((None, None, N_CMP_PAD, LANES), lambda b, g: (b, g, 0, 0)),
            pl.BlockSpec((None, None, HEAD_DIM, N_CMP_PAD), lambda b, g: (b, g, 0, 0)),
            pl.BlockSpec((None, N_QB, GATE_PAD, Q_BLOCK), lambda b, g: (b, 0, 0, 0)),
        ],
        out_shape=[
            jax.ShapeDtypeStruct((BATCH, KV_GROUPS, N_QB, Q_BLOCK, LANES), BF16),
            jax.ShapeDtypeStruct((BATCH, KV_GROUPS, N_QB, LANES, Q_BLOCK), BF16),
            jax.ShapeDtypeStruct((BATCH, KV_GROUPS, N_CMP_PAD, LANES), BF16),
            jax.ShapeDtypeStruct((BATCH, KV_GROUPS, HEAD_DIM, N_CMP_PAD), BF16),
            jax.ShapeDtypeStruct((BATCH, N_QB, GATE_PAD, Q_BLOCK), F32),
        ],
        scratch_shapes=[
            pltpu.VMEM((N_CMP_PAD, 2 * CMP_HIDDEN), F32),
            pltpu.VMEM((N_CMP_PAD + 8, 2 * CMP_HIDDEN), F32),
        ],
        compiler_params=_cparams("arbitrary", "arbitrary"),
        name="kprep",
    )(proj, proj, proj, gl, knw2, kcw, pos2, w1p, w2p)


def _softmax_cols(s, mask):
    s = jnp.where(mask, s, NEG_BIG)
    m = jnp.max(s, axis=0, keepdims=True)
    e = jnp.where(mask, jnp.exp(s - m), 0.0)
    return e, jnp.sum(e, axis=0, keepdims=True)


def _attn_kernel(slopes_ref, q_ref, k2_ref, vt2_ref, kc_ref, vct_ref, gt_ref, ov_ref, wq_ref,
                 o_ref, sel_ref):
    g = pl.program_id(1)
    qb = pl.program_id(2)
    t_i = qb * Q_BLOCK + lax.broadcasted_iota(jnp.int32, (1, Q_BLOCK), 1)
    t_f = t_i.astype(F32)
    slopes = [slopes_ref[g * REP + r] for r in range(REP)]
    head = lambda a, r: a[:, r * Q_BLOCK:(r + 1) * Q_BLOCK]

    q_t = q_ref[...].T
    parts = []
    for r in range(REP):
        xq = q_t[r * HEAD_DIM:(r + 1) * HEAD_DIM, :]
        ms = jnp.sum(xq * xq, axis=0, keepdims=True) * (1.0 / HEAD_DIM)
        parts.append(xq * lax.rsqrt(ms + NORM_EPS) * wq_ref[...] * (HEAD_DIM ** -0.5))
    qn = jnp.concatenate(parts, axis=1).astype(BF16)
    zq = jnp.zeros_like(qn)
    q_lo = jnp.concatenate([qn, zq], axis=0)
    q_hi = jnp.concatenate([zq, qn], axis=0)

    sc = _dot(kc_ref[...], q_lo)
    ci = lax.broadcasted_iota(jnp.int32, (N_CMP_PAD, Q_BLOCK), 0)
    cmask = (ci * CMP_STRIDE + (CMP_BLOCK - 1)) <= t_i
    dc = t_f - (ci.astype(F32) * CMP_STRIDE + (CMP_BLOCK - 1) * 0.5)
    pcs = []
    for r in range(REP):
        e, d = _softmax_cols(head(sc, r) - slopes[r] * dc, cmask)
        pcs.append(e / jnp.maximum(d, 1e-30))
    o_c = _dot(vct_ref[...], jnp.concatenate(pcs, axis=1).astype(BF16))

    psum = (pcs[0] + pcs[1]) + (pcs[2] + pcs[3])
    p_hi = psum.astype(BF16)
    p_lo = (psum - p_hi.astype(F32)).astype(BF16)
    imp = _dot(ov_ref[...], p_hi) + _dot(ov_ref[...], p_lo)
    jj = lax.broadcasted_iota(jnp.int32, (N_SEL, Q_BLOCK), 0)
    cur = lax.shift_right_logical(t_i, 6)
    valid = jj <= cur
    sel_ref[...] = jnp.where(valid, 1.0, 0.0)

    @pl.when(qb * Q_BLOCK + Q_BLOCK > SEL_TOPK * SEL_BLOCK)
    def _():
        forced = (jj == 0) | (jj == cur) | (jj == cur - 1)
        score = jnp.where(valid, jnp.where(forced, SEL_BIG, imp), -SEL_BIG)
        rows8 = [score[8 * v:8 * v + 8, :] for v in range(N_SEL // 8)]
        j8 = lax.broadcasted_iota(jnp.int32, (8, Q_BLOCK), 0)
        ranks = [jnp.zeros((8, Q_BLOCK), F32) for _ in rows8]
        for jp in range(N_SEL):
            row = score[jp:jp + 1, :]
            for v, blk in enumerate(rows8):
                if 8 * v > jp:
                    ahead = jnp.where(row >= blk, 1.0, 0.0)
                elif 8 * v + 7 <= jp:
                    ahead = jnp.where(row > blk, 1.0, 0.0)
                else:
                    ahead = jnp.where(j8 + 8 * v > jp, jnp.where(row >= blk, 1.0, 0.0),
                                      jnp.where(row > blk, 1.0, 0.0))
                ranks[v] = ranks[v] + ahead
        rank = jnp.concatenate(ranks, axis=0)
        sel_ref[...] = jnp.where(valid & (rank < SEL_TOPK), 1.0, 0.0)

    tile_keys = SEL_TILE_BLOCKS * Q_BLOCK
    sub = tile_keys // SEL_BLOCK

    def sel_tile(kt, carry):
        m_run, l_run, acc = carry
        kb0 = kt * SEL_TILE_BLOCKS
        k_tile = k2_ref[pl.ds(kb0, SEL_TILE_BLOCKS)].reshape(tile_keys, LANES)
        s = _dot(k_tile, q_lo)
        pos = kt * tile_keys + lax.broadcasted_iota(jnp.int32, (tile_keys, Q_BLOCK), 0)
        dist = t_i - pos
        member = jnp.concatenate(
            [jnp.broadcast_to(sel_ref[pl.ds(kt * sub + x, 1), :], (SEL_BLOCK, Q_BLOCK)) for x in range(sub)],
            axis=0)
        mask = (member > 0.5) & (dist >= 0)
        dist_f = dist.astype(F32)
        ps, ms_, ls_, als = [], [], [], []
        for r in range(REP):
            sr = jnp.where(mask, head(s, r) - slopes[r] * dist_f, NEG_BIG)
            m_old = head(m_run, r)
            m_new = jnp.maximum(m_old, jnp.max(sr, axis=0, keepdims=True))
            p = jnp.where(mask, jnp.exp(sr - m_new), 0.0)
            alpha = jnp.exp(m_old - m_new)
            ps.append(p.astype(BF16))
            ms_.append(m_new)
            als.append(alpha)
            ls_.append(alpha * head(l_run, r) + jnp.sum(p, axis=0, keepdims=True))
        v_t = jnp.concatenate([vt2_ref[kb0 + x, 0:HEAD_DIM, :] for x in range(SEL_TILE_BLOCKS)], axis=1)
        pv = _dot(v_t, jnp.concatenate(ps, axis=1))
        acc = acc * jnp.concatenate(als, axis=1) + pv
        return jnp.concatenate(ms_, axis=1), jnp.concatenate(ls_, axis=1), acc

    init = (jnp.full((1, REP * Q_BLOCK), NEG_BIG, F32), jnp.zeros((1, REP * Q_BLOCK), F32),
            jnp.zeros((HEAD_DIM, REP * Q_BLOCK), F32))
    _, l_s, acc_s = lax.fori_loop(0, qb // SEL_TILE_BLOCKS + 1, sel_tile, init)
    o_s = acc_s / jnp.maximum(l_s, 1e-30)

    kbs = [jnp.maximum(qb - (WIN_BLOCKS - 1) + x, 0) for x in range(WIN_BLOCKS)]
    kw_tile = jnp.concatenate([k2_ref[kb] for kb in kbs], axis=0)
    sw = _dot(kw_tile, q_hi)
    posw = (qb - (WIN_BLOCKS - 1)) * Q_BLOCK + lax.broadcasted_iota(
        jnp.int32, (WIN_BLOCKS * Q_BLOCK, Q_BLOCK), 0)
    distw = t_i - posw
    wmask = (posw >= 0) & (distw >= 0) & (distw < WINDOW)
    distw_f = distw.astype(F32)
    pws, lws = [], []
    for r in range(REP):
        e, d = _softmax_cols(head(sw, r) - slopes[r] * distw_f, wmask)
        pws.append(e.astype(BF16))
        lws.append(d)
    vw_t = jnp.concatenate([vt2_ref[kb, HEAD_DIM:2 * HEAD_DIM, :] for kb in kbs], axis=1)
    o_w = _dot(vw_t, jnp.concatenate(pws, axis=1)) / jnp.maximum(jnp.concatenate(lws, axis=1), 1e-30)

    gates = gt_ref[...]
    outs = []
    for r in range(REP):
        outs.append(gates[r:r + 1, :] * head(o_c, r)
                    + gates[REP + r:REP + r + 1, :] * head(o_s, r)
                    + gates[2 * REP + r:2 * REP + r + 1, :] * head(o_w, r))
    o_ref[...] = jnp.concatenate(outs, axis=0).T.astype(BF16)


def _attn(slopes, proj, k2, vt2, kc, vct, gt, ov_t, wq_b):
    q0 = Z_COLS // (REP * HEAD_DIM)
    kvspec = lambda shape: pl.BlockSpec((None, None) + shape, lambda b, g, qb: (b, g) + (0,) * len(shape))
    return pl.pallas_call(
        _attn_kernel,
        grid=(BATCH, KV_GROUPS, N_QB),
        in_specs=[
            pl.BlockSpec(memory_space=pltpu.SMEM),
            pl.BlockSpec((Q_BLOCK, REP * HEAD_DIM), lambda b, g, qb: (b * N_QB + qb, q0 + g)),
            kvspec((N_QB, Q_BLOCK, LANES)),
            kvspec((N_QB, LANES, Q_BLOCK)),
            kvspec((N_CMP_PAD, LANES)),
            kvspec((HEAD_DIM, N_CMP_PAD)),
            pl.BlockSpec((None, None, GATE_ROWS_PER_GROUP, Q_BLOCK), lambda b, g, qb: (b, qb, g, 0)),
            pl.BlockSpec((N_SEL, N_CMP_PAD), lambda b, g, qb: (0, 0)),
            pl.BlockSpec((HEAD_DIM, Q_BLOCK), lambda b, g, qb: (0, 0)),
        ],
        out_specs=pl.BlockSpec((Q_BLOCK, REP * HEAD_DIM), lambda b, g, qb: (b * N_QB + qb, g)),
        out_shape=jax.ShapeDtypeStruct((N_TOK, NSA_WIDTH), BF16),
        scratch_shapes=[pltpu.VMEM((N_SEL, Q_BLOCK), F32)],
        compiler_params=_cparams("arbitrary", "arbitrary", "arbitrary"),
        name="attn",
    )(slopes, proj, k2, vt2, kc, vct, gt, ov_t, wq_b)


def _gmlp_kernel(z_ref, lnw_ref, lnb_ref, sw_ref, sbx_ref, o_ref):
    ge = jax.nn.gelu(z_ref[...])
    u = ge[:, :GMLP_WIDTH]
    v = ge[:, GMLP_WIDTH:]
    mu = jnp.mean(v, axis=-1, keepdims=True)
    var = jnp.mean(jnp.square(v - mu), axis=-1, keepdims=True)
    vn = ((v - mu) * lax.rsqrt(var + LN_EPS) * lnw_ref[...] + lnb_ref[...]).astype(BF16)
    ti = lax.broadcasted_iota(jnp.int32, (GMLP_CHUNK, GMLP_CHUNK), 0)
    si = lax.broadcasted_iota(jnp.int32, (GMLP_CHUNK, GMLP_CHUNK), 1)
    ws = [jnp.where(ti >= si, sw_ref[gg], 0.0).astype(BF16) for gg in range(GMLP_GROUPS)]
    gd = GMLP_WIDTH // GMLP_GROUPS
    for c in range(GMLP_TM // GMLP_CHUNK):
        rows = slice(c * GMLP_CHUNK, (c + 1) * GMLP_CHUNK)
        mix = jnp.concatenate(
            [_dot(ws[gg], vn[rows, gg * gd:(gg + 1) * gd]) for gg in range(GMLP_GROUPS)], axis=1)
        o_ref[rows, :] = (u[rows, :] * (mix + sbx_ref[...])).astype(BF16)


def _gmlp(proj, lnw, lnb, sw, sbx):
    return pl.pallas_call(
        _gmlp_kernel,
        grid=(N_TOK // GMLP_TM,),
        in_specs=[
            pl.BlockSpec((GMLP_TM, Z_COLS), lambda i: (i, 0)),
            pl.BlockSpec((1, GMLP_WIDTH), lambda i: (0, 0)),
            pl.BlockSpec((1, GMLP_WIDTH), lambda i: (0, 0)),
            pl.BlockSpec((GMLP_GROUPS, GMLP_CHUNK, GMLP_CHUNK), lambda i: (0, 0, 0)),
            pl.BlockSpec((GMLP_CHUNK, GMLP_WIDTH), lambda i: (0, 0)),
        ],
        out_specs=pl.BlockSpec((GMLP_TM, GMLP_WIDTH), lambda i: (i, 0)),
        out_shape=jax.ShapeDtypeStruct((N_TOK, GMLP_WIDTH), BF16),
        compiler_params=_cparams("arbitrary"),
        name="gmlp",
    )(proj, lnw, lnb, sw, sbx)


def _oproj_kernel(a_ref, b_ref, x_ref, wa_ref, wb_ref, nw_ref, x1_ref, xn_ref):
    y = x_ref[...] + (_dot(a_ref[...], wa_ref[...]) + _dot(b_ref[...], wb_ref[...]))
    x1_ref[...] = y
    ms = jnp.mean(y * y, axis=-1, keepdims=True)
    xn_ref[...] = (y * lax.rsqrt(ms + NORM_EPS) * nw_ref[...]).astype(BF16)


def _oproj(a, b, x2, wa, wb, nw):
    return pl.pallas_call(
        _oproj_kernel,
        grid=(N_TOK // OPROJ_TM,),
        in_specs=[
            pl.BlockSpec((OPROJ_TM, NSA_WIDTH), lambda i: (i, 0)),
            pl.BlockSpec((OPROJ_TM, GMLP_WIDTH), lambda i: (i, 0)),
            pl.BlockSpec((OPROJ_TM, D_MODEL), lambda i: (i, 0)),
            pl.BlockSpec((NSA_WIDTH, D_MODEL), lambda i: (0, 0)),
            pl.BlockSpec((GMLP_WIDTH, D_MODEL), lambda i: (0, 0)),
            pl.BlockSpec((1, D_MODEL), lambda i: (0, 0)),
        ],
        out_specs=[
            pl.BlockSpec((OPROJ_TM, D_MODEL), lambda i: (i, 0)),
            pl.BlockSpec((OPROJ_TM, D_MODEL), lambda i: (i, 0)),
        ],
        out_shape=[
            jax.ShapeDtypeStruct((N_TOK, D_MODEL), F32),
            jax.ShapeDtypeStruct((N_TOK, D_MODEL), BF16),
        ],
        compiler_params=_cparams("arbitrary"),
        name="oproj",
    )(a, b, x2, wa, wb, nw)


def _causal_conv(h, prev, cw, cb):
    rid = lax.broadcasted_iota(jnp.int32, prev.shape, 0)

    def shifted(k):
        body = pltpu.roll(h, k, 0)
        top = jnp.where(rid < k, pltpu.roll(prev, k, 0), body[0:8, :])
        return jnp.concatenate([top, body[8:, :]], axis=0)

    return cb + ((cw[0:1, :] * shifted(2) + cw[1:2, :] * shifted(1)) + cw[2:3, :] * h)


def _ffn_kernel(xn_ref, x1_ref, wg_ref, wu_ref, cwg_ref, cwu_ref, cbg_ref, cbu_ref, wd_ref,
                o_ref, carry_ref):
    i = pl.program_id(0)
    j = pl.program_id(1)
    xn = xn_ref[...]
    hg = _dot(xn, wg_ref[...])
    hu = _dot(xn, wu_ref[...])
    @pl.when((i == 0) & (j == 0))
    def _():
        carry_ref[...] = jnp.zeros_like(carry_ref)

    seq_start = i % (SEQ // FFN_TM) == 0
    prev_g = jnp.where(seq_start, 0.0, carry_ref[j, 0])
    prev_u = jnp.where(seq_start, 0.0, carry_ref[j, 1])
    carry_ref[j, 0] = hg[FFN_TM - 8:, :]
    carry_ref[j, 1] = hu[FFN_TM - 8:, :]
    cg = _causal_conv(hg, prev_g, cwg_ref[...], cbg_ref[...])
    cu = _causal_conv(hu, prev_u, cwu_ref[...], cbu_ref[...])
    act = (jax.nn.silu(cg) * cu).astype(BF16)

    @pl.when(j == 0)
    def _():
        o_ref[...] = x1_ref[...]

    o_ref[...] += _dot(act, wd_ref[...])


def _ffn(xn, x1, w_up, conv_w, conv_b, w_down):
    nj = D_FF // FFN_TF
    return pl.pallas_call(
        _ffn_kernel,
        grid=(N_TOK // FFN_TM, nj),
        in_specs=[
            pl.BlockSpec((FFN_TM, D_MODEL), lambda i, j: (i, 0)),
            pl.BlockSpec((FFN_TM, D_MODEL), lambda i, j: (i, 0)),
            pl.BlockSpec((D_MODEL, FFN_TF), lambda i, j: (0, j)),
            pl.BlockSpec((D_MODEL, FFN_TF), lambda i, j: (0, nj + j)),
            pl.BlockSpec((CONV_WIDTH, FFN_TF), lambda i, j: (0, j)),
            pl.BlockSpec((CONV_WIDTH, FFN_TF), lambda i, j: (0, nj + j)),
            pl.BlockSpec((1, FFN_TF), lambda i, j: (0, j)),
            pl.BlockSpec((1, FFN_TF), lambda i, j: (0, nj + j)),
            pl.BlockSpec((FFN_TF, D_MODEL), lambda i, j: (j, 0)),
        ],
        out_specs=pl.BlockSpec((FFN_TM, D_MODEL), lambda i, j: (i, 0)),
        out_shape=jax.ShapeDtypeStruct((N_TOK, D_MODEL), F32),
        scratch_shapes=[pltpu.VMEM((nj, 2, 8, FFN_TF), F32)],
        compiler_params=_cparams("arbitrary", "arbitrary"),
        name="ffn",
    )(xn, x1, w_up, w_up, conv_w, conv_w, conv_b, conv_b, w_down)


def _overlap_t():
    start = np.arange(N_CMP_PAD)[None, :] * CMP_STRIDE
    s0 = np.arange(N_SEL)[:, None] * SEL_BLOCK
    ov = (start < s0 + SEL_BLOCK) & (start + CMP_BLOCK > s0) & (np.arange(N_CMP_PAD)[None, :] < N_CMP_PAD - 1)
    return ov.astype(np.float32)


def _alibi_slopes():
    return np.power(2.0, -8.0 * np.arange(1, NSA_HEADS + 1) / NSA_HEADS).astype(np.float32)


def _layout_w_in(w_in):
    q_end = NSA_WIDTH
    kv_end = q_end + 6 * KV_COLS
    g_end = kv_end + NSA_HEADS * N_BRANCH
    w_q = w_in[:, :q_end]
    w_kv = w_in[:, q_end:kv_end].reshape(D_MODEL, 6, KV_GROUPS, HEAD_DIM)
    w_kv = jnp.stack([w_kv[:, s] for s in (2, 4, 0, 1, 3, 5)], axis=2).reshape(D_MODEL, 6 * KV_COLS)
    w_z = w_in[:, g_end:]
    w_main = jnp.concatenate([w_z, w_q, w_kv], axis=1).astype(BF16)
    w_g = w_in[:, kv_end:g_end].reshape(D_MODEL, KV_GROUPS, REP, N_BRANCH).transpose(0, 1, 3, 2)
    w_g = w_g.reshape(D_MODEL, KV_GROUPS, N_BRANCH * REP)
    w_g = jnp.pad(w_g, ((0, 0), (0, 0), (0, GATE_ROWS_PER_GROUP - N_BRANCH * REP)))
    w_g = jnp.pad(w_g.reshape(D_MODEL, KV_GROUPS * GATE_ROWS_PER_GROUP),
                  ((0, 0), (0, GATE_PAD - KV_GROUPS * GATE_ROWS_PER_GROUP))).astype(BF16)
    return w_main, w_g


def _layout_compress(cmp_pos, cmp_w1, cmp_w2):
    pos2 = jnp.concatenate([cmp_pos[0], cmp_pos[1]], axis=1)
    w1 = cmp_w1.reshape(2, CMP_BLOCK, HEAD_DIM, CMP_HIDDEN)
    zero = jnp.zeros_like(w1[0])
    w1k = jnp.concatenate([w1[0], zero], axis=1)
    w1v = jnp.concatenate([zero, w1[1]], axis=1)
    w1p = jnp.concatenate([w1k, w1v], axis=2).astype(BF16)
    w2p = jnp.pad(cmp_w2, ((0, 0), (0, 0), (0, LANES - HEAD_DIM))).astype(BF16)
    return pos2, w1p, w2p


def kernel(x, attn_norm_w, w_in, q_norm_w, k_norm_w, cmp_pos, cmp_w1, cmp_w2, gmlp_ln_w, gmlp_ln_b,
           spatial_w, spatial_b, w_out, ffn_norm_w, w_up, conv_w, conv_b, w_down):
    x2 = x.reshape(N_TOK, D_MODEL)
    w_main, w_gate = _layout_w_in(w_in)
    proj, gl = _proj(x2, attn_norm_w.reshape(1, D_MODEL), w_main, w_gate)

    knw2 = jnp.concatenate([k_norm_w[1], k_norm_w[2]]).reshape(1, LANES)
    kcw = jnp.concatenate([k_norm_w[0], jnp.zeros((LANES - HEAD_DIM,), F32)]).reshape(1, LANES)
    pos2, w1p, w2p = _layout_compress(cmp_pos, cmp_w1, cmp_w2)
    k2, vt2, kc, vct, gt = _kprep(proj, gl, knw2, kcw, pos2, w1p, w2p)

    wq_b = jnp.broadcast_to(q_norm_w.reshape(HEAD_DIM, 1), (HEAD_DIM, Q_BLOCK))
    a = _attn(jnp.asarray(_alibi_slopes()), proj, k2, vt2, kc, vct, gt,
              jnp.asarray(_overlap_t(), dtype=BF16), wq_b)

    sbx = jnp.repeat(spatial_b.T, GMLP_WIDTH // GMLP_GROUPS, axis=1)
    b = _gmlp(proj, gmlp_ln_w.reshape(1, GMLP_WIDTH), gmlp_ln_b.reshape(1, GMLP_WIDTH), spatial_w, sbx)

    w_out_b = w_out.astype(BF16)
    x1, xn = _oproj(a, b, x2, w_out_b[:NSA_WIDTH], w_out_b[NSA_WIDTH:], ffn_norm_w.reshape(1, D_MODEL))

    out = _ffn(xn, x1, w_up.astype(BF16), conv_w, conv_b.reshape(1, 2 * D_FF), w_down.astype(BF16))
    return out.reshape(BATCH, SEQ, D_MODEL)
```

```python
import numpy as np
import jax
import jax.numpy as jnp
from jax import lax
from jax.experimental import pallas as pl
from jax.experimental.pallas import tpu as pltpu

F32 = jnp.float32
BF16 = jnp.bfloat16

D_MODEL = 2048
BATCH = 4
SEQ = 4096
N_TOK = BATCH * SEQ
NSA_HEADS = 16
KV_GROUPS = 4
REP = NSA_HEADS // KV_GROUPS
HEAD_DIM = 64
NSA_WIDTH = NSA_HEADS * HEAD_DIM
GMLP_WIDTH = D_MODEL - NSA_WIDTH
GMLP_GROUPS = 8
GMLP_CHUNK = 128
CMP_BLOCK = 32
CMP_STRIDE = 16
CMP_HIDDEN = 256
N_CMP_PAD = SEQ // CMP_STRIDE
SEL_BLOCK = 64
N_SEL = SEQ // SEL_BLOCK
SEL_TOPK = 16
WINDOW = 512
Q_BLOCK = 128
N_QB = SEQ // Q_BLOCK
N_BRANCH = 3
KV_COLS = KV_GROUPS * HEAD_DIM
D_FF = 5632
CONV_WIDTH = 3
NORM_EPS = 1e-6
LN_EPS = 1e-5
NEG_BIG = -1e30
SEL_BIG = 1e9

LANES = 128
VMEM_LIMIT_BYTES = 56 * 1024 * 1024

Z_COLS = 2 * GMLP_WIDTH
PROJ_COLS = Z_COLS + NSA_WIDTH + 6 * KV_COLS
GATE_PAD = LANES
GATE_ROWS_PER_GROUP = 16

PROJ_TM = 1024
PROJ_TN = 512
GMLP_TM = 512
OPROJ_TM = 256
FFN_TM = 512
FFN_TF = 512
SEL_TILE_BLOCKS = 4
MASK_LANE0 = 80
SUM_ROWS = 16
VT_ROWS = HEAD_DIM + SUM_ROWS
LOG2E = 1.4426950408889634
WIN_BLOCKS = WINDOW // Q_BLOCK + 1


def _cparams(*sem):
    return pltpu.CompilerParams(dimension_semantics=sem, vmem_limit_bytes=VMEM_LIMIT_BYTES)


def _dot(a, b):
    return jnp.dot(a, b, preferred_element_type=F32)


def _proj_kernel(x_ref, nw_ref, w_ref, wg_ref, o_ref, og_ref, h_ref):
    @pl.when(pl.program_id(1) == 0)
    def _():
        x = x_ref[...]
        ms = jnp.mean(x * x, axis=-1, keepdims=True)
        h = (x * lax.rsqrt(ms + NORM_EPS) * nw_ref[...]).astype(BF16)
        h_ref[...] = h
        og_ref[...] = _dot(h, wg_ref[...])

    o_ref[...] = _dot(h_ref[...], w_ref[...])


def _proj(x2, nw, w_main, w_gate):
    return pl.pallas_call(
        _proj_kernel,
        grid=(N_TOK // PROJ_TM, PROJ_COLS // PROJ_TN),
        in_specs=[
            pl.BlockSpec((PROJ_TM, D_MODEL), lambda i, j: (i, 0)),
            pl.BlockSpec((1, D_MODEL), lambda i, j: (0, 0)),
            pl.BlockSpec((D_MODEL, PROJ_TN), lambda i, j: (0, j)),
            pl.BlockSpec((D_MODEL, GATE_PAD), lambda i, j: (0, 0)),
        ],
        out_specs=[
            pl.BlockSpec((PROJ_TM, PROJ_TN), lambda i, j: (i, j)),
            pl.BlockSpec((PROJ_TM, GATE_PAD), lambda i, j: (i, 0)),
        ],
        out_shape=[
            jax.ShapeDtypeStruct((N_TOK, PROJ_COLS), F32),
            jax.ShapeDtypeStruct((N_TOK, GATE_PAD), F32),
        ],
        scratch_shapes=[pltpu.VMEM((PROJ_TM, D_MODEL), BF16)],
        compiler_params=_cparams("arbitrary", "arbitrary"),
        name="proj",
    )(x2, nw, w_main, w_gate)


def _pos_features(lane, first, f1, f2):
    return jnp.where(lane < first + 3, f1, jnp.where(lane < first + 6, f2,
                                                     jnp.where(lane == first + 6, 1.0, 0.0)))


def _kprep_kernel(s0_ref, s1_ref, s2_ref, gl_ref, knw_ref, kcw_ref, pos_ref, w1_ref, w2_ref,
                  ksel_ref, kwin_ref, vt_ref, kc_ref, vct_ref, gt_ref, ha_ref, hb_ref):
    lane = lax.broadcasted_iota(jnp.int32, (Q_BLOCK, LANES), 1)
    row = lax.broadcasted_iota(jnp.int32, (Q_BLOCK, LANES), 0)
    lo = lane < HEAD_DIM
    inv_d = 1.0 / HEAD_DIM
    ones_rows = jnp.where(lax.broadcasted_iota(jnp.int32, (SUM_ROWS, Q_BLOCK), 0) == 0, 1.0, 0.0)

    def chunk(c, carry):
        r0 = pl.multiple_of(c * Q_BLOCK, Q_BLOCK)
        x = s0_ref[pl.ds(r0, Q_BLOCK), :]
        sq = x * x
        s_lo = jnp.sum(jnp.where(lo, sq, 0.0), axis=1, keepdims=True)
        s_hi = jnp.sum(jnp.where(lo, 0.0, sq), axis=1, keepdims=True)
        ms = jnp.where(lo, s_lo, s_hi) * inv_d
        y = x * lax.rsqrt(ms + NORM_EPS) * knw_ref[...]
        pos = r0 + row
        blk = lax.shift_right_logical(pos, 6)
        f1 = (blk * SEL_BLOCK).astype(F32)
        f2 = (pos & (SEL_BLOCK - 1)).astype(F32)
        onehot = jnp.where(lane - MASK_LANE0 == (blk & (SEL_TILE_BLOCKS * 2 - 1)), 1.0, 0.0)
        feat_sel = jnp.where(lane < MASK_LANE0, _pos_features(lane, HEAD_DIM, f1, f2), onehot)
        ksel_ref[c] = jnp.where(lo, y, feat_sel).astype(BF16)
        kwin_ref[c] = jnp.where(lo, _pos_features(lane, 0, f1, f2), y).astype(BF16)
        v_t = s2_ref[pl.ds(r0, Q_BLOCK), :].T
        vt_ref[c] = jnp.concatenate(
            [v_t[0:HEAD_DIM], ones_rows, v_t[HEAD_DIM:], ones_rows], axis=0).astype(BF16)
        return carry

    lax.fori_loop(0, N_QB, chunk, 0)

    @pl.when(pl.program_id(1) == 0)
    def _():
        def gchunk(c, carry):
            r0 = pl.multiple_of(c * Q_BLOCK, Q_BLOCK)
            gt_ref[c] = jax.nn.sigmoid(gl_ref[pl.ds(r0, Q_BLOCK), :]).T
            return carry

        lax.fori_loop(0, N_QB, gchunk, 0)

    ha_ref[...] = jnp.zeros_like(ha_ref)
    hb_ref[...] = jnp.zeros_like(hb_ref)
    for tt in range(CMP_STRIDE):
        rows = s1_ref[pl.ds(tt, N_CMP_PAD, stride=CMP_STRIDE), :]
        xa = (rows + pos_ref[tt:tt + 1, :]).astype(BF16)
        xb = (rows + pos_ref[CMP_STRIDE + tt:CMP_STRIDE + tt + 1, :]).astype(BF16)
        ha_ref[...] += _dot(xa, w1_ref[tt])
        hb_ref[0:N_CMP_PAD, :] += _dot(xb, w1_ref[CMP_STRIDE + tt])
    hid = jax.nn.gelu(ha_ref[...] + hb_ref[1:N_CMP_PAD + 1, :]).astype(BF16)
    kc = _dot(hid[:, :CMP_HIDDEN], w2_ref[0])
    vc = _dot(hid[:, CMP_HIDDEN:], w2_ref[1])
    ms = jnp.sum(kc * kc, axis=1, keepdims=True) * inv_d
    kcn = kc * lax.rsqrt(ms + NORM_EPS) * kcw_ref[...]
    ci = lax.broadcasted_iota(jnp.int32, (N_CMP_PAD, LANES), 0)
    cl = lax.broadcasted_iota(jnp.int32, (N_CMP_PAD, LANES), 1)
    c1 = (lax.shift_right_logical(ci, 2) * SEL_BLOCK).astype(F32)
    c2 = ((ci & 3) * CMP_STRIDE).astype(F32) + (CMP_BLOCK - 1) * 0.5
    kc_ref[...] = jnp.where(cl < HEAD_DIM, kcn, _pos_features(cl, HEAD_DIM, c1, c2)).astype(BF16)
    vct_ref[...] = vc.T[0:HEAD_DIM, :].astype(BF16)


def _kprep(proj, gl, knw2, kcw, pos2, w1p, w2p):
    kv0 = (Z_COLS + NSA_WIDTH) // LANES
    slab = lambda s: pl.BlockSpec((SEQ, LANES), lambda b, g, s=s: (b, kv0 + 3 * g + s))
    full = lambda shape: pl.BlockSpec(shape, lambda b, g: (0,) * len(shape))
    per_bg = lambda shape: pl.BlockSpec((None, None) + shape, lambda b, g: (b, g) + (0,) * len(shape))
    bg_shape = lambda shape, dt: jax.ShapeDtypeStruct((BATCH, KV_GROUPS) + shape, dt)
    return pl.pallas_call(
        _kprep_kernel,
        grid=(BATCH, KV_GROUPS),
        in_specs=[
            slab(0), slab(1), slab(2),
            pl.BlockSpec((SEQ, GATE_PAD), lambda b, g: (b, 0)),
            full((1, LANES)), full((1, LANES)), full((CMP_BLOCK, LANES)),
            full((CMP_BLOCK, LANES, 2 * CMP_HIDDEN)), full((2, CMP_HIDDEN, LANES)),
        ],
        out_specs=[
            per_bg((N_QB, Q_BLOCK, LANES)),
            per_bg((N_QB, Q_BLOCK, LANES)),
            per_bg((N_QB, 2 * VT_ROWS, Q_BLOCK)),
            per_bg((N_CMP_PAD, LANES)),
            per_bg((HEAD_DIM, N_CMP_PAD)),
            pl.BlockSpec((None, N_QB, GATE_PAD, Q_BLOCK), lambda b, g: (b, 0, 0, 0)),
        ],
        out_shape=[
            bg_shape((N_QB, Q_BLOCK, LANES), BF16),
            bg_shape((N_QB, Q_BLOCK, LANES), BF16),
            bg_shape((N_QB, 2 * VT_ROWS, Q_BLOCK), BF16),
            bg_shape((N_CMP_PAD, LANES), BF16),
            bg_shape((HEAD_DIM, N_CMP_PAD), BF16),
            jax.ShapeDtypeStruct((BATCH, N_QB, GATE_PAD, Q_BLOCK), F32),
        ],
        scratch_shapes=[
            pltpu.VMEM((N_CMP_PAD, 2 * CMP_HIDDEN), F32),
            pltpu.VMEM((N_CMP_PAD + 8, 2 * CMP_HIDDEN), F32),
        ],
        compiler_params=_cparams("arbitrary", "arbitrary"),
        name="kprep",
    )(proj, proj, proj, gl, knw2, kcw, pos2, w1p, w2p)


def _attn_kernel(q_ref, ksel_ref, kwin_ref, vt_ref, kc_ref, vct_ref, gt_ref, sf_ref, wm_ref, dm_ref,
                 ov_ref, wq_ref, o_ref, sel_ref, qs_ref, sa_ref, sb_ref):
    qb = pl.program_id(2)
    t_i = qb * Q_BLOCK + lax.broadcasted_iota(jnp.int32, (1, Q_BLOCK), 1)
    head = lambda a, r: a[:, r * Q_BLOCK:(r + 1) * Q_BLOCK]
    heads = lambda parts: jnp.concatenate(parts, axis=1)

    q_t = q_ref[...].T
    parts = []
    for r in range(REP):
        xq = q_t[r * HEAD_DIM:(r + 1) * HEAD_DIM, :]
        ms = jnp.sum(xq * xq, axis=0, keepdims=True) * (1.0 / HEAD_DIM)
        parts.append(xq * lax.rsqrt(ms + NORM_EPS) * wq_ref[...] * (HEAD_DIM ** -0.5) * LOG2E)
    qn = heads(parts)
    sf = sf_ref[...]
    sf = jnp.where(lax.broadcasted_iota(jnp.int32, sf.shape, 0) == 6, sf * (qb * Q_BLOCK).astype(F32), sf)
    wide = REP * Q_BLOCK
    pad = jnp.zeros((HEAD_DIM - 8, wide), F32)
    qs_ref[...] = jnp.concatenate([qn, sf, pad], axis=0).astype(BF16)
    q_win = jnp.concatenate([sf, pad, qn], axis=0).astype(BF16)

    sc = _dot(kc_ref[...], qs_ref[...])
    ci = lax.broadcasted_iota(jnp.int32, (N_CMP_PAD, Q_BLOCK), 0)
    cm = jnp.where(ci * CMP_STRIDE + (CMP_BLOCK - 1) <= t_i, 0.0, NEG_BIG)
    any_valid = t_i >= CMP_BLOCK - 1
    pcs = []
    for r in range(REP):
        s = head(sc, r) + cm
        e = jnp.exp2(s - jnp.max(s, axis=0, keepdims=True))
        d = jnp.sum(e, axis=0, keepdims=True)
        pcs.append(e * jnp.where(any_valid, 1.0 / d, 0.0))
    o_c = _dot(vct_ref[...], heads(pcs).astype(BF16))

    kbs = [jnp.maximum(qb - (WIN_BLOCKS - 1) + x, 0) for x in range(WIN_BLOCKS)]
    sw = _dot(jnp.concatenate([kwin_ref[kb] for kb in kbs], axis=0), q_win)
    wm = jnp.concatenate(
        [wm_ref[x * Q_BLOCK:(x + 1) * Q_BLOCK, :] + jnp.where(qb - (WIN_BLOCKS - 1) + x < 0, NEG_BIG, 0.0)
         for x in range(WIN_BLOCKS)], axis=0)
    pws = []
    for r in range(REP):
        s = head(sw, r) + wm
        pws.append(jnp.exp2(s - jnp.max(s, axis=0, keepdims=True)).astype(BF16))
    vw_t = jnp.concatenate([vt_ref[kb, VT_ROWS:2 * VT_ROWS, :] for kb in kbs], axis=1)
    acc_w = _dot(vw_t, heads(pws))
    o_w = acc_w[0:HEAD_DIM] / acc_w[HEAD_DIM:HEAD_DIM + 1]

    psum = (pcs[0] + pcs[1]) + (pcs[2] + pcs[3])
    p_hi = psum.astype(BF16)
    p_lo = (psum - p_hi.astype(F32)).astype(BF16)
    imp = _dot(ov_ref[...], p_hi) + _dot(ov_ref[...], p_lo)
    jj = lax.broadcasted_iota(jnp.int32, (N_SEL, Q_BLOCK), 0)
    cur = lax.shift_right_logical(t_i, 6)
    valid = jj <= cur
    sel_ref[...] = jnp.where(valid, 1.0, 0.0)

    @pl.when(qb * Q_BLOCK + Q_BLOCK > SEL_TOPK * SEL_BLOCK)
    def _():
        forced = (jj == 0) | (jj == cur) | (jj == cur - 1)
        score = jnp.where(valid, jnp.where(forced, SEL_BIG, imp), -SEL_BIG)
        rows8 = [score[8 * v:8 * v + 8, :] for v in range(N_SEL // 8)]
        j8 = lax.broadcasted_iota(jnp.int32, (8, Q_BLOCK), 0)
        ranks = [jnp.zeros((8, Q_BLOCK), F32) for _ in rows8]
        for jp in range(N_SEL):
            row = score[jp:jp + 1, :]
            for v, blk in enumerate(rows8):
                if 8 * v > jp:
                    ahead = jnp.where(row >= blk, 1.0, 0.0)
                elif 8 * v + 7 <= jp:
                    ahead = jnp.where(row > blk, 1.0, 0.0)
                else:
                    ahead = jnp.where(j8 + 8 * v > jp, jnp.where(row >= blk, 1.0, 0.0),
                                      jnp.where(row > blk, 1.0, 0.0))
                ranks[v] = ranks[v] + ahead
        rank = jnp.concatenate(ranks, axis=0)
        sel_ref[...] = jnp.where(valid & (rank < SEL_TOPK), 1.0, 0.0)

    tile_keys = SEL_TILE_BLOCKS * Q_BLOCK
    sub = tile_keys // SEL_BLOCK

    def scores_into(kt, s_ref):
        member = sel_ref[pl.ds(kt * sub, sub), :]
        mrows = (member - 1.0) * (-NEG_BIG)
        qs_ref[MASK_LANE0:MASK_LANE0 + 16, :] = jnp.concatenate(
            [heads([mrows] * REP), jnp.zeros((16 - sub, wide), F32)], axis=0).astype(BF16)
        k_tile = ksel_ref[pl.ds(kt * SEL_TILE_BLOCKS, SEL_TILE_BLOCKS)].reshape(tile_keys, LANES)
        s_ref[...] = _dot(k_tile, qs_ref[...])

    def absorb(kt, s_ref, carry, diagonal):
        m_run, acc = carry
        kb0 = kt * SEL_TILE_BLOCKS
        if diagonal:
            causal = dm_ref[qb % SEL_TILE_BLOCKS]
        v_t = jnp.concatenate([vt_ref[kb0 + x, 0:VT_ROWS, :] for x in range(SEL_TILE_BLOCKS)], axis=1)
        ms_, accs = [], []
        for half in range(REP // 2):
            ps, als = [], []
            for r in range(2 * half, 2 * half + 2):
                sr = s_ref[:, r * Q_BLOCK:(r + 1) * Q_BLOCK]
                if diagonal:
                    sr = sr + causal
                m_old = head(m_run, r)
                m_new = jnp.maximum(m_old, jnp.max(sr, axis=0, keepdims=True))
                ps.append(jnp.exp2(sr - m_new).astype(BF16))
                als.append(jnp.exp2(m_old - m_new))
                ms_.append(m_new)
            acc_h = acc[:, half * 2 * Q_BLOCK:(half + 1) * 2 * Q_BLOCK]
            accs.append(acc_h * heads(als) + _dot(v_t, heads(ps)))
        return heads(ms_), heads(accs)

    carry = (jnp.full((1, wide), NEG_BIG, F32), jnp.zeros((VT_ROWS, wide), F32))
    n_full = qb // SEL_TILE_BLOCKS
    scores_into(0, sa_ref)

    def tile_pair(i, carry):
        scores_into(2 * i + 1, sb_ref)
        carry = absorb(2 * i, sa_ref, carry, False)
        scores_into(2 * i + 2, sa_ref)
        return absorb(2 * i + 1, sb_ref, carry, False)

    carry = lax.fori_loop(0, n_full // 2, tile_pair, carry)
    last = 2 * (n_full // 2)

    def odd_tail(carry):
        scores_into(last + 1, sb_ref)
        return absorb(last + 1, sb_ref, absorb(last, sa_ref, carry, False), True)

    def even_tail(carry):
        return absorb(last, sa_ref, carry, True)

    _, acc_s = lax.cond(n_full % 2 == 1, odd_tail, even_tail, carry)
    o_s = acc_s[0:HEAD_DIM] / acc_s[HEAD_DIM:HEAD_DIM + 1]

    gates = gt_ref[...]
    outs = []
    for r in range(REP):
        outs.append(gates[r:r + 1, :] * head(o_c, r)
                    + gates[REP + r:REP + r + 1, :] * head(o_s, r)
                    + gates[2 * REP + r:2 * REP + r + 1, :] * head(o_w, r))
    o_ref[...] = jnp.concatenate(outs, axis=0).T.astype(BF16)


def _attn(proj, ksel, kwin, vt, kc, vct, gt, sf, wm, dm, ov_t, wq_b):
    q0 = Z_COLS // (REP * HEAD_DIM)
    kvspec = lambda shape: pl.BlockSpec((None, None) + shape, lambda b, g, qb: (b, g) + (0,) * len(shape))
    const = lambda shape: pl.BlockSpec(shape, lambda b, g, qb: (0,) * len(shape))
    return pl.pallas_call(
        _attn_kernel,
        grid=(BATCH, KV_GROUPS, N_QB),
        in_specs=[
            pl.BlockSpec((Q_BLOCK, REP * HEAD_DIM), lambda b, g, qb: (b * N_QB + qb, q0 + g)),
            kvspec((N_QB, Q_BLOCK, LANES)),
            kvspec((N_QB, Q_BLOCK, LANES)),
            kvspec((N_QB, 2 * VT_ROWS, Q_BLOCK)),
            kvspec((N_CMP_PAD, LANES)),
            kvspec((HEAD_DIM, N_CMP_PAD)),
            pl.BlockSpec((None, None, GATE_ROWS_PER_GROUP, Q_BLOCK), lambda b, g, qb: (b, qb, g, 0)),
            pl.BlockSpec((None, 8, REP * Q_BLOCK), lambda b, g, qb: (g, 0, 0)),
            const((WIN_BLOCKS * Q_BLOCK, Q_BLOCK)),
            const((SEL_TILE_BLOCKS, SEL_TILE_BLOCKS * Q_BLOCK, Q_BLOCK)),
            const((N_SEL, N_CMP_PAD)),
            const((HEAD_DIM, Q_BLOCK)),
        ],
        out_specs=pl.BlockSpec((Q_BLOCK, REP * HEAD_DIM), lambda b, g, qb: (b * N_QB + qb, g)),
        out_shape=jax.ShapeDtypeStruct((N_TOK, NSA_WIDTH), BF16),
        scratch_shapes=[pltpu.VMEM((N_SEL, Q_BLOCK), F32), pltpu.VMEM((LANES, REP * Q_BLOCK), BF16),
                        pltpu.VMEM((SEL_TILE_BLOCKS * Q_BLOCK, REP * Q_BLOCK), F32),
                        pltpu.VMEM((SEL_TILE_BLOCKS * Q_BLOCK, REP * Q_BLOCK), F32)],
        compiler_params=_cparams("arbitrary", "arbitrary", "arbitrary"),
        name="attn",
    )(proj, ksel, kwin, vt, kc, vct, gt, sf, wm, dm, ov_t, wq_b)


def _gmlp_kernel(z_ref, lnw_ref, lnb_ref, sw_ref, sbx_ref, o_ref):
    ge = jax.nn.gelu(z_ref[...])
    u = ge[:, :GMLP_WIDTH]
    v = ge[:, GMLP_WIDTH:]
    mu = jnp.mean(v, axis=-1, keepdims=True)
    var = jnp.mean(jnp.square(v - mu), axis=-1, keepdims=True)
    vn = ((v - mu) * lax.rsqrt(var + LN_EPS) * lnw_ref[...] + lnb_ref[...]).astype(BF16)
    ti = lax.broadcasted_iota(jnp.int32, (GMLP_CHUNK, GMLP_CHUNK), 0)
    si = lax.broadcasted_iota(jnp.int32, (GMLP_CHUNK, GMLP_CHUNK), 1)
    ws = [jnp.where(ti >= si, sw_ref[gg], 0.0).astype(BF16) for gg in range(GMLP_GROUPS)]
    gd = GMLP_WIDTH // GMLP_GROUPS
    for c in range(GMLP_TM // GMLP_CHUNK):
        rows = slice(c * GMLP_CHUNK, (c + 1) * GMLP_CHUNK)
        mix = jnp.concatenate(
            [_dot(ws[gg], vn[rows, gg * gd:(gg + 1) * gd]) for gg in range(GMLP_GROUPS)], axis=1)
        o_ref[rows, :] = (u[rows, :] * (mix + sbx_ref[...])).astype(BF16)


def _gmlp(proj, lnw, lnb, sw, sbx):
    return pl.pallas_call(
        _gmlp_kernel,
        grid=(N_TOK // GMLP_TM,),
        in_specs=[
            pl.BlockSpec((GMLP_TM, Z_COLS), lambda i: (i, 0)),
            pl.BlockSpec((1, GMLP_WIDTH), lambda i: (0, 0)),
            pl.BlockSpec((1, GMLP_WIDTH), lambda i: (0, 0)),
            pl.BlockSpec((GMLP_GROUPS, GMLP_CHUNK, GMLP_CHUNK), lambda i: (0, 0, 0)),
            pl.BlockSpec((GMLP_CHUNK, GMLP_WIDTH), lambda i: (0, 0)),
        ],
        out_specs=pl.BlockSpec((GMLP_TM, GMLP_WIDTH), lambda i: (i, 0)),
        out_shape=jax.ShapeDtypeStruct((N_TOK, GMLP_WIDTH), BF16),
        compiler_params=_cparams("arbitrary"),
        name="gmlp",
    )(proj, lnw, lnb, sw, sbx)


def _oproj_kernel(a_ref, b_ref, x_ref, wa_ref, wb_ref, nw_ref, x1_ref, xn_ref):
    y = x_ref[...] + (_dot(a_ref[...], wa_ref[...]) + _dot(b_ref[...], wb_ref[...]))
    x1_ref[...] = y
    ms = jnp.mean(y * y, axis=-1, keepdims=True)
    xn_ref[...] = (y * lax.rsqrt(ms + NORM_EPS) * nw_ref[...]).astype(BF16)


def _oproj(a, b, x2, wa, wb, nw):
    return pl.pallas_call(
        _oproj_kernel,
        grid=(N_TOK // OPROJ_TM,),
        in_specs=[
            pl.BlockSpec((OPROJ_TM, NSA_WIDTH), lambda i: (i, 0)),
            pl.BlockSpec((OPROJ_TM, GMLP_WIDTH), lambda i: (i, 0)),
            pl.BlockSpec((OPROJ_TM, D_MODEL), lambda i: (i, 0)),
            pl.BlockSpec((NSA_WIDTH, D_MODEL), lambda i: (0, 0)),
            pl.BlockSpec((GMLP_WIDTH, D_MODEL), lambda i: (0, 0)),
            pl.BlockSpec((1, D_MODEL), lambda i: (0, 0)),
        ],
        out_specs=[
            pl.BlockSpec((OPROJ_TM, D_MODEL), lambda i: (i, 0)),
            pl.BlockSpec((OPROJ_TM, D_MODEL), lambda i: (i, 0)),
        ],
        out_shape=[
            jax.ShapeDtypeStruct((N_TOK, D_MODEL), F32),
            jax.ShapeDtypeStruct((N_TOK, D_MODEL), BF16),
        ],
        compiler_params=_cparams("arbitrary"),
        name="oproj",
    )(a, b, x2, wa, wb, nw)


def _causal_conv(h, prev, cw, cb):
    rid = lax.broadcasted_iota(jnp.int32, prev.shape, 0)

    def shifted(k):
        body = pltpu.roll(h, k, 0)
        top = jnp.where(rid < k, pltpu.roll(prev, k, 0), body[0:8, :])
        return jnp.concatenate([top, body[8:, :]], axis=0)

    return cb + ((cw[0:1, :] * shifted(2) + cw[1:2, :] * shifted(1)) + cw[2:3, :] * h)


def _ffn_kernel(xn_ref, x1_ref, wg_ref, wu_ref, cwg_ref, cwu_ref, cbg_ref, cbu_ref, wd_ref,
                o_ref, carry_ref):
    i = pl.program_id(0)
    j = pl.program_id(1)
    xn = xn_ref[...]
    hg = _dot(xn, wg_ref[...])
    hu = _dot(xn, wu_ref[...])
    @pl.when((i == 0) & (j == 0))
    def _():
        carry_ref[...] = jnp.zeros_like(carry_ref)

    seq_start = i % (SEQ // FFN_TM) == 0
    prev_g = jnp.where(seq_start, 0.0, carry_ref[j, 0])
    prev_u = jnp.where(seq_start, 0.0, carry_ref[j, 1])
    carry_ref[j, 0] = hg[FFN_TM - 8:, :]
    carry_ref[j, 1] = hu[FFN_TM - 8:, :]
    cg = _causal_conv(hg, prev_g, cwg_ref[...], cbg_ref[...])
    cu = _causal_conv(hu, prev_u, cwu_ref[...], cbu_ref[...])
    act = (jax.nn.silu(cg) * cu).astype(BF16)

    @pl.when(j == 0)
    def _():
        o_ref[...] = x1_ref[...]

    o_ref[...] += _dot(act, wd_ref[...])


def _ffn(xn, x1, w_up, conv_w, conv_b, w_down):
    nj = D_FF // FFN_TF
    return pl.pallas_call(
        _ffn_kernel,
        grid=(N_TOK // FFN_TM, nj),
        in_specs=[
            pl.BlockSpec((FFN_TM, D_MODEL), lambda i, j: (i, 0)),
            pl.BlockSpec((FFN_TM, D_MODEL), lambda i, j: (i, 0)),
            pl.BlockSpec((D_MODEL, FFN_TF), lambda i, j: (0, j)),
            pl.BlockSpec((D_MODEL, FFN_TF), lambda i, j: (0, nj + j)),
            pl.BlockSpec((CONV_WIDTH, FFN_TF), lambda i, j: (0, j)),
            pl.BlockSpec((CONV_WIDTH, FFN_TF), lambda i, j: (0, nj + j)),
            pl.BlockSpec((1, FFN_TF), lambda i, j: (0, j)),
            pl.BlockSpec((1, FFN_TF), lambda i, j: (0, nj + j)),
            pl.BlockSpec((FFN_TF, D_MODEL), lambda i, j: (j, 0)),
        ],
        out_specs=pl.BlockSpec((FFN_TM, D_MODEL), lambda i, j: (i, 0)),
        out_shape=jax.ShapeDtypeStruct((N_TOK, D_MODEL), F32),
        scratch_shapes=[pltpu.VMEM((nj, 2, 8, FFN_TF), F32)],
        compiler_params=_cparams("arbitrary", "arbitrary"),
        name="ffn",
    )(xn, x1, w_up, w_up, conv_w, conv_w, conv_b, conv_b, w_down)


def _overlap_t():
    start = np.arange(N_CMP_PAD)[None, :] * CMP_STRIDE
    s0 = np.arange(N_SEL)[:, None] * SEL_BLOCK
    ov = (start < s0 + SEL_BLOCK) & (start + CMP_BLOCK > s0) & (np.arange(N_CMP_PAD)[None, :] < N_CMP_PAD - 1)
    return ov.astype(np.float32)


def _bf16_round(a):
    return np.asarray(a, np.float32).astype(BF16).astype(np.float32)


def _slope_features():
    sl = (np.power(2.0, -8.0 * np.arange(1, NSA_HEADS + 1) / NSA_HEADS).astype(np.float32)
          * np.float32(LOG2E)).astype(np.float32)
    s1 = _bf16_round(sl)
    s2 = _bf16_round(sl - s1)
    s3 = _bf16_round(sl - s1 - s2)
    rows = np.stack([s1, s2, s3, s1, s2, s3, -sl, np.zeros_like(sl)], axis=0)
    rows = rows.reshape(8, KV_GROUPS, REP).transpose(1, 0, 2)
    return np.repeat(rows, Q_BLOCK, axis=2).astype(np.float32)


def _window_mask():
    kl = np.arange(WIN_BLOCKS * Q_BLOCK)[:, None]
    ql = np.arange(Q_BLOCK)[None, :]
    dist = ql + WINDOW - kl
    return np.where((dist >= 0) & (dist < WINDOW), 0.0, NEG_BIG).astype(np.float32)


def _diag_masks():
    kl = np.arange(SEL_TILE_BLOCKS * Q_BLOCK)[None, :, None]
    ql = np.arange(Q_BLOCK)[None, None, :]
    off = np.arange(SEL_TILE_BLOCKS)[:, None, None] * Q_BLOCK
    return np.where(kl > ql + off, NEG_BIG, 0.0).astype(np.float32)


def _layout_w_in(w_in):
    q_end = NSA_WIDTH
    kv_end = q_end + 6 * KV_COLS
    g_end = kv_end + NSA_HEADS * N_BRANCH
    w_q = w_in[:, :q_end]
    w_kv = w_in[:, q_end:kv_end].reshape(D_MODEL, 6, KV_GROUPS, HEAD_DIM)
    w_kv = jnp.stack([w_kv[:, s] for s in (2, 4, 0, 1, 3, 5)], axis=2).reshape(D_MODEL, 6 * KV_COLS)
    w_z = w_in[:, g_end:]
    w_main = jnp.concatenate([w_z, w_q, w_kv], axis=1).astype(BF16)
    w_g = w_in[:, kv_end:g_end].reshape(D_MODEL, KV_GROUPS, REP, N_BRANCH).transpose(0, 1, 3, 2)
    w_g = w_g.reshape(D_MODEL, KV_GROUPS, N_BRANCH * REP)
    w_g = jnp.pad(w_g, ((0, 0), (0, 0), (0, GATE_ROWS_PER_GROUP - N_BRANCH * REP)))
    w_g = jnp.pad(w_g.reshape(D_MODEL, KV_GROUPS * GATE_ROWS_PER_GROUP),
                  ((0, 0), (0, GATE_PAD - KV_GROUPS * GATE_ROWS_PER_GROUP))).astype(BF16)
    return w_main, w_g


def _layout_compress(cmp_pos, cmp_w1, cmp_w2):
    pos2 = jnp.concatenate([cmp_pos[0], cmp_pos[1]], axis=1)
    w1 = cmp_w1.reshape(2, CMP_BLOCK, HEAD_DIM, CMP_HIDDEN)
    zero = jnp.zeros_like(w1[0])
    w1k = jnp.concatenate([w1[0], zero], axis=1)
    w1v = jnp.concatenate([zero, w1[1]], axis=1)
    w1p = jnp.concatenate([w1k, w1v], axis=2).astype(BF16)
    w2p = jnp.pad(cmp_w2, ((0, 0), (0, 0), (0, LANES - HEAD_DIM))).astype(BF16)
    return pos2, w1p, w2p


def kernel(x, attn_norm_w, w_in, q_norm_w, k_norm_w, cmp_pos, cmp_w1, cmp_w2, gmlp_ln_w, gmlp_ln_b,
           spatial_w, spatial_b, w_out, ffn_norm_w, w_up, conv_w, conv_b, w_down):
    x2 = x.reshape(N_TOK, D_MODEL)
    w_main, w_gate = _layout_w_in(w_in)
    proj, gl = _proj(x2, attn_norm_w.reshape(1, D_MODEL), w_main, w_gate)

    knw2 = jnp.concatenate([k_norm_w[1], k_norm_w[2]]).reshape(1, LANES)
    kcw = jnp.concatenate([k_norm_w[0], jnp.zeros((LANES - HEAD_DIM,), F32)]).reshape(1, LANES)
    pos2, w1p, w2p = _layout_compress(cmp_pos, cmp_w1, cmp_w2)
    ksel, kwin, vt, kc, vct, gt = _kprep(proj, gl, knw2, kcw, pos2, w1p, w2p)

    wq_b = jnp.broadcast_to(q_norm_w.reshape(HEAD_DIM, 1), (HEAD_DIM, Q_BLOCK))
    a = _attn(proj, ksel, kwin, vt, kc, vct, gt, jnp.asarray(_slope_features()),
              jnp.asarray(_window_mask()), jnp.asarray(_diag_masks()),
              jnp.asarray(_overlap_t(), dtype=BF16), wq_b)

    sbx = jnp.repeat(spatial_b.T, GMLP_WIDTH // GMLP_GROUPS, axis=1)
    b = _gmlp(proj, gmlp_ln_w.reshape(1, GMLP_WIDTH), gmlp_ln_b.reshape(1, GMLP_WIDTH), spatial_w, sbx)

    w_out_b = w_out.astype(BF16)
    x1, xn = _oproj(a, b, x2, w_out_b[:NSA_WIDTH], w_out_b[NSA_WIDTH:], ffn_norm_w.reshape(1, D_MODEL))

    out = _ffn(xn, x1, w_up.astype(BF16), conv_w, conv_b.reshape(1, 2 * D_FF), w_down.astype(BF16))
    return out.reshape(BATCH, SEQ, D_MODEL)
```

```python
import numpy as np
import jax
import jax.numpy as jnp
from jax import lax
from jax.experimental import pallas as pl
from jax.experimental.pallas import tpu as pltpu

F32 = jnp.float32
BF16 = jnp.bfloat16

D_MODEL = 2048
BATCH = 4
SEQ = 4096
N_TOK = BATCH * SEQ
NSA_HEADS = 16
KV_GROUPS = 4
REP = NSA_HEADS // KV_GROUPS
HEAD_DIM = 64
NSA_WIDTH = NSA_HEADS * HEAD_DIM
GMLP_WIDTH = D_MODEL - NSA_WIDTH
GMLP_GROUPS = 8
GMLP_CHUNK = 128
CMP_BLOCK = 32
CMP_STRIDE = 16
CMP_HIDDEN = 256
N_CMP_PAD = SEQ // CMP_STRIDE
SEL_BLOCK = 64
N_SEL = SEQ // SEL_BLOCK
SEL_TOPK = 16
WINDOW = 512
K_BLOCK = 128
N_KB = SEQ // K_BLOCK
Q_TILE = 256
N_QT = SEQ // Q_TILE
N_BRANCH = 3
KV_COLS = KV_GROUPS * HEAD_DIM
D_FF = 5632
CONV_WIDTH = 3
NORM_EPS = 1e-6
LN_EPS = 1e-5
NEG_BIG = -1e30
SEL_BIG = 1e9

LANES = 128
VMEM_LIMIT_BYTES = 56 * 1024 * 1024

Z_COLS = 2 * GMLP_WIDTH
PROJ_COLS = Z_COLS + NSA_WIDTH + 6 * KV_COLS
GATE_PAD = LANES
GATE_ROWS_PER_GROUP = 16

PROJ_TM = 1024
PROJ_TN = 512
GMLP_TM = 512
OPROJ_TM = 256
FFN_TM = 512
FFN_TF = 512
SEL_TILE_BLOCKS = 4
SEL_TILE_KEYS = SEL_TILE_BLOCKS * K_BLOCK
MASK_LANE0 = 80
SUM_ROWS = 16
VT_ROWS = HEAD_DIM + SUM_ROWS
LOG2E = 1.4426950408889634
WIN_BLOCKS = (WINDOW + Q_TILE) // K_BLOCK
WIN_KEYS = WIN_BLOCKS * K_BLOCK


def _cparams(*sem):
    return pltpu.CompilerParams(dimension_semantics=sem, vmem_limit_bytes=VMEM_LIMIT_BYTES)


def _dot(a, b):
    return jnp.dot(a, b, preferred_element_type=F32)


def _proj_kernel(x_ref, nw_ref, w_ref, wg_ref, o_ref, og_ref, h_ref):
    @pl.when(pl.program_id(1) == 0)
    def _():
        x = x_ref[...]
        ms = jnp.mean(x * x, axis=-1, keepdims=True)
        h = (x * lax.rsqrt(ms + NORM_EPS) * nw_ref[...]).astype(BF16)
        h_ref[...] = h
        og_ref[...] = _dot(h, wg_ref[...])

    o_ref[...] = _dot(h_ref[...], w_ref[...])


def _proj(x2, nw, w_main, w_gate):
    return pl.pallas_call(
        _proj_kernel,
        grid=(N_TOK // PROJ_TM, PROJ_COLS // PROJ_TN),
        in_specs=[
            pl.BlockSpec((PROJ_TM, D_MODEL), lambda i, j: (i, 0)),
            pl.BlockSpec((1, D_MODEL), lambda i, j: (0, 0)),
            pl.BlockSpec((D_MODEL, PROJ_TN), lambda i, j: (0, j)),
            pl.BlockSpec((D_MODEL, GATE_PAD), lambda i, j: (0, 0)),
        ],
        out_specs=[
            pl.BlockSpec((PROJ_TM, PROJ_TN), lambda i, j: (i, j)),
            pl.BlockSpec((PROJ_TM, GATE_PAD), lambda i, j: (i, 0)),
        ],
        out_shape=[
            jax.ShapeDtypeStruct((N_TOK, PROJ_COLS), F32),
            jax.ShapeDtypeStruct((N_TOK, GATE_PAD), F32),
        ],
        scratch_shapes=[pltpu.VMEM((PROJ_TM, D_MODEL), BF16)],
        compiler_params=_cparams("arbitrary", "arbitrary"),
        name="proj",
    )(x2, nw, w_main, w_gate)


def _pos_features(lane, first, f1, f2):
    return jnp.where(lane < first + 3, f1, jnp.where(lane < first + 6, f2,
                                                     jnp.where(lane == first + 6, 1.0, 0.0)))


def _kprep_kernel(s0_ref, s1_ref, s2_ref, gl_ref, knw_ref, kcw_ref, pos_ref, w1_ref, w2_ref,
                  ksel_ref, kwin_ref, vt_ref, kc_ref, vct_ref, gt_ref, ha_ref, hb_ref):
    lane = lax.broadcasted_iota(jnp.int32, (K_BLOCK, LANES), 1)
    row = lax.broadcasted_iota(jnp.int32, (K_BLOCK, LANES), 0)
    lo = lane < HEAD_DIM
    inv_d = 1.0 / HEAD_DIM
    ones_rows = jnp.where(lax.broadcasted_iota(jnp.int32, (SUM_ROWS, K_BLOCK), 0) == 0, 1.0, 0.0)

    def chunk(c, carry):
        r0 = pl.multiple_of(c * K_BLOCK, K_BLOCK)
        x = s0_ref[pl.ds(r0, K_BLOCK), :]
        sq = x * x
        s_lo = jnp.sum(jnp.where(lo, sq, 0.0), axis=1, keepdims=True)
        s_hi = jnp.sum(jnp.where(lo, 0.0, sq), axis=1, keepdims=True)
        ms = jnp.where(lo, s_lo, s_hi) * inv_d
        y = x * lax.rsqrt(ms + NORM_EPS) * knw_ref[...]
        pos = r0 + row
        blk = lax.shift_right_logical(pos, 6)
        f1 = (blk * SEL_BLOCK).astype(F32)
        f2 = (pos & (SEL_BLOCK - 1)).astype(F32)
        onehot = jnp.where(lane - MASK_LANE0 == (blk & (SEL_TILE_BLOCKS * 2 - 1)), 1.0, 0.0)
        feat_sel = jnp.where(lane < MASK_LANE0, _pos_features(lane, HEAD_DIM, f1, f2), onehot)
        ksel_ref[c] = jnp.where(lo, y, feat_sel).astype(BF16)
        kwin_ref[c] = jnp.where(lo, _pos_features(lane, 0, f1, f2), y).astype(BF16)
        v_t = s2_ref[pl.ds(r0, K_BLOCK), :].T
        vt_ref[c] = jnp.concatenate(
            [v_t[0:HEAD_DIM], ones_rows, v_t[HEAD_DIM:], ones_rows], axis=0).astype(BF16)
        return carry

    lax.fori_loop(0, N_KB, chunk, 0, unroll=2)

    @pl.when(pl.program_id(1) == 0)
    def _():
        def gchunk(c, carry):
            r0 = pl.multiple_of(c * K_BLOCK, K_BLOCK)
            gt_ref[c] = jax.nn.sigmoid(gl_ref[pl.ds(r0, K_BLOCK), :]).T
            return carry

        lax.fori_loop(0, N_KB, gchunk, 0, unroll=2)

    ha_ref[...] = jnp.zeros_like(ha_ref)
    hb_ref[...] = jnp.zeros_like(hb_ref)
    for tt in range(CMP_STRIDE):
        rows = s1_ref[pl.ds(tt, N_CMP_PAD, stride=CMP_STRIDE), :]
        xa = (rows + pos_ref[tt:tt + 1, :]).astype(BF16)
        xb = (rows + pos_ref[CMP_STRIDE + tt:CMP_STRIDE + tt + 1, :]).astype(BF16)
        ha_ref[...] += _dot(xa, w1_ref[tt])
        hb_ref[0:N_CMP_PAD, :] += _dot(xb, w1_ref[CMP_STRIDE + tt])
    hid = jax.nn.gelu(ha_ref[...] + hb_ref[1:N_CMP_PAD + 1, :]).astype(BF16)
    kc = _dot(hid[:, :CMP_HIDDEN], w2_ref[0])
    vc = _dot(hid[:, CMP_HIDDEN:], w2_ref[1])
    ms = jnp.sum(kc * kc, axis=1, keepdims=True) * inv_d
    kcn = kc * lax.rsqrt(ms + NORM_EPS) * kcw_ref[...]
    ci = lax.broadcasted_iota(jnp.int32, (N_CMP_PAD, LANES), 0)
    cl = lax.broadcasted_iota(jnp.int32, (N_CMP_PAD, LANES), 1)
    c1 = (lax.shift_right_logical(ci, 2) * SEL_BLOCK).astype(F32)
    c2 = ((ci & 3) * CMP_STRIDE).astype(F32) + (CMP_BLOCK - 1) * 0.5
    kc_ref[...] = jnp.where(cl < HEAD_DIM, kcn, _pos_features(cl, HEAD_DIM, c1, c2)).astype(BF16)
    vct_ref[...] = vc.T[0:HEAD_DIM, :].astype(BF16)


def _kprep(proj, gl, knw2, kcw, pos2, w1p, w2p):
    kv0 = (Z_COLS + NSA_WIDTH) // LANES
    slab = lambda s: pl.BlockSpec((SEQ, LANES), lambda b, g, s=s: (b, kv0 + 3 * g + s))
    full = lambda shape: pl.BlockSpec(shape, lambda b, g: (0,) * len(shape))
    per_bg = lambda shape: pl.BlockSpec((None, None) + shape, lambda b, g: (b, g) + (0,) * len(shape))
    bg_shape = lambda shape, dt: jax.ShapeDtypeStruct((BATCH, KV_GROUPS) + shape, dt)
    return pl.pallas_call(
        _kprep_kernel,
        grid=(BATCH, KV_GROUPS),
        in_specs=[
            slab(0), slab(1), slab(2),
            pl.BlockSpec((SEQ, GATE_PAD), lambda b, g: (b, 0)),
            full((1, LANES)), full((1, LANES)), full((CMP_BLOCK, LANES)),
            full((CMP_BLOCK, LANES, 2 * CMP_HIDDEN)), full((2, CMP_HIDDEN, LANES)),
        ],
        out_specs=[
            per_bg((N_KB, K_BLOCK, LANES)),
            per_bg((N_KB, K_BLOCK, LANES)),
            per_bg((N_KB, 2 * VT_ROWS, K_BLOCK)),
            per_bg((N_CMP_PAD, LANES)),
            per_bg((HEAD_DIM, N_CMP_PAD)),
            pl.BlockSpec((None, N_KB, GATE_PAD, K_BLOCK), lambda b, g: (b, 0, 0, 0)),
        ],
        out_shape=[
            bg_shape((N_KB, K_BLOCK, LANES), BF16),
            bg_shape((N_KB, K_BLOCK, LANES), BF16),
            bg_shape((N_KB, 2 * VT_ROWS, K_BLOCK), BF16),
            bg_shape((N_CMP_PAD, LANES), BF16),
            bg_shape((HEAD_DIM, N_CMP_PAD), BF16),
            jax.ShapeDtypeStruct((BATCH, N_KB, GATE_PAD, K_BLOCK), F32),
        ],
        scratch_shapes=[
            pltpu.VMEM((N_CMP_PAD, 2 * CMP_HIDDEN), F32),
            pltpu.VMEM((N_CMP_PAD + 8, 2 * CMP_HIDDEN), F32),
        ],
        compiler_params=_cparams("arbitrary", "arbitrary"),
        name="kprep",
    )(proj, proj, proj, gl, knw2, kcw, pos2, w1p, w2p)


def _attn_kernel(q_ref, ksel_ref, kwin_ref, vt_ref, kc_ref, vct_ref, gt_ref, sf_ref, wm_ref, dm_ref,
                 ov_ref, wq_ref, o_ref, sel_ref, qs_ref, sa_ref, sb_ref):
    qt = pl.program_id(2)
    t_i = qt * Q_TILE + lax.broadcasted_iota(jnp.int32, (1, Q_TILE), 1)
    head = lambda a, r: a[:, r * Q_TILE:(r + 1) * Q_TILE]
    heads = lambda parts: jnp.concatenate(parts, axis=1)

    q_t = q_ref[...].T
    parts = []
    for r in range(REP):
        xq = q_t[r * HEAD_DIM:(r + 1) * HEAD_DIM, :]
        ms = jnp.sum(xq * xq, axis=0, keepdims=True) * (1.0 / HEAD_DIM)
        parts.append(xq * lax.rsqrt(ms + NORM_EPS) * wq_ref[...] * (HEAD_DIM ** -0.5) * LOG2E)
    qn = heads(parts)
    sf = sf_ref[...]
    sf = jnp.where(lax.broadcasted_iota(jnp.int32, sf.shape, 0) == 6, sf * (qt * Q_TILE).astype(F32), sf)
    wide = REP * Q_TILE
    pad = jnp.zeros((HEAD_DIM - 8, wide), F32)
    qs_ref[...] = jnp.concatenate([qn, sf, pad], axis=0).astype(BF16)
    q_win = jnp.concatenate([sf, pad, qn], axis=0).astype(BF16)

    sc = _dot(kc_ref[...], qs_ref[...])
    ci = lax.broadcasted_iota(jnp.int32, (N_CMP_PAD, Q_TILE), 0)
    cm = jnp.where(ci * CMP_STRIDE + (CMP_BLOCK - 1) <= t_i, 0.0, NEG_BIG)
    any_valid = t_i >= CMP_BLOCK - 1
    pcs = []
    for r in range(REP):
        s = head(sc, r) + cm
        e = jnp.exp2(s - jnp.max(s, axis=0, keepdims=True))
        d = jnp.sum(e, axis=0, keepdims=True)
        pcs.append((e * jnp.where(any_valid, 1.0 / d, 0.0)).astype(BF16))
    oc_imp = _dot(jnp.concatenate([vct_ref[...], ov_ref[...]], axis=0), heads(pcs))
    o_c = oc_imp[0:HEAD_DIM]
    imp4 = oc_imp[HEAD_DIM:]
    imp = (head(imp4, 0) + head(imp4, 1)) + (head(imp4, 2) + head(imp4, 3))

    kb_first = qt * (Q_TILE // K_BLOCK) - WINDOW // K_BLOCK
    kbs = [jnp.maximum(kb_first + x, 0) for x in range(WIN_BLOCKS)]
    sw = _dot(jnp.concatenate([kwin_ref[kb] for kb in kbs], axis=0), q_win)
    wm = jnp.concatenate(
        [wm_ref[x * K_BLOCK:(x + 1) * K_BLOCK, :] + jnp.where(kb_first + x < 0, NEG_BIG, 0.0)
         for x in range(WIN_BLOCKS)], axis=0)
    pws = []
    for r in range(REP):
        s = head(sw, r) + wm
        pws.append(jnp.exp2(s - jnp.max(s, axis=0, keepdims=True)).astype(BF16))
    vw_t = jnp.concatenate([vt_ref[kb, VT_ROWS:2 * VT_ROWS, :] for kb in kbs], axis=1)
    acc_w = _dot(vw_t, heads(pws))
    o_w = acc_w[0:HEAD_DIM] / acc_w[HEAD_DIM:HEAD_DIM + 1]

    jj = lax.broadcasted_iota(jnp.int32, (N_SEL, Q_TILE), 0)
    cur = lax.shift_right_logical(t_i, 6)
    valid = jj <= cur
    sel_ref[...] = jnp.where(valid, 1.0, 0.0)

    @pl.when(qt * Q_TILE + Q_TILE > SEL_TOPK * SEL_BLOCK)
    def _():
        forced = (jj == 0) | (jj == cur) | (jj == cur - 1)
        score = jnp.where(valid, jnp.where(forced, SEL_BIG, imp), -SEL_BIG)
        rows8 = [score[8 * v:8 * v + 8, :] for v in range(N_SEL // 8)]
        j8 = lax.broadcasted_iota(jnp.int32, (8, Q_TILE), 0)
        ranks = [jnp.zeros((8, Q_TILE), F32) for _ in rows8]
        for jp in range(N_SEL):
            row = score[jp:jp + 1, :]
            for v, blk in enumerate(rows8):
                if 8 * v > jp:
                    ahead = jnp.where(row >= blk, 1.0, 0.0)
                elif 8 * v + 7 <= jp:
                    ahead = jnp.where(row > blk, 1.0, 0.0)
                else:
                    ahead = jnp.where(j8 + 8 * v > jp, jnp.where(row >= blk, 1.0, 0.0),
                                      jnp.where(row > blk, 1.0, 0.0))
                ranks[v] = ranks[v] + ahead
        rank = jnp.concatenate(ranks, axis=0)
        sel_ref[...] = jnp.where(valid & (rank < SEL_TOPK), 1.0, 0.0)

    sub = SEL_TILE_KEYS // SEL_BLOCK

    def scores_into(kt, s_ref):
        member = sel_ref[pl.ds(kt * sub, sub), :]
        mrows = (member - 1.0) * (-NEG_BIG)
        qs_ref[MASK_LANE0:MASK_LANE0 + 16, :] = jnp.concatenate(
            [heads([mrows] * REP), jnp.zeros((16 - sub, wide), F32)], axis=0).astype(BF16)
        k_tile = ksel_ref[pl.ds(kt * SEL_TILE_BLOCKS, SEL_TILE_BLOCKS)].reshape(SEL_TILE_KEYS, LANES)
        s_ref[...] = _dot(k_tile, qs_ref[...])

    def absorb(kt, s_ref, carry, diagonal):
        m_run, acc = carry
        kb0 = kt * SEL_TILE_BLOCKS
        if diagonal:
            causal = dm_ref[qt % (SEL_TILE_KEYS // Q_TILE)]
        v_t = jnp.concatenate([vt_ref[kb0 + x, 0:VT_ROWS, :] for x in range(SEL_TILE_BLOCKS)], axis=1)
        ms_, accs = [], []
        for r in range(REP):
            sr = s_ref[:, r * Q_TILE:(r + 1) * Q_TILE]
            if diagonal:
                sr = sr + causal
            m_old = head(m_run, r)
            m_new = jnp.maximum(m_old, jnp.max(sr, axis=0, keepdims=True))
            p = jnp.exp2(sr - m_new).astype(BF16)
            accs.append(head(acc, r) * jnp.exp2(m_old - m_new) + _dot(v_t, p))
            ms_.append(m_new)
        return heads(ms_), heads(accs)

    carry = (jnp.full((1, wide), NEG_BIG, F32), jnp.zeros((VT_ROWS, wide), F32))
    n_full = (qt * Q_TILE) // SEL_TILE_KEYS
    scores_into(0, sa_ref)

    def tile_pair(i, carry):
        scores_into(2 * i + 1, sb_ref)
        carry = absorb(2 * i, sa_ref, carry, False)
        scores_into(2 * i + 2, sa_ref)
        return absorb(2 * i + 1, sb_ref, carry, False)

    carry = lax.fori_loop(0, n_full // 2, tile_pair, carry)
    last = 2 * (n_full // 2)

    def odd_tail(carry):
        scores_into(last + 1, sb_ref)
        return absorb(last + 1, sb_ref, absorb(last, sa_ref, carry, False), True)

    def even_tail(carry):
        return absorb(last, sa_ref, carry, True)

    _, acc_s = lax.cond(n_full % 2 == 1, odd_tail, even_tail, carry)
    o_s = acc_s[0:HEAD_DIM] / acc_s[HEAD_DIM:HEAD_DIM + 1]

    gates = jnp.concatenate([gt_ref[x] for x in range(Q_TILE // K_BLOCK)], axis=1)
    outs = []
    for r in range(REP):
        outs.append(gates[r:r + 1, :] * head(o_c, r)
                    + gates[REP + r:REP + r + 1, :] * head(o_s, r)
                    + gates[2 * REP + r:2 * REP + r + 1, :] * head(o_w, r))
    o_ref[...] = jnp.concatenate(outs, axis=0).T.astype(BF16)


def _attn(proj, ksel, kwin, vt, kc, vct, gt, sf, wm, dm, ov_t, wq_b):
    q0 = Z_COLS // (REP * HEAD_DIM)
    kvspec = lambda shape: pl.BlockSpec((None, None) + shape, lambda b, g, qt: (b, g) + (0,) * len(shape))
    const = lambda shape: pl.BlockSpec(shape, lambda b, g, qt: (0,) * len(shape))
    return pl.pallas_call(
        _attn_kernel,
        grid=(BATCH, KV_GROUPS, N_QT),
        in_specs=[
            pl.BlockSpec((Q_TILE, REP * HEAD_DIM), lambda b, g, qt: (b * N_QT + qt, q0 + g)),
            kvspec((N_KB, K_BLOCK, LANES)),
            kvspec((N_KB, K_BLOCK, LANES)),
            kvspec((N_KB, 2 * VT_ROWS, K_BLOCK)),
            kvspec((N_CMP_PAD, LANES)),
            kvspec((HEAD_DIM, N_CMP_PAD)),
            pl.BlockSpec((None, Q_TILE // K_BLOCK, GATE_ROWS_PER_GROUP, K_BLOCK), lambda b, g, qt: (b, qt, g, 0)),
            pl.BlockSpec((None, 8, REP * Q_TILE), lambda b, g, qt: (g, 0, 0)),
            const((WIN_KEYS, Q_TILE)),
            const((SEL_TILE_KEYS // Q_TILE, SEL_TILE_KEYS, Q_TILE)),
            const((N_SEL, N_CMP_PAD)),
            const((HEAD_DIM, Q_TILE)),
        ],
        out_specs=pl.BlockSpec((Q_TILE, REP * HEAD_DIM), lambda b, g, qt: (b * N_QT + qt, g)),
        out_shape=jax.ShapeDtypeStruct((N_TOK, NSA_WIDTH), BF16),
        scratch_shapes=[pltpu.VMEM((N_SEL, Q_TILE), F32), pltpu.VMEM((LANES, REP * Q_TILE), BF16),
                        pltpu.VMEM((SEL_TILE_KEYS, REP * Q_TILE), F32),
                        pltpu.VMEM((SEL_TILE_KEYS, REP * Q_TILE), F32)],
        compiler_params=_cparams("arbitrary", "arbitrary", "arbitrary"),
        name="attn",
    )(proj, ksel, kwin, vt, kc, vct, gt, sf, wm, dm, ov_t, wq_b)


def _gmlp_kernel(z_ref, lnw_ref, lnb_ref, sw_ref, sbx_ref, o_ref):
    ge = jax.nn.gelu(z_ref[...])
    u = ge[:, :GMLP_WIDTH]
    v = ge[:, GMLP_WIDTH:]
    mu = jnp.mean(v, axis=-1, keepdims=True)
    var = jnp.mean(jnp.square(v - mu), axis=-1, keepdims=True)
    vn = ((v - mu) * lax.rsqrt(var + LN_EPS) * lnw_ref[...] + lnb_ref[...]).astype(BF16)
    ti = lax.broadcasted_iota(jnp.int32, (GMLP_CHUNK, GMLP_CHUNK), 0)
    si = lax.broadcasted_iota(jnp.int32, (GMLP_CHUNK, GMLP_CHUNK), 1)
    ws = [jnp.where(ti >= si, sw_ref[gg], 0.0).astype(BF16) for gg in range(GMLP_GROUPS)]
    gd = GMLP_WIDTH // GMLP_GROUPS
    for c in range(GMLP_TM // GMLP_CHUNK):
        rows = slice(c * GMLP_CHUNK, (c + 1) * GMLP_CHUNK)
        mix = jnp.concatenate(
            [_dot(ws[gg], vn[rows, gg * gd:(gg + 1) * gd]) for gg in range(GMLP_GROUPS)], axis=1)
        o_ref[rows, :] = (u[rows, :] * (mix + sbx_ref[...])).astype(BF16)


def _gmlp(proj, lnw, lnb, sw, sbx):
    return pl.pallas_call(
        _gmlp_kernel,
        grid=(N_TOK // GMLP_TM,),
        in_specs=[
            pl.BlockSpec((GMLP_TM, Z_COLS), lambda i: (i, 0)),
            pl.BlockSpec((1, GMLP_WIDTH), lambda i: (0, 0)),
            pl.BlockSpec((1, GMLP_WIDTH), lambda i: (0, 0)),
            pl.BlockSpec((GMLP_GROUPS, GMLP_CHUNK, GMLP_CHUNK), lambda i: (0, 0, 0)),
            pl.BlockSpec((GMLP_CHUNK, GMLP_WIDTH), lambda i: (0, 0)),
        ],
        out_specs=pl.BlockSpec((GMLP_TM, GMLP_WIDTH), lambda i: (i, 0)),
        out_shape=jax.ShapeDtypeStruct((N_TOK, GMLP_WIDTH), BF16),
        compiler_params=_cparams("arbitrary"),
        name="gmlp",
    )(proj, lnw, lnb, sw, sbx)


def _oproj_kernel(a_ref, b_ref, x_ref, wa_ref, wb_ref, nw_ref, x1_ref, xn_ref):
    y = x_ref[...] + (_dot(a_ref[...], wa_ref[...]) + _dot(b_ref[...], wb_ref[...]))
    x1_ref[...] = y
    ms = jnp.mean(y * y, axis=-1, keepdims=True)
    xn_ref[...] = (y * lax.rsqrt(ms + NORM_EPS) * nw_ref[...]).astype(BF16)


def _oproj(a, b, x2, wa, wb, nw):
    return pl.pallas_call(
        _oproj_kernel,
        grid=(N_TOK // OPROJ_TM,),
        in_specs=[
            pl.BlockSpec((OPROJ_TM, NSA_WIDTH), lambda i: (i, 0)),
            pl.BlockSpec((OPROJ_TM, GMLP_WIDTH), lambda i: (i, 0)),
            pl.BlockSpec((OPROJ_TM, D_MODEL), lambda i: (i, 0)),
            pl.BlockSpec((NSA_WIDTH, D_MODEL), lambda i: (0, 0)),
            pl.BlockSpec((GMLP_WIDTH, D_MODEL), lambda i: (0, 0)),
            pl.BlockSpec((1, D_MODEL), lambda i: (0, 0)),
        ],
        out_specs=[
            pl.BlockSpec((OPROJ_TM, D_MODEL), lambda i: (i, 0)),
            pl.BlockSpec((OPROJ_TM, D_MODEL), lambda i: (i, 0)),
        ],
        out_shape=[
            jax.ShapeDtypeStruct((N_TOK, D_MODEL), F32),
            jax.ShapeDtypeStruct((N_TOK, D_MODEL), BF16),
        ],
        compiler_params=_cparams("arbitrary"),
        name="oproj",
    )(a, b, x2, wa, wb, nw)


def _causal_conv(h, prev, cw, cb):
    rid = lax.broadcasted_iota(jnp.int32, prev.shape, 0)

    def shifted(k):
        body = pltpu.roll(h, k, 0)
        top = jnp.where(rid < k, pltpu.roll(prev, k, 0), body[0:8, :])
        return jnp.concatenate([top, body[8:, :]], axis=0)

    return cb + ((cw[0:1, :] * shifted(2) + cw[1:2, :] * shifted(1)) + cw[2:3, :] * h)


def _ffn_kernel(xn_ref, x1_ref, wg_ref, wu_ref, cwg_ref, cwu_ref, cbg_ref, cbu_ref, wd_ref,
                cwg_p_ref, cwu_p_ref, cbg_p_ref, cbu_p_ref, wd_p_ref, o_ref, h0_ref, h1_ref, carry_ref):
    i = pl.program_id(0)
    q = pl.program_id(1)
    nq = pl.num_programs(1)
    half = FFN_TF // 2

    @pl.when((i == 0) & (q == 0))
    def _():
        carry_ref[...] = jnp.zeros_like(carry_ref)
        h1_ref[...] = jnp.zeros_like(h1_ref)

    @pl.when(q == 0)
    def _():
        o_ref[...] = x1_ref[...]

    seq_start = i % (SEQ // FFN_TM) == 0
    xn = xn_ref[...]

    def produce(c, h_ref):
        h_ref[0] = _dot(xn, wg_ref[:, c * half:(c + 1) * half])
        h_ref[1] = _dot(xn, wu_ref[:, c * half:(c + 1) * half])

    def consume(h_ref, tile, c, cwg, cwu, cbg, cbu, wd, scale=None):
        cols = slice(c * half, (c + 1) * half)
        hg = h_ref[0]
        hu = h_ref[1]
        prev_g = jnp.where(seq_start, 0.0, carry_ref[tile, 0])
        prev_u = jnp.where(seq_start, 0.0, carry_ref[tile, 1])
        carry_ref[tile, 0] = hg[FFN_TM - 8:, :]
        carry_ref[tile, 1] = hu[FFN_TM - 8:, :]
        cg = _causal_conv(hg, prev_g, cwg[:, cols], cbg[:, cols])
        cu = _causal_conv(hu, prev_u, cwu[:, cols], cbu[:, cols])
        act = jax.nn.silu(cg) * cu
        if scale is not None:
            act = act * scale
        o_ref[...] += _dot(act.astype(BF16), wd[c * half:(c + 1) * half, :])

    produce(0, h0_ref)
    consume(h1_ref, jnp.where(q == 0, 2 * nq - 1, 2 * q - 1), 1, cwg_p_ref, cwu_p_ref, cbg_p_ref, cbu_p_ref,
            wd_p_ref, scale=jnp.where(q == 0, 0.0, 1.0))
    produce(1, h1_ref)
    consume(h0_ref, 2 * q, 0, cwg_ref, cwu_ref, cbg_ref, cbu_ref, wd_ref)

    @pl.when(q == nq - 1)
    def _():
        consume(h1_ref, 2 * q + 1, 1, cwg_ref, cwu_ref, cbg_ref, cbu_ref, wd_ref)


def _ffn(xn, x1, w_up, conv_w, conv_b, w_down):
    nj = D_FF // FFN_TF
    prev = lambda q: jnp.maximum(q - 1, 0)
    return pl.pallas_call(
        _ffn_kernel,
        grid=(N_TOK // FFN_TM, nj),
        in_specs=[
            pl.BlockSpec((FFN_TM, D_MODEL), lambda i, q: (i, 0)),
            pl.BlockSpec((FFN_TM, D_MODEL), lambda i, q: (i, 0)),
            pl.BlockSpec((D_MODEL, FFN_TF), lambda i, q: (0, q)),
            pl.BlockSpec((D_MODEL, FFN_TF), lambda i, q: (0, nj + q)),
            pl.BlockSpec((CONV_WIDTH, FFN_TF), lambda i, q: (0, q)),
            pl.BlockSpec((CONV_WIDTH, FFN_TF), lambda i, q: (0, nj + q)),
            pl.BlockSpec((1, FFN_TF), lambda i, q: (0, q)),
            pl.BlockSpec((1, FFN_TF), lambda i, q: (0, nj + q)),
            pl.BlockSpec((FFN_TF, D_MODEL), lambda i, q: (q, 0)),
            pl.BlockSpec((CONV_WIDTH, FFN_TF), lambda i, q: (0, prev(q))),
            pl.BlockSpec((CONV_WIDTH, FFN_TF), lambda i, q: (0, nj + prev(q))),
            pl.BlockSpec((1, FFN_TF), lambda i, q: (0, prev(q))),
            pl.BlockSpec((1, FFN_TF), lambda i, q: (0, nj + prev(q))),
            pl.BlockSpec((FFN_TF, D_MODEL), lambda i, q: (prev(q), 0)),
        ],
        out_specs=pl.BlockSpec((FFN_TM, D_MODEL), lambda i, q: (i, 0)),
        out_shape=jax.ShapeDtypeStruct((N_TOK, D_MODEL), F32),
        scratch_shapes=[pltpu.VMEM((2, FFN_TM, FFN_TF // 2), F32), pltpu.VMEM((2, FFN_TM, FFN_TF // 2), F32),
                        pltpu.VMEM((2 * nj, 2, 8, FFN_TF // 2), F32)],
        compiler_params=_cparams("arbitrary", "arbitrary"),
        name="ffn",
    )(xn, x1, w_up, w_up, conv_w, conv_w, conv_b, conv_b, w_down, conv_w, conv_w, conv_b, conv_b, w_down)


def _overlap_t():
    start = np.arange(N_CMP_PAD)[None, :] * CMP_STRIDE
    s0 = np.arange(N_SEL)[:, None] * SEL_BLOCK
    ov = (start < s0 + SEL_BLOCK) & (start + CMP_BLOCK > s0) & (np.arange(N_CMP_PAD)[None, :] < N_CMP_PAD - 1)
    return ov.astype(np.float32)


def _bf16_round(a):
    return np.asarray(a, np.float32).astype(BF16).astype(np.float32)


def _slope_features():
    sl = (np.power(2.0, -8.0 * np.arange(1, NSA_HEADS + 1) / NSA_HEADS).astype(np.float32)
          * np.float32(LOG2E)).astype(np.float32)
    s1 = _bf16_round(sl)
    s2 = _bf16_round(sl - s1)
    s3 = _bf16_round(sl - s1 - s2)
    rows = np.stack([s1, s2, s3, s1, s2, s3, -sl, np.zeros_like(sl)], axis=0)
    rows = rows.reshape(8, KV_GROUPS, REP).transpose(1, 0, 2)
    return np.repeat(rows, Q_TILE, axis=2).astype(np.float32)


def _window_mask():
    kl = np.arange(WIN_KEYS)[:, None]
    ql = np.arange(Q_TILE)[None, :]
    dist = ql + WINDOW - kl
    return np.where((dist >= 0) & (dist < WINDOW), 0.0, NEG_BIG).astype(np.float32)


def _diag_masks():
    kl = np.arange(SEL_TILE_KEYS)[None, :, None]
    ql = np.arange(Q_TILE)[None, None, :]
    off = np.arange(SEL_TILE_KEYS // Q_TILE)[:, None, None] * Q_TILE
    return np.where(kl > ql + off, NEG_BIG, 0.0).astype(np.float32)


def _layout_w_in(w_in):
    q_end = NSA_WIDTH
    kv_end = q_end + 6 * KV_COLS
    g_end = kv_end + NSA_HEADS * N_BRANCH
    w_q = w_in[:, :q_end]
    w_kv = w_in[:, q_end:kv_end].reshape(D_MODEL, 6, KV_GROUPS, HEAD_DIM)
    w_kv = jnp.stack([w_kv[:, s] for s in (2, 4, 0, 1, 3, 5)], axis=2).reshape(D_MODEL, 6 * KV_COLS)
    w_z = w_in[:, g_end:]
    w_main = jnp.concatenate([w_z, w_q, w_kv], axis=1).astype(BF16)
    w_g = w_in[:, kv_end:g_end].reshape(D_MODEL, KV_GROUPS, REP, N_BRANCH).transpose(0, 1, 3, 2)
    w_g = w_g.reshape(D_MODEL, KV_GROUPS, N_BRANCH * REP)
    w_g = jnp.pad(w_g, ((0, 0), (0, 0), (0, GATE_ROWS_PER_GROUP - N_BRANCH * REP)))
    w_g = jnp.pad(w_g.reshape(D_MODEL, KV_GROUPS * GATE_ROWS_PER_GROUP),
                  ((0, 0), (0, GATE_PAD - KV_GROUPS * GATE_ROWS_PER_GROUP))).astype(BF16)
    return w_main, w_g


def _layout_compress(cmp_pos, cmp_w1, cmp_w2):
    pos2 = jnp.concatenate([cmp_pos[0], cmp_pos[1]], axis=1)
    w1 = cmp_w1.reshape(2, CMP_BLOCK, HEAD_DIM, CMP_HIDDEN)
    zero = jnp.zeros_like(w1[0])
    w1k = jnp.concatenate([w1[0], zero], axis=1)
    w1v = jnp.concatenate([zero, w1[1]], axis=1)
    w1p = jnp.concatenate([w1k, w1v], axis=2).astype(BF16)
    w2p = jnp.pad(cmp_w2, ((0, 0), (0, 0), (0, LANES - HEAD_DIM))).astype(BF16)
    return pos2, w1p, w2p


def kernel(x, attn_norm_w, w_in, q_norm_w, k_norm_w, cmp_pos, cmp_w1, cmp_w2, gmlp_ln_w, gmlp_ln_b,
           spatial_w, spatial_b, w_out, ffn_norm_w, w_up, conv_w, conv_b, w_down):
    x2 = x.reshape(N_TOK, D_MODEL)
    w_main, w_gate = _layout_w_in(w_in)
    proj, gl = _proj(x2, attn_norm_w.reshape(1, D_MODEL), w_main, w_gate)

    knw2 = jnp.concatenate([k_norm_w[1], k_norm_w[2]]).reshape(1, LANES)
    kcw = jnp.concatenate([k_norm_w[0], jnp.zeros((LANES - HEAD_DIM,), F32)]).reshape(1, LANES)
    pos2, w1p, w2p = _layout_compress(cmp_pos, cmp_w1, cmp_w2)
    ksel, kwin, vt, kc, vct, gt = _kprep(proj, gl, knw2, kcw, pos2, w1p, w2p)

    wq_b = jnp.broadcast_to(q_norm_w.reshape(HEAD_DIM, 1), (HEAD_DIM, Q_TILE))
    a = _attn(proj, ksel, kwin, vt, kc, vct, gt, jnp.asarray(_slope_features()),
              jnp.asarray(_window_mask()), jnp.asarray(_diag_masks()),
              jnp.asarray(_overlap_t(), dtype=BF16), wq_b)

    sbx = jnp.repeat(spatial_b.T, GMLP_WIDTH // GMLP_GROUPS, axis=1)
    b = _gmlp(proj, gmlp_ln_w.reshape(1, GMLP_WIDTH), gmlp_ln_b.reshape(1, GMLP_WIDTH), spatial_w, sbx)

    w_out_b = w_out.astype(BF16)
    x1, xn = _oproj(a, b, x2, w_out_b[:NSA_WIDTH], w_out_b[NSA_WIDTH:], ffn_norm_w.reshape(1, D_MODEL))

    out = _ffn(xn, x1, w_up.astype(BF16), conv_w, conv_b.reshape(1, 2 * D_FF), w_down.astype(BF16))
    return out.reshape(BATCH, SEQ, D_MODEL)
```

```python
import numpy as np
import jax
import jax.numpy as jnp
from jax import lax
from jax.experimental import pallas as pl
from jax.experimental.pallas import tpu as pltpu

F32 = jnp.float32
BF16 = jnp.bfloat16

D_MODEL = 2048
BATCH = 4
SEQ = 4096
N_TOK = BATCH * SEQ
NSA_HEADS = 16
KV_GROUPS = 4
REP = NSA_HEADS // KV_GROUPS
HEAD_DIM = 64
NSA_WIDTH = NSA_HEADS * HEAD_DIM
GMLP_WIDTH = D_MODEL - NSA_WIDTH
GMLP_GROUPS = 8
GMLP_CHUNK = 128
CMP_BLOCK = 32
CMP_STRIDE = 16
CMP_HIDDEN = 256
N_CMP_PAD = SEQ // CMP_STRIDE
SEL_BLOCK = 64
N_SEL = SEQ // SEL_BLOCK
SEL_TOPK = 16
WINDOW = 512
K_BLOCK = 128
N_KB = SEQ // K_BLOCK
Q_TILE = 256
N_QT = SEQ // Q_TILE
N_BRANCH = 3
KV_COLS = KV_GROUPS * HEAD_DIM
D_FF = 5632
CONV_WIDTH = 3
NORM_EPS = 1e-6
LN_EPS = 1e-5
NEG_BIG = -1e30
SEL_BIG = 1e9

LANES = 128
VMEM_LIMIT_BYTES = 56 * 1024 * 1024

Z_COLS = 2 * GMLP_WIDTH
PROJ_COLS = Z_COLS + NSA_WIDTH + 6 * KV_COLS
GATE_PAD = LANES
GATE_ROWS_PER_GROUP = 16

PROJ_TM = 1024
PROJ_TN = 1536
GMLP_TM = 512
OPROJ_TM = 512
FFN_TM = 512
FFN_TF = 512
SEL_TILE_BLOCKS = 4
SEL_TILE_KEYS = SEL_TILE_BLOCKS * K_BLOCK
MASK_LANE0 = 80
SUM_ROWS = 16
VT_ROWS = HEAD_DIM + SUM_ROWS
LOG2E = 1.4426950408889634
WIN_BLOCKS = (WINDOW + Q_TILE) // K_BLOCK
WIN_KEYS = WIN_BLOCKS * K_BLOCK


def _cparams(*sem):
    return pltpu.CompilerParams(dimension_semantics=sem, vmem_limit_bytes=VMEM_LIMIT_BYTES)


def _dot(a, b):
    return jnp.dot(a, b, preferred_element_type=F32)


def _proj_kernel(x_ref, nw_ref, w_ref, wg_ref, o_ref, og_ref, h_ref):
    @pl.when(pl.program_id(1) == 0)
    def _():
        x = x_ref[...]
        ms = jnp.mean(x * x, axis=-1, keepdims=True)
        h = (x * lax.rsqrt(ms + NORM_EPS) * nw_ref[...]).astype(BF16)
        h_ref[...] = h
        og_ref[...] = _dot(h, wg_ref[...])

    o_ref[...] = _dot(h_ref[...], w_ref[...])


def _proj(x2, nw, w_main, w_gate):
    return pl.pallas_call(
        _proj_kernel,
        grid=(N_TOK // PROJ_TM, PROJ_COLS // PROJ_TN),
        in_specs=[
            pl.BlockSpec((PROJ_TM, D_MODEL), lambda i, j: (i, 0)),
            pl.BlockSpec((1, D_MODEL), lambda i, j: (0, 0)),
            pl.BlockSpec((D_MODEL, PROJ_TN), lambda i, j: (0, j)),
            pl.BlockSpec((D_MODEL, GATE_PAD), lambda i, j: (0, 0)),
        ],
        out_specs=[
            pl.BlockSpec((PROJ_TM, PROJ_TN), lambda i, j: (i, j)),
            pl.BlockSpec((PROJ_TM, GATE_PAD), lambda i, j: (i, 0)),
        ],
        out_shape=[
            jax.ShapeDtypeStruct((N_TOK, PROJ_COLS), F32),
            jax.ShapeDtypeStruct((N_TOK, GATE_PAD), F32),
        ],
        scratch_shapes=[pltpu.VMEM((PROJ_TM, D_MODEL), BF16)],
        compiler_params=_cparams("arbitrary", "arbitrary"),
        name="proj",
    )(x2, nw, w_main, w_gate)


def _pos_features(lane, first, f1, f2):
    return jnp.where(lane < first + 3, f1, jnp.where(lane < first + 6, f2,
                                                     jnp.where(lane == first + 6, 1.0, 0.0)))


def _kprep_kernel(s0_ref, s1_ref, s2_ref, gl_ref, knw_ref, kcw_ref, pos_ref, w1_ref, w2_ref,
                  ksel_ref, kwin_ref, vt_ref, kc_ref, vct_ref, gt_ref, ha_ref, hb_ref):
    lane = lax.broadcasted_iota(jnp.int32, (K_BLOCK, LANES), 1)
    row = lax.broadcasted_iota(jnp.int32, (K_BLOCK, LANES), 0)
    lo = lane < HEAD_DIM
    inv_d = 1.0 / HEAD_DIM
    ones_rows = jnp.where(lax.broadcasted_iota(jnp.int32, (SUM_ROWS, K_BLOCK), 0) == 0, 1.0, 0.0)

    def chunk(c, carry):
        r0 = pl.multiple_of(c * K_BLOCK, K_BLOCK)
        x = s0_ref[pl.ds(r0, K_BLOCK), :]
        sq = x * x
        s_lo = jnp.sum(jnp.where(lo, sq, 0.0), axis=1, keepdims=True)
        s_hi = jnp.sum(jnp.where(lo, 0.0, sq), axis=1, keepdims=True)
        ms = jnp.where(lo, s_lo, s_hi) * inv_d
        y = x * lax.rsqrt(ms + NORM_EPS) * knw_ref[...]
        pos = r0 + row
        blk = lax.shift_right_logical(pos, 6)
        f1 = (blk * SEL_BLOCK).astype(F32)
        f2 = (pos & (SEL_BLOCK - 1)).astype(F32)
        onehot = jnp.where(lane - MASK_LANE0 == (blk & (SEL_TILE_BLOCKS * 2 - 1)), 1.0, 0.0)
        feat_sel = jnp.where(lane < MASK_LANE0, _pos_features(lane, HEAD_DIM, f1, f2), onehot)
        ksel_ref[c] = jnp.where(lo, y, feat_sel).astype(BF16)
        kwin_ref[c] = jnp.where(lo, _pos_features(lane, 0, f1, f2), y).astype(BF16)
        v_t = s2_ref[pl.ds(r0, K_BLOCK), :].T
        vt_ref[c] = jnp.concatenate(
            [v_t[0:HEAD_DIM], ones_rows, v_t[HEAD_DIM:], ones_rows], axis=0).astype(BF16)
        return carry

    lax.fori_loop(0, N_KB, chunk, 0, unroll=2)

    @pl.when(pl.program_id(1) == 0)
    def _():
        def gchunk(c, carry):
            r0 = pl.multiple_of(c * K_BLOCK, K_BLOCK)
            gt_ref[c] = jax.nn.sigmoid(gl_ref[pl.ds(r0, K_BLOCK), :]).T
            return carry

        lax.fori_loop(0, N_KB, gchunk, 0, unroll=2)

    ha_ref[...] = jnp.zeros_like(ha_ref)
    hb_ref[...] = jnp.zeros_like(hb_ref)
    for tt in range(CMP_STRIDE):
        rows = s1_ref[pl.ds(tt, N_CMP_PAD, stride=CMP_STRIDE), :]
        xa = (rows + pos_ref[tt:tt + 1, :]).astype(BF16)
        xb = (rows + pos_ref[CMP_STRIDE + tt:CMP_STRIDE + tt + 1, :]).astype(BF16)
        ha_ref[...] += _dot(xa, w1_ref[tt])
        hb_ref[0:N_CMP_PAD, :] += _dot(xb, w1_ref[CMP_STRIDE + tt])
    hid = jax.nn.gelu(ha_ref[...] + hb_ref[1:N_CMP_PAD + 1, :]).astype(BF16)
    kc = _dot(hid[:, :CMP_HIDDEN], w2_ref[0])
    vc = _dot(hid[:, CMP_HIDDEN:], w2_ref[1])
    ms = jnp.sum(kc * kc, axis=1, keepdims=True) * inv_d
    kcn = kc * lax.rsqrt(ms + NORM_EPS) * kcw_ref[...]
    ci = lax.broadcasted_iota(jnp.int32, (N_CMP_PAD, LANES), 0)
    cl = lax.broadcasted_iota(jnp.int32, (N_CMP_PAD, LANES), 1)
    c1 = (lax.shift_right_logical(ci, 2) * SEL_BLOCK).astype(F32)
    c2 = ((ci & 3) * CMP_STRIDE).astype(F32) + (CMP_BLOCK - 1) * 0.5
    kc_ref[...] = jnp.where(cl < HEAD_DIM, kcn, _pos_features(cl, HEAD_DIM, c1, c2)).astype(BF16)
    vct_ref[...] = vc.T[0:HEAD_DIM, :].astype(BF16)


def _kprep(proj, gl, knw2, kcw, pos2, w1p, w2p):
    kv0 = (Z_COLS + NSA_WIDTH) // LANES
    slab = lambda s: pl.BlockSpec((SEQ, LANES), lambda b, g, s=s: (b, kv0 + 3 * g + s))
    full = lambda shape: pl.BlockSpec(shape, lambda b, g: (0,) * len(shape))
    per_bg = lambda shape: pl.BlockSpec((None, None) + shape, lambda b, g: (b, g) + (0,) * len(shape))
    bg_shape = lambda shape, dt: jax.ShapeDtypeStruct((BATCH, KV_GROUPS) + shape, dt)
    return pl.pallas_call(
        _kprep_kernel,
        grid=(BATCH, KV_GROUPS),
        in_specs=[
            slab(0), slab(1), slab(2),
            pl.BlockSpec((SEQ, GATE_PAD), lambda b, g: (b, 0)),
            full((1, LANES)), full((1, LANES)), full((CMP_BLOCK, LANES)),
            full((CMP_BLOCK, LANES, 2 * CMP_HIDDEN)), full((2, CMP_HIDDEN, LANES)),
        ],
        out_specs=[
            per_bg((N_KB, K_BLOCK, LANES)),
            per_bg((N_KB, K_BLOCK, LANES)),
            per_bg((N_KB, 2 * VT_ROWS, K_BLOCK)),
            per_bg((N_CMP_PAD, LANES)),
            per_bg((HEAD_DIM, N_CMP_PAD)),
            pl.BlockSpec((None, N_KB, GATE_PAD, K_BLOCK), lambda b, g: (b, 0, 0, 0)),
        ],
        out_shape=[
            bg_shape((N_KB, K_BLOCK, LANES), BF16),
            bg_shape((N_KB, K_BLOCK, LANES), BF16),
            bg_shape((N_KB, 2 * VT_ROWS, K_BLOCK), BF16),
            bg_shape((N_CMP_PAD, LANES), BF16),
            bg_shape((HEAD_DIM, N_CMP_PAD), BF16),
            jax.ShapeDtypeStruct((BATCH, N_KB, GATE_PAD, K_BLOCK), F32),
        ],
        scratch_shapes=[
            pltpu.VMEM((N_CMP_PAD, 2 * CMP_HIDDEN), F32),
            pltpu.VMEM((N_CMP_PAD + 8, 2 * CMP_HIDDEN), F32),
        ],
        compiler_params=_cparams("arbitrary", "arbitrary"),
        name="kprep",
    )(proj, proj, proj, gl, knw2, kcw, pos2, w1p, w2p)


def _attn_kernel(q_ref, ksel_ref, kwin_ref, vt_ref, kc_ref, vct_ref, gt_ref, sf_ref, wm_ref, dm_ref,
                 ov_ref, wq_ref, o_ref, sel_ref, qs_ref, sa_ref, sb_ref):
    qt = pl.program_id(2)
    t_i = qt * Q_TILE + lax.broadcasted_iota(jnp.int32, (1, Q_TILE), 1)
    head = lambda a, r: a[:, r * Q_TILE:(r + 1) * Q_TILE]
    heads = lambda parts: jnp.concatenate(parts, axis=1)

    q_t = q_ref[...].T
    parts = []
    for r in range(REP):
        xq = q_t[r * HEAD_DIM:(r + 1) * HEAD_DIM, :]
        ms = jnp.sum(xq * xq, axis=0, keepdims=True) * (1.0 / HEAD_DIM)
        parts.append(xq * lax.rsqrt(ms + NORM_EPS) * wq_ref[...] * (HEAD_DIM ** -0.5) * LOG2E)
    qn = heads(parts)
    sf = sf_ref[...]
    sf = jnp.where(lax.broadcasted_iota(jnp.int32, sf.shape, 0) == 6, sf * (qt * Q_TILE).astype(F32), sf)
    wide = REP * Q_TILE
    pad = jnp.zeros((HEAD_DIM - 8, wide), F32)
    qs_ref[...] = jnp.concatenate([qn, sf, pad], axis=0).astype(BF16)
    q_win = jnp.concatenate([sf, pad, qn], axis=0).astype(BF16)

    sc = _dot(kc_ref[...], qs_ref[...])
    ci = lax.broadcasted_iota(jnp.int32, (N_CMP_PAD, Q_TILE), 0)
    cm = jnp.where(ci * CMP_STRIDE + (CMP_BLOCK - 1) <= t_i, 0.0, NEG_BIG)
    any_valid = t_i >= CMP_BLOCK - 1
    pcs = []
    for r in range(REP):
        s = head(sc, r) + cm
        e = jnp.exp2(s - jnp.max(s, axis=0, keepdims=True))
        d = jnp.sum(e, axis=0, keepdims=True)
        pcs.append((e * jnp.where(any_valid, 1.0 / d, 0.0)).astype(BF16))
    oc_imp = _dot(jnp.concatenate([vct_ref[...], ov_ref[...]], axis=0), heads(pcs))
    o_c = oc_imp[0:HEAD_DIM]
    imp4 = oc_imp[HEAD_DIM:]
    imp = (head(imp4, 0) + head(imp4, 1)) + (head(imp4, 2) + head(imp4, 3))

    jj = lax.broadcasted_iota(jnp.int32, (N_SEL, Q_TILE), 0)
    cur = lax.shift_right_logical(t_i, 6)
    valid = jj <= cur
    sel_ref[...] = jnp.where(valid, 1.0, 0.0)

    @pl.when(qt * Q_TILE + Q_TILE > SEL_TOPK * SEL_BLOCK)
    def _():
        forced = (jj == 0) | (jj == cur) | (jj == cur - 1)
        score = jnp.where(valid, jnp.where(forced, SEL_BIG, imp), -SEL_BIG)
        rows8 = [score[8 * v:8 * v + 8, :] for v in range(N_SEL // 8)]
        j8 = lax.broadcasted_iota(jnp.int32, (8, Q_TILE), 0)
        ranks = [jnp.zeros((8, Q_TILE), F32) for _ in rows8]
        for jp in range(N_SEL):
            row = score[jp:jp + 1, :]
            for v, blk in enumerate(rows8):
                if 8 * v > jp:
                    ahead = jnp.where(row >= blk, 1.0, 0.0)
                elif 8 * v + 7 <= jp:
                    ahead = jnp.where(row > blk, 1.0, 0.0)
                else:
                    ahead = jnp.where(j8 + 8 * v > jp, jnp.where(row >= blk, 1.0, 0.0),
                                      jnp.where(row > blk, 1.0, 0.0))
                ranks[v] = ranks[v] + ahead
        rank = jnp.concatenate(ranks, axis=0)
        sel_ref[...] = jnp.where(valid & (rank < SEL_TOPK), 1.0, 0.0)

    sub = SEL_TILE_KEYS // SEL_BLOCK

    def scores_into(kt, s_ref):
        member = sel_ref[pl.ds(kt * sub, sub), :]
        mrows = (member - 1.0) * (-NEG_BIG)
        qs_ref[MASK_LANE0:MASK_LANE0 + 16, :] = jnp.concatenate(
            [heads([mrows] * REP), jnp.zeros((16 - sub, wide), F32)], axis=0).astype(BF16)
        k_tile = ksel_ref[pl.ds(kt * SEL_TILE_BLOCKS, SEL_TILE_BLOCKS)].reshape(SEL_TILE_KEYS, LANES)
        s_ref[...] = _dot(k_tile, qs_ref[...])

    def absorb(kt, s_ref, carry, diagonal):
        m_run, acc = carry
        kb0 = kt * SEL_TILE_BLOCKS
        if diagonal:
            causal = dm_ref[qt % (SEL_TILE_KEYS // Q_TILE)]
        v_t = jnp.concatenate([vt_ref[kb0 + x, 0:VT_ROWS, :] for x in range(SEL_TILE_BLOCKS)], axis=1)
        ms_, accs = [], []
        for r in range(REP):
            sr = s_ref[:, r * Q_TILE:(r + 1) * Q_TILE]
            if diagonal:
                sr = sr + causal
            m_old = head(m_run, r)
            m_new = jnp.maximum(m_old, jnp.max(sr, axis=0, keepdims=True))
            p = jnp.exp2(sr - m_new).astype(BF16)
            accs.append(head(acc, r) * jnp.exp2(m_old - m_new) + _dot(v_t, p))
            ms_.append(m_new)
        return heads(ms_), heads(accs)

    carry = (jnp.full((1, wide), NEG_BIG, F32), jnp.zeros((VT_ROWS, wide), F32))
    n_full = (qt * Q_TILE) // SEL_TILE_KEYS

    kb_first = qt * (Q_TILE // K_BLOCK) - WINDOW // K_BLOCK
    kbs = [jnp.maximum(kb_first + x, 0) for x in range(WIN_BLOCKS)]
    sw = _dot(jnp.concatenate([kwin_ref[kb] for kb in kbs], axis=0), q_win)
    scores_into(0, sa_ref)
    wm = jnp.concatenate(
        [wm_ref[x * K_BLOCK:(x + 1) * K_BLOCK, :] + jnp.where(kb_first + x < 0, NEG_BIG, 0.0)
         for x in range(WIN_BLOCKS)], axis=0)
    pws = []
    for r in range(REP):
        s = head(sw, r) + wm
        pws.append(jnp.exp2(s - jnp.max(s, axis=0, keepdims=True)).astype(BF16))
    vw_t = jnp.concatenate([vt_ref[kb, VT_ROWS:2 * VT_ROWS, :] for kb in kbs], axis=1)
    acc_w = _dot(vw_t, heads(pws))
    o_w = acc_w[0:HEAD_DIM] / acc_w[HEAD_DIM:HEAD_DIM + 1]

    def tile_pair(i, carry):
        scores_into(2 * i + 1, sb_ref)
        carry = absorb(2 * i, sa_ref, carry, False)
        scores_into(2 * i + 2, sa_ref)
        return absorb(2 * i + 1, sb_ref, carry, False)

    carry = lax.fori_loop(0, n_full // 2, tile_pair, carry)
    last = 2 * (n_full // 2)

    def odd_tail(carry):
        scores_into(last + 1, sb_ref)
        return absorb(last + 1, sb_ref, absorb(last, sa_ref, carry, False), True)

    def even_tail(carry):
        return absorb(last, sa_ref, carry, True)

    _, acc_s = lax.cond(n_full % 2 == 1, odd_tail, even_tail, carry)
    o_s = acc_s[0:HEAD_DIM] / acc_s[HEAD_DIM:HEAD_DIM + 1]

    gates = jnp.concatenate([gt_ref[x] for x in range(Q_TILE // K_BLOCK)], axis=1)
    outs = []
    for r in range(REP):
        outs.append(gates[r:r + 1, :] * head(o_c, r)
                    + gates[REP + r:REP + r + 1, :] * head(o_s, r)
                    + gates[2 * REP + r:2 * REP + r + 1, :] * head(o_w, r))
    o_ref[...] = jnp.concatenate(outs, axis=0).T.astype(BF16)


def _attn(proj, ksel, kwin, vt, kc, vct, gt, sf, wm, dm, ov_t, wq_b):
    q0 = Z_COLS // (REP * HEAD_DIM)
    kvspec = lambda shape: pl.BlockSpec((None, None) + shape, lambda b, g, qt: (b, g) + (0,) * len(shape))
    const = lambda shape: pl.BlockSpec(shape, lambda b, g, qt: (0,) * len(shape))
    return pl.pallas_call(
        _attn_kernel,
        grid=(BATCH, KV_GROUPS, N_QT),
        in_specs=[
            pl.BlockSpec((Q_TILE, REP * HEAD_DIM), lambda b, g, qt: (b * N_QT + qt, q0 + g)),
            kvspec((N_KB, K_BLOCK, LANES)),
            kvspec((N_KB, K_BLOCK, LANES)),
            kvspec((N_KB, 2 * VT_ROWS, K_BLOCK)),
            kvspec((N_CMP_PAD, LANES)),
            kvspec((HEAD_DIM, N_CMP_PAD)),
            pl.BlockSpec((None, Q_TILE // K_BLOCK, GATE_ROWS_PER_GROUP, K_BLOCK), lambda b, g, qt: (b, qt, g, 0)),
            pl.BlockSpec((None, 8, REP * Q_TILE), lambda b, g, qt: (g, 0, 0)),
            const((WIN_KEYS, Q_TILE)),
            const((SEL_TILE_KEYS // Q_TILE, SEL_TILE_KEYS, Q_TILE)),
            const((N_SEL, N_CMP_PAD)),
            const((HEAD_DIM, Q_TILE)),
        ],
        out_specs=pl.BlockSpec((Q_TILE, REP * HEAD_DIM), lambda b, g, qt: (b * N_QT + qt, g)),
        out_shape=jax.ShapeDtypeStruct((N_TOK, NSA_WIDTH), BF16),
        scratch_shapes=[pltpu.VMEM((N_SEL, Q_TILE), F32), pltpu.VMEM((LANES, REP * Q_TILE), BF16),
                        pltpu.VMEM((SEL_TILE_KEYS, REP * Q_TILE), F32),
                        pltpu.VMEM((SEL_TILE_KEYS, REP * Q_TILE), F32)],
        compiler_params=_cparams("arbitrary", "arbitrary", "arbitrary"),
        name="attn",
    )(proj, ksel, kwin, vt, kc, vct, gt, sf, wm, dm, ov_t, wq_b)


def _gmlp_kernel(z_ref, lnw_ref, lnb_ref, sw_ref, sbx_ref, o_ref):
    ge = jax.nn.gelu(z_ref[...])
    u = ge[:, :GMLP_WIDTH]
    v = ge[:, GMLP_WIDTH:]
    mu = jnp.mean(v, axis=-1, keepdims=True)
    var = jnp.mean(jnp.square(v - mu), axis=-1, keepdims=True)
    vn = ((v - mu) * lax.rsqrt(var + LN_EPS) * lnw_ref[...] + lnb_ref[...]).astype(BF16)
    ti = lax.broadcasted_iota(jnp.int32, (GMLP_CHUNK, GMLP_CHUNK), 0)
    si = lax.broadcasted_iota(jnp.int32, (GMLP_CHUNK, GMLP_CHUNK), 1)
    ws = [jnp.where(ti >= si, sw_ref[gg], 0.0).astype(BF16) for gg in range(GMLP_GROUPS)]
    gd = GMLP_WIDTH // GMLP_GROUPS
    for c in range(GMLP_TM // GMLP_CHUNK):
        rows = slice(c * GMLP_CHUNK, (c + 1) * GMLP_CHUNK)
        mix = jnp.concatenate(
            [_dot(ws[gg], vn[rows, gg * gd:(gg + 1) * gd]) for gg in range(GMLP_GROUPS)], axis=1)
        o_ref[rows, :] = (u[rows, :] * (mix + sbx_ref[...])).astype(BF16)


def _gmlp(proj, lnw, lnb, sw, sbx):
    return pl.pallas_call(
        _gmlp_kernel,
        grid=(N_TOK // GMLP_TM,),
        in_specs=[
            pl.BlockSpec((GMLP_TM, Z_COLS), lambda i: (i, 0)),
            pl.BlockSpec((1, GMLP_WIDTH), lambda i: (0, 0)),
            pl.BlockSpec((1, GMLP_WIDTH), lambda i: (0, 0)),
            pl.BlockSpec((GMLP_GROUPS, GMLP_CHUNK, GMLP_CHUNK), lambda i: (0, 0, 0)),
            pl.BlockSpec((GMLP_CHUNK, GMLP_WIDTH), lambda i: (0, 0)),
        ],
        out_specs=pl.BlockSpec((GMLP_TM, GMLP_WIDTH), lambda i: (i, 0)),
        out_shape=jax.ShapeDtypeStruct((N_TOK, GMLP_WIDTH), BF16),
        compiler_params=_cparams("arbitrary"),
        name="gmlp",
    )(proj, lnw, lnb, sw, sbx)


def _oproj_kernel(a_ref, b_ref, x_ref, wa_ref, wb_ref, nw_ref, x1_ref, xn_ref):
    y = x_ref[...] + (_dot(a_ref[...], wa_ref[...]) + _dot(b_ref[...], wb_ref[...]))
    x1_ref[...] = y
    ms = jnp.mean(y * y, axis=-1, keepdims=True)
    xn_ref[...] = (y * lax.rsqrt(ms + NORM_EPS) * nw_ref[...]).astype(BF16)


def _oproj(a, b, x2, w_out, nw):
    return pl.pallas_call(
        _oproj_kernel,
        grid=(N_TOK // OPROJ_TM,),
        in_specs=[
            pl.BlockSpec((OPROJ_TM, NSA_WIDTH), lambda i: (i, 0)),
            pl.BlockSpec((OPROJ_TM, GMLP_WIDTH), lambda i: (i, 0)),
            pl.BlockSpec((OPROJ_TM, D_MODEL), lambda i: (i, 0)),
            pl.BlockSpec((NSA_WIDTH, D_MODEL), lambda i: (0, 0)),
            pl.BlockSpec((GMLP_WIDTH, D_MODEL), lambda i: (1, 0)),
            pl.BlockSpec((1, D_MODEL), lambda i: (0, 0)),
        ],
        out_specs=[
            pl.BlockSpec((OPROJ_TM, D_MODEL), lambda i: (i, 0)),
            pl.BlockSpec((OPROJ_TM, D_MODEL), lambda i: (i, 0)),
        ],
        out_shape=[
            jax.ShapeDtypeStruct((N_TOK, D_MODEL), F32),
            jax.ShapeDtypeStruct((N_TOK, D_MODEL), BF16),
        ],
        compiler_params=_cparams("arbitrary"),
        name="oproj",
    )(a, b, x2, w_out, w_out, nw)


def _causal_conv(h, prev, cw, cb):
    rid = lax.broadcasted_iota(jnp.int32, prev.shape, 0)

    def shifted(k):
        body = pltpu.roll(h, k, 0)
        top = jnp.where(rid < k, pltpu.roll(prev, k, 0), body[0:8, :])
        return jnp.concatenate([top, body[8:, :]], axis=0)

    return cb + ((cw[0:1, :] * shifted(2) + cw[1:2, :] * shifted(1)) + cw[2:3, :] * h)


def _ffn_kernel(xn_ref, x1_ref, wg_ref, wu_ref, cwg_ref, cwu_ref, cbg_ref, cbu_ref, wd_ref,
                cwg_p_ref, cwu_p_ref, cbg_p_ref, cbu_p_ref, wd_p_ref, o_ref, h0_ref, h1_ref, carry_ref):
    i = pl.program_id(0)
    q = pl.program_id(1)
    nq = pl.num_programs(1)
    half = FFN_TF // 2

    @pl.when((i == 0) & (q == 0))
    def _():
        carry_ref[...] = jnp.zeros_like(carry_ref)
        h1_ref[...] = jnp.zeros_like(h1_ref)

    @pl.when(q == 0)
    def _():
        o_ref[...] = x1_ref[...]

    seq_start = i % (SEQ // FFN_TM) == 0
    xn = xn_ref[...]

    def produce(c, h_ref):
        h_ref[0] = _dot(xn, wg_ref[:, c * half:(c + 1) * half])
        h_ref[1] = _dot(xn, wu_ref[:, c * half:(c + 1) * half])

    def gate(h_ref, tile, c, cwg, cwu, cbg, cbu, scale=None):
        cols = slice(c * half, (c + 1) * half)
        hg = h_ref[0]
        hu = h_ref[1]
        prev_g = jnp.where(seq_start, 0.0, carry_ref[tile, 0])
        prev_u = jnp.where(seq_start, 0.0, carry_ref[tile, 1])
        carry_ref[tile, 0] = hg[FFN_TM - 8:, :]
        carry_ref[tile, 1] = hu[FFN_TM - 8:, :]
        cg = _causal_conv(hg, prev_g, cwg[:, cols], cbg[:, cols])
        cu = _causal_conv(hu, prev_u, cwu[:, cols], cbu[:, cols])
        act = jax.nn.silu(cg) * cu
        if scale is not None:
            act = act * scale
        return act.astype(BF16)

    act_prev = gate(h1_ref, jnp.where(q == 0, 2 * nq - 1, 2 * q - 1), 1, cwg_p_ref, cwu_p_ref, cbg_p_ref,
                    cbu_p_ref, scale=jnp.where(q == 0, 0.0, 1.0))
    produce(0, h0_ref)
    down_prev = _dot(act_prev, wd_p_ref[half:, :])
    act_cur = gate(h0_ref, 2 * q, 0, cwg_ref, cwu_ref, cbg_ref, cbu_ref)
    produce(1, h1_ref)
    o_ref[...] += down_prev + _dot(act_cur, wd_ref[0:half, :])

    @pl.when(q == nq - 1)
    def _():
        act_last = gate(h1_ref, 2 * q + 1, 1, cwg_ref, cwu_ref, cbg_ref, cbu_ref)
        o_ref[...] += _dot(act_last, wd_ref[half:, :])


def _ffn(xn, x1, w_up, conv_w, conv_b, w_down):
    nj = D_FF // FFN_TF
    prev = lambda q: jnp.maximum(q - 1, 0)
    return pl.pallas_call(
        _ffn_kernel,
        grid=(N_TOK // FFN_TM, nj),
        in_specs=[
            pl.BlockSpec((FFN_TM, D_MODEL), lambda i, q: (i, 0)),
            pl.BlockSpec((FFN_TM, D_MODEL), lambda i, q: (i, 0)),
            pl.BlockSpec((D_MODEL, FFN_TF), lambda i, q: (0, q)),
            pl.BlockSpec((D_MODEL, FFN_TF), lambda i, q: (0, nj + q)),
            pl.BlockSpec((CONV_WIDTH, FFN_TF), lambda i, q: (0, q)),
            pl.BlockSpec((CONV_WIDTH, FFN_TF), lambda i, q: (0, nj + q)),
            pl.BlockSpec((1, FFN_TF), lambda i, q: (0, q)),
            pl.BlockSpec((1, FFN_TF), lambda i, q: (0, nj + q)),
            pl.BlockSpec((FFN_TF, D_MODEL), lambda i, q: (q, 0)),
            pl.BlockSpec((CONV_WIDTH, FFN_TF), lambda i, q: (0, prev(q))),
            pl.BlockSpec((CONV_WIDTH, FFN_TF), lambda i, q: (0, nj + prev(q))),
            pl.BlockSpec((1, FFN_TF), lambda i, q: (0, prev(q))),
            pl.BlockSpec((1, FFN_TF), lambda i, q: (0, nj + prev(q))),
            pl.BlockSpec((FFN_TF, D_MODEL), lambda i, q: (prev(q), 0)),
        ],
        out_specs=pl.BlockSpec((FFN_TM, D_MODEL), lambda i, q: (i, 0)),
        out_shape=jax.ShapeDtypeStruct((N_TOK, D_MODEL), F32),
        scratch_shapes=[pltpu.VMEM((2, FFN_TM, FFN_TF // 2), F32), pltpu.VMEM((2, FFN_TM, FFN_TF // 2), F32),
                        pltpu.VMEM((2 * nj, 2, 8, FFN_TF // 2), F32)],
        compiler_params=_cparams("arbitrary", "arbitrary"),
        name="ffn",
    )(xn, x1, w_up, w_up, conv_w, conv_w, conv_b, conv_b, w_down, conv_w, conv_w, conv_b, conv_b, w_down)


def _overlap_t():
    start = np.arange(N_CMP_PAD)[None, :] * CMP_STRIDE
    s0 = np.arange(N_SEL)[:, None] * SEL_BLOCK
    ov = (start < s0 + SEL_BLOCK) & (start + CMP_BLOCK > s0) & (np.arange(N_CMP_PAD)[None, :] < N_CMP_PAD - 1)
    return ov.astype(np.float32)


def _bf16_round(a):
    return np.asarray(a, np.float32).astype(BF16).astype(np.float32)


def _slope_features():
    sl = (np.power(2.0, -8.0 * np.arange(1, NSA_HEADS + 1) / NSA_HEADS).astype(np.float32)
          * np.float32(LOG2E)).astype(np.float32)
    s1 = _bf16_round(sl)
    s2 = _bf16_round(sl - s1)
    s3 = _bf16_round(sl - s1 - s2)
    rows = np.stack([s1, s2, s3, s1, s2, s3, -sl, np.zeros_like(sl)], axis=0)
    rows = rows.reshape(8, KV_GROUPS, REP).transpose(1, 0, 2)
    return np.repeat(rows, Q_TILE, axis=2).astype(np.float32)


def _window_mask():
    kl = np.arange(WIN_KEYS)[:, None]
    ql = np.arange(Q_TILE)[None, :]
    dist = ql + WINDOW - kl
    return np.where((dist >= 0) & (dist < WINDOW), 0.0, NEG_BIG).astype(np.float32)


def _diag_masks():
    kl = np.arange(SEL_TILE_KEYS)[None, :, None]
    ql = np.arange(Q_TILE)[None, None, :]
    off = np.arange(SEL_TILE_KEYS // Q_TILE)[:, None, None] * Q_TILE
    return np.where(kl > ql + off, NEG_BIG, 0.0).astype(np.float32)


def _layout_w_in(w_in):
    q_end = NSA_WIDTH
    kv_end = q_end + 6 * KV_COLS
    g_end = kv_end + NSA_HEADS * N_BRANCH
    w_q = w_in[:, :q_end]
    w_kv = w_in[:, q_end:kv_end].reshape(D_MODEL, 6, KV_GROUPS, HEAD_DIM)
    w_kv = jnp.stack([w_kv[:, s] for s in (2, 4, 0, 1, 3, 5)], axis=2).reshape(D_MODEL, 6 * KV_COLS)
    w_z = w_in[:, g_end:]
    w_main = jnp.concatenate([w_z, w_q, w_kv], axis=1).astype(BF16)
    w_g = w_in[:, kv_end:g_end].reshape(D_MODEL, KV_GROUPS, REP, N_BRANCH).transpose(0, 1, 3, 2)
    w_g = w_g.reshape(D_MODEL, KV_GROUPS, N_BRANCH * REP)
    w_g = jnp.pad(w_g, ((0, 0), (0, 0), (0, GATE_ROWS_PER_GROUP - N_BRANCH * REP)))
    w_g = jnp.pad(w_g.reshape(D_MODEL, KV_GROUPS * GATE_ROWS_PER_GROUP),
                  ((0, 0), (0, GATE_PAD - KV_GROUPS * GATE_ROWS_PER_GROUP))).astype(BF16)
    return w_main, w_g


def _layout_compress(cmp_pos, cmp_w1, cmp_w2):
    pos2 = jnp.concatenate([cmp_pos[0], cmp_pos[1]], axis=1)
    w1 = cmp_w1.reshape(2, CMP_BLOCK, HEAD_DIM, CMP_HIDDEN)
    zero = jnp.zeros_like(w1[0])
    w1k = jnp.concatenate([w1[0], zero], axis=1)
    w1v = jnp.concatenate([zero, w1[1]], axis=1)
    w1p = jnp.concatenate([w1k, w1v], axis=2).astype(BF16)
    w2p = jnp.pad(cmp_w2, ((0, 0), (0, 0), (0, LANES - HEAD_DIM))).astype(BF16)
    return pos2, w1p, w2p


def kernel(x, attn_norm_w, w_in, q_norm_w, k_norm_w, cmp_pos, cmp_w1, cmp_w2, gmlp_ln_w, gmlp_ln_b,
           spatial_w, spatial_b, w_out, ffn_norm_w, w_up, conv_w, conv_b, w_down):
    x2 = x.reshape(N_TOK, D_MODEL)
    w_main, w_gate = _layout_w_in(w_in)
    proj, gl = _proj(x2, attn_norm_w.reshape(1, D_MODEL), w_main, w_gate)

    knw2 = jnp.concatenate([k_norm_w[1], k_norm_w[2]]).reshape(1, LANES)
    kcw = jnp.concatenate([k_norm_w[0], jnp.zeros((LANES - HEAD_DIM,), F32)]).reshape(1, LANES)
    pos2, w1p, w2p = _layout_compress(cmp_pos, cmp_w1, cmp_w2)
    ksel, kwin, vt, kc, vct, gt = _kprep(proj, gl, knw2, kcw, pos2, w1p, w2p)

    wq_b = jnp.broadcast_to(q_norm_w.reshape(HEAD_DIM, 1), (HEAD_DIM, Q_TILE))
    a = _attn(proj, ksel, kwin, vt, kc, vct, gt, jnp.asarray(_slope_features()),
              jnp.asarray(_window_mask()), jnp.asarray(_diag_masks()),
              jnp.asarray(_overlap_t(), dtype=BF16), wq_b)

    sbx = jnp.repeat(spatial_b.T, GMLP_WIDTH // GMLP_GROUPS, axis=1)
    b = _gmlp(proj, gmlp_ln_w.reshape(1, GMLP_WIDTH), gmlp_ln_b.reshape(1, GMLP_WIDTH), spatial_w, sbx)

    x1, xn = _oproj(a, b, x2, w_out.astype(BF16), ffn_norm_w.reshape(1, D_MODEL))

    out = _ffn(xn, x1, w_up.astype(BF16), conv_w, conv_b.reshape(1, 2 * D_FF), w_down.astype(BF16))
    return out.reshape(BATCH, SEQ, D_MODEL)
```

```python
import numpy as np
import jax
import jax.numpy as jnp
from jax import lax
from jax.experimental import pallas as pl
from jax.experimental.pallas import tpu as pltpu

F32 = jnp.float32
BF16 = jnp.bfloat16

D_MODEL = 2048
BATCH = 4
SEQ = 4096
N_TOK = BATCH * SEQ
NSA_HEADS = 16
KV_GROUPS = 4
REP = NSA_HEADS // KV_GROUPS
HEAD_DIM = 64
NSA_WIDTH = NSA_HEADS * HEAD_DIM
GMLP_WIDTH = D_MODEL - NSA_WIDTH
GMLP_GROUPS = 8
GMLP_CHUNK = 128
CMP_BLOCK = 32
CMP_STRIDE = 16
CMP_HIDDEN = 256
N_CMP_PAD = SEQ // CMP_STRIDE
SEL_BLOCK = 64
N_SEL = SEQ // SEL_BLOCK
SEL_TOPK = 16
WINDOW = 512
K_BLOCK = 128
N_KB = SEQ // K_BLOCK
Q_TILE = 256
N_QT = SEQ // Q_TILE
N_BRANCH = 3
KV_COLS = KV_GROUPS * HEAD_DIM
D_FF = 5632
CONV_WIDTH = 3
NORM_EPS = 1e-6
LN_EPS = 1e-5
NEG_BIG = -1e30
SEL_BIG = 1e9

LANES = 128
VMEM_LIMIT_BYTES = 56 * 1024 * 1024

Z_COLS = 2 * GMLP_WIDTH
PROJ_COLS = Z_COLS + NSA_WIDTH + 6 * KV_COLS
GATE_PAD = LANES
GATE_ROWS_PER_GROUP = 16

PROJ_TM = 1024
PROJ_TN = 1536
GMLP_TM = 512
OPROJ_TM = 512
FFN_TM = 512
FFN_TF = 512
SEL_TILE_BLOCKS = 4
SEL_TILE_KEYS = SEL_TILE_BLOCKS * K_BLOCK
MASK_LANE0 = 80
SUM_ROWS = 16
VT_ROWS = HEAD_DIM + SUM_ROWS
LOG2E = 1.4426950408889634
WIN_BLOCKS = (WINDOW + Q_TILE) // K_BLOCK
WIN_KEYS = WIN_BLOCKS * K_BLOCK


def _cparams(*sem):
    return pltpu.CompilerParams(dimension_semantics=sem, vmem_limit_bytes=VMEM_LIMIT_BYTES)


def _dot(a, b):
    return jnp.dot(a, b, preferred_element_type=F32)


def _proj_kernel(x_ref, nw_ref, w_ref, wg_ref, o_ref, og_ref, h_ref):
    @pl.when(pl.program_id(1) == 0)
    def _():
        x = x_ref[...]
        ms = jnp.mean(x * x, axis=-1, keepdims=True)
        h = (x * lax.rsqrt(ms + NORM_EPS) * nw_ref[...]).astype(BF16)
        h_ref[...] = h
        og_ref[...] = _dot(h, wg_ref[...])

    o_ref[...] = _dot(h_ref[...], w_ref[...])


def _proj(x2, nw, w_main, w_gate):
    return pl.pallas_call(
        _proj_kernel,
        grid=(N_TOK // PROJ_TM, PROJ_COLS // PROJ_TN),
        in_specs=[
            pl.BlockSpec((PROJ_TM, D_MODEL), lambda i, j: (i, 0)),
            pl.BlockSpec((1, D_MODEL), lambda i, j: (0, 0)),
            pl.BlockSpec((D_MODEL, PROJ_TN), lambda i, j: (0, j)),
            pl.BlockSpec((D_MODEL, GATE_PAD), lambda i, j: (0, 0)),
        ],
        out_specs=[
            pl.BlockSpec((PROJ_TM, PROJ_TN), lambda i, j: (i, j)),
            pl.BlockSpec((PROJ_TM, GATE_PAD), lambda i, j: (i, 0)),
        ],
        out_shape=[
            jax.ShapeDtypeStruct((N_TOK, PROJ_COLS), F32),
            jax.ShapeDtypeStruct((N_TOK, GATE_PAD), F32),
        ],
        scratch_shapes=[pltpu.VMEM((PROJ_TM, D_MODEL), BF16)],
        compiler_params=_cparams("arbitrary", "arbitrary"),
        name="proj",
    )(x2, nw, w_main, w_gate)


def _pos_features(lane, first, f1, f2):
    return jnp.where(lane < first + 3, f1, jnp.where(lane < first + 6, f2,
                                                     jnp.where(lane == first + 6, 1.0, 0.0)))


def _kprep_kernel(s0_ref, s1_ref, s2_ref, gl_ref, knw_ref, kcw_ref, pos_ref, w1_ref, w2_ref,
                  ksel_ref, kwin_ref, vt_ref, kc_ref, vct_ref, gt_ref, ha_ref, hb_ref):
    lane = lax.broadcasted_iota(jnp.int32, (K_BLOCK, LANES), 1)
    row = lax.broadcasted_iota(jnp.int32, (K_BLOCK, LANES), 0)
    lo = lane < HEAD_DIM
    inv_d = 1.0 / HEAD_DIM
    ones_rows = jnp.where(lax.broadcasted_iota(jnp.int32, (SUM_ROWS, K_BLOCK), 0) == 0, 1.0, 0.0)

    def chunk(c, carry):
        r0 = pl.multiple_of(c * K_BLOCK, K_BLOCK)
        x = s0_ref[pl.ds(r0, K_BLOCK), :]
        sq = x * x
        s_lo = jnp.sum(jnp.where(lo, sq, 0.0), axis=1, keepdims=True)
        s_hi = jnp.sum(jnp.where(lo, 0.0, sq), axis=1, keepdims=True)
        ms = jnp.where(lo, s_lo, s_hi) * inv_d
        y = x * lax.rsqrt(ms + NORM_EPS) * knw_ref[...]
        pos = r0 + row
        blk = lax.shift_right_logical(pos, 6)
        f1 = (blk * SEL_BLOCK).astype(F32)
        f2 = (pos & (SEL_BLOCK - 1)).astype(F32)
        onehot = jnp.where(lane - MASK_LANE0 == (blk & (SEL_TILE_BLOCKS * 2 - 1)), 1.0, 0.0)
        feat_sel = jnp.where(lane < MASK_LANE0, _pos_features(lane, HEAD_DIM, f1, f2), onehot)
        ksel_ref[c] = jnp.where(lo, y, feat_sel).astype(BF16)
        kwin_ref[c] = jnp.where(lo, _pos_features(lane, 0, f1, f2), y).astype(BF16)
        v_t = s2_ref[pl.ds(r0, K_BLOCK), :].T
        vt_ref[c] = jnp.concatenate(
            [v_t[0:HEAD_DIM], ones_rows, v_t[HEAD_DIM:], ones_rows], axis=0).astype(BF16)
        return carry

    lax.fori_loop(0, N_KB, chunk, 0, unroll=2)

    @pl.when(pl.program_id(1) == 0)
    def _():
        def gchunk(c, carry):
            r0 = pl.multiple_of(c * K_BLOCK, K_BLOCK)
            gt_ref[c] = jax.nn.sigmoid(gl_ref[pl.ds(r0, K_BLOCK), :]).T
            return carry

        lax.fori_loop(0, N_KB, gchunk, 0, unroll=2)

    ha_ref[...] = jnp.zeros_like(ha_ref)
    hb_ref[...] = jnp.zeros_like(hb_ref)
    for tt in range(CMP_STRIDE):
        rows = s1_ref[pl.ds(tt, N_CMP_PAD, stride=CMP_STRIDE), :]
        xa = (rows + pos_ref[tt:tt + 1, :]).astype(BF16)
        xb = (rows + pos_ref[CMP_STRIDE + tt:CMP_STRIDE + tt + 1, :]).astype(BF16)
        ha_ref[...] += _dot(xa, w1_ref[tt])
        hb_ref[0:N_CMP_PAD, :] += _dot(xb, w1_ref[CMP_STRIDE + tt])
    hid = jax.nn.gelu(ha_ref[...] + hb_ref[1:N_CMP_PAD + 1, :]).astype(BF16)
    kc = _dot(hid[:, :CMP_HIDDEN], w2_ref[0])
    vc = _dot(hid[:, CMP_HIDDEN:], w2_ref[1])
    ms = jnp.sum(kc * kc, axis=1, keepdims=True) * inv_d
    kcn = kc * lax.rsqrt(ms + NORM_EPS) * kcw_ref[...]
    ci = lax.broadcasted_iota(jnp.int32, (N_CMP_PAD, LANES), 0)
    cl = lax.broadcasted_iota(jnp.int32, (N_CMP_PAD, LANES), 1)
    c1 = (lax.shift_right_logical(ci, 2) * SEL_BLOCK).astype(F32)
    c2 = ((ci & 3) * CMP_STRIDE).astype(F32) + (CMP_BLOCK - 1) * 0.5
    kc_ref[...] = jnp.where(cl < HEAD_DIM, kcn, _pos_features(cl, HEAD_DIM, c1, c2)).astype(BF16)
    vct_ref[...] = vc.T[0:HEAD_DIM, :].astype(BF16)


def _kprep(proj, gl, knw2, kcw, pos2, w1p, w2p):
    kv0 = (Z_COLS + NSA_WIDTH) // LANES
    slab = lambda s: pl.BlockSpec((SEQ, LANES), lambda b, g, s=s: (b, kv0 + 3 * g + s))
    full = lambda shape: pl.BlockSpec(shape, lambda b, g: (0,) * len(shape))
    per_bg = lambda shape: pl.BlockSpec((None, None) + shape, lambda b, g: (b, g) + (0,) * len(shape))
    bg_shape = lambda shape, dt: jax.ShapeDtypeStruct((BATCH, KV_GROUPS) + shape, dt)
    return pl.pallas_call(
        _kprep_kernel,
        grid=(BATCH, KV_GROUPS),
        in_specs=[
            slab(0), slab(1), slab(2),
            pl.BlockSpec((SEQ, GATE_PAD), lambda b, g: (b, 0)),
            full((1, LANES)), full((1, LANES)), full((CMP_BLOCK, LANES)),
            full((CMP_BLOCK, LANES, 2 * CMP_HIDDEN)), full((2, CMP_HIDDEN, LANES)),
        ],
        out_specs=[
            per_bg((N_KB, K_BLOCK, LANES)),
            per_bg((N_KB, K_BLOCK, LANES)),
            per_bg((N_KB, 2 * VT_ROWS, K_BLOCK)),
            per_bg((N_CMP_PAD, LANES)),
            per_bg((HEAD_DIM, N_CMP_PAD)),
            pl.BlockSpec((None, N_KB, GATE_PAD, K_BLOCK), lambda b, g: (b, 0, 0, 0)),
        ],
        out_shape=[
            bg_shape((N_KB, K_BLOCK, LANES), BF16),
            bg_shape((N_KB, K_BLOCK, LANES), BF16),
            bg_shape((N_KB, 2 * VT_ROWS, K_BLOCK), BF16),
            bg_shape((N_CMP_PAD, LANES), BF16),
            bg_shape((HEAD_DIM, N_CMP_PAD), BF16),
            jax.ShapeDtypeStruct((BATCH, N_KB, GATE_PAD, K_BLOCK), F32),
        ],
        scratch_shapes=[
            pltpu.VMEM((N_CMP_PAD, 2 * CMP_HIDDEN), F32),
            pltpu.VMEM((N_CMP_PAD + 8, 2 * CMP_HIDDEN), F32),
        ],
        compiler_params=_cparams("arbitrary", "arbitrary"),
        name="kprep",
    )(proj, proj, proj, gl, knw2, kcw, pos2, w1p, w2p)


def _attn_kernel(q_ref, ksel_ref, kwin_ref, vt_ref, kc_ref, vct_ref, gt_ref, sf_ref, wm_ref, dm_ref,
                 ov_ref, wq_ref, o_ref, sel_ref, qs_ref, sa_ref, sb_ref, tiles_ref):
    qt = pl.program_id(2)
    t_i = qt * Q_TILE + lax.broadcasted_iota(jnp.int32, (1, Q_TILE), 1)
    head = lambda a, r: a[:, r * Q_TILE:(r + 1) * Q_TILE]
    heads = lambda parts: jnp.concatenate(parts, axis=1)

    q_t = q_ref[...].T
    parts = []
    for r in range(REP):
        xq = q_t[r * HEAD_DIM:(r + 1) * HEAD_DIM, :]
        ms = jnp.sum(xq * xq, axis=0, keepdims=True) * (1.0 / HEAD_DIM)
        parts.append(xq * lax.rsqrt(ms + NORM_EPS) * wq_ref[...] * (HEAD_DIM ** -0.5) * LOG2E)
    qn = heads(parts)
    sf = sf_ref[...]
    sf = jnp.where(lax.broadcasted_iota(jnp.int32, sf.shape, 0) == 6, sf * (qt * Q_TILE).astype(F32), sf)
    wide = REP * Q_TILE
    pad = jnp.zeros((HEAD_DIM - 8, wide), F32)
    qs_ref[...] = jnp.concatenate([qn, sf, pad], axis=0).astype(BF16)
    q_win = jnp.concatenate([sf, pad, qn], axis=0).astype(BF16)

    sc = _dot(kc_ref[...], qs_ref[...])
    ci = lax.broadcasted_iota(jnp.int32, (N_CMP_PAD, Q_TILE), 0)
    cm = jnp.where(ci * CMP_STRIDE + (CMP_BLOCK - 1) <= t_i, 0.0, NEG_BIG)
    any_valid = t_i >= CMP_BLOCK - 1
    pcs = []
    for r in range(REP):
        s = head(sc, r) + cm
        e = jnp.exp2(s - jnp.max(s, axis=0, keepdims=True))
        d = jnp.sum(e, axis=0, keepdims=True)
        pcs.append((e * jnp.where(any_valid, 1.0 / d, 0.0)).astype(BF16))
    oc_imp = _dot(jnp.concatenate([vct_ref[...], ov_ref[...]], axis=0), heads(pcs))
    o_c = oc_imp[0:HEAD_DIM]
    imp4 = oc_imp[HEAD_DIM:]
    imp = (head(imp4, 0) + head(imp4, 1)) + (head(imp4, 2) + head(imp4, 3))

    jj = lax.broadcasted_iota(jnp.int32, (N_SEL, Q_TILE), 0)
    cur = lax.shift_right_logical(t_i, 6)
    valid = jj <= cur
    sel_ref[...] = jnp.where(valid, 1.0, 0.0)

    @pl.when(qt * Q_TILE + Q_TILE > SEL_TOPK * SEL_BLOCK)
    def _():
        forced = (jj == 0) | (jj == cur) | (jj == cur - 1)
        score = jnp.where(valid, jnp.where(forced, SEL_BIG, imp), -SEL_BIG)
        rows8 = [score[8 * v:8 * v + 8, :] for v in range(N_SEL // 8)]
        j8 = lax.broadcasted_iota(jnp.int32, (8, Q_TILE), 0)
        ranks = [jnp.zeros((8, Q_TILE), F32) for _ in rows8]
        for jp in range(N_SEL):
            row = score[jp:jp + 1, :]
            for v, blk in enumerate(rows8):
                if 8 * v > jp:
                    ahead = jnp.where(row >= blk, 1.0, 0.0)
                elif 8 * v + 7 <= jp:
                    ahead = jnp.where(row > blk, 1.0, 0.0)
                else:
                    ahead = jnp.where(j8 + 8 * v > jp, jnp.where(row >= blk, 1.0, 0.0),
                                      jnp.where(row > blk, 1.0, 0.0))
                ranks[v] = ranks[v] + ahead
        rank = jnp.concatenate(ranks, axis=0)
        sel_ref[...] = jnp.where(valid & (rank < SEL_TOPK), 1.0, 0.0)

    sub = SEL_TILE_KEYS // SEL_BLOCK

    def scores_into(kt, s_ref):
        member = sel_ref[pl.ds(kt * sub, sub), :]
        mrows = (member - 1.0) * (-NEG_BIG)
        qs_ref[MASK_LANE0:MASK_LANE0 + 16, :] = jnp.concatenate(
            [heads([mrows] * REP), jnp.zeros((16 - sub, wide), F32)], axis=0).astype(BF16)
        k_tile = ksel_ref[pl.ds(kt * SEL_TILE_BLOCKS, SEL_TILE_BLOCKS)].reshape(SEL_TILE_KEYS, LANES)
        s_ref[...] = _dot(k_tile, qs_ref[...])

    def absorb(kt, s_ref, carry, diagonal):
        m_run, acc = carry
        kb0 = kt * SEL_TILE_BLOCKS
        if diagonal:
            causal = dm_ref[qt % (SEL_TILE_KEYS // Q_TILE)]
        v_t = jnp.concatenate([vt_ref[kb0 + x, 0:VT_ROWS, :] for x in range(SEL_TILE_BLOCKS)], axis=1)
        ms_, accs = [], []
        for r in range(REP):
            sr = s_ref[:, r * Q_TILE:(r + 1) * Q_TILE]
            if diagonal:
                sr = sr + causal
            m_old = head(m_run, r)
            m_new = jnp.maximum(m_old, jnp.max(sr, axis=0, keepdims=True))
            p = jnp.exp2(sr - m_new).astype(BF16)
            accs.append(head(acc, r) * jnp.exp2(m_old - m_new) + _dot(v_t, p))
            ms_.append(m_new)
        return heads(ms_), heads(accs)

    carry = (jnp.full((1, wide), NEG_BIG, F32), jnp.zeros((VT_ROWS, wide), F32))
    n_full = (qt * Q_TILE) // SEL_TILE_KEYS

    n_sel_tiles = SEQ // SEL_TILE_KEYS
    tr = lax.broadcasted_iota(jnp.int32, (n_sel_tiles, N_SEL), 0)
    tc = lax.broadcasted_iota(jnp.int32, (n_sel_tiles, N_SEL), 1)
    tile_of_block = jnp.where(lax.shift_right_logical(tc, 3) == tr, 1.0, 0.0).astype(BF16)
    per_query = _dot(tile_of_block, sel_ref[...].astype(BF16))
    per_tile = _dot(per_query.astype(BF16), jnp.ones((Q_TILE, LANES), BF16))
    bit = lax.shift_left(1, lax.broadcasted_iota(jnp.int32, (n_sel_tiles, LANES), 0)).astype(F32)
    used_bits = jnp.sum(jnp.where(per_tile > 0.5, bit, 0.0), axis=0, keepdims=True)[0, 0].astype(jnp.int32)

    kb_first = qt * (Q_TILE // K_BLOCK) - WINDOW // K_BLOCK
    kbs = [jnp.maximum(kb_first + x, 0) for x in range(WIN_BLOCKS)]
    sw = _dot(jnp.concatenate([kwin_ref[kb] for kb in kbs], axis=0), q_win)

    n_visit = jnp.int32(0)
    for kt in range(n_sel_tiles):
        tiles_ref[n_visit] = kt
        take = (kt < n_full) & ((lax.shift_right_logical(used_bits, kt) & 1) == 1)
        n_visit = n_visit + jnp.where(take, 1, 0)
    tiles_ref[n_visit] = n_full

    scores_into(tiles_ref[0], sa_ref)
    wm = jnp.concatenate(
        [wm_ref[x * K_BLOCK:(x + 1) * K_BLOCK, :] + jnp.where(kb_first + x < 0, NEG_BIG, 0.0)
         for x in range(WIN_BLOCKS)], axis=0)
    pws = []
    for r in range(REP):
        s = head(sw, r) + wm
        pws.append(jnp.exp2(s - jnp.max(s, axis=0, keepdims=True)).astype(BF16))
    vw_t = jnp.concatenate([vt_ref[kb, VT_ROWS:2 * VT_ROWS, :] for kb in kbs], axis=1)
    acc_w = _dot(vw_t, heads(pws))
    o_w = acc_w[0:HEAD_DIM] / acc_w[HEAD_DIM:HEAD_DIM + 1]

    def tile_pair(i, carry):
        scores_into(tiles_ref[2 * i + 1], sb_ref)
        carry = absorb(tiles_ref[2 * i], sa_ref, carry, False)
        scores_into(tiles_ref[2 * i + 2], sa_ref)
        return absorb(tiles_ref[2 * i + 1], sb_ref, carry, False)

    carry = lax.fori_loop(0, n_visit // 2, tile_pair, carry)
    last = 2 * (n_visit // 2)

    def odd_tail(carry):
        scores_into(n_full, sb_ref)
        return absorb(n_full, sb_ref, absorb(tiles_ref[last], sa_ref, carry, False), True)

    def even_tail(carry):
        return absorb(n_full, sa_ref, carry, True)

    _, acc_s = lax.cond(n_visit % 2 == 1, odd_tail, even_tail, carry)
    o_s = acc_s[0:HEAD_DIM] / acc_s[HEAD_DIM:HEAD_DIM + 1]

    gates = jnp.concatenate([gt_ref[x] for x in range(Q_TILE // K_BLOCK)], axis=1)
    outs = []
    for r in range(REP):
        outs.append(gates[r:r + 1, :] * head(o_c, r)
                    + gates[REP + r:REP + r + 1, :] * head(o_s, r)
                    + gates[2 * REP + r:2 * REP + r + 1, :] * head(o_w, r))
    o_ref[...] = jnp.concatenate(outs, axis=0).T.astype(BF16)


def _attn(proj, ksel, kwin, vt, kc, vct, gt, sf, wm, dm, ov_t, wq_b):
    q0 = Z_COLS // (REP * HEAD_DIM)
    kvspec = lambda shape: pl.BlockSpec((None, None) + shape, lambda b, g, qt: (b, g) + (0,) * len(shape))
    const = lambda shape: pl.BlockSpec(shape, lambda b, g, qt: (0,) * len(shape))
    return pl.pallas_call(
        _attn_kernel,
        grid=(BATCH, KV_GROUPS, N_QT),
        in_specs=[
            pl.BlockSpec((Q_TILE, REP * HEAD_DIM), lambda b, g, qt: (b * N_QT + qt, q0 + g)),
            kvspec((N_KB, K_BLOCK, LANES)),
            kvspec((N_KB, K_BLOCK, LANES)),
            kvspec((N_KB, 2 * VT_ROWS, K_BLOCK)),
            kvspec((N_CMP_PAD, LANES)),
            kvspec((HEAD_DIM, N_CMP_PAD)),
            pl.BlockSpec((None, Q_TILE // K_BLOCK, GATE_ROWS_PER_GROUP, K_BLOCK), lambda b, g, qt: (b, qt, g, 0)),
            pl.BlockSpec((None, 8, REP * Q_TILE), lambda b, g, qt: (g, 0, 0)),
            const((WIN_KEYS, Q_TILE)),
            const((SEL_TILE_KEYS // Q_TILE, SEL_TILE_KEYS, Q_TILE)),
            const((N_SEL, N_CMP_PAD)),
            const((HEAD_DIM, Q_TILE)),
        ],
        out_specs=pl.BlockSpec((Q_TILE, REP * HEAD_DIM), lambda b, g, qt: (b * N_QT + qt, g)),
        out_shape=jax.ShapeDtypeStruct((N_TOK, NSA_WIDTH), BF16),
        scratch_shapes=[pltpu.VMEM((N_SEL, Q_TILE), F32), pltpu.VMEM((LANES, REP * Q_TILE), BF16),
                        pltpu.VMEM((SEL_TILE_KEYS, REP * Q_TILE), F32),
                        pltpu.VMEM((SEL_TILE_KEYS, REP * Q_TILE), F32),
                        pltpu.SMEM((SEQ // SEL_TILE_KEYS + 1,), jnp.int32)],
        compiler_params=_cparams("arbitrary", "arbitrary", "arbitrary"),
        name="attn",
    )(proj, ksel, kwin, vt, kc, vct, gt, sf, wm, dm, ov_t, wq_b)


def _gmlp_kernel(z_ref, lnw_ref, lnb_ref, sw_ref, sbx_ref, o_ref):
    ge = jax.nn.gelu(z_ref[...])
    u = ge[:, :GMLP_WIDTH]
    v = ge[:, GMLP_WIDTH:]
    mu = jnp.mean(v, axis=-1, keepdims=True)
    var = jnp.mean(jnp.square(v - mu), axis=-1, keepdims=True)
    vn = ((v - mu) * lax.rsqrt(var + LN_EPS) * lnw_ref[...] + lnb_ref[...]).astype(BF16)
    ti = lax.broadcasted_iota(jnp.int32, (GMLP_CHUNK, GMLP_CHUNK), 0)
    si = lax.broadcasted_iota(jnp.int32, (GMLP_CHUNK, GMLP_CHUNK), 1)
    ws = [jnp.where(ti >= si, sw_ref[gg], 0.0).astype(BF16) for gg in range(GMLP_GROUPS)]
    gd = GMLP_WIDTH // GMLP_GROUPS
    for c in range(GMLP_TM // GMLP_CHUNK):
        rows = slice(c * GMLP_CHUNK, (c + 1) * GMLP_CHUNK)
        mix = jnp.concatenate(
            [_dot(ws[gg], vn[rows, gg * gd:(gg + 1) * gd]) for gg in range(GMLP_GROUPS)], axis=1)
        o_ref[rows, :] = (u[rows, :] * (mix + sbx_ref[...])).astype(BF16)


def _gmlp(proj, lnw, lnb, sw, sbx):
    return pl.pallas_call(
        _gmlp_kernel,
        grid=(N_TOK // GMLP_TM,),
        in_specs=[
            pl.BlockSpec((GMLP_TM, Z_COLS), lambda i: (i, 0)),
            pl.BlockSpec((1, GMLP_WIDTH), lambda i: (0, 0)),
            pl.BlockSpec((1, GMLP_WIDTH), lambda i: (0, 0)),
            pl.BlockSpec((GMLP_GROUPS, GMLP_CHUNK, GMLP_CHUNK), lambda i: (0, 0, 0)),
            pl.BlockSpec((GMLP_CHUNK, GMLP_WIDTH), lambda i: (0, 0)),
        ],
        out_specs=pl.BlockSpec((GMLP_TM, GMLP_WIDTH), lambda i: (i, 0)),
        out_shape=jax.ShapeDtypeStruct((N_TOK, GMLP_WIDTH), BF16),
        compiler_params=_cparams("arbitrary"),
        name="gmlp",
    )(proj, lnw, lnb, sw, sbx)


def _oproj_kernel(a_ref, b_ref, x_ref, wa_ref, wb_ref, nw_ref, x1_ref, xn_ref):
    y = x_ref[...] + (_dot(a_ref[...], wa_ref[...]) + _dot(b_ref[...], wb_ref[...]))
    x1_ref[...] = y
    ms = jnp.mean(y * y, axis=-1, keepdims=True)
    xn_ref[...] = (y * lax.rsqrt(ms + NORM_EPS) * nw_ref[...]).astype(BF16)


def _oproj(a, b, x2, w_out, nw):
    return pl.pallas_call(
        _oproj_kernel,
        grid=(N_TOK // OPROJ_TM,),
        in_specs=[
            pl.BlockSpec((OPROJ_TM, NSA_WIDTH), lambda i: (i, 0)),
            pl.BlockSpec((OPROJ_TM, GMLP_WIDTH), lambda i: (i, 0)),
            pl.BlockSpec((OPROJ_TM, D_MODEL), lambda i: (i, 0)),
            pl.BlockSpec((NSA_WIDTH, D_MODEL), lambda i: (0, 0)),
            pl.BlockSpec((GMLP_WIDTH, D_MODEL), lambda i: (1, 0)),
            pl.BlockSpec((1, D_MODEL), lambda i: (0, 0)),
        ],
        out_specs=[
            pl.BlockSpec((OPROJ_TM, D_MODEL), lambda i: (i, 0)),
            pl.BlockSpec((OPROJ_TM, D_MODEL), lambda i: (i, 0)),
        ],
        out_shape=[
            jax.ShapeDtypeStruct((N_TOK, D_MODEL), F32),
            jax.ShapeDtypeStruct((N_TOK, D_MODEL), BF16),
        ],
        compiler_params=_cparams("arbitrary"),
        name="oproj",
    )(a, b, x2, w_out, w_out, nw)


def _causal_conv(h, prev, cw, cb):
    rid = lax.broadcasted_iota(jnp.int32, prev.shape, 0)

    def shifted(k):
        body = pltpu.roll(h, k, 0)
        top = jnp.where(rid < k, pltpu.roll(prev, k, 0), body[0:8, :])
        return jnp.concatenate([top, body[8:, :]], axis=0)

    return cb + ((cw[0:1, :] * shifted(2) + cw[1:2, :] * shifted(1)) + cw[2:3, :] * h)


def _ffn_kernel(xn_ref, x1_ref, wg_ref, wu_ref, cwg_ref, cwu_ref, cbg_ref, cbu_ref, wd_ref,
                cwg_p_ref, cwu_p_ref, cbg_p_ref, cbu_p_ref, wd_p_ref, o_ref, h0_ref, h1_ref, carry_ref):
    i = pl.program_id(0)
    q = pl.program_id(1)
    nq = pl.num_programs(1)
    half = FFN_TF // 2

    @pl.when((i == 0) & (q == 0))
    def _():
        carry_ref[...] = jnp.zeros_like(carry_ref)
        h1_ref[...] = jnp.zeros_like(h1_ref)

    @pl.when(q == 0)
    def _():
        o_ref[...] = x1_ref[...]

    seq_start = i % (SEQ // FFN_TM) == 0
    xn = xn_ref[...]

    def produce(c, h_ref):
        h_ref[0, 8:, :] = _dot(xn, wg_ref[:, c * half:(c + 1) * half])
        h_ref[1, 8:, :] = _dot(xn, wu_ref[:, c * half:(c + 1) * half])

    def conv(h_ref, part, tile, cw, cb):
        h_ref[part, 0:8, :] = jnp.where(seq_start, 0.0, carry_ref[tile, part])
        carry_ref[tile, part] = h_ref[part, FFN_TM:FFN_TM + 8, :]
        taps = [h_ref[part, 8 - k:8 - k + FFN_TM, :] for k in (2, 1, 0)]
        return cb + ((cw[0:1, :] * taps[0] + cw[1:2, :] * taps[1]) + cw[2:3, :] * taps[2])

    def gate(h_ref, tile, c, cwg, cwu, cbg, cbu, scale=None):
        cols = slice(c * half, (c + 1) * half)
        cg = conv(h_ref, 0, tile, cwg[:, cols], cbg[:, cols])
        cu = conv(h_ref, 1, tile, cwu[:, cols], cbu[:, cols])
        act = jax.nn.silu(cg) * cu
        if scale is not None:
            act = act * scale
        return act.astype(BF16)

    act_prev = gate(h1_ref, jnp.where(q == 0, 2 * nq - 1, 2 * q - 1), 1, cwg_p_ref, cwu_p_ref, cbg_p_ref,
                    cbu_p_ref, scale=jnp.where(q == 0, 0.0, 1.0))
    produce(0, h0_ref)
    down_prev = _dot(act_prev, wd_p_ref[half:, :])
    act_cur = gate(h0_ref, 2 * q, 0, cwg_ref, cwu_ref, cbg_ref, cbu_ref)
    produce(1, h1_ref)
    o_ref[...] += down_prev + _dot(act_cur, wd_ref[0:half, :])

    @pl.when(q == nq - 1)
    def _():
        act_last = gate(h1_ref, 2 * q + 1, 1, cwg_ref, cwu_ref, cbg_ref, cbu_ref)
        o_ref[...] += _dot(act_last, wd_ref[half:, :])


def _ffn(xn, x1, w_up, conv_w, conv_b, w_down):
    nj = D_FF // FFN_TF
    prev = lambda q: jnp.maximum(q - 1, 0)
    return pl.pallas_call(
        _ffn_kernel,
        grid=(N_TOK // FFN_TM, nj),
        in_specs=[
            pl.BlockSpec((FFN_TM, D_MODEL), lambda i, q: (i, 0)),
            pl.BlockSpec((FFN_TM, D_MODEL), lambda i, q: (i, 0)),
            pl.BlockSpec((D_MODEL, FFN_TF), lambda i, q: (0, q)),
            pl.BlockSpec((D_MODEL, FFN_TF), lambda i, q: (0, nj + q)),
            pl.BlockSpec((CONV_WIDTH, FFN_TF), lambda i, q: (0, q)),
            pl.BlockSpec((CONV_WIDTH, FFN_TF), lambda i, q: (0, nj + q)),
            pl.BlockSpec((1, FFN_TF), lambda i, q: (0, q)),
            pl.BlockSpec((1, FFN_TF), lambda i, q: (0, nj + q)),
            pl.BlockSpec((FFN_TF, D_MODEL), lambda i, q: (q, 0)),
            pl.BlockSpec((CONV_WIDTH, FFN_TF), lambda i, q: (0, prev(q))),
            pl.BlockSpec((CONV_WIDTH, FFN_TF), lambda i, q: (0, nj + prev(q))),
            pl.BlockSpec((1, FFN_TF), lambda i, q: (0, prev(q))),
            pl.BlockSpec((1, FFN_TF), lambda i, q: (0, nj + prev(q))),
            pl.BlockSpec((FFN_TF, D_MODEL), lambda i, q: (prev(q), 0)),
        ],
        out_specs=pl.BlockSpec((FFN_TM, D_MODEL), lambda i, q: (i, 0)),
        out_shape=jax.ShapeDtypeStruct((N_TOK, D_MODEL), F32),
        scratch_shapes=[pltpu.VMEM((2, FFN_TM + 8, FFN_TF // 2), F32), pltpu.VMEM((2, FFN_TM + 8, FFN_TF // 2), F32),
                        pltpu.VMEM((2 * nj, 2, 8, FFN_TF // 2), F32)],
        compiler_params=_cparams("arbitrary", "arbitrary"),
        name="ffn",
    )(xn, x1, w_up, w_up, conv_w, conv_w, conv_b, conv_b, w_down, conv_w, conv_w, conv_b, conv_b, w_down)


def _overlap_t():
    start = np.arange(N_CMP_PAD)[None, :] * CMP_STRIDE
    s0 = np.arange(N_SEL)[:, None] * SEL_BLOCK
    ov = (start < s0 + SEL_BLOCK) & (start + CMP_BLOCK > s0) & (np.arange(N_CMP_PAD)[None, :] < N_CMP_PAD - 1)
    return ov.astype(np.float32)


def _bf16_round(a):
    return np.asarray(a, np.float32).astype(BF16).astype(np.float32)


def _slope_features():
    sl = (np.power(2.0, -8.0 * np.arange(1, NSA_HEADS + 1) / NSA_HEADS).astype(np.float32)
          * np.float32(LOG2E)).astype(np.float32)
    s1 = _bf16_round(sl)
    s2 = _bf16_round(sl - s1)
    s3 = _bf16_round(sl - s1 - s2)
    rows = np.stack([s1, s2, s3, s1, s2, s3, -sl, np.zeros_like(sl)], axis=0)
    rows = rows.reshape(8, KV_GROUPS, REP).transpose(1, 0, 2)
    return np.repeat(rows, Q_TILE, axis=2).astype(np.float32)


def _window_mask():
    kl = np.arange(WIN_KEYS)[:, None]
    ql = np.arange(Q_TILE)[None, :]
    dist = ql + WINDOW - kl
    return np.where((dist >= 0) & (dist < WINDOW), 0.0, NEG_BIG).astype(np.float32)


def _diag_masks():
    kl = np.arange(SEL_TILE_KEYS)[None, :, None]
    ql = np.arange(Q_TILE)[None, None, :]
    off = np.arange(SEL_TILE_KEYS // Q_TILE)[:, None, None] * Q_TILE
    return np.where(kl > ql + off, NEG_BIG, 0.0).astype(np.float32)


def _layout_w_in(w_in):
    q_end = NSA_WIDTH
    kv_end = q_end + 6 * KV_COLS
    g_end = kv_end + NSA_HEADS * N_BRANCH
    w_q = w_in[:, :q_end]
    w_kv = w_in[:, q_end:kv_end].reshape(D_MODEL, 6, KV_GROUPS, HEAD_DIM)
    w_kv = jnp.stack([w_kv[:, s] for s in (2, 4, 0, 1, 3, 5)], axis=2).reshape(D_MODEL, 6 * KV_COLS)
    w_z = w_in[:, g_end:]
    w_main = jnp.concatenate([w_z, w_q, w_kv], axis=1).astype(BF16)
    w_g = w_in[:, kv_end:g_end].reshape(D_MODEL, KV_GROUPS, REP, N_BRANCH).transpose(0, 1, 3, 2)
    w_g = w_g.reshape(D_MODEL, KV_GROUPS, N_BRANCH * REP)
    w_g = jnp.pad(w_g, ((0, 0), (0, 0), (0, GATE_ROWS_PER_GROUP - N_BRANCH * REP)))
    w_g = jnp.pad(w_g.reshape(D_MODEL, KV_GROUPS * GATE_ROWS_PER_GROUP),
                  ((0, 0), (0, GATE_PAD - KV_GROUPS * GATE_ROWS_PER_GROUP))).astype(BF16)
    return w_main, w_g


def _layout_compress(cmp_pos, cmp_w1, cmp_w2):
    pos2 = jnp.concatenate([cmp_pos[0], cmp_pos[1]], axis=1)
    w1 = cmp_w1.reshape(2, CMP_BLOCK, HEAD_DIM, CMP_HIDDEN)
    zero = jnp.zeros_like(w1[0])
    w1k = jnp.concatenate([w1[0], zero], axis=1)
    w1v = jnp.concatenate([zero, w1[1]], axis=1)
    w1p = jnp.concatenate([w1k, w1v], axis=2).astype(BF16)
    w2p = jnp.pad(cmp_w2, ((0, 0), (0, 0), (0, LANES - HEAD_DIM))).astype(BF16)
    return pos2, w1p, w2p


def kernel(x, attn_norm_w, w_in, q_norm_w, k_norm_w, cmp_pos, cmp_w1, cmp_w2, gmlp_ln_w, gmlp_ln_b,
           spatial_w, spatial_b, w_out, ffn_norm_w, w_up, conv_w, conv_b, w_down):
    x2 = x.reshape(N_TOK, D_MODEL)
    w_main, w_gate = _layout_w_in(w_in)
    proj, gl = _proj(x2, attn_norm_w.reshape(1, D_MODEL), w_main, w_gate)

    knw2 = jnp.concatenate([k_norm_w[1], k_norm_w[2]]).reshape(1, LANES)
    kcw = jnp.concatenate([k_norm_w[0], jnp.zeros((LANES - HEAD_DIM,), F32)]).reshape(1, LANES)
    pos2, w1p, w2p = _layout_compress(cmp_pos, cmp_w1, cmp_w2)
    ksel, kwin, vt, kc, vct, gt = _kprep(proj, gl, knw2, kcw, pos2, w1p, w2p)

    wq_b = jnp.broadcast_to(q_norm_w.reshape(HEAD_DIM, 1), (HEAD_DIM, Q_TILE))
    a = _attn(proj, ksel, kwin, vt, kc, vct, gt, jnp.asarray(_slope_features()),
              jnp.asarray(_window_mask()), jnp.asarray(_diag_masks()),
              jnp.asarray(_overlap_t(), dtype=BF16), wq_b)

    sbx = jnp.repeat(spatial_b.T, GMLP_WIDTH // GMLP_GROUPS, axis=1)
    b = _gmlp(proj, gmlp_ln_w.reshape(1, GMLP_WIDTH), gmlp_ln_b.reshape(1, GMLP_WIDTH), spatial_w, sbx)

    x1, xn = _oproj(a, b, x2, w_out.astype(BF16), ffn_norm_w.reshape(1, D_MODEL))

    out = _ffn(xn, x1, w_up.astype(BF16), conv_w, conv_b.reshape(1, 2 * D_FF), w_down.astype(BF16))
    return out.reshape(BATCH, SEQ, D_MODEL)
```

```python
import numpy as np
import jax
import jax.numpy as jnp
from jax import lax
from jax.experimental import pallas as pl
from jax.experimental.pallas import tpu as pltpu

F32 = jnp.float32
BF16 = jnp.bfloat16

D_MODEL = 2048
BATCH = 4
SEQ = 4096
N_TOK = BATCH * SEQ
NSA_HEADS = 16
KV_GROUPS = 4
REP = NSA_HEADS // KV_GROUPS
HEAD_DIM = 64
NSA_WIDTH = NSA_HEADS * HEAD_DIM
GMLP_WIDTH = D_MODEL - NSA_WIDTH
GMLP_GROUPS = 8
GMLP_CHUNK = 128
CMP_BLOCK = 32
CMP_STRIDE = 16
CMP_HIDDEN = 256
N_CMP_PAD = SEQ // CMP_STRIDE
SEL_BLOCK = 64
N_SEL = SEQ // SEL_BLOCK
SEL_TOPK = 16
WINDOW = 512
K_BLOCK = 128
N_KB = SEQ // K_BLOCK
Q_TILE = 256
N_QT = SEQ // Q_TILE
N_BRANCH = 3
KV_COLS = KV_GROUPS * HEAD_DIM
D_FF = 5632
CONV_WIDTH = 3
NORM_EPS = 1e-6
LN_EPS = 1e-5
NEG_BIG = -1e30
SEL_BIG = 1e9

LANES = 128
VMEM_LIMIT_BYTES = 56 * 1024 * 1024

Z_COLS = 2 * GMLP_WIDTH
PROJ_COLS = Z_COLS + NSA_WIDTH + 6 * KV_COLS
GATE_PAD = LANES
GATE_ROWS_PER_GROUP = 16

PROJ_TM = 1024
PROJ_TN = 1536
GMLP_TM = 512
OPROJ_TM = 512
FFN_TM = 512
FFN_TF = 512
SEL_TILE_BLOCKS = 4
SEL_TILE_KEYS = SEL_TILE_BLOCKS * K_BLOCK
MASK_LANE0 = 80
SUM_ROWS = 16
VT_ROWS = HEAD_DIM + SUM_ROWS
LOG2E = 1.4426950408889634
WIN_BLOCKS = (WINDOW + Q_TILE) // K_BLOCK
WIN_KEYS = WIN_BLOCKS * K_BLOCK


def _cparams(*sem):
    return pltpu.CompilerParams(dimension_semantics=sem, vmem_limit_bytes=VMEM_LIMIT_BYTES)


def _dot(a, b):
    return jnp.dot(a, b, preferred_element_type=F32)


def _proj_kernel(x_ref, nw_ref, w_ref, wg_ref, o_ref, og_ref, h_ref):
    @pl.when(pl.program_id(1) == 0)
    def _():
        x = x_ref[...]
        ms = jnp.mean(x * x, axis=-1, keepdims=True)
        h = (x * lax.rsqrt(ms + NORM_EPS) * nw_ref[...]).astype(BF16)
        h_ref[...] = h
        og_ref[...] = _dot(h, wg_ref[...])

    o_ref[...] = _dot(h_ref[...], w_ref[...])


def _proj(x2, nw, w_main, w_gate):
    return pl.pallas_call(
        _proj_kernel,
        grid=(N_TOK // PROJ_TM, PROJ_COLS // PROJ_TN),
        in_specs=[
            pl.BlockSpec((PROJ_TM, D_MODEL), lambda i, j: (i, 0)),
            pl.BlockSpec((1, D_MODEL), lambda i, j: (0, 0)),
            pl.BlockSpec((D_MODEL, PROJ_TN), lambda i, j: (0, j)),
            pl.BlockSpec((D_MODEL, GATE_PAD), lambda i, j: (0, 0)),
        ],
        out_specs=[
            pl.BlockSpec((PROJ_TM, PROJ_TN), lambda i, j: (i, j)),
            pl.BlockSpec((PROJ_TM, GATE_PAD), lambda i, j: (i, 0)),
        ],
        out_shape=[
            jax.ShapeDtypeStruct((N_TOK, PROJ_COLS), F32),
            jax.ShapeDtypeStruct((N_TOK, GATE_PAD), F32),
        ],
        scratch_shapes=[pltpu.VMEM((PROJ_TM, D_MODEL), BF16)],
        compiler_params=_cparams("arbitrary", "arbitrary"),
        name="proj",
    )(x2, nw, w_main, w_gate)


def _pos_features(lane, first, f1, f2):
    return jnp.where(lane < first + 3, f1, jnp.where(lane < first + 6, f2,
                                                     jnp.where(lane == first + 6, 1.0, 0.0)))


def _kprep_kernel(s0_ref, s1_ref, s2_ref, gl_ref, knw_ref, kcw_ref, pos_ref, w1_ref, w2_ref,
                  ksel_ref, kwin_ref, vt_ref, kc_ref, vct_ref, gt_ref, ha_ref, hb_ref):
    lane = lax.broadcasted_iota(jnp.int32, (K_BLOCK, LANES), 1)
    row = lax.broadcasted_iota(jnp.int32, (K_BLOCK, LANES), 0)
    lo = lane < HEAD_DIM
    inv_d = 1.0 / HEAD_DIM
    ones_rows = jnp.where(lax.broadcasted_iota(jnp.int32, (SUM_ROWS, K_BLOCK), 0) == 0, 1.0, 0.0)

    def chunk(c, carry):
        r0 = pl.multiple_of(c * K_BLOCK, K_BLOCK)
        x = s0_ref[pl.ds(r0, K_BLOCK), :]
        sq = x * x
        s_lo = jnp.sum(jnp.where(lo, sq, 0.0), axis=1, keepdims=True)
        s_hi = jnp.sum(jnp.where(lo, 0.0, sq), axis=1, keepdims=True)
        ms = jnp.where(lo, s_lo, s_hi) * inv_d
        y = x * lax.rsqrt(ms + NORM_EPS) * knw_ref[...]
        pos = r0 + row
        blk = lax.shift_right_logical(pos, 6)
        f1 = (blk * SEL_BLOCK).astype(F32)
        f2 = (pos & (SEL_BLOCK - 1)).astype(F32)
        onehot = jnp.where(lane - MASK_LANE0 == (blk & (SEL_TILE_BLOCKS * 2 - 1)), 1.0, 0.0)
        feat_sel = jnp.where(lane < MASK_LANE0, _pos_features(lane, HEAD_DIM, f1, f2), onehot)
        ksel_ref[c] = jnp.where(lo, y, feat_sel).astype(BF16)
        kwin_ref[c] = jnp.where(lo, _pos_features(lane, 0, f1, f2), y).astype(BF16)
        v_t = s2_ref[pl.ds(r0, K_BLOCK), :].T
        vt_ref[c] = jnp.concatenate(
            [v_t[0:HEAD_DIM], ones_rows, v_t[HEAD_DIM:], ones_rows], axis=0).astype(BF16)
        return carry

    lax.fori_loop(0, N_KB, chunk, 0, unroll=2)

    @pl.when(pl.program_id(1) == 0)
    def _():
        def gchunk(c, carry):
            r0 = pl.multiple_of(c * K_BLOCK, K_BLOCK)
            gt_ref[c] = jax.nn.sigmoid(gl_ref[pl.ds(r0, K_BLOCK), :]).T
            return carry

        lax.fori_loop(0, N_KB, gchunk, 0, unroll=2)

    ha_ref[...] = jnp.zeros_like(ha_ref)
    hb_ref[...] = jnp.zeros_like(hb_ref)
    for tt in range(CMP_STRIDE):
        rows = s1_ref[pl.ds(tt, N_CMP_PAD, stride=CMP_STRIDE), :]
        xa = (rows + pos_ref[tt:tt + 1, :]).astype(BF16)
        xb = (rows + pos_ref[CMP_STRIDE + tt:CMP_STRIDE + tt + 1, :]).astype(BF16)
        ha_ref[...] += _dot(xa, w1_ref[tt])
        hb_ref[0:N_CMP_PAD, :] += _dot(xb, w1_ref[CMP_STRIDE + tt])
    hid = jax.nn.gelu(ha_ref[...] + hb_ref[1:N_CMP_PAD + 1, :]).astype(BF16)
    kc = _dot(hid[:, :CMP_HIDDEN], w2_ref[0])
    vc = _dot(hid[:, CMP_HIDDEN:], w2_ref[1])
    ms = jnp.sum(kc * kc, axis=1, keepdims=True) * inv_d
    kcn = kc * lax.rsqrt(ms + NORM_EPS) * kcw_ref[...]
    ci = lax.broadcasted_iota(jnp.int32, (N_CMP_PAD, LANES), 0)
    cl = lax.broadcasted_iota(jnp.int32, (N_CMP_PAD, LANES), 1)
    c1 = (lax.shift_right_logical(ci, 2) * SEL_BLOCK).astype(F32)
    c2 = ((ci & 3) * CMP_STRIDE).astype(F32) + (CMP_BLOCK - 1) * 0.5
    kc_ref[...] = jnp.where(cl < HEAD_DIM, kcn, _pos_features(cl, HEAD_DIM, c1, c2)).astype(BF16)
    vct_ref[...] = vc.T[0:HEAD_DIM, :].astype(BF16)


def _kprep(proj, gl, knw2, kcw, pos2, w1p, w2p):
    kv0 = (Z_COLS + NSA_WIDTH) // LANES
    slab = lambda s: pl.BlockSpec((SEQ, LANES), lambda b, g, s=s: (b, kv0 + 3 * g + s))
    full = lambda shape: pl.BlockSpec(shape, lambda b, g: (0,) * len(shape))
    per_bg = lambda shape: pl.BlockSpec((None, None) + shape, lambda b, g: (b, g) + (0,) * len(shape))
    bg_shape = lambda shape, dt: jax.ShapeDtypeStruct((BATCH, KV_GROUPS) + shape, dt)
    return pl.pallas_call(
        _kprep_kernel,
        grid=(BATCH, KV_GROUPS),
        in_specs=[
            slab(0), slab(1), slab(2),
            pl.BlockSpec((SEQ, GATE_PAD), lambda b, g: (b, 0)),
            full((1, LANES)), full((1, LANES)), full((CMP_BLOCK, LANES)),
            full((CMP_BLOCK, LANES, 2 * CMP_HIDDEN)), full((2, CMP_HIDDEN, LANES)),
        ],
        out_specs=[
            per_bg((N_KB, K_BLOCK, LANES)),
            per_bg((N_KB, K_BLOCK, LANES)),
            per_bg((N_KB, 2 * VT_ROWS, K_BLOCK)),
            per_bg((N_CMP_PAD, LANES)),
            per_bg((HEAD_DIM, N_CMP_PAD)),
            pl.BlockSpec((None, N_KB, GATE_PAD, K_BLOCK), lambda b, g: (b, 0, 0, 0)),
        ],
        out_shape=[
            bg_shape((N_KB, K_BLOCK, LANES), BF16),
            bg_shape((N_KB, K_BLOCK, LANES), BF16),
            bg_shape((N_KB, 2 * VT_ROWS, K_BLOCK), BF16),
            bg_shape((N_CMP_PAD, LANES), BF16),
            bg_shape((HEAD_DIM, N_CMP_PAD), BF16),
            jax.ShapeDtypeStruct((BATCH, N_KB, GATE_PAD, K_BLOCK), F32),
        ],
        scratch_shapes=[
            pltpu.VMEM((N_CMP_PAD, 2 * CMP_HIDDEN), F32),
            pltpu.VMEM((N_CMP_PAD + 8, 2 * CMP_HIDDEN), F32),
        ],
        compiler_params=_cparams("arbitrary", "arbitrary"),
        name="kprep",
    )(proj, proj, proj, gl, knw2, kcw, pos2, w1p, w2p)


def _attn_kernel(q_ref, ksel_ref, kwin_ref, vt_ref, kc_ref, vct_ref, gt_ref, sf_ref, wm_ref, dm_ref,
                 ov_ref, wq_ref, o_ref, sel_ref, qs_ref, sa_ref, sb_ref, tiles_ref):
    qt = pl.program_id(2)
    t_i = qt * Q_TILE + lax.broadcasted_iota(jnp.int32, (1, Q_TILE), 1)
    head = lambda a, r: a[:, r * Q_TILE:(r + 1) * Q_TILE]
    heads = lambda parts: jnp.concatenate(parts, axis=1)

    q_t = q_ref[...].T
    parts = []
    for r in range(REP):
        xq = q_t[r * HEAD_DIM:(r + 1) * HEAD_DIM, :]
        ms = jnp.sum(xq * xq, axis=0, keepdims=True) * (1.0 / HEAD_DIM)
        parts.append(xq * lax.rsqrt(ms + NORM_EPS) * wq_ref[...] * (HEAD_DIM ** -0.5) * LOG2E)
    qn = heads(parts)
    sf = sf_ref[...]
    sf = jnp.where(lax.broadcasted_iota(jnp.int32, sf.shape, 0) == 6, sf * (qt * Q_TILE).astype(F32), sf)
    wide = REP * Q_TILE
    pad = jnp.zeros((HEAD_DIM - 8, wide), F32)
    qs_ref[...] = jnp.concatenate([qn, sf, pad], axis=0).astype(BF16)
    q_win = jnp.concatenate([sf, pad, qn], axis=0).astype(BF16)

    sc = _dot(kc_ref[...], qs_ref[...])
    ci = lax.broadcasted_iota(jnp.int32, (N_CMP_PAD, Q_TILE), 0)
    cm = jnp.where(ci * CMP_STRIDE + (CMP_BLOCK - 1) <= t_i, 0.0, NEG_BIG)
    any_valid = t_i >= CMP_BLOCK - 1
    pcs = []
    for r in range(REP):
        s = head(sc, r) + cm
        e = jnp.exp2(s - jnp.max(s, axis=0, keepdims=True))
        d = jnp.sum(e, axis=0, keepdims=True)
        pcs.append((e * jnp.where(any_valid, 1.0 / d, 0.0)).astype(BF16))
    oc_imp = _dot(jnp.concatenate([vct_ref[...], ov_ref[...]], axis=0), heads(pcs))
    o_c = oc_imp[0:HEAD_DIM]
    imp4 = oc_imp[HEAD_DIM:]
    imp = (head(imp4, 0) + head(imp4, 1)) + (head(imp4, 2) + head(imp4, 3))

    jj = lax.broadcasted_iota(jnp.int32, (N_SEL, Q_TILE), 0)
    cur = lax.shift_right_logical(t_i, 6)
    valid = jj <= cur
    sel_ref[...] = jnp.where(valid, 1.0, 0.0)
    forced = (jj == 0) | (jj == cur) | (jj == cur - 1)
    score = jnp.where(valid, jnp.where(forced, SEL_BIG, imp), -SEL_BIG)

    def rank_rows(n8):
        def branch():
            rows8 = [score[8 * v:8 * v + 8, :] for v in range(n8)]
            j8 = lax.broadcasted_iota(jnp.int32, (8, Q_TILE), 0)
            ranks = [jnp.zeros((8, Q_TILE), F32) for _ in rows8]
            for jp in range(8 * n8):
                row = score[jp:jp + 1, :]
                for v, blk in enumerate(rows8):
                    if 8 * v > jp:
                        ahead = jnp.where(row >= blk, 1.0, 0.0)
                    elif 8 * v + 7 <= jp:
                        ahead = jnp.where(row > blk, 1.0, 0.0)
                    else:
                        ahead = jnp.where(j8 + 8 * v > jp, jnp.where(row >= blk, 1.0, 0.0),
                                          jnp.where(row > blk, 1.0, 0.0))
                    ranks[v] = ranks[v] + ahead
            rank = jnp.concatenate(ranks, axis=0)
            sel_ref[0:8 * n8, :] = jnp.where(valid[0:8 * n8] & (rank < SEL_TOPK), 1.0, 0.0)
        return branch

    n8_needed = (qt * Q_TILE + Q_TILE - 1) // (8 * SEL_BLOCK) + 1
    lax.switch(n8_needed - 1, [(lambda: None) if n8 * 8 <= SEL_TOPK else rank_rows(n8)
                               for n8 in range(1, N_SEL // 8 + 1)])

    sub = SEL_TILE_KEYS // SEL_BLOCK

    def scores_into(kt, s_ref):
        member = sel_ref[pl.ds(kt * sub, sub), :]
        mrows = (member - 1.0) * (-NEG_BIG)
        qs_ref[MASK_LANE0:MASK_LANE0 + 16, :] = jnp.concatenate(
            [heads([mrows] * REP), jnp.zeros((16 - sub, wide), F32)], axis=0).astype(BF16)
        k_tile = ksel_ref[pl.ds(kt * SEL_TILE_BLOCKS, SEL_TILE_BLOCKS)].reshape(SEL_TILE_KEYS, LANES)
        s_ref[...] = _dot(k_tile, qs_ref[...])

    def absorb(kt, s_ref, carry, diagonal):
        m_run, acc = carry
        kb0 = kt * SEL_TILE_BLOCKS
        if diagonal:
            causal = dm_ref[qt % (SEL_TILE_KEYS // Q_TILE)]
        v_t = jnp.concatenate([vt_ref[kb0 + x, 0:VT_ROWS, :] for x in range(SEL_TILE_BLOCKS)], axis=1)
        ms_, accs = [], []
        for r in range(REP):
            sr = s_ref[:, r * Q_TILE:(r + 1) * Q_TILE]
            if diagonal:
                sr = sr + causal
            m_old = head(m_run, r)
            m_new = jnp.maximum(m_old, jnp.max(sr, axis=0, keepdims=True))
            p = jnp.exp2(sr - m_new).astype(BF16)
            accs.append(head(acc, r) * jnp.exp2(m_old - m_new) + _dot(v_t, p))
            ms_.append(m_new)
        return heads(ms_), heads(accs)

    carry = (jnp.full((1, wide), NEG_BIG, F32), jnp.zeros((VT_ROWS, wide), F32))
    n_full = (qt * Q_TILE) // SEL_TILE_KEYS

    n_sel_tiles = SEQ // SEL_TILE_KEYS
    tr = lax.broadcasted_iota(jnp.int32, (n_sel_tiles, N_SEL), 0)
    tc = lax.broadcasted_iota(jnp.int32, (n_sel_tiles, N_SEL), 1)
    tile_of_block = jnp.where(lax.shift_right_logical(tc, 3) == tr, 1.0, 0.0).astype(BF16)
    per_query = _dot(tile_of_block, sel_ref[...].astype(BF16))
    per_tile = _dot(per_query.astype(BF16), jnp.ones((Q_TILE, LANES), BF16))
    bit = lax.shift_left(1, lax.broadcasted_iota(jnp.int32, (n_sel_tiles, LANES), 0)).astype(F32)
    used_bits = jnp.sum(jnp.where(per_tile > 0.5, bit, 0.0), axis=0, keepdims=True)[0, 0].astype(jnp.int32)

    kb_first = qt * (Q_TILE // K_BLOCK) - WINDOW // K_BLOCK
    kbs = [jnp.maximum(kb_first + x, 0) for x in range(WIN_BLOCKS)]
    sw = _dot(jnp.concatenate([kwin_ref[kb] for kb in kbs], axis=0), q_win)

    n_visit = jnp.int32(0)
    for kt in range(n_sel_tiles):
        tiles_ref[n_visit] = kt
        take = (kt < n_full) & ((lax.shift_right_logical(used_bits, kt) & 1) == 1)
        n_visit = n_visit + jnp.where(take, 1, 0)
    tiles_ref[n_visit] = n_full

    scores_into(tiles_ref[0], sa_ref)
    wm = jnp.concatenate(
        [wm_ref[x * K_BLOCK:(x + 1) * K_BLOCK, :] + jnp.where(kb_first + x < 0, NEG_BIG, 0.0)
         for x in range(WIN_BLOCKS)], axis=0)
    pws = []
    for r in range(REP):
        s = head(sw, r) + wm
        pws.append(jnp.exp2(s - jnp.max(s, axis=0, keepdims=True)).astype(BF16))
    vw_t = jnp.concatenate([vt_ref[kb, VT_ROWS:2 * VT_ROWS, :] for kb in kbs], axis=1)
    acc_w = _dot(vw_t, heads(pws))
    o_w = acc_w[0:HEAD_DIM] / acc_w[HEAD_DIM:HEAD_DIM + 1]

    def tile_pair(i, carry):
        scores_into(tiles_ref[2 * i + 1], sb_ref)
        carry = absorb(tiles_ref[2 * i], sa_ref, carry, False)
        scores_into(tiles_ref[2 * i + 2], sa_ref)
        return absorb(tiles_ref[2 * i + 1], sb_ref, carry, False)

    carry = lax.fori_loop(0, n_visit // 2, tile_pair, carry)
    last = 2 * (n_visit // 2)

    def odd_tail(carry):
        scores_into(n_full, sb_ref)
        return absorb(n_full, sb_ref, absorb(tiles_ref[last], sa_ref, carry, False), True)

    def even_tail(carry):
        return absorb(n_full, sa_ref, carry, True)

    _, acc_s = lax.cond(n_visit % 2 == 1, odd_tail, even_tail, carry)
    o_s = acc_s[0:HEAD_DIM] / acc_s[HEAD_DIM:HEAD_DIM + 1]

    gates = jnp.concatenate([gt_ref[x] for x in range(Q_TILE // K_BLOCK)], axis=1)
    outs = []
    for r in range(REP):
        outs.append(gates[r:r + 1, :] * head(o_c, r)
                    + gates[REP + r:REP + r + 1, :] * head(o_s, r)
                    + gates[2 * REP + r:2 * REP + r + 1, :] * head(o_w, r))
    o_ref[...] = jnp.concatenate(outs, axis=0).T.astype(BF16)


def _attn(proj, ksel, kwin, vt, kc, vct, gt, sf, wm, dm, ov_t, wq_b):
    q0 = Z_COLS // (REP * HEAD_DIM)
    kvspec = lambda shape: pl.BlockSpec((None, None) + shape, lambda b, g, qt: (b, g) + (0,) * len(shape))
    const = lambda shape: pl.BlockSpec(shape, lambda b, g, qt: (0,) * len(shape))
    return pl.pallas_call(
        _attn_kernel,
        grid=(BATCH, KV_GROUPS, N_QT),
        in_specs=[
            pl.BlockSpec((Q_TILE, REP * HEAD_DIM), lambda b, g, qt: (b * N_QT + qt, q0 + g)),
            kvspec((N_KB, K_BLOCK, LANES)),
            kvspec((N_KB, K_BLOCK, LANES)),
            kvspec((N_KB, 2 * VT_ROWS, K_BLOCK)),
            kvspec((N_CMP_PAD, LANES)),
            kvspec((HEAD_DIM, N_CMP_PAD)),
            pl.BlockSpec((None, Q_TILE // K_BLOCK, GATE_ROWS_PER_GROUP, K_BLOCK), lambda b, g, qt: (b, qt, g, 0)),
            pl.BlockSpec((None, 8, REP * Q_TILE), lambda b, g, qt: (g, 0, 0)),
            const((WIN_KEYS, Q_TILE)),
            const((SEL_TILE_KEYS // Q_TILE, SEL_TILE_KEYS, Q_TILE)),
            const((N_SEL, N_CMP_PAD)),
            const((HEAD_DIM, Q_TILE)),
        ],
        out_specs=pl.BlockSpec((Q_TILE, REP * HEAD_DIM), lambda b, g, qt: (b * N_QT + qt, g)),
        out_shape=jax.ShapeDtypeStruct((N_TOK, NSA_WIDTH), BF16),
        scratch_shapes=[pltpu.VMEM((N_SEL, Q_TILE), F32), pltpu.VMEM((LANES, REP * Q_TILE), BF16),
                        pltpu.VMEM((SEL_TILE_KEYS, REP * Q_TILE), F32),
                        pltpu.VMEM((SEL_TILE_KEYS, REP * Q_TILE), F32),
                        pltpu.SMEM((SEQ // SEL_TILE_KEYS + 1,), jnp.int32)],
        compiler_params=_cparams("arbitrary", "arbitrary", "arbitrary"),
        name="attn",
    )(proj, ksel, kwin, vt, kc, vct, gt, sf, wm, dm, ov_t, wq_b)


def _gmlp_kernel(z_ref, lnw_ref, lnb_ref, sw_ref, sbx_ref, o_ref):
    ge = jax.nn.gelu(z_ref[...])
    u = ge[:, :GMLP_WIDTH]
    v = ge[:, GMLP_WIDTH:]
    mu = jnp.mean(v, axis=-1, keepdims=True)
    var = jnp.mean(jnp.square(v - mu), axis=-1, keepdims=True)
    vn = ((v - mu) * lax.rsqrt(var + LN_EPS) * lnw_ref[...] + lnb_ref[...]).astype(BF16)
    ti = lax.broadcasted_iota(jnp.int32, (GMLP_CHUNK, GMLP_CHUNK), 0)
    si = lax.broadcasted_iota(jnp.int32, (GMLP_CHUNK, GMLP_CHUNK), 1)
    ws = [jnp.where(ti >= si, sw_ref[gg], 0.0).astype(BF16) for gg in range(GMLP_GROUPS)]
    gd = GMLP_WIDTH // GMLP_GROUPS
    for c in range(GMLP_TM // GMLP_CHUNK):
        rows = slice(c * GMLP_CHUNK, (c + 1) * GMLP_CHUNK)
        mix = jnp.concatenate(
            [_dot(ws[gg], vn[rows, gg * gd:(gg + 1) * gd]) for gg in range(GMLP_GROUPS)], axis=1)
        o_ref[rows, :] = (u[rows, :] * (mix + sbx_ref[...])).astype(BF16)


def _gmlp(proj, lnw, lnb, sw, sbx):
    return pl.pallas_call(
        _gmlp_kernel,
        grid=(N_TOK // GMLP_TM,),
        in_specs=[
            pl.BlockSpec((GMLP_TM, Z_COLS), lambda i: (i, 0)),
            pl.BlockSpec((1, GMLP_WIDTH), lambda i: (0, 0)),
            pl.BlockSpec((1, GMLP_WIDTH), lambda i: (0, 0)),
            pl.BlockSpec((GMLP_GROUPS, GMLP_CHUNK, GMLP_CHUNK), lambda i: (0, 0, 0)),
            pl.BlockSpec((GMLP_CHUNK, GMLP_WIDTH), lambda i: (0, 0)),
        ],
        out_specs=pl.BlockSpec((GMLP_TM, GMLP_WIDTH), lambda i: (i, 0)),
        out_shape=jax.ShapeDtypeStruct((N_TOK, GMLP_WIDTH), BF16),
        compiler_params=_cparams("arbitrary"),
        name="gmlp",
    )(proj, lnw, lnb, sw, sbx)


def _oproj_kernel(a_ref, b_ref, x_ref, wa_ref, wb_ref, nw_ref, x1_ref, xn_ref):
    y = x_ref[...] + (_dot(a_ref[...], wa_ref[...]) + _dot(b_ref[...], wb_ref[...]))
    x1_ref[...] = y
    ms = jnp.mean(y * y, axis=-1, keepdims=True)
    xn_ref[...] = (y * lax.rsqrt(ms + NORM_EPS) * nw_ref[...]).astype(BF16)


def _oproj(a, b, x2, w_out, nw):
    return pl.pallas_call(
        _oproj_kernel,
        grid=(N_TOK // OPROJ_TM,),
        in_specs=[
            pl.BlockSpec((OPROJ_TM, NSA_WIDTH), lambda i: (i, 0)),
            pl.BlockSpec((OPROJ_TM, GMLP_WIDTH), lambda i: (i, 0)),
            pl.BlockSpec((OPROJ_TM, D_MODEL), lambda i: (i, 0)),
            pl.BlockSpec((NSA_WIDTH, D_MODEL), lambda i: (0, 0)),
            pl.BlockSpec((GMLP_WIDTH, D_MODEL), lambda i: (1, 0)),
            pl.BlockSpec((1, D_MODEL), lambda i: (0, 0)),
        ],
        out_specs=[
            pl.BlockSpec((OPROJ_TM, D_MODEL), lambda i: (i, 0)),
            pl.BlockSpec((OPROJ_TM, D_MODEL), lambda i: (i, 0)),
        ],
        out_shape=[
            jax.ShapeDtypeStruct((N_TOK, D_MODEL), F32),
            jax.ShapeDtypeStruct((N_TOK, D_MODEL), BF16),
        ],
        compiler_params=_cparams("arbitrary"),
        name="oproj",
    )(a, b, x2, w_out, w_out, nw)


def _causal_conv(h, prev, cw, cb):
    rid = lax.broadcasted_iota(jnp.int32, prev.shape, 0)

    def shifted(k):
        body = pltpu.roll(h, k, 0)
        top = jnp.where(rid < k, pltpu.roll(prev, k, 0), body[0:8, :])
        return jnp.concatenate([top, body[8:, :]], axis=0)

    return cb + ((cw[0:1, :] * shifted(2) + cw[1:2, :] * shifted(1)) + cw[2:3, :] * h)


def _ffn_kernel(xn_ref, x1_ref, wg_ref, wu_ref, cwg_ref, cwu_ref, cbg_ref, cbu_ref, wd_ref,
                cwg_p_ref, cwu_p_ref, cbg_p_ref, cbu_p_ref, wd_p_ref, o_ref, h0_ref, h1_ref, carry_ref):
    i = pl.program_id(0)
    q = pl.program_id(1)
    nq = pl.num_programs(1)
    half = FFN_TF // 2

    @pl.when((i == 0) & (q == 0))
    def _():
        carry_ref[...] = jnp.zeros_like(carry_ref)
        h1_ref[...] = jnp.zeros_like(h1_ref)

    @pl.when(q == 0)
    def _():
        o_ref[...] = x1_ref[...]

    seq_start = i % (SEQ // FFN_TM) == 0
    xn = xn_ref[...]

    def produce(c, h_ref):
        h_ref[0, 8:, :] = _dot(xn, wg_ref[:, c * half:(c + 1) * half])
        h_ref[1, 8:, :] = _dot(xn, wu_ref[:, c * half:(c + 1) * half])

    def conv(h_ref, part, tile, cw, cb):
        h_ref[part, 0:8, :] = jnp.where(seq_start, 0.0, carry_ref[tile, part])
        carry_ref[tile, part] = h_ref[part, FFN_TM:FFN_TM + 8, :]
        taps = [h_ref[part, 8 - k:8 - k + FFN_TM, :] for k in (2, 1, 0)]
        return cb + ((cw[0:1, :] * taps[0] + cw[1:2, :] * taps[1]) + cw[2:3, :] * taps[2])

    def gate(h_ref, tile, c, cwg, cwu, cbg, cbu, scale=None):
        cols = slice(c * half, (c + 1) * half)
        cg = conv(h_ref, 0, tile, cwg[:, cols], cbg[:, cols])
        cu = conv(h_ref, 1, tile, cwu[:, cols], cbu[:, cols])
        act = jax.nn.silu(cg) * cu
        if scale is not None:
            act = act * scale
        return act.astype(BF16)

    act_prev = gate(h1_ref, jnp.where(q == 0, 2 * nq - 1, 2 * q - 1), 1, cwg_p_ref, cwu_p_ref, cbg_p_ref,
                    cbu_p_ref, scale=jnp.where(q == 0, 0.0, 1.0))
    produce(0, h0_ref)
    down_prev = _dot(act_prev, wd_p_ref[half:, :])
    act_cur = gate(h0_ref, 2 * q, 0, cwg_ref, cwu_ref, cbg_ref, cbu_ref)
    produce(1, h1_ref)
    o_ref[...] += down_prev + _dot(act_cur, wd_ref[0:half, :])

    @pl.when(q == nq - 1)
    def _():
        act_last = gate(h1_ref, 2 * q + 1, 1, cwg_ref, cwu_ref, cbg_ref, cbu_ref)
        o_ref[...] += _dot(act_last, wd_ref[half:, :])


def _ffn(xn, x1, w_up, conv_w, conv_b, w_down):
    nj = D_FF // FFN_TF
    prev = lambda q: jnp.maximum(q - 1, 0)
    return pl.pallas_call(
        _ffn_kernel,
        grid=(N_TOK // FFN_TM, nj),
        in_specs=[
            pl.BlockSpec((FFN_TM, D_MODEL), lambda i, q: (i, 0)),
            pl.BlockSpec((FFN_TM, D_MODEL), lambda i, q: (i, 0)),
            pl.BlockSpec((D_MODEL, FFN_TF), lambda i, q: (0, q)),
            pl.BlockSpec((D_MODEL, FFN_TF), lambda i, q: (0, nj + q)),
            pl.BlockSpec((CONV_WIDTH, FFN_TF), lambda i, q: (0, q)),
            pl.BlockSpec((CONV_WIDTH, FFN_TF), lambda i, q: (0, nj + q)),
            pl.BlockSpec((1, FFN_TF), lambda i, q: (0, q)),
            pl.BlockSpec((1, FFN_TF), lambda i, q: (0, nj + q)),
            pl.BlockSpec((FFN_TF, D_MODEL), lambda i, q: (q, 0)),
            pl.BlockSpec((CONV_WIDTH, FFN_TF), lambda i, q: (0, prev(q))),
            pl.BlockSpec((CONV_WIDTH, FFN_TF), lambda i, q: (0, nj + prev(q))),
            pl.BlockSpec((1, FFN_TF), lambda i, q: (0, prev(q))),
            pl.BlockSpec((1, FFN_TF), lambda i, q: (0, nj + prev(q))),
            pl.BlockSpec((FFN_TF, D_MODEL), lambda i, q: (prev(q), 0)),
        ],
        out_specs=pl.BlockSpec((FFN_TM, D_MODEL), lambda i, q: (i, 0)),
        out_shape=jax.ShapeDtypeStruct((N_TOK, D_MODEL), F32),
        scratch_shapes=[pltpu.VMEM((2, FFN_TM + 8, FFN_TF // 2), F32), pltpu.VMEM((2, FFN_TM + 8, FFN_TF // 2), F32),
                        pltpu.VMEM((2 * nj, 2, 8, FFN_TF // 2), F32)],
        compiler_params=_cparams("arbitrary", "arbitrary"),
        name="ffn",
    )(xn, x1, w_up, w_up, conv_w, conv_w, conv_b, conv_b, w_down, conv_w, conv_w, conv_b, conv_b, w_down)


def _overlap_t():
    start = np.arange(N_CMP_PAD)[None, :] * CMP_STRIDE
    s0 = np.arange(N_SEL)[:, None] * SEL_BLOCK
    ov = (start < s0 + SEL_BLOCK) & (start + CMP_BLOCK > s0) & (np.arange(N_CMP_PAD)[None, :] < N_CMP_PAD - 1)
    return ov.astype(np.float32)


def _bf16_round(a):
    return np.asarray(a, np.float32).astype(BF16).astype(np.float32)


def _slope_features():
    sl = (np.power(2.0, -8.0 * np.arange(1, NSA_HEADS + 1) / NSA_HEADS).astype(np.float32)
          * np.float32(LOG2E)).astype(np.float32)
    s1 = _bf16_round(sl)
    s2 = _bf16_round(sl - s1)
    s3 = _bf16_round(sl - s1 - s2)
    rows = np.stack([s1, s2, s3, s1, s2, s3, -sl, np.zeros_like(sl)], axis=0)
    rows = rows.reshape(8, KV_GROUPS, REP).transpose(1, 0, 2)
    return np.repeat(rows, Q_TILE, axis=2).astype(np.float32)


def _window_mask():
    kl = np.arange(WIN_KEYS)[:, None]
    ql = np.arange(Q_TILE)[None, :]
    dist = ql + WINDOW - kl
    return np.where((dist >= 0) & (dist < WINDOW), 0.0, NEG_BIG).astype(np.float32)


def _diag_masks():
    kl = np.arange(SEL_TILE_KEYS)[None, :, None]
    ql = np.arange(Q_TILE)[None, None, :]
    off = np.arange(SEL_TILE_KEYS // Q_TILE)[:, None, None] * Q_TILE
    return np.where(kl > ql + off, NEG_BIG, 0.0).astype(np.float32)


def _layout_w_in(w_in):
    q_end = NSA_WIDTH
    kv_end = q_end + 6 * KV_COLS
    g_end = kv_end + NSA_HEADS * N_BRANCH
    w_in = w_in.astype(BF16)
    w_q = w_in[:, :q_end]
    w_kv = w_in[:, q_end:kv_end].reshape(D_MODEL, 6, KV_GROUPS, HEAD_DIM)
    w_kv = jnp.stack([w_kv[:, s] for s in (2, 4, 0, 1, 3, 5)], axis=2).reshape(D_MODEL, 6 * KV_COLS)
    w_z = w_in[:, g_end:]
    w_main = jnp.concatenate([w_z, w_q, w_kv], axis=1)
    w_g = w_in[:, kv_end:g_end].reshape(D_MODEL, KV_GROUPS, REP, N_BRANCH).transpose(0, 1, 3, 2)
    w_g = w_g.reshape(D_MODEL, KV_GROUPS, N_BRANCH * REP)
    w_g = jnp.pad(w_g, ((0, 0), (0, 0), (0, GATE_ROWS_PER_GROUP - N_BRANCH * REP)))
    w_g = jnp.pad(w_g.reshape(D_MODEL, KV_GROUPS * GATE_ROWS_PER_GROUP),
                  ((0, 0), (0, GATE_PAD - KV_GROUPS * GATE_ROWS_PER_GROUP)))
    return w_main, w_g


def _layout_compress(cmp_pos, cmp_w1, cmp_w2):
    pos2 = jnp.concatenate([cmp_pos[0], cmp_pos[1]], axis=1)
    w1 = cmp_w1.reshape(2, CMP_BLOCK, HEAD_DIM, CMP_HIDDEN)
    zero = jnp.zeros_like(w1[0])
    w1k = jnp.concatenate([w1[0], zero], axis=1)
    w1v = jnp.concatenate([zero, w1[1]], axis=1)
    w1p = jnp.concatenate([w1k, w1v], axis=2).astype(BF16)
    w2p = jnp.pad(cmp_w2, ((0, 0), (0, 0), (0, LANES - HEAD_DIM))).astype(BF16)
    return pos2, w1p, w2p


def kernel(x, attn_norm_w, w_in, q_norm_w, k_norm_w, cmp_pos, cmp_w1, cmp_w2, gmlp_ln_w, gmlp_ln_b,
           spatial_w, spatial_b, w_out, ffn_norm_w, w_up, conv_w, conv_b, w_down):
    x2 = x.reshape(N_TOK, D_MODEL)
    w_main, w_gate = _layout_w_in(w_in)
    proj, gl = _proj(x2, attn_norm_w.reshape(1, D_MODEL), w_main, w_gate)

    knw2 = jnp.concatenate([k_norm_w[1], k_norm_w[2]]).reshape(1, LANES)
    kcw = jnp.concatenate([k_norm_w[0], jnp.zeros((LANES - HEAD_DIM,), F32)]).reshape(1, LANES)
    pos2, w1p, w2p = _layout_compress(cmp_pos, cmp_w1, cmp_w2)
    ksel, kwin, vt, kc, vct, gt = _kprep(proj, gl, knw2, kcw, pos2, w1p, w2p)

    wq_b = jnp.broadcast_to(q_norm_w.reshape(HEAD_DIM, 1), (HEAD_DIM, Q_TILE))
    a = _attn(proj, ksel, kwin, vt, kc, vct, gt, jnp.asarray(_slope_features()),
              jnp.asarray(_window_mask()), jnp.asarray(_diag_masks()),
              jnp.asarray(_overlap_t(), dtype=BF16), wq_b)

    sbx = jnp.repeat(spatial_b.T, GMLP_WIDTH // GMLP_GROUPS, axis=1)
    b = _gmlp(proj, gmlp_ln_w.reshape(1, GMLP_WIDTH), gmlp_ln_b.reshape(1, GMLP_WIDTH), spatial_w, sbx)

    x1, xn = _oproj(a, b, x2, w_out.astype(BF16), ffn_norm_w.reshape(1, D_MODEL))

    out = _ffn(xn, x1, w_up.astype(BF16), conv_w, conv_b.reshape(1, 2 * D_FF), w_down.astype(BF16))
    return out.reshape(BATCH, SEQ, D_MODEL)
```

```python
import numpy as np
import jax
import jax.numpy as jnp
from jax import lax
from jax.experimental import pallas as pl
from jax.experimental.pallas import tpu as pltpu

F32 = jnp.float32
BF16 = jnp.bfloat16

D_MODEL = 2048
BATCH = 4
SEQ = 4096
N_TOK = BATCH * SEQ
NSA_HEADS = 16
KV_GROUPS = 4
REP = NSA_HEADS // KV_GROUPS
HEAD_DIM = 64
NSA_WIDTH = NSA_HEADS * HEAD_DIM
GMLP_WIDTH = D_MODEL - NSA_WIDTH
GMLP_GROUPS = 8
GMLP_CHUNK = 128
CMP_BLOCK = 32
CMP_STRIDE = 16
CMP_HIDDEN = 256
N_CMP_PAD = SEQ // CMP_STRIDE
SEL_BLOCK = 64
N_SEL = SEQ // SEL_BLOCK
SEL_TOPK = 16
WINDOW = 512
K_BLOCK = 128
N_KB = SEQ // K_BLOCK
Q_TILE = 256
N_QT = SEQ // Q_TILE
N_BRANCH = 3
KV_COLS = KV_GROUPS * HEAD_DIM
D_FF = 5632
CONV_WIDTH = 3
NORM_EPS = 1e-6
LN_EPS = 1e-5
NEG_BIG = -1e30
SEL_BIG = 1e9

LANES = 128
VMEM_LIMIT_BYTES = 56 * 1024 * 1024

Z_COLS = 2 * GMLP_WIDTH
PROJ_COLS = Z_COLS + NSA_WIDTH + 6 * KV_COLS
GATE_PAD = LANES
GATE_ROWS_PER_GROUP = 16

PROJ_TM = 1024
PROJ_TN = 1536
OPROJ_TM = 512
FFN_TM = 512
FFN_TF = 512
SEL_TILE_BLOCKS = 4
SEL_TILE_KEYS = SEL_TILE_BLOCKS * K_BLOCK
MASK_LANE0 = 80
SUM_ROWS = 16
VT_ROWS = HEAD_DIM + SUM_ROWS
LOG2E = 1.4426950408889634
WIN_BLOCKS = (WINDOW + Q_TILE) // K_BLOCK
WIN_KEYS = WIN_BLOCKS * K_BLOCK


def _cparams(*sem):
    return pltpu.CompilerParams(dimension_semantics=sem, vmem_limit_bytes=VMEM_LIMIT_BYTES)


def _dot(a, b):
    return jnp.dot(a, b, preferred_element_type=F32)


def _proj_kernel(x_ref, nw_ref, w_ref, wg_ref, o_ref, og_ref, h_ref):
    @pl.when(pl.program_id(1) == 0)
    def _():
        x = x_ref[...]
        ms = jnp.mean(x * x, axis=-1, keepdims=True)
        h = (x * lax.rsqrt(ms + NORM_EPS) * nw_ref[...]).astype(BF16)
        h_ref[...] = h
        og_ref[...] = _dot(h, wg_ref[...])

    o_ref[...] = _dot(h_ref[...], w_ref[...])


def _proj(x2, nw, w_main, w_gate):
    return pl.pallas_call(
        _proj_kernel,
        grid=(N_TOK // PROJ_TM, PROJ_COLS // PROJ_TN),
        in_specs=[
            pl.BlockSpec((PROJ_TM, D_MODEL), lambda i, j: (i, 0)),
            pl.BlockSpec((1, D_MODEL), lambda i, j: (0, 0)),
            pl.BlockSpec((D_MODEL, PROJ_TN), lambda i, j: (0, j)),
            pl.BlockSpec((D_MODEL, GATE_PAD), lambda i, j: (0, 0)),
        ],
        out_specs=[
            pl.BlockSpec((PROJ_TM, PROJ_TN), lambda i, j: (i, j)),
            pl.BlockSpec((PROJ_TM, GATE_PAD), lambda i, j: (i, 0)),
        ],
        out_shape=[
            jax.ShapeDtypeStruct((N_TOK, PROJ_COLS), F32),
            jax.ShapeDtypeStruct((N_TOK, GATE_PAD), F32),
        ],
        scratch_shapes=[pltpu.VMEM((PROJ_TM, D_MODEL), BF16)],
        compiler_params=_cparams("arbitrary", "arbitrary"),
        name="proj",
    )(x2, nw, w_main, w_gate)


def _pos_features(lane, first, f1, f2):
    return jnp.where(lane < first + 3, f1, jnp.where(lane < first + 6, f2,
                                                     jnp.where(lane == first + 6, 1.0, 0.0)))


def _kprep_kernel(s0_ref, s1_ref, s2_ref, gl_ref, knw_ref, kcw_ref, pos_ref, w1_ref, w2_ref,
                  ksel_ref, kwin_ref, vt_ref, kc_ref, vct_ref, gt_ref, ha_ref, hb_ref):
    lane = lax.broadcasted_iota(jnp.int32, (K_BLOCK, LANES), 1)
    row = lax.broadcasted_iota(jnp.int32, (K_BLOCK, LANES), 0)
    lo = lane < HEAD_DIM
    inv_d = 1.0 / HEAD_DIM
    ones_rows = jnp.where(lax.broadcasted_iota(jnp.int32, (SUM_ROWS, K_BLOCK), 0) == 0, 1.0, 0.0)

    def chunk(c, carry):
        r0 = pl.multiple_of(c * K_BLOCK, K_BLOCK)
        x = s0_ref[pl.ds(r0, K_BLOCK), :]
        sq = x * x
        s_lo = jnp.sum(jnp.where(lo, sq, 0.0), axis=1, keepdims=True)
        s_hi = jnp.sum(jnp.where(lo, 0.0, sq), axis=1, keepdims=True)
        ms = jnp.where(lo, s_lo, s_hi) * inv_d
        y = x * lax.rsqrt(ms + NORM_EPS) * knw_ref[...]
        pos = r0 + row
        blk = lax.shift_right_logical(pos, 6)
        f1 = (blk * SEL_BLOCK).astype(F32)
        f2 = (pos & (SEL_BLOCK - 1)).astype(F32)
        onehot = jnp.where(lane - MASK_LANE0 == (blk & (SEL_TILE_BLOCKS * 2 - 1)), 1.0, 0.0)
        feat_sel = jnp.where(lane < MASK_LANE0, _pos_features(lane, HEAD_DIM, f1, f2), onehot)
        ksel_ref[c] = jnp.where(lo, y, feat_sel).astype(BF16)
        kwin_ref[c] = jnp.where(lo, _pos_features(lane, 0, f1, f2), y).astype(BF16)
        v_t = s2_ref[pl.ds(r0, K_BLOCK), :].T
        vt_ref[c] = jnp.concatenate(
            [v_t[0:HEAD_DIM], ones_rows, v_t[HEAD_DIM:], ones_rows], axis=0).astype(BF16)
        return carry

    lax.fori_loop(0, N_KB, chunk, 0, unroll=2)

    @pl.when(pl.program_id(1) == 0)
    def _():
        def gchunk(c, carry):
            r0 = pl.multiple_of(c * K_BLOCK, K_BLOCK)
            gt_ref[c] = jax.nn.sigmoid(gl_ref[pl.ds(r0, K_BLOCK), :]).T
            return carry

        lax.fori_loop(0, N_KB, gchunk, 0, unroll=2)

    ha_ref[...] = jnp.zeros_like(ha_ref)
    hb_ref[...] = jnp.zeros_like(hb_ref)
    for tp in range(CMP_STRIDE // 2):
        rows = [s1_ref[pl.ds(2 * tp + e, N_CMP_PAD, stride=CMP_STRIDE), :] for e in range(2)]
        xa = jnp.concatenate([(rows[e] + pos_ref[2 * tp + e:2 * tp + e + 1, :]).astype(BF16)
                              for e in range(2)], axis=1)
        xb = jnp.concatenate([(rows[e] + pos_ref[CMP_STRIDE + 2 * tp + e:CMP_STRIDE + 2 * tp + e + 1, :])
                              .astype(BF16) for e in range(2)], axis=1)
        ha_ref[...] += _dot(xa, w1_ref[tp])
        hb_ref[0:N_CMP_PAD, :] += _dot(xb, w1_ref[CMP_STRIDE // 2 + tp])
    hid = jax.nn.gelu(ha_ref[...] + hb_ref[1:N_CMP_PAD + 1, :]).astype(BF16)
    kc = _dot(hid[:, :CMP_HIDDEN], w2_ref[0])
    vc = _dot(hid[:, CMP_HIDDEN:], w2_ref[1])
    ms = jnp.sum(kc * kc, axis=1, keepdims=True) * inv_d
    kcn = kc * lax.rsqrt(ms + NORM_EPS) * kcw_ref[...]
    ci = lax.broadcasted_iota(jnp.int32, (N_CMP_PAD, LANES), 0)
    cl = lax.broadcasted_iota(jnp.int32, (N_CMP_PAD, LANES), 1)
    c1 = (lax.shift_right_logical(ci, 2) * SEL_BLOCK).astype(F32)
    c2 = ((ci & 3) * CMP_STRIDE).astype(F32) + (CMP_BLOCK - 1) * 0.5
    kc_ref[...] = jnp.where(cl < HEAD_DIM, kcn, _pos_features(cl, HEAD_DIM, c1, c2)).astype(BF16)
    vct_ref[...] = vc.T[0:HEAD_DIM, :].astype(BF16)


def _kprep(proj, gl, knw2, kcw, pos2, w1p, w2p):
    kv0 = (Z_COLS + NSA_WIDTH) // LANES
    slab = lambda s: pl.BlockSpec((SEQ, LANES), lambda b, g, s=s: (b, kv0 + 3 * g + s))
    full = lambda shape: pl.BlockSpec(shape, lambda b, g: (0,) * len(shape))
    per_bg = lambda shape: pl.BlockSpec((None, None) + shape, lambda b, g: (b, g) + (0,) * len(shape))
    bg_shape = lambda shape, dt: jax.ShapeDtypeStruct((BATCH, KV_GROUPS) + shape, dt)
    return pl.pallas_call(
        _kprep_kernel,
        grid=(BATCH, KV_GROUPS),
        in_specs=[
            slab(0), slab(1), slab(2),
            pl.BlockSpec((SEQ, GATE_PAD), lambda b, g: (b, 0)),
            full((1, LANES)), full((1, LANES)), full((CMP_BLOCK, LANES)),
            full((CMP_BLOCK // 2, 2 * LANES, 2 * CMP_HIDDEN)), full((2, CMP_HIDDEN, LANES)),
        ],
        out_specs=[
            per_bg((N_KB, K_BLOCK, LANES)),
            per_bg((N_KB, K_BLOCK, LANES)),
            per_bg((N_KB, 2 * VT_ROWS, K_BLOCK)),
            per_bg((N_CMP_PAD, LANES)),
            per_bg((HEAD_DIM, N_CMP_PAD)),
            pl.BlockSpec((None, N_KB, GATE_PAD, K_BLOCK), lambda b, g: (b, 0, 0, 0)),
        ],
        out_shape=[
            bg_shape((N_KB, K_BLOCK, LANES), BF16),
            bg_shape((N_KB, K_BLOCK, LANES), BF16),
            bg_shape((N_KB, 2 * VT_ROWS, K_BLOCK), BF16),
            bg_shape((N_CMP_PAD, LANES), BF16),
            bg_shape((HEAD_DIM, N_CMP_PAD), BF16),
            jax.ShapeDtypeStruct((BATCH, N_KB, GATE_PAD, K_BLOCK), F32),
        ],
        scratch_shapes=[
            pltpu.VMEM((N_CMP_PAD, 2 * CMP_HIDDEN), F32),
            pltpu.VMEM((N_CMP_PAD + 8, 2 * CMP_HIDDEN), F32),
        ],
        compiler_params=_cparams("arbitrary", "arbitrary"),
        name="kprep",
    )(proj, proj, proj, gl, knw2, kcw, pos2, w1p, w2p)


def _attn_kernel(q_ref, ksel_ref, kwin_ref, vt_ref, kc_ref, vct_ref, gt_ref, sf_ref, wm_ref, dm_ref,
                 ov_ref, wq_ref, o_ref, sel_ref, qs_ref, sa_ref, sb_ref, tiles_ref):
    qt = pl.program_id(2)
    t_i = qt * Q_TILE + lax.broadcasted_iota(jnp.int32, (1, Q_TILE), 1)
    head = lambda a, r: a[:, r * Q_TILE:(r + 1) * Q_TILE]
    heads = lambda parts: jnp.concatenate(parts, axis=1)

    q_t = q_ref[...].T
    parts = []
    for r in range(REP):
        xq = q_t[r * HEAD_DIM:(r + 1) * HEAD_DIM, :]
        ms = jnp.sum(xq * xq, axis=0, keepdims=True) * (1.0 / HEAD_DIM)
        parts.append(xq * lax.rsqrt(ms + NORM_EPS) * wq_ref[...] * (HEAD_DIM ** -0.5) * LOG2E)
    qn = heads(parts)
    sf = sf_ref[...]
    sf = jnp.where(lax.broadcasted_iota(jnp.int32, sf.shape, 0) == 6, sf * (qt * Q_TILE).astype(F32), sf)
    wide = REP * Q_TILE
    pad = jnp.zeros((HEAD_DIM - 8, wide), F32)
    qs_ref[...] = jnp.concatenate([qn, sf, pad], axis=0).astype(BF16)
    q_win = jnp.concatenate([sf, pad, qn], axis=0).astype(BF16)

    sc = _dot(kc_ref[...], qs_ref[...])
    ci = lax.broadcasted_iota(jnp.int32, (N_CMP_PAD, Q_TILE), 0)
    cm = jnp.where(ci * CMP_STRIDE + (CMP_BLOCK - 1) <= t_i, 0.0, NEG_BIG)
    any_valid = t_i >= CMP_BLOCK - 1
    pcs = []
    for r in range(REP):
        s = head(sc, r) + cm
        e = jnp.exp2(s - jnp.max(s, axis=0, keepdims=True))
        d = jnp.sum(e, axis=0, keepdims=True)
        pcs.append((e * jnp.where(any_valid, 1.0 / d, 0.0)).astype(BF16))
    oc_imp = _dot(jnp.concatenate([vct_ref[...], ov_ref[...]], axis=0), heads(pcs))
    o_c = oc_imp[0:HEAD_DIM]
    imp4 = oc_imp[HEAD_DIM:]
    imp = (head(imp4, 0) + head(imp4, 1)) + (head(imp4, 2) + head(imp4, 3))

    jj = lax.broadcasted_iota(jnp.int32, (N_SEL, Q_TILE), 0)
    cur = lax.shift_right_logical(t_i, 6)
    valid = jj <= cur
    sel_ref[...] = jnp.where(valid, 1.0, 0.0)
    forced = (jj == 0) | (jj == cur) | (jj == cur - 1)
    score = jnp.where(valid, jnp.where(forced, SEL_BIG, imp), -SEL_BIG)

    def rank_rows(n8):
        def branch():
            rows8 = [score[8 * v:8 * v + 8, :] for v in range(n8)]
            j8 = lax.broadcasted_iota(jnp.int32, (8, Q_TILE), 0)
            ranks = [jnp.zeros((8, Q_TILE), F32) for _ in rows8]
            for jp in range(8 * n8):
                row = score[jp:jp + 1, :]
                for v, blk in enumerate(rows8):
                    if 8 * v > jp:
                        ahead = jnp.where(row >= blk, 1.0, 0.0)
                    elif 8 * v + 7 <= jp:
                        ahead = jnp.where(row > blk, 1.0, 0.0)
                    else:
                        ahead = jnp.where(j8 + 8 * v > jp, jnp.where(row >= blk, 1.0, 0.0),
                                          jnp.where(row > blk, 1.0, 0.0))
                    ranks[v] = ranks[v] + ahead
            rank = jnp.concatenate(ranks, axis=0)
            sel_ref[0:8 * n8, :] = jnp.where(valid[0:8 * n8] & (rank < SEL_TOPK), 1.0, 0.0)
        return branch

    n8_needed = (qt * Q_TILE + Q_TILE - 1) // (8 * SEL_BLOCK) + 1
    lax.switch(n8_needed - 1, [(lambda: None) if n8 * 8 <= SEL_TOPK else rank_rows(n8)
                               for n8 in range(1, N_SEL // 8 + 1)])

    sub = SEL_TILE_KEYS // SEL_BLOCK

    def scores_into(kt, s_ref):
        member = sel_ref[pl.ds(kt * sub, sub), :]
        mrows = (member - 1.0) * (-NEG_BIG)
        qs_ref[MASK_LANE0:MASK_LANE0 + 16, :] = jnp.concatenate(
            [heads([mrows] * REP), jnp.zeros((16 - sub, wide), F32)], axis=0).astype(BF16)
        k_tile = ksel_ref[pl.ds(kt * SEL_TILE_BLOCKS, SEL_TILE_BLOCKS)].reshape(SEL_TILE_KEYS, LANES)
        s_ref[...] = _dot(k_tile, qs_ref[...])

    def absorb(kt, s_ref, carry, diagonal, keys=SEL_TILE_KEYS):
        m_run, acc = carry
        kb0 = kt * SEL_TILE_BLOCKS
        if diagonal:
            causal = dm_ref[qt % (SEL_TILE_KEYS // Q_TILE), 0:keys, :]
        v_t = jnp.concatenate([vt_ref[kb0 + x, 0:VT_ROWS, :] for x in range(keys // K_BLOCK)], axis=1)
        ms_, accs = [], []
        for r in range(REP):
            sr = s_ref[0:keys, r * Q_TILE:(r + 1) * Q_TILE]
            if diagonal:
                sr = sr + causal
            m_old = head(m_run, r)
            m_new = jnp.maximum(m_old, jnp.max(sr, axis=0, keepdims=True))
            p = jnp.exp2(sr - m_new).astype(BF16)
            accs.append(head(acc, r) * jnp.exp2(m_old - m_new) + _dot(v_t, p))
            ms_.append(m_new)
        return heads(ms_), heads(accs)

    carry = (jnp.full((1, wide), NEG_BIG, F32), jnp.zeros((VT_ROWS, wide), F32))
    n_full = (qt * Q_TILE) // SEL_TILE_KEYS

    n_sel_tiles = SEQ // SEL_TILE_KEYS
    tr = lax.broadcasted_iota(jnp.int32, (n_sel_tiles, N_SEL), 0)
    tc = lax.broadcasted_iota(jnp.int32, (n_sel_tiles, N_SEL), 1)
    tile_of_block = jnp.where(lax.shift_right_logical(tc, 3) == tr, 1.0, 0.0).astype(BF16)
    per_query = _dot(tile_of_block, sel_ref[...].astype(BF16))
    per_tile = _dot(per_query.astype(BF16), jnp.ones((Q_TILE, LANES), BF16))
    bit = lax.shift_left(1, lax.broadcasted_iota(jnp.int32, (n_sel_tiles, LANES), 0)).astype(F32)
    used_bits = jnp.sum(jnp.where(per_tile > 0.5, bit, 0.0), axis=0, keepdims=True)[0, 0].astype(jnp.int32)

    kb_first = qt * (Q_TILE // K_BLOCK) - WINDOW // K_BLOCK
    kbs = [jnp.maximum(kb_first + x, 0) for x in range(WIN_BLOCKS)]
    sw = _dot(jnp.concatenate([kwin_ref[kb] for kb in kbs], axis=0), q_win)

    n_visit = jnp.int32(0)
    for kt in range(n_sel_tiles):
        tiles_ref[n_visit] = kt
        take = (kt < n_full) & ((lax.shift_right_logical(used_bits, kt) & 1) == 1)
        n_visit = n_visit + jnp.where(take, 1, 0)
    tiles_ref[n_visit] = n_full

    scores_into(tiles_ref[0], sa_ref)
    wm = jnp.concatenate(
        [wm_ref[x * K_BLOCK:(x + 1) * K_BLOCK, :] + jnp.where(kb_first + x < 0, NEG_BIG, 0.0)
         for x in range(WIN_BLOCKS)], axis=0)
    pws = []
    for r in range(REP):
        s = head(sw, r) + wm
        pws.append(jnp.exp2(s - jnp.max(s, axis=0, keepdims=True)).astype(BF16))
    vw_t = jnp.concatenate([vt_ref[kb, VT_ROWS:2 * VT_ROWS, :] for kb in kbs], axis=1)
    acc_w = _dot(vw_t, heads(pws))
    o_w = acc_w[0:HEAD_DIM] / acc_w[HEAD_DIM:HEAD_DIM + 1]

    def tile_pair(i, carry):
        scores_into(tiles_ref[2 * i + 1], sb_ref)
        carry = absorb(tiles_ref[2 * i], sa_ref, carry, False)
        scores_into(tiles_ref[2 * i + 2], sa_ref)
        return absorb(tiles_ref[2 * i + 1], sb_ref, carry, False)

    carry = lax.fori_loop(0, n_visit // 2, tile_pair, carry)
    last = 2 * (n_visit // 2)

    def absorb_diagonal(s_ref, carry):
        return lax.cond(qt % (SEL_TILE_KEYS // Q_TILE) == 0,
                        lambda c: absorb(n_full, s_ref, c, True, keys=Q_TILE),
                        lambda c: absorb(n_full, s_ref, c, True), carry)

    def odd_tail(carry):
        scores_into(n_full, sb_ref)
        return absorb_diagonal(sb_ref, absorb(tiles_ref[last], sa_ref, carry, False))

    def even_tail(carry):
        return absorb_diagonal(sa_ref, carry)

    _, acc_s = lax.cond(n_visit % 2 == 1, odd_tail, even_tail, carry)
    o_s = acc_s[0:HEAD_DIM] / acc_s[HEAD_DIM:HEAD_DIM + 1]

    gates = jnp.concatenate([gt_ref[x] for x in range(Q_TILE // K_BLOCK)], axis=1)
    outs = []
    for r in range(REP):
        outs.append(gates[r:r + 1, :] * head(o_c, r)
                    + gates[REP + r:REP + r + 1, :] * head(o_s, r)
                    + gates[2 * REP + r:2 * REP + r + 1, :] * head(o_w, r))
    o_ref[...] = jnp.concatenate(outs, axis=0).T.astype(BF16)


def _attn(proj, ksel, kwin, vt, kc, vct, gt, sf, wm, dm, ov_t, wq_b):
    q0 = Z_COLS // (REP * HEAD_DIM)
    kvspec = lambda shape: pl.BlockSpec((None, None) + shape, lambda b, g, qt: (b, g) + (0,) * len(shape))
    const = lambda shape: pl.BlockSpec(shape, lambda b, g, qt: (0,) * len(shape))
    return pl.pallas_call(
        _attn_kernel,
        grid=(BATCH, KV_GROUPS, N_QT),
        in_specs=[
            pl.BlockSpec((Q_TILE, REP * HEAD_DIM), lambda b, g, qt: (b * N_QT + qt, q0 + g)),
            kvspec((N_KB, K_BLOCK, LANES)),
            kvspec((N_KB, K_BLOCK, LANES)),
            kvspec((N_KB, 2 * VT_ROWS, K_BLOCK)),
            kvspec((N_CMP_PAD, LANES)),
            kvspec((HEAD_DIM, N_CMP_PAD)),
            pl.BlockSpec((None, Q_TILE // K_BLOCK, GATE_ROWS_PER_GROUP, K_BLOCK), lambda b, g, qt: (b, qt, g, 0)),
            pl.BlockSpec((None, 8, REP * Q_TILE), lambda b, g, qt: (g, 0, 0)),
            const((WIN_KEYS, Q_TILE)),
            const((SEL_TILE_KEYS // Q_TILE, SEL_TILE_KEYS, Q_TILE)),
            const((N_SEL, N_CMP_PAD)),
            const((HEAD_DIM, Q_TILE)),
        ],
        out_specs=pl.BlockSpec((Q_TILE, REP * HEAD_DIM), lambda b, g, qt: (b * N_QT + qt, g)),
        out_shape=jax.ShapeDtypeStruct((N_TOK, NSA_WIDTH), BF16),
        scratch_shapes=[pltpu.VMEM((N_SEL, Q_TILE), F32), pltpu.VMEM((LANES, REP * Q_TILE), BF16),
                        pltpu.VMEM((SEL_TILE_KEYS, REP * Q_TILE), F32),
                        pltpu.VMEM((SEL_TILE_KEYS, REP * Q_TILE), F32),
                        pltpu.SMEM((SEQ // SEL_TILE_KEYS + 1,), jnp.int32)],
        compiler_params=_cparams("arbitrary", "arbitrary", "arbitrary"),
        name="attn",
    )(proj, ksel, kwin, vt, kc, vct, gt, sf, wm, dm, ov_t, wq_b)


def _spatial_gating(z, lnw, lnb, sw_ref, sbx):
    ge = jax.nn.gelu(z)
    u = ge[:, :GMLP_WIDTH]
    v = ge[:, GMLP_WIDTH:]
    mu = jnp.mean(v, axis=-1, keepdims=True)
    var = jnp.mean(jnp.square(v - mu), axis=-1, keepdims=True)
    vn = ((v - mu) * lax.rsqrt(var + LN_EPS) * lnw + lnb).astype(BF16)
    ti = lax.broadcasted_iota(jnp.int32, (GMLP_CHUNK, GMLP_CHUNK), 0)
    si = lax.broadcasted_iota(jnp.int32, (GMLP_CHUNK, GMLP_CHUNK), 1)
    ws = [jnp.where(ti >= si, sw_ref[gg], 0.0).astype(BF16) for gg in range(GMLP_GROUPS)]
    gd = GMLP_WIDTH // GMLP_GROUPS
    outs = []
    for c in range(z.shape[0] // GMLP_CHUNK):
        rows = slice(c * GMLP_CHUNK, (c + 1) * GMLP_CHUNK)
        mix = jnp.concatenate(
            [_dot(ws[gg], vn[rows, gg * gd:(gg + 1) * gd]) for gg in range(GMLP_GROUPS)], axis=1)
        outs.append((u[rows, :] * (mix + sbx)).astype(BF16))
    return jnp.concatenate(outs, axis=0)


def _oproj_kernel(a_ref, z_ref, x_ref, wa_ref, wb_ref, nw_ref, lnw_ref, lnb_ref, sw_ref, sbx_ref,
                  x1_ref, xn_ref):
    attn_part = _dot(a_ref[...], wa_ref[...])
    b = _spatial_gating(z_ref[...], lnw_ref[...], lnb_ref[...], sw_ref, sbx_ref[...])
    y = x_ref[...] + (attn_part + _dot(b, wb_ref[...]))
    x1_ref[...] = y
    ms = jnp.mean(y * y, axis=-1, keepdims=True)
    xn_ref[...] = (y * lax.rsqrt(ms + NORM_EPS) * nw_ref[...]).astype(BF16)


def _oproj(a, proj, x2, w_out, nw, lnw, lnb, sw, sbx):
    return pl.pallas_call(
        _oproj_kernel,
        grid=(N_TOK // OPROJ_TM,),
        in_specs=[
            pl.BlockSpec((OPROJ_TM, NSA_WIDTH), lambda i: (i, 0)),
            pl.BlockSpec((OPROJ_TM, Z_COLS), lambda i: (i, 0)),
            pl.BlockSpec((OPROJ_TM, D_MODEL), lambda i: (i, 0)),
            pl.BlockSpec((NSA_WIDTH, D_MODEL), lambda i: (0, 0)),
            pl.BlockSpec((GMLP_WIDTH, D_MODEL), lambda i: (1, 0)),
            pl.BlockSpec((1, D_MODEL), lambda i: (0, 0)),
            pl.BlockSpec((1, GMLP_WIDTH), lambda i: (0, 0)),
            pl.BlockSpec((1, GMLP_WIDTH), lambda i: (0, 0)),
            pl.BlockSpec((GMLP_GROUPS, GMLP_CHUNK, GMLP_CHUNK), lambda i: (0, 0, 0)),
            pl.BlockSpec((GMLP_CHUNK, GMLP_WIDTH), lambda i: (0, 0)),
        ],
        out_specs=[
            pl.BlockSpec((OPROJ_TM, D_MODEL), lambda i: (i, 0)),
            pl.BlockSpec((OPROJ_TM, D_MODEL), lambda i: (i, 0)),
        ],
        out_shape=[
            jax.ShapeDtypeStruct((N_TOK, D_MODEL), F32),
            jax.ShapeDtypeStruct((N_TOK, D_MODEL), BF16),
        ],
        compiler_params=_cparams("arbitrary"),
        name="oproj",
    )(a, proj, x2, w_out, w_out, nw, lnw, lnb, sw, sbx)


def _ffn_kernel(xn_ref, x1_ref, wg_ref, wu_ref, cwg_ref, cwu_ref, cbg_ref, cbu_ref, wd_ref,
                cwg_p_ref, cwu_p_ref, cbg_p_ref, cbu_p_ref, wd_p_ref, o_ref, h0_ref, h1_ref, carry_ref):
    i = pl.program_id(0)
    q = pl.program_id(1)
    nq = pl.num_programs(1)
    half = FFN_TF // 2

    @pl.when((i == 0) & (q == 0))
    def _():
        carry_ref[...] = jnp.zeros_like(carry_ref)
        h1_ref[...] = jnp.zeros_like(h1_ref)

    @pl.when(q == 0)
    def _():
        o_ref[...] = x1_ref[...]

    seq_start = i % (SEQ // FFN_TM) == 0
    xn = xn_ref[...]

    def produce(c, h_ref):
        h_ref[0, 8:, :] = _dot(xn, wg_ref[:, c * half:(c + 1) * half])
        h_ref[1, 8:, :] = _dot(xn, wu_ref[:, c * half:(c + 1) * half])

    def conv(h_ref, part, tile, cw, cb):
        h_ref[part, 0:8, :] = jnp.where(seq_start, 0.0, carry_ref[tile, part])
        carry_ref[tile, part] = h_ref[part, FFN_TM:FFN_TM + 8, :]
        taps = [h_ref[part, 8 - k:8 - k + FFN_TM, :] for k in (2, 1, 0)]
        return cb + ((cw[0:1, :] * taps[0] + cw[1:2, :] * taps[1]) + cw[2:3, :] * taps[2])

    def gate(h_ref, tile, c, cwg, cwu, cbg, cbu, scale=None):
        cols = slice(c * half, (c + 1) * half)
        cg = conv(h_ref, 0, tile, cwg[:, cols], cbg[:, cols])
        cu = conv(h_ref, 1, tile, cwu[:, cols], cbu[:, cols])
        act = jax.nn.silu(cg) * cu
        if scale is not None:
            act = act * scale
        return act.astype(BF16)

    act_prev = gate(h1_ref, jnp.where(q == 0, 2 * nq - 1, 2 * q - 1), 1, cwg_p_ref, cwu_p_ref, cbg_p_ref,
                    cbu_p_ref, scale=jnp.where(q == 0, 0.0, 1.0))
    produce(0, h0_ref)
    down_prev = _dot(act_prev, wd_p_ref[half:, :])
    act_cur = gate(h0_ref, 2 * q, 0, cwg_ref, cwu_ref, cbg_ref, cbu_ref)
    produce(1, h1_ref)
    o_ref[...] += down_prev + _dot(act_cur, wd_ref[0:half, :])

    @pl.when(q == nq - 1)
    def _():
        act_last = gate(h1_ref, 2 * q + 1, 1, cwg_ref, cwu_ref, cbg_ref, cbu_ref)
        o_ref[...] += _dot(act_last, wd_ref[half:, :])


def _ffn(xn, x1, w_up, conv_w, conv_b, w_down):
    nj = D_FF // FFN_TF
    prev = lambda q: jnp.maximum(q - 1, 0)
    return pl.pallas_call(
        _ffn_kernel,
        grid=(N_TOK // FFN_TM, nj),
        in_specs=[
            pl.BlockSpec((FFN_TM, D_MODEL), lambda i, q: (i, 0)),
            pl.BlockSpec((FFN_TM, D_MODEL), lambda i, q: (i, 0)),
            pl.BlockSpec((D_MODEL, FFN_TF), lambda i, q: (0, q)),
            pl.BlockSpec((D_MODEL, FFN_TF), lambda i, q: (0, nj + q)),
            pl.BlockSpec((CONV_WIDTH, FFN_TF), lambda i, q: (0, q)),
            pl.BlockSpec((CONV_WIDTH, FFN_TF), lambda i, q: (0, nj + q)),
            pl.BlockSpec((1, FFN_TF), lambda i, q: (0, q)),
            pl.BlockSpec((1, FFN_TF), lambda i, q: (0, nj + q)),
            pl.BlockSpec((FFN_TF, D_MODEL), lambda i, q: (q, 0)),
            pl.BlockSpec((CONV_WIDTH, FFN_TF), lambda i, q: (0, prev(q))),
            pl.BlockSpec((CONV_WIDTH, FFN_TF), lambda i, q: (0, nj + prev(q))),
            pl.BlockSpec((1, FFN_TF), lambda i, q: (0, prev(q))),
            pl.BlockSpec((1, FFN_TF), lambda i, q: (0, nj + prev(q))),
            pl.BlockSpec((FFN_TF, D_MODEL), lambda i, q: (prev(q), 0)),
        ],
        out_specs=pl.BlockSpec((FFN_TM, D_MODEL), lambda i, q: (i, 0)),
        out_shape=jax.ShapeDtypeStruct((N_TOK, D_MODEL), F32),
        scratch_shapes=[pltpu.VMEM((2, FFN_TM + 8, FFN_TF // 2), F32), pltpu.VMEM((2, FFN_TM + 8, FFN_TF // 2), F32),
                        pltpu.VMEM((2 * nj, 2, 8, FFN_TF // 2), F32)],
        compiler_params=_cparams("arbitrary", "arbitrary"),
        name="ffn",
    )(xn, x1, w_up, w_up, conv_w, conv_w, conv_b, conv_b, w_down, conv_w, conv_w, conv_b, conv_b, w_down)


def _overlap_t():
    start = np.arange(N_CMP_PAD)[None, :] * CMP_STRIDE
    s0 = np.arange(N_SEL)[:, None] * SEL_BLOCK
    ov = (start < s0 + SEL_BLOCK) & (start + CMP_BLOCK > s0) & (np.arange(N_CMP_PAD)[None, :] < N_CMP_PAD - 1)
    return ov.astype(np.float32)


def _bf16_round(a):
    return np.asarray(a, np.float32).astype(BF16).astype(np.float32)


def _slope_features():
    sl = (np.power(2.0, -8.0 * np.arange(1, NSA_HEADS + 1) / NSA_HEADS).astype(np.float32)
          * np.float32(LOG2E)).astype(np.float32)
    s1 = _bf16_round(sl)
    s2 = _bf16_round(sl - s1)
    s3 = _bf16_round(sl - s1 - s2)
    rows = np.stack([s1, s2, s3, s1, s2, s3, -sl, np.zeros_like(sl)], axis=0)
    rows = rows.reshape(8, KV_GROUPS, REP).transpose(1, 0, 2)
    return np.repeat(rows, Q_TILE, axis=2).astype(np.float32)


def _window_mask():
    kl = np.arange(WIN_KEYS)[:, None]
    ql = np.arange(Q_TILE)[None, :]
    dist = ql + WINDOW - kl
    return np.where((dist >= 0) & (dist < WINDOW), 0.0, NEG_BIG).astype(np.float32)


def _diag_masks():
    kl = np.arange(SEL_TILE_KEYS)[None, :, None]
    ql = np.arange(Q_TILE)[None, None, :]
    off = np.arange(SEL_TILE_KEYS // Q_TILE)[:, None, None] * Q_TILE
    return np.where(kl > ql + off, NEG_BIG, 0.0).astype(np.float32)


def _layout_w_in(w_in):
    q_end = NSA_WIDTH
    kv_end = q_end + 6 * KV_COLS
    g_end = kv_end + NSA_HEADS * N_BRANCH
    w_in = w_in.astype(BF16)
    w_q = w_in[:, :q_end]
    w_kv = w_in[:, q_end:kv_end].reshape(D_MODEL, 6, KV_GROUPS, HEAD_DIM)
    w_kv = jnp.stack([w_kv[:, s] for s in (2, 4, 0, 1, 3, 5)], axis=2).reshape(D_MODEL, 6 * KV_COLS)
    w_z = w_in[:, g_end:]
    w_main = jnp.concatenate([w_z, w_q, w_kv], axis=1)
    w_g = w_in[:, kv_end:g_end].reshape(D_MODEL, KV_GROUPS, REP, N_BRANCH).transpose(0, 1, 3, 2)
    w_g = w_g.reshape(D_MODEL, KV_GROUPS, N_BRANCH * REP)
    w_g = jnp.pad(w_g, ((0, 0), (0, 0), (0, GATE_ROWS_PER_GROUP - N_BRANCH * REP)))
    w_g = jnp.pad(w_g.reshape(D_MODEL, KV_GROUPS * GATE_ROWS_PER_GROUP),
                  ((0, 0), (0, GATE_PAD - KV_GROUPS * GATE_ROWS_PER_GROUP)))
    return w_main, w_g


def _layout_compress(cmp_pos, cmp_w1, cmp_w2):
    pos2 = jnp.concatenate([cmp_pos[0], cmp_pos[1]], axis=1)
    w1 = cmp_w1.reshape(2, CMP_BLOCK, HEAD_DIM, CMP_HIDDEN)
    zero = jnp.zeros_like(w1[0])
    w1k = jnp.concatenate([w1[0], zero], axis=1)
    w1v = jnp.concatenate([zero, w1[1]], axis=1)
    w1p = jnp.concatenate([w1k, w1v], axis=2).astype(BF16)
    w1p = w1p.reshape(CMP_BLOCK // 2, 2 * LANES, 2 * CMP_HIDDEN)
    w2p = jnp.pad(cmp_w2, ((0, 0), (0, 0), (0, LANES - HEAD_DIM))).astype(BF16)
    return pos2, w1p, w2p


def kernel(x, attn_norm_w, w_in, q_norm_w, k_norm_w, cmp_pos, cmp_w1, cmp_w2, gmlp_ln_w, gmlp_ln_b,
           spatial_w, spatial_b, w_out, ffn_norm_w, w_up, conv_w, conv_b, w_down):
    x2 = x.reshape(N_TOK, D_MODEL)
    w_main, w_gate = _layout_w_in(w_in)
    proj, gl = _proj(x2, attn_norm_w.reshape(1, D_MODEL), w_main, w_gate)

    knw2 = jnp.concatenate([k_norm_w[1], k_norm_w[2]]).reshape(1, LANES)
    kcw = jnp.concatenate([k_norm_w[0], jnp.zeros((LANES - HEAD_DIM,), F32)]).reshape(1, LANES)
    pos2, w1p, w2p = _layout_compress(cmp_pos, cmp_w1, cmp_w2)
    ksel, kwin, vt, kc, vct, gt = _kprep(proj, gl, knw2, kcw, pos2, w1p, w2p)

    wq_b = jnp.broadcast_to(q_norm_w.reshape(HEAD_DIM, 1), (HEAD_DIM, Q_TILE))
    a = _attn(proj, ksel, kwin, vt, kc, vct, gt, jnp.asarray(_slope_features()),
              jnp.asarray(_window_mask()), jnp.asarray(_diag_masks()),
              jnp.asarray(_overlap_t(), dtype=BF16), wq_b)

    sbx = jnp.repeat(spatial_b.T, GMLP_WIDTH // GMLP_GROUPS, axis=1)
    x1, xn = _oproj(a, proj, x2, w_out.astype(BF16), ffn_norm_w.reshape(1, D_MODEL),
                    gmlp_ln_w.reshape(1, GMLP_WIDTH), gmlp_ln_b.reshape(1, GMLP_WIDTH), spatial_w, sbx)

    out = _ffn(xn, x1, w_up.astype(BF16), conv_w, conv_b.reshape(1, 2 * D_FF), w_down.astype(BF16))
    return out.reshape(BATCH, SEQ, D_MODEL)
```

```python
import numpy as np
import jax
import jax.numpy as jnp
from jax import lax
from jax.experimental import pallas as pl
from jax.experimental.pallas import tpu as pltpu

F32 = jnp.float32
BF16 = jnp.bfloat16

D_MODEL = 2048
BATCH = 4
SEQ = 4096
N_TOK = BATCH * SEQ
NSA_HEADS = 16
KV_GROUPS = 4
REP = NSA_HEADS // KV_GROUPS
HEAD_DIM = 64
NSA_WIDTH = NSA_HEADS * HEAD_DIM
GMLP_WIDTH = D_MODEL - NSA_WIDTH
GMLP_GROUPS = 8
GMLP_CHUNK = 128
CMP_BLOCK = 32
CMP_STRIDE = 16
CMP_HIDDEN = 256
N_CMP_PAD = SEQ // CMP_STRIDE
SEL_BLOCK = 64
N_SEL = SEQ // SEL_BLOCK
SEL_TOPK = 16
WINDOW = 512
K_BLOCK = 128
N_KB = SEQ // K_BLOCK
Q_TILE = 256
N_QT = SEQ // Q_TILE
N_BRANCH = 3
KV_COLS = KV_GROUPS * HEAD_DIM
D_FF = 5632
CONV_WIDTH = 3
NORM_EPS = 1e-6
LN_EPS = 1e-5
NEG_BIG = -1e30
SEL_BIG = 1e9

LANES = 128
VMEM_LIMIT_BYTES = 56 * 1024 * 1024

Z_COLS = 2 * GMLP_WIDTH
PROJ_COLS = Z_COLS + NSA_WIDTH + 6 * KV_COLS
GATE_PAD = LANES
GATE_ROWS_PER_GROUP = 16

PROJ_TM = 1024
PROJ_TN = 1536
OPROJ_TM = 512
FFN_TM = 512
FFN_TF = 512
SEL_TILE_BLOCKS = 4
SEL_TILE_KEYS = SEL_TILE_BLOCKS * K_BLOCK
MASK_LANE0 = 80
SUM_ROWS = 16
VT_ROWS = HEAD_DIM + SUM_ROWS
LOG2E = 1.4426950408889634
WIN_BLOCKS = (WINDOW + Q_TILE) // K_BLOCK
WIN_KEYS = WIN_BLOCKS * K_BLOCK


def _cparams(*sem):
    return pltpu.CompilerParams(dimension_semantics=sem, vmem_limit_bytes=VMEM_LIMIT_BYTES)


def _dot(a, b):
    return jnp.dot(a, b, preferred_element_type=F32)


def _proj_kernel(x_ref, nw_ref, w_ref, wg_ref, o_ref, og_ref, h_ref):
    @pl.when(pl.program_id(1) == 0)
    def _():
        x = x_ref[...]
        ms = jnp.mean(x * x, axis=-1, keepdims=True)
        h = (x * lax.rsqrt(ms + NORM_EPS) * nw_ref[...]).astype(BF16)
        h_ref[...] = h
        og_ref[...] = _dot(h, wg_ref[...])

    o_ref[...] = _dot(h_ref[...], w_ref[...])


def _proj(x2, nw, w_main, w_gate):
    return pl.pallas_call(
        _proj_kernel,
        grid=(N_TOK // PROJ_TM, PROJ_COLS // PROJ_TN),
        in_specs=[
            pl.BlockSpec((PROJ_TM, D_MODEL), lambda i, j: (i, 0)),
            pl.BlockSpec((1, D_MODEL), lambda i, j: (0, 0)),
            pl.BlockSpec((D_MODEL, PROJ_TN), lambda i, j: (0, j)),
            pl.BlockSpec((D_MODEL, GATE_PAD), lambda i, j: (0, 0)),
        ],
        out_specs=[
            pl.BlockSpec((PROJ_TM, PROJ_TN), lambda i, j: (i, j)),
            pl.BlockSpec((PROJ_TM, GATE_PAD), lambda i, j: (i, 0)),
        ],
        out_shape=[
            jax.ShapeDtypeStruct((N_TOK, PROJ_COLS), F32),
            jax.ShapeDtypeStruct((N_TOK, GATE_PAD), F32),
        ],
        scratch_shapes=[pltpu.VMEM((PROJ_TM, D_MODEL), BF16)],
        compiler_params=_cparams("arbitrary", "arbitrary"),
        name="proj",
    )(x2, nw, w_main, w_gate)


def _pos_features(lane, first, f1, f2):
    return jnp.where(lane < first + 3, f1, jnp.where(lane < first + 6, f2,
                                                     jnp.where(lane == first + 6, 1.0, 0.0)))


def _kprep_kernel(s0_ref, s1_ref, s2_ref, gl_ref, knw_ref, kcw_ref, pos_ref, w1_ref, w2_ref,
                  ksel_ref, kwin_ref, vt_ref, kc_ref, vct_ref, gt_ref, ha_ref, hb_ref):
    lane = lax.broadcasted_iota(jnp.int32, (K_BLOCK, LANES), 1)
    row = lax.broadcasted_iota(jnp.int32, (K_BLOCK, LANES), 0)
    lo = lane < HEAD_DIM
    inv_d = 1.0 / HEAD_DIM
    ones_rows = jnp.where(lax.broadcasted_iota(jnp.int32, (SUM_ROWS, K_BLOCK), 0) == 0, 1.0, 0.0)

    def chunk(c, carry):
        r0 = pl.multiple_of(c * K_BLOCK, K_BLOCK)
        x = s0_ref[pl.ds(r0, K_BLOCK), :]
        sq = x * x
        s_lo = jnp.sum(jnp.where(lo, sq, 0.0), axis=1, keepdims=True)
        s_hi = jnp.sum(jnp.where(lo, 0.0, sq), axis=1, keepdims=True)
        ms = jnp.where(lo, s_lo, s_hi) * inv_d
        y = x * lax.rsqrt(ms + NORM_EPS) * knw_ref[...]
        pos = r0 + row
        blk = lax.shift_right_logical(pos, 6)
        f1 = (blk * SEL_BLOCK).astype(F32)
        f2 = (pos & (SEL_BLOCK - 1)).astype(F32)
        onehot = jnp.where(lane - MASK_LANE0 == (blk & (SEL_TILE_BLOCKS * 2 - 1)), 1.0, 0.0)
        feat_sel = jnp.where(lane < MASK_LANE0, _pos_features(lane, HEAD_DIM, f1, f2), onehot)
        ksel_ref[c] = jnp.where(lo, y, feat_sel).astype(BF16)
        kwin_ref[c] = jnp.where(lo, _pos_features(lane, 0, f1, f2), y).astype(BF16)
        v_t = s2_ref[pl.ds(r0, K_BLOCK), :].T
        vt_ref[c] = jnp.concatenate(
            [v_t[0:HEAD_DIM], ones_rows, v_t[HEAD_DIM:], ones_rows], axis=0).astype(BF16)
        return carry

    lax.fori_loop(0, N_KB, chunk, 0, unroll=2)

    @pl.when(pl.program_id(1) == 0)
    def _():
        def gchunk(c, carry):
            r0 = pl.multiple_of(c * K_BLOCK, K_BLOCK)
            gt_ref[c] = jax.nn.sigmoid(gl_ref[pl.ds(r0, K_BLOCK), :]).T
            return carry

        lax.fori_loop(0, N_KB, gchunk, 0, unroll=2)

    ha_ref[...] = jnp.zeros_like(ha_ref)
    hb_ref[...] = jnp.zeros_like(hb_ref)
    for tp in range(CMP_STRIDE // 2):
        rows = [s1_ref[pl.ds(2 * tp + e, N_CMP_PAD, stride=CMP_STRIDE), :] for e in range(2)]
        xa = jnp.concatenate([(rows[e] + pos_ref[2 * tp + e:2 * tp + e + 1, :]).astype(BF16)
                              for e in range(2)], axis=1)
        xb = jnp.concatenate([(rows[e] + pos_ref[CMP_STRIDE + 2 * tp + e:CMP_STRIDE + 2 * tp + e + 1, :])
                              .astype(BF16) for e in range(2)], axis=1)
        ha_ref[...] += _dot(xa, w1_ref[tp])
        hb_ref[0:N_CMP_PAD, :] += _dot(xb, w1_ref[CMP_STRIDE // 2 + tp])
    hid = jax.nn.gelu(ha_ref[...] + hb_ref[1:N_CMP_PAD + 1, :]).astype(BF16)
    kc = _dot(hid[:, :CMP_HIDDEN], w2_ref[0])
    vc = _dot(hid[:, CMP_HIDDEN:], w2_ref[1])
    ms = jnp.sum(kc * kc, axis=1, keepdims=True) * inv_d
    kcn = kc * lax.rsqrt(ms + NORM_EPS) * kcw_ref[...]
    ci = lax.broadcasted_iota(jnp.int32, (N_CMP_PAD, LANES), 0)
    cl = lax.broadcasted_iota(jnp.int32, (N_CMP_PAD, LANES), 1)
    c1 = (lax.shift_right_logical(ci, 2) * SEL_BLOCK).astype(F32)
    c2 = ((ci & 3) * CMP_STRIDE).astype(F32) + (CMP_BLOCK - 1) * 0.5
    kc_ref[...] = jnp.where(cl < HEAD_DIM, kcn, _pos_features(cl, HEAD_DIM, c1, c2)).astype(BF16)
    vct_ref[...] = vc.T[0:HEAD_DIM, :].astype(BF16)


def _kprep(proj, gl, knw2, kcw, pos2, w1p, w2p):
    kv0 = (Z_COLS + NSA_WIDTH) // LANES
    slab = lambda s: pl.BlockSpec((SEQ, LANES), lambda b, g, s=s: (b, kv0 + 3 * g + s))
    full = lambda shape: pl.BlockSpec(shape, lambda b, g: (0,) * len(shape))
    per_bg = lambda shape: pl.BlockSpec((None, None) + shape, lambda b, g: (b, g) + (0,) * len(shape))
    bg_shape = lambda shape, dt: jax.ShapeDtypeStruct((BATCH, KV_GROUPS) + shape, dt)
    return pl.pallas_call(
        _kprep_kernel,
        grid=(BATCH, KV_GROUPS),
        in_specs=[
            slab(0), slab(1), slab(2),
            pl.BlockSpec((SEQ, GATE_PAD), lambda b, g: (b, 0)),
            full((1, LANES)), full((1, LANES)), full((CMP_BLOCK, LANES)),
            full((CMP_BLOCK // 2, 2 * LANES, 2 * CMP_HIDDEN)), full((2, CMP_HIDDEN, LANES)),
        ],
        out_specs=[
            per_bg((N_KB, K_BLOCK, LANES)),
            per_bg((N_KB, K_BLOCK, LANES)),
            per_bg((N_KB, 2 * VT_ROWS, K_BLOCK)),
            per_bg((N_CMP_PAD, LANES)),
            per_bg((HEAD_DIM, N_CMP_PAD)),
            pl.BlockSpec((None, N_KB, GATE_PAD, K_BLOCK), lambda b, g: (b, 0, 0, 0)),
        ],
        out_shape=[
            bg_shape((N_KB, K_BLOCK, LANES), BF16),
            bg_shape((N_KB, K_BLOCK, LANES), BF16),
            bg_shape((N_KB, 2 * VT_ROWS, K_BLOCK), BF16),
            bg_shape((N_CMP_PAD, LANES), BF16),
            bg_shape((HEAD_DIM, N_CMP_PAD), BF16),
            jax.ShapeDtypeStruct((BATCH, N_KB, GATE_PAD, K_BLOCK), F32),
        ],
        scratch_shapes=[
            pltpu.VMEM((N_CMP_PAD, 2 * CMP_HIDDEN), F32),
            pltpu.VMEM((N_CMP_PAD + 8, 2 * CMP_HIDDEN), F32),
        ],
        compiler_params=_cparams("arbitrary", "arbitrary"),
        name="kprep",
    )(proj, proj, proj, gl, knw2, kcw, pos2, w1p, w2p)


def _attn_kernel(q_ref, ksel_ref, kwin_ref, vt_ref, kc_ref, vct_ref, gt_ref, sf_ref, wm_ref, dm_ref,
                 ov_ref, wq_ref, o_ref, sel_ref, qs_ref, sa_ref, sb_ref, tiles_ref):
    qt = pl.program_id(2)
    t_i = qt * Q_TILE + lax.broadcasted_iota(jnp.int32, (1, Q_TILE), 1)
    head = lambda a, r: a[:, r * Q_TILE:(r + 1) * Q_TILE]
    heads = lambda parts: jnp.concatenate(parts, axis=1)

    q_t = q_ref[...].T
    parts = []
    for r in range(REP):
        xq = q_t[r * HEAD_DIM:(r + 1) * HEAD_DIM, :]
        ms = jnp.sum(xq * xq, axis=0, keepdims=True) * (1.0 / HEAD_DIM)
        parts.append(xq * lax.rsqrt(ms + NORM_EPS) * wq_ref[...] * (HEAD_DIM ** -0.5) * LOG2E)
    qn = heads(parts)
    sf = sf_ref[...]
    sf = jnp.where(lax.broadcasted_iota(jnp.int32, sf.shape, 0) == 6, sf * (qt * Q_TILE).astype(F32), sf)
    wide = REP * Q_TILE
    pad = jnp.zeros((HEAD_DIM - 8, wide), F32)
    qs_ref[...] = jnp.concatenate([qn, sf, pad], axis=0).astype(BF16)
    q_win = jnp.concatenate([sf, pad, qn], axis=0).astype(BF16)

    sc = _dot(kc_ref[...], qs_ref[...])
    ci = lax.broadcasted_iota(jnp.int32, (N_CMP_PAD, Q_TILE), 0)
    cm = jnp.where(ci * CMP_STRIDE + (CMP_BLOCK - 1) <= t_i, 0.0, NEG_BIG)
    any_valid = t_i >= CMP_BLOCK - 1
    pcs = []
    for r in range(REP):
        s = head(sc, r) + cm
        e = jnp.exp2(s - jnp.max(s, axis=0, keepdims=True))
        d = jnp.sum(e, axis=0, keepdims=True)
        pcs.append((e * jnp.where(any_valid, 1.0 / d, 0.0)).astype(BF16))
    oc_imp = _dot(jnp.concatenate([vct_ref[...], ov_ref[...]], axis=0), heads(pcs))
    o_c = oc_imp[0:HEAD_DIM]
    imp4 = oc_imp[HEAD_DIM:]
    imp = (head(imp4, 0) + head(imp4, 1)) + (head(imp4, 2) + head(imp4, 3))

    jj = lax.broadcasted_iota(jnp.int32, (N_SEL, Q_TILE), 0)
    cur = lax.shift_right_logical(t_i, 6)
    valid = jj <= cur
    sel_ref[...] = jnp.where(valid, 1.0, 0.0)
    forced = (jj == 0) | (jj == cur) | (jj == cur - 1)
    score = jnp.where(valid, jnp.where(forced, SEL_BIG, imp), -SEL_BIG)

    def rank_rows(n8):
        def branch():
            rows8 = [score[8 * v:8 * v + 8, :] for v in range(n8)]
            j8 = lax.broadcasted_iota(jnp.int32, (8, Q_TILE), 0)
            ranks = [jnp.zeros((8, Q_TILE), F32) for _ in rows8]
            for jp in range(8 * n8):
                row = score[jp:jp + 1, :]
                for v, blk in enumerate(rows8):
                    if 8 * v > jp:
                        ahead = jnp.where(row >= blk, 1.0, 0.0)
                    elif 8 * v + 7 <= jp:
                        ahead = jnp.where(row > blk, 1.0, 0.0)
                    else:
                        ahead = jnp.where(j8 + 8 * v > jp, jnp.where(row >= blk, 1.0, 0.0),
                                          jnp.where(row > blk, 1.0, 0.0))
                    ranks[v] = ranks[v] + ahead
            rank = jnp.concatenate(ranks, axis=0)
            sel_ref[0:8 * n8, :] = jnp.where(valid[0:8 * n8] & (rank < SEL_TOPK), 1.0, 0.0)
        return branch

    n8_needed = (qt * Q_TILE + Q_TILE - 1) // (8 * SEL_BLOCK) + 1
    lax.switch(n8_needed - 1, [(lambda: None) if n8 * 8 <= SEL_TOPK else rank_rows(n8)
                               for n8 in range(1, N_SEL // 8 + 1)])

    sub = SEL_TILE_KEYS // SEL_BLOCK

    def scores_into(kt, s_ref):
        member = sel_ref[pl.ds(kt * sub, sub), :]
        mrows = (member - 1.0) * (-NEG_BIG)
        qs_ref[MASK_LANE0:MASK_LANE0 + 16, :] = jnp.concatenate(
            [heads([mrows] * REP), jnp.zeros((16 - sub, wide), F32)], axis=0).astype(BF16)
        k_tile = ksel_ref[pl.ds(kt * SEL_TILE_BLOCKS, SEL_TILE_BLOCKS)].reshape(SEL_TILE_KEYS, LANES)
        s_ref[...] = _dot(k_tile, qs_ref[...])

    def absorb(kt, s_ref, carry, diagonal):
        m_run, acc = carry
        kb0 = kt * SEL_TILE_BLOCKS
        if diagonal:
            causal = dm_ref[qt % (SEL_TILE_KEYS // Q_TILE)]
        v_t = jnp.concatenate([vt_ref[kb0 + x, 0:VT_ROWS, :] for x in range(SEL_TILE_BLOCKS)], axis=1)
        ms_, accs = [], []
        for r in range(REP):
            sr = s_ref[:, r * Q_TILE:(r + 1) * Q_TILE]
            if diagonal:
                sr = sr + causal
            m_old = head(m_run, r)
            m_new = jnp.maximum(m_old, jnp.max(sr, axis=0, keepdims=True))
            p = jnp.exp2(sr - m_new).astype(BF16)
            accs.append(head(acc, r) * jnp.exp2(m_old - m_new) + _dot(v_t, p))
            ms_.append(m_new)
        return heads(ms_), heads(accs)

    carry = (jnp.full((1, wide), NEG_BIG, F32), jnp.zeros((VT_ROWS, wide), F32))
    n_full = (qt * Q_TILE) // SEL_TILE_KEYS

    n_sel_tiles = SEQ // SEL_TILE_KEYS
    tr = lax.broadcasted_iota(jnp.int32, (n_sel_tiles, N_SEL), 0)
    tc = lax.broadcasted_iota(jnp.int32, (n_sel_tiles, N_SEL), 1)
    tile_of_block = jnp.where(lax.shift_right_logical(tc, 3) == tr, 1.0, 0.0).astype(BF16)
    per_query = _dot(tile_of_block, sel_ref[...].astype(BF16))
    per_tile = _dot(per_query.astype(BF16), jnp.ones((Q_TILE, LANES), BF16))
    bit = lax.shift_left(1, lax.broadcasted_iota(jnp.int32, (n_sel_tiles, LANES), 0)).astype(F32)
    used_bits = jnp.sum(jnp.where(per_tile > 0.5, bit, 0.0), axis=0, keepdims=True)[0, 0].astype(jnp.int32)

    kb_first = qt * (Q_TILE // K_BLOCK) - WINDOW // K_BLOCK
    kbs = [jnp.maximum(kb_first + x, 0) for x in range(WIN_BLOCKS)]
    sw = _dot(jnp.concatenate([kwin_ref[kb] for kb in kbs], axis=0), q_win)

    n_visit = jnp.int32(0)
    for kt in range(n_sel_tiles):
        tiles_ref[n_visit] = kt
        take = (kt < n_full) & ((lax.shift_right_logical(used_bits, kt) & 1) == 1)
        n_visit = n_visit + jnp.where(take, 1, 0)
    tiles_ref[n_visit] = n_full

    scores_into(tiles_ref[0], sa_ref)
    wm = jnp.concatenate(
        [wm_ref[x * K_BLOCK:(x + 1) * K_BLOCK, :] + jnp.where(kb_first + x < 0, NEG_BIG, 0.0)
         for x in range(WIN_BLOCKS)], axis=0)
    pws = []
    for r in range(REP):
        s = head(sw, r) + wm
        pws.append(jnp.exp2(s - jnp.max(s, axis=0, keepdims=True)).astype(BF16))
    vw_t = jnp.concatenate([vt_ref[kb, VT_ROWS:2 * VT_ROWS, :] for kb in kbs], axis=1)
    acc_w = _dot(vw_t, heads(pws))
    o_w = acc_w[0:HEAD_DIM] / acc_w[HEAD_DIM:HEAD_DIM + 1]

    def tile_pair(i, carry):
        scores_into(tiles_ref[2 * i + 1], sb_ref)
        carry = absorb(tiles_ref[2 * i], sa_ref, carry, False)
        scores_into(tiles_ref[2 * i + 2], sa_ref)
        return absorb(tiles_ref[2 * i + 1], sb_ref, carry, False)

    carry = lax.fori_loop(0, n_visit // 2, tile_pair, carry)
    last = 2 * (n_visit // 2)

    def odd_tail(carry):
        scores_into(n_full, sb_ref)
        return absorb(n_full, sb_ref, absorb(tiles_ref[last], sa_ref, carry, False), True)

    def even_tail(carry):
        return absorb(n_full, sa_ref, carry, True)

    _, acc_s = lax.cond(n_visit % 2 == 1, odd_tail, even_tail, carry)
    o_s = acc_s[0:HEAD_DIM] / acc_s[HEAD_DIM:HEAD_DIM + 1]

    gates = jnp.concatenate([gt_ref[x] for x in range(Q_TILE // K_BLOCK)], axis=1)
    outs = []
    for r in range(REP):
        outs.append(gates[r:r + 1, :] * head(o_c, r)
                    + gates[REP + r:REP + r + 1, :] * head(o_s, r)
                    + gates[2 * REP + r:2 * REP + r + 1, :] * head(o_w, r))
    o_ref[...] = jnp.concatenate(outs, axis=0).T.astype(BF16)


def _attn(proj, ksel, kwin, vt, kc, vct, gt, sf, wm, dm, ov_t, wq_b):
    q0 = Z_COLS // (REP * HEAD_DIM)
    kvspec = lambda shape: pl.BlockSpec((None, None) + shape, lambda b, g, qt: (b, g) + (0,) * len(shape))
    const = lambda shape: pl.BlockSpec(shape, lambda b, g, qt: (0,) * len(shape))
    return pl.pallas_call(
        _attn_kernel,
        grid=(BATCH, KV_GROUPS, N_QT),
        in_specs=[
            pl.BlockSpec((Q_TILE, REP * HEAD_DIM), lambda b, g, qt: (b * N_QT + qt, q0 + g)),
            kvspec((N_KB, K_BLOCK, LANES)),
            kvspec((N_KB, K_BLOCK, LANES)),
            kvspec((N_KB, 2 * VT_ROWS, K_BLOCK)),
            kvspec((N_CMP_PAD, LANES)),
            kvspec((HEAD_DIM, N_CMP_PAD)),
            pl.BlockSpec((None, Q_TILE // K_BLOCK, GATE_ROWS_PER_GROUP, K_BLOCK), lambda b, g, qt: (b, qt, g, 0)),
            pl.BlockSpec((None, 8, REP * Q_TILE), lambda b, g, qt: (g, 0, 0)),
            const((WIN_KEYS, Q_TILE)),
            const((SEL_TILE_KEYS // Q_TILE, SEL_TILE_KEYS, Q_TILE)),
            const((N_SEL, N_CMP_PAD)),
            const((HEAD_DIM, Q_TILE)),
        ],
        out_specs=pl.BlockSpec((Q_TILE, REP * HEAD_DIM), lambda b, g, qt: (b * N_QT + qt, g)),
        out_shape=jax.ShapeDtypeStruct((N_TOK, NSA_WIDTH), BF16),
        scratch_shapes=[pltpu.VMEM((N_SEL, Q_TILE), F32), pltpu.VMEM((LANES, REP * Q_TILE), BF16),
                        pltpu.VMEM((SEL_TILE_KEYS, REP * Q_TILE), F32),
                        pltpu.VMEM((SEL_TILE_KEYS, REP * Q_TILE), F32),
                        pltpu.SMEM((SEQ // SEL_TILE_KEYS + 1,), jnp.int32)],
        compiler_params=_cparams("arbitrary", "arbitrary", "arbitrary"),
        name="attn",
    )(proj, ksel, kwin, vt, kc, vct, gt, sf, wm, dm, ov_t, wq_b)


def _spatial_gating(z, lnw, lnb, sw_ref, sbx):
    ge = jax.nn.gelu(z)
    u = ge[:, :GMLP_WIDTH]
    v = ge[:, GMLP_WIDTH:]
    mu = jnp.mean(v, axis=-1, keepdims=True)
    var = jnp.mean(jnp.square(v - mu), axis=-1, keepdims=True)
    vn = ((v - mu) * lax.rsqrt(var + LN_EPS) * lnw + lnb).astype(BF16)
    ti = lax.broadcasted_iota(jnp.int32, (GMLP_CHUNK, GMLP_CHUNK), 0)
    si = lax.broadcasted_iota(jnp.int32, (GMLP_CHUNK, GMLP_CHUNK), 1)
    ws = [jnp.where(ti >= si, sw_ref[gg], 0.0).astype(BF16) for gg in range(GMLP_GROUPS)]
    gd = GMLP_WIDTH // GMLP_GROUPS
    outs = []
    for c in range(z.shape[0] // GMLP_CHUNK):
        rows = slice(c * GMLP_CHUNK, (c + 1) * GMLP_CHUNK)
        mix = jnp.concatenate(
            [_dot(ws[gg], vn[rows, gg * gd:(gg + 1) * gd]) for gg in range(GMLP_GROUPS)], axis=1)
        outs.append((u[rows, :] * (mix + sbx)).astype(BF16))
    return jnp.concatenate(outs, axis=0)


def _oproj_kernel(a_ref, z_ref, x_ref, wa_ref, wb_ref, nw_ref, lnw_ref, lnb_ref, sw_ref, sbx_ref,
                  x1_ref, xn_ref):
    attn_part = _dot(a_ref[...], wa_ref[...])
    b = _spatial_gating(z_ref[...], lnw_ref[...], lnb_ref[...], sw_ref, sbx_ref[...])
    y = x_ref[...] + (attn_part + _dot(b, wb_ref[...]))
    x1_ref[...] = y
    ms = jnp.mean(y * y, axis=-1, keepdims=True)
    xn_ref[...] = (y * lax.rsqrt(ms + NORM_EPS) * nw_ref[...]).astype(BF16)


def _oproj(a, proj, x2, w_out, nw, lnw, lnb, sw, sbx):
    return pl.pallas_call(
        _oproj_kernel,
        grid=(N_TOK // OPROJ_TM,),
        in_specs=[
            pl.BlockSpec((OPROJ_TM, NSA_WIDTH), lambda i: (i, 0)),
            pl.BlockSpec((OPROJ_TM, Z_COLS), lambda i: (i, 0)),
            pl.BlockSpec((OPROJ_TM, D_MODEL), lambda i: (i, 0)),
            pl.BlockSpec((NSA_WIDTH, D_MODEL), lambda i: (0, 0)),
            pl.BlockSpec((GMLP_WIDTH, D_MODEL), lambda i: (1, 0)),
            pl.BlockSpec((1, D_MODEL), lambda i: (0, 0)),
            pl.BlockSpec((1, GMLP_WIDTH), lambda i: (0, 0)),
            pl.BlockSpec((1, GMLP_WIDTH), lambda i: (0, 0)),
            pl.BlockSpec((GMLP_GROUPS, GMLP_CHUNK, GMLP_CHUNK), lambda i: (0, 0, 0)),
            pl.BlockSpec((GMLP_CHUNK, GMLP_WIDTH), lambda i: (0, 0)),
        ],
        out_specs=[
            pl.BlockSpec((OPROJ_TM, D_MODEL), lambda i: (i, 0)),
            pl.BlockSpec((OPROJ_TM, D_MODEL), lambda i: (i, 0)),
        ],
        out_shape=[
            jax.ShapeDtypeStruct((N_TOK, D_MODEL), F32),
            jax.ShapeDtypeStruct((N_TOK, D_MODEL), BF16),
        ],
        compiler_params=_cparams("arbitrary"),
        name="oproj",
    )(a, proj, x2, w_out, w_out, nw, lnw, lnb, sw, sbx)


def _ffn_kernel(xn_ref, x1_ref, wg_ref, wu_ref, cwg_ref, cwu_ref, cbg_ref, cbu_ref, wd_ref,
                cwg_p_ref, cwu_p_ref, cbg_p_ref, cbu_p_ref, o_ref, h0_ref, h1_ref, carry_ref, wd_p_ref):
    i = pl.program_id(0)
    q = pl.program_id(1)
    nq = pl.num_programs(1)
    half = FFN_TF // 2

    @pl.when((i == 0) & (q == 0))
    def _():
        carry_ref[...] = jnp.zeros_like(carry_ref)
        h1_ref[...] = jnp.zeros_like(h1_ref)
        wd_p_ref[...] = jnp.zeros_like(wd_p_ref)

    @pl.when(q == 0)
    def _():
        o_ref[...] = x1_ref[...]

    seq_start = i % (SEQ // FFN_TM) == 0
    xn = xn_ref[...]

    def produce(c, h_ref):
        h_ref[0, 8:, :] = _dot(xn, wg_ref[:, c * half:(c + 1) * half])
        h_ref[1, 8:, :] = _dot(xn, wu_ref[:, c * half:(c + 1) * half])

    def conv(h_ref, part, tile, cw, cb):
        h_ref[part, 0:8, :] = jnp.where(seq_start, 0.0, carry_ref[tile, part])
        carry_ref[tile, part] = h_ref[part, FFN_TM:FFN_TM + 8, :]
        taps = [h_ref[part, 8 - k:8 - k + FFN_TM, :] for k in (2, 1, 0)]
        return cb + ((cw[0:1, :] * taps[0] + cw[1:2, :] * taps[1]) + cw[2:3, :] * taps[2])

    def gate(h_ref, tile, c, cwg, cwu, cbg, cbu, scale=None):
        cols = slice(c * half, (c + 1) * half)
        cg = conv(h_ref, 0, tile, cwg[:, cols], cbg[:, cols])
        cu = conv(h_ref, 1, tile, cwu[:, cols], cbu[:, cols])
        act = jax.nn.silu(cg) * cu
        if scale is not None:
            act = act * scale
        return act.astype(BF16)

    act_prev = gate(h1_ref, jnp.where(q == 0, 2 * nq - 1, 2 * q - 1), 1, cwg_p_ref, cwu_p_ref, cbg_p_ref,
                    cbu_p_ref, scale=jnp.where(q == 0, 0.0, 1.0))
    produce(0, h0_ref)
    down_prev = _dot(act_prev, wd_p_ref[...])
    act_cur = gate(h0_ref, 2 * q, 0, cwg_ref, cwu_ref, cbg_ref, cbu_ref)
    produce(1, h1_ref)
    o_ref[...] += down_prev + _dot(act_cur, wd_ref[0:half, :])
    wd_p_ref[...] = wd_ref[half:, :]

    @pl.when(q == nq - 1)
    def _():
        act_last = gate(h1_ref, 2 * q + 1, 1, cwg_ref, cwu_ref, cbg_ref, cbu_ref)
        o_ref[...] += _dot(act_last, wd_ref[half:, :])


def _ffn(xn, x1, w_up, conv_w, conv_b, w_down):
    nj = D_FF // FFN_TF
    prev = lambda q: jnp.maximum(q - 1, 0)
    return pl.pallas_call(
        _ffn_kernel,
        grid=(N_TOK // FFN_TM, nj),
        in_specs=[
            pl.BlockSpec((FFN_TM, D_MODEL), lambda i, q: (i, 0)),
            pl.BlockSpec((FFN_TM, D_MODEL), lambda i, q: (i, 0)),
            pl.BlockSpec((None, D_MODEL, FFN_TF), lambda i, q: (q, 0, 0)),
            pl.BlockSpec((None, D_MODEL, FFN_TF), lambda i, q: (nj + q, 0, 0)),
            pl.BlockSpec((CONV_WIDTH, FFN_TF), lambda i, q: (0, q)),
            pl.BlockSpec((CONV_WIDTH, FFN_TF), lambda i, q: (0, nj + q)),
            pl.BlockSpec((1, FFN_TF), lambda i, q: (0, q)),
            pl.BlockSpec((1, FFN_TF), lambda i, q: (0, nj + q)),
            pl.BlockSpec((FFN_TF, D_MODEL), lambda i, q: (q, 0)),
            pl.BlockSpec((CONV_WIDTH, FFN_TF), lambda i, q: (0, prev(q))),
            pl.BlockSpec((CONV_WIDTH, FFN_TF), lambda i, q: (0, nj + prev(q))),
            pl.BlockSpec((1, FFN_TF), lambda i, q: (0, prev(q))),
            pl.BlockSpec((1, FFN_TF), lambda i, q: (0, nj + prev(q))),
        ],
        out_specs=pl.BlockSpec((FFN_TM, D_MODEL), lambda i, q: (i, 0)),
        out_shape=jax.ShapeDtypeStruct((N_TOK, D_MODEL), F32),
        scratch_shapes=[pltpu.VMEM((2, FFN_TM + 8, FFN_TF // 2), F32), pltpu.VMEM((2, FFN_TM + 8, FFN_TF // 2), F32),
                        pltpu.VMEM((2 * nj, 2, 8, FFN_TF // 2), F32),
                        pltpu.VMEM((FFN_TF // 2, D_MODEL), BF16)],
        compiler_params=_cparams("arbitrary", "arbitrary"),
        name="ffn",
    )(xn, x1, w_up, w_up, conv_w, conv_w, conv_b, conv_b, w_down, conv_w, conv_w, conv_b, conv_b)


def _overlap_t():
    start = np.arange(N_CMP_PAD)[None, :] * CMP_STRIDE
    s0 = np.arange(N_SEL)[:, None] * SEL_BLOCK
    ov = (start < s0 + SEL_BLOCK) & (start + CMP_BLOCK > s0) & (np.arange(N_CMP_PAD)[None, :] < N_CMP_PAD - 1)
    return ov.astype(np.float32)


def _bf16_round(a):
    return np.asarray(a, np.float32).astype(BF16).astype(np.float32)


def _slope_features():
    sl = (np.power(2.0, -8.0 * np.arange(1, NSA_HEADS + 1) / NSA_HEADS).astype(np.float32)
          * np.float32(LOG2E)).astype(np.float32)
    s1 = _bf16_round(sl)
    s2 = _bf16_round(sl - s1)
    s3 = _bf16_round(sl - s1 - s2)
    rows = np.stack([s1, s2, s3, s1, s2, s3, -sl, np.zeros_like(sl)], axis=0)
    rows = rows.reshape(8, KV_GROUPS, REP).transpose(1, 0, 2)
    return np.repeat(rows, Q_TILE, axis=2).astype(np.float32)


def _window_mask():
    kl = np.arange(WIN_KEYS)[:, None]
    ql = np.arange(Q_TILE)[None, :]
    dist = ql + WINDOW - kl
    return np.where((dist >= 0) & (dist < WINDOW), 0.0, NEG_BIG).astype(np.float32)


def _diag_masks():
    kl = np.arange(SEL_TILE_KEYS)[None, :, None]
    ql = np.arange(Q_TILE)[None, None, :]
    off = np.arange(SEL_TILE_KEYS // Q_TILE)[:, None, None] * Q_TILE
    return np.where(kl > ql + off, NEG_BIG, 0.0).astype(np.float32)


def _layout_w_in(w_in):
    q_end = NSA_WIDTH
    kv_end = q_end + 6 * KV_COLS
    g_end = kv_end + NSA_HEADS * N_BRANCH
    w_in = w_in.astype(BF16)
    w_q = w_in[:, :q_end]
    w_kv = w_in[:, q_end:kv_end].reshape(D_MODEL, 6, KV_GROUPS, HEAD_DIM)
    w_kv = jnp.stack([w_kv[:, s] for s in (2, 4, 0, 1, 3, 5)], axis=2).reshape(D_MODEL, 6 * KV_COLS)
    w_z = w_in[:, g_end:]
    w_main = jnp.concatenate([w_z, w_q, w_kv], axis=1)
    w_g = w_in[:, kv_end:g_end].reshape(D_MODEL, KV_GROUPS, REP, N_BRANCH).transpose(0, 1, 3, 2)
    w_g = w_g.reshape(D_MODEL, KV_GROUPS, N_BRANCH * REP)
    w_g = jnp.pad(w_g, ((0, 0), (0, 0), (0, GATE_ROWS_PER_GROUP - N_BRANCH * REP)))
    w_g = jnp.pad(w_g.reshape(D_MODEL, KV_GROUPS * GATE_ROWS_PER_GROUP),
                  ((0, 0), (0, GATE_PAD - KV_GROUPS * GATE_ROWS_PER_GROUP)))
    return w_main, w_g


def _layout_compress(cmp_pos, cmp_w1, cmp_w2):
    pos2 = jnp.concatenate([cmp_pos[0], cmp_pos[1]], axis=1)
    w1 = cmp_w1.reshape(2, CMP_BLOCK, HEAD_DIM, CMP_HIDDEN)
    zero = jnp.zeros_like(w1[0])
    w1k = jnp.concatenate([w1[0], zero], axis=1)
    w1v = jnp.concatenate([zero, w1[1]], axis=1)
    w1p = jnp.concatenate([w1k, w1v], axis=2).astype(BF16)
    w1p = w1p.reshape(CMP_BLOCK // 2, 2 * LANES, 2 * CMP_HIDDEN)
    w2p = jnp.pad(cmp_w2, ((0, 0), (0, 0), (0, LANES - HEAD_DIM))).astype(BF16)
    return pos2, w1p, w2p


def kernel(x, attn_norm_w, w_in, q_norm_w, k_norm_w, cmp_pos, cmp_w1, cmp_w2, gmlp_ln_w, gmlp_ln_b,
           spatial_w, spatial_b, w_out, ffn_norm_w, w_up, conv_w, conv_b, w_down):
    x2 = x.reshape(N_TOK, D_MODEL)
    w_main, w_gate = _layout_w_in(w_in)
    proj, gl = _proj(x2, attn_norm_w.reshape(1, D_MODEL), w_main, w_gate)

    knw2 = jnp.concatenate([k_norm_w[1], k_norm_w[2]]).reshape(1, LANES)
    kcw = jnp.concatenate([k_norm_w[0], jnp.zeros((LANES - HEAD_DIM,), F32)]).reshape(1, LANES)
    pos2, w1p, w2p = _layout_compress(cmp_pos, cmp_w1, cmp_w2)
    ksel, kwin, vt, kc, vct, gt = _kprep(proj, gl, knw2, kcw, pos2, w1p, w2p)

    wq_b = jnp.broadcast_to(q_norm_w.reshape(HEAD_DIM, 1), (HEAD_DIM, Q_TILE))
    a = _attn(proj, ksel, kwin, vt, kc, vct, gt, jnp.asarray(_slope_features()),
              jnp.asarray(_window_mask()), jnp.asarray(_diag_masks()),
              jnp.asarray(_overlap_t(), dtype=BF16), wq_b)

    sbx = jnp.repeat(spatial_b.T, GMLP_WIDTH // GMLP_GROUPS, axis=1)
    x1, xn = _oproj(a, proj, x2, w_out.astype(BF16), ffn_norm_w.reshape(1, D_MODEL),
                    gmlp_ln_w.reshape(1, GMLP_WIDTH), gmlp_ln_b.reshape(1, GMLP_WIDTH), spatial_w, sbx)

    w_up_tiles = w_up.reshape(D_MODEL, 2 * D_FF // FFN_TF, FFN_TF).transpose(1, 0, 2).astype(BF16)
    out = _ffn(xn, x1, w_up_tiles, conv_w, conv_b.reshape(1, 2 * D_FF), w_down.astype(BF16))
    return out.reshape(BATCH, SEQ, D_MODEL)
```

```python
import numpy as np
import jax
import jax.numpy as jnp
from jax import lax
from jax.experimental import pallas as pl
from jax.experimental.pallas import tpu as pltpu

F32 = jnp.float32
BF16 = jnp.bfloat16

D_MODEL = 2048
BATCH = 4
SEQ = 4096
N_TOK = BATCH * SEQ
NSA_HEADS = 16
KV_GROUPS = 4
REP = NSA_HEADS // KV_GROUPS
HEAD_DIM = 64
NSA_WIDTH = NSA_HEADS * HEAD_DIM
GMLP_WIDTH = D_MODEL - NSA_WIDTH
GMLP_GROUPS = 8
GMLP_CHUNK = 128
CMP_BLOCK = 32
CMP_STRIDE = 16
CMP_HIDDEN = 256
N_CMP_PAD = SEQ // CMP_STRIDE
SEL_BLOCK = 64
N_SEL = SEQ // SEL_BLOCK
SEL_TOPK = 16
WINDOW = 512
K_BLOCK = 128
N_KB = SEQ // K_BLOCK
Q_TILE = 256
N_QT = SEQ // Q_TILE
N_BRANCH = 3
KV_COLS = KV_GROUPS * HEAD_DIM
D_FF = 5632
CONV_WIDTH = 3
NORM_EPS = 1e-6
LN_EPS = 1e-5
NEG_BIG = -1e30
SEL_BIG = 1e9

LANES = 128
VMEM_LIMIT_BYTES = 56 * 1024 * 1024

Z_COLS = 2 * GMLP_WIDTH
PROJ_COLS = Z_COLS + NSA_WIDTH + 6 * KV_COLS
GATE_PAD = LANES
GATE_ROWS_PER_GROUP = 16

PROJ_TM = 1024
PROJ_TN = 1536
OPROJ_TM = 512
FFN_TM = 512
FFN_TF = 512
SEL_TILE_BLOCKS = 4
SEL_TILE_KEYS = SEL_TILE_BLOCKS * K_BLOCK
MASK_LANE0 = 80
SUM_ROWS = 16
VT_ROWS = HEAD_DIM + SUM_ROWS
LOG2E = 1.4426950408889634
WIN_BLOCKS = (WINDOW + Q_TILE) // K_BLOCK
WIN_KEYS = WIN_BLOCKS * K_BLOCK


def _cparams(*sem):
    return pltpu.CompilerParams(dimension_semantics=sem, vmem_limit_bytes=VMEM_LIMIT_BYTES)


def _dot(a, b):
    return jnp.dot(a, b, preferred_element_type=F32)


def _proj_kernel(x_ref, nw_ref, w_ref, wg_ref, o_ref, og_ref, h_ref):
    @pl.when(pl.program_id(1) == 0)
    def _():
        x = x_ref[...]
        ms = jnp.mean(x * x, axis=-1, keepdims=True)
        h = (x * lax.rsqrt(ms + NORM_EPS) * nw_ref[...]).astype(BF16)
        h_ref[...] = h
        og_ref[...] = _dot(h, wg_ref[...])

    o_ref[...] = _dot(h_ref[...], w_ref[...])


def _proj(x2, nw, w_main, w_gate):
    return pl.pallas_call(
        _proj_kernel,
        grid=(N_TOK // PROJ_TM, PROJ_COLS // PROJ_TN),
        in_specs=[
            pl.BlockSpec((PROJ_TM, D_MODEL), lambda i, j: (i, 0)),
            pl.BlockSpec((1, D_MODEL), lambda i, j: (0, 0)),
            pl.BlockSpec((D_MODEL, PROJ_TN), lambda i, j: (0, j)),
            pl.BlockSpec((D_MODEL, GATE_PAD), lambda i, j: (0, 0)),
        ],
        out_specs=[
            pl.BlockSpec((PROJ_TM, PROJ_TN), lambda i, j: (i, j)),
            pl.BlockSpec((PROJ_TM, GATE_PAD), lambda i, j: (i, 0)),
        ],
        out_shape=[
            jax.ShapeDtypeStruct((N_TOK, PROJ_COLS), F32),
            jax.ShapeDtypeStruct((N_TOK, GATE_PAD), F32),
        ],
        scratch_shapes=[pltpu.VMEM((PROJ_TM, D_MODEL), BF16)],
        compiler_params=_cparams("arbitrary", "arbitrary"),
        name="proj",
    )(x2, nw, w_main, w_gate)


def _pos_features(lane, first, f1, f2):
    return jnp.where(lane < first + 3, f1, jnp.where(lane < first + 6, f2,
                                                     jnp.where(lane == first + 6, 1.0, 0.0)))


def _kprep_kernel(s0_ref, s1_ref, s2_ref, gl_ref, knw_ref, kcw_ref, pos_ref, w1_ref, w2_ref,
                  ksel_ref, kwin_ref, vt_ref, kc_ref, vct_ref, gt_ref, ha_ref, hb_ref):
    lane = lax.broadcasted_iota(jnp.int32, (K_BLOCK, LANES), 1)
    row = lax.broadcasted_iota(jnp.int32, (K_BLOCK, LANES), 0)
    lo = lane < HEAD_DIM
    inv_d = 1.0 / HEAD_DIM
    ones_rows = jnp.where(lax.broadcasted_iota(jnp.int32, (SUM_ROWS, K_BLOCK), 0) == 0, 1.0, 0.0)

    def chunk(c, carry):
        r0 = pl.multiple_of(c * K_BLOCK, K_BLOCK)
        x = s0_ref[pl.ds(r0, K_BLOCK), :]
        sq = x * x
        s_lo = jnp.sum(jnp.where(lo, sq, 0.0), axis=1, keepdims=True)
        s_hi = jnp.sum(jnp.where(lo, 0.0, sq), axis=1, keepdims=True)
        ms = jnp.where(lo, s_lo, s_hi) * inv_d
        y = x * lax.rsqrt(ms + NORM_EPS) * knw_ref[...]
        pos = r0 + row
        blk = lax.shift_right_logical(pos, 6)
        f1 = (blk * SEL_BLOCK).astype(F32)
        f2 = (pos & (SEL_BLOCK - 1)).astype(F32)
        onehot = jnp.where(lane - MASK_LANE0 == (blk & (SEL_TILE_BLOCKS * 2 - 1)), 1.0, 0.0)
        feat_sel = jnp.where(lane < MASK_LANE0, _pos_features(lane, HEAD_DIM, f1, f2), onehot)
        ksel_ref[c] = jnp.where(lo, y, feat_sel).astype(BF16)
        kwin_ref[c] = jnp.where(lo, _pos_features(lane, 0, f1, f2), y).astype(BF16)
        v_t = s2_ref[pl.ds(r0, K_BLOCK), :].T
        vt_ref[c] = jnp.concatenate(
            [v_t[0:HEAD_DIM], ones_rows, v_t[HEAD_DIM:], ones_rows], axis=0).astype(BF16)
        return carry

    lax.fori_loop(0, N_KB, chunk, 0, unroll=2)

    @pl.when(pl.program_id(1) == 0)
    def _():
        def gchunk(c, carry):
            r0 = pl.multiple_of(c * K_BLOCK, K_BLOCK)
            gt_ref[c] = jax.nn.sigmoid(gl_ref[pl.ds(r0, K_BLOCK), :]).T
            return carry

        lax.fori_loop(0, N_KB, gchunk, 0, unroll=2)

    ha_ref[...] = jnp.zeros_like(ha_ref)
    hb_ref[...] = jnp.zeros_like(hb_ref)
    for tp in range(CMP_STRIDE // 2):
        rows = [s1_ref[pl.ds(2 * tp + e, N_CMP_PAD, stride=CMP_STRIDE), :] for e in range(2)]
        xa = jnp.concatenate([(rows[e] + pos_ref[2 * tp + e:2 * tp + e + 1, :]).astype(BF16)
                              for e in range(2)], axis=1)
        xb = jnp.concatenate([(rows[e] + pos_ref[CMP_STRIDE + 2 * tp + e:CMP_STRIDE + 2 * tp + e + 1, :])
                              .astype(BF16) for e in range(2)], axis=1)
        ha_ref[...] += _dot(xa, w1_ref[tp])
        hb_ref[0:N_CMP_PAD, :] += _dot(xb, w1_ref[CMP_STRIDE // 2 + tp])
    hid = jax.nn.gelu(ha_ref[...] + hb_ref[1:N_CMP_PAD + 1, :]).astype(BF16)
    kc = _dot(hid[:, :CMP_HIDDEN], w2_ref[0])
    vc = _dot(hid[:, CMP_HIDDEN:], w2_ref[1])
    ms = jnp.sum(kc * kc, axis=1, keepdims=True) * inv_d
    kcn = kc * lax.rsqrt(ms + NORM_EPS) * kcw_ref[...]
    ci = lax.broadcasted_iota(jnp.int32, (N_CMP_PAD, LANES), 0)
    cl = lax.broadcasted_iota(jnp.int32, (N_CMP_PAD, LANES), 1)
    c1 = (lax.shift_right_logical(ci, 2) * SEL_BLOCK).astype(F32)
    c2 = ((ci & 3) * CMP_STRIDE).astype(F32) + (CMP_BLOCK - 1) * 0.5
    kc_ref[...] = jnp.where(cl < HEAD_DIM, kcn, _pos_features(cl, HEAD_DIM, c1, c2)).astype(BF16)
    vct_ref[...] = vc.T[0:HEAD_DIM, :].astype(BF16)


def _kprep(proj, gl, knw2, kcw, pos2, w1p, w2p):
    kv0 = (Z_COLS + NSA_WIDTH) // LANES
    slab = lambda s: pl.BlockSpec((SEQ, LANES), lambda b, g, s=s: (b, kv0 + 3 * g + s))
    full = lambda shape: pl.BlockSpec(shape, lambda b, g: (0,) * len(shape))
    per_bg = lambda shape: pl.BlockSpec((None, None) + shape, lambda b, g: (b, g) + (0,) * len(shape))
    bg_shape = lambda shape, dt: jax.ShapeDtypeStruct((BATCH, KV_GROUPS) + shape, dt)
    return pl.pallas_call(
        _kprep_kernel,
        grid=(BATCH, KV_GROUPS),
        in_specs=[
            slab(0), slab(1), slab(2),
            pl.BlockSpec((SEQ, GATE_PAD), lambda b, g: (b, 0)),
            full((1, LANES)), full((1, LANES)), full((CMP_BLOCK, LANES)),
            full((CMP_BLOCK // 2, 2 * LANES, 2 * CMP_HIDDEN)), full((2, CMP_HIDDEN, LANES)),
        ],
        out_specs=[
            per_bg((N_KB, K_BLOCK, LANES)),
            per_bg((N_KB, K_BLOCK, LANES)),
            per_bg((N_KB, 2 * VT_ROWS, K_BLOCK)),
            per_bg((N_CMP_PAD, LANES)),
            per_bg((HEAD_DIM, N_CMP_PAD)),
            pl.BlockSpec((None, N_KB, GATE_PAD, K_BLOCK), lambda b, g: (b, 0, 0, 0)),
        ],
        out_shape=[
            bg_shape((N_KB, K_BLOCK, LANES), BF16),
            bg_shape((N_KB, K_BLOCK, LANES), BF16),
            bg_shape((N_KB, 2 * VT_ROWS, K_BLOCK), BF16),
            bg_shape((N_CMP_PAD, LANES), BF16),
            bg_shape((HEAD_DIM, N_CMP_PAD), BF16),
            jax.ShapeDtypeStruct((BATCH, N_KB, GATE_PAD, K_BLOCK), F32),
        ],
        scratch_shapes=[
            pltpu.VMEM((N_CMP_PAD, 2 * CMP_HIDDEN), F32),
            pltpu.VMEM((N_CMP_PAD + 8, 2 * CMP_HIDDEN), F32),
        ],
        compiler_params=_cparams("arbitrary", "arbitrary"),
        name="kprep",
    )(proj, proj, proj, gl, knw2, kcw, pos2, w1p, w2p)


def _attn_kernel(q_ref, ksel_ref, kwin_ref, vt_ref, kc_ref, vct_ref, gt_ref, sf_ref, wm_ref, dm_ref,
                 ov_ref, wq_ref, o_ref, sel_ref, qs_ref, sa_ref, sb_ref, tiles_ref):
    qt = pl.program_id(2)
    t_i = qt * Q_TILE + lax.broadcasted_iota(jnp.int32, (1, Q_TILE), 1)
    head = lambda a, r: a[:, r * Q_TILE:(r + 1) * Q_TILE]
    heads = lambda parts: jnp.concatenate(parts, axis=1)

    q_t = q_ref[...].T
    parts = []
    for r in range(REP):
        xq = q_t[r * HEAD_DIM:(r + 1) * HEAD_DIM, :]
        ms = jnp.sum(xq * xq, axis=0, keepdims=True) * (1.0 / HEAD_DIM)
        parts.append(xq * lax.rsqrt(ms + NORM_EPS) * wq_ref[...] * (HEAD_DIM ** -0.5) * LOG2E)
    qn = heads(parts)
    sf = sf_ref[...]
    sf = jnp.where(lax.broadcasted_iota(jnp.int32, sf.shape, 0) == 6, sf * (qt * Q_TILE).astype(F32), sf)
    wide = REP * Q_TILE
    pad = jnp.zeros((HEAD_DIM - 8, wide), F32)
    qs_ref[...] = jnp.concatenate([qn, sf, pad], axis=0).astype(BF16)
    q_win = jnp.concatenate([sf, pad, qn], axis=0).astype(BF16)

    sc = _dot(kc_ref[...], qs_ref[...])
    ci = lax.broadcasted_iota(jnp.int32, (N_CMP_PAD, Q_TILE), 0)
    cm = jnp.where(ci * CMP_STRIDE + (CMP_BLOCK - 1) <= t_i, 0.0, NEG_BIG)
    any_valid = t_i >= CMP_BLOCK - 1
    pcs = []
    for r in range(REP):
        s = head(sc, r) + cm
        e = jnp.exp2(s - jnp.max(s, axis=0, keepdims=True))
        d = jnp.sum(e, axis=0, keepdims=True)
        pcs.append((e * jnp.where(any_valid, 1.0 / d, 0.0)).astype(BF16))
    oc_imp = _dot(jnp.concatenate([vct_ref[...], ov_ref[...]], axis=0), heads(pcs))
    o_c = oc_imp[0:HEAD_DIM]
    imp4 = oc_imp[HEAD_DIM:]
    imp = (head(imp4, 0) + head(imp4, 1)) + (head(imp4, 2) + head(imp4, 3))

    jj = lax.broadcasted_iota(jnp.int32, (N_SEL, Q_TILE), 0)
    cur = lax.shift_right_logical(t_i, 6)
    valid = jj <= cur
    sel_ref[...] = jnp.where(valid, 1.0, 0.0)
    forced = (jj == 0) | (jj == cur) | (jj == cur - 1)
    score = jnp.where(valid, jnp.where(forced, SEL_BIG, imp), -SEL_BIG)

    def rank_rows(n8):
        def branch():
            rows8 = [score[8 * v:8 * v + 8, :] for v in range(n8)]
            j8 = lax.broadcasted_iota(jnp.int32, (8, Q_TILE), 0)
            ranks = [jnp.zeros((8, Q_TILE), F32) for _ in rows8]
            for jp in range(8 * n8):
                row = score[jp:jp + 1, :]
                for v, blk in enumerate(rows8):
                    if 8 * v > jp:
                        ahead = jnp.where(row >= blk, 1.0, 0.0)
                    elif 8 * v + 7 <= jp:
                        ahead = jnp.where(row > blk, 1.0, 0.0)
                    else:
                        ahead = jnp.where(j8 + 8 * v > jp, jnp.where(row >= blk, 1.0, 0.0),
                                          jnp.where(row > blk, 1.0, 0.0))
                    ranks[v] = ranks[v] + ahead
            rank = jnp.concatenate(ranks, axis=0)
            sel_ref[0:8 * n8, :] = jnp.where(valid[0:8 * n8] & (rank < SEL_TOPK), 1.0, 0.0)
        return branch

    n8_needed = (qt * Q_TILE + Q_TILE - 1) // (8 * SEL_BLOCK) + 1
    lax.switch(n8_needed - 1, [(lambda: None) if n8 * 8 <= SEL_TOPK else rank_rows(n8)
                               for n8 in range(1, N_SEL // 8 + 1)])

    sub = SEL_TILE_KEYS // SEL_BLOCK

    def scores_into(kt, s_ref):
        member = sel_ref[pl.ds(kt * sub, sub), :]
        mrows = (member - 1.0) * (-NEG_BIG)
        qs_ref[MASK_LANE0:MASK_LANE0 + 16, :] = jnp.concatenate(
            [heads([mrows] * REP), jnp.zeros((16 - sub, wide), F32)], axis=0).astype(BF16)
        k_tile = ksel_ref[pl.ds(kt * SEL_TILE_BLOCKS, SEL_TILE_BLOCKS)].reshape(SEL_TILE_KEYS, LANES)
        s_ref[...] = _dot(k_tile, qs_ref[...])

    def absorb(kt, s_ref, carry, diagonal):
        m_run, acc = carry
        kb0 = kt * SEL_TILE_BLOCKS
        if diagonal:
            causal = dm_ref[qt % (SEL_TILE_KEYS // Q_TILE)]
        v_t = jnp.concatenate([vt_ref[kb0 + x, 0:VT_ROWS, :] for x in range(SEL_TILE_BLOCKS)], axis=1)
        ms_, accs = [], []
        for r in range(REP):
            sr = s_ref[:, r * Q_TILE:(r + 1) * Q_TILE]
            if diagonal:
                sr = sr + causal
            m_old = head(m_run, r)
            m_new = jnp.maximum(m_old, jnp.max(sr, axis=0, keepdims=True))
            p = jnp.exp2(sr - m_new).astype(BF16)
            accs.append(head(acc, r) * jnp.exp2(m_old - m_new) + _dot(v_t, p))
            ms_.append(m_new)
        return heads(ms_), heads(accs)

    carry = (jnp.full((1, wide), NEG_BIG, F32), jnp.zeros((VT_ROWS, wide), F32))
    n_full = (qt * Q_TILE) // SEL_TILE_KEYS

    n_sel_tiles = SEQ // SEL_TILE_KEYS
    tr = lax.broadcasted_iota(jnp.int32, (n_sel_tiles, N_SEL), 0)
    tc = lax.broadcasted_iota(jnp.int32, (n_sel_tiles, N_SEL), 1)
    tile_of_block = jnp.where(lax.shift_right_logical(tc, 3) == tr, 1.0, 0.0).astype(BF16)
    per_query = _dot(tile_of_block, sel_ref[...].astype(BF16))
    per_tile = _dot(per_query.astype(BF16), jnp.ones((Q_TILE, LANES), BF16))
    bit = lax.shift_left(1, lax.broadcasted_iota(jnp.int32, (n_sel_tiles, LANES), 0)).astype(F32)
    used_bits = jnp.sum(jnp.where(per_tile > 0.5, bit, 0.0), axis=0, keepdims=True)[0, 0].astype(jnp.int32)

    kb_first = qt * (Q_TILE // K_BLOCK) - WINDOW // K_BLOCK
    kbs = [jnp.maximum(kb_first + x, 0) for x in range(WIN_BLOCKS)]
    sw = _dot(jnp.concatenate([kwin_ref[kb] for kb in kbs], axis=0), q_win)

    n_visit = jnp.int32(0)
    for kt in range(n_sel_tiles):
        tiles_ref[n_visit] = kt
        take = (kt < n_full) & ((lax.shift_right_logical(used_bits, kt) & 1) == 1)
        n_visit = n_visit + jnp.where(take, 1, 0)
    tiles_ref[n_visit] = n_full

    scores_into(tiles_ref[0], sa_ref)
    wm = jnp.concatenate(
        [wm_ref[x * K_BLOCK:(x + 1) * K_BLOCK, :] + jnp.where(kb_first + x < 0, NEG_BIG, 0.0)
         for x in range(WIN_BLOCKS)], axis=0)
    pws = []
    for r in range(REP):
        s = head(sw, r) + wm
        pws.append(jnp.exp2(s - jnp.max(s, axis=0, keepdims=True)).astype(BF16))
    vw_t = jnp.concatenate([vt_ref[kb, VT_ROWS:2 * VT_ROWS, :] for kb in kbs], axis=1)
    acc_w = _dot(vw_t, heads(pws))
    o_w = acc_w[0:HEAD_DIM] / acc_w[HEAD_DIM:HEAD_DIM + 1]

    def tile_pair(i, carry):
        scores_into(tiles_ref[2 * i + 1], sb_ref)
        carry = absorb(tiles_ref[2 * i], sa_ref, carry, False)
        scores_into(tiles_ref[2 * i + 2], sa_ref)
        return absorb(tiles_ref[2 * i + 1], sb_ref, carry, False)

    carry = lax.fori_loop(0, n_visit // 2, tile_pair, carry)
    last = 2 * (n_visit // 2)

    def odd_tail(carry):
        scores_into(n_full, sb_ref)
        return absorb(n_full, sb_ref, absorb(tiles_ref[last], sa_ref, carry, False), True)

    def even_tail(carry):
        return absorb(n_full, sa_ref, carry, True)

    _, acc_s = lax.cond(n_visit % 2 == 1, odd_tail, even_tail, carry)
    o_s = acc_s[0:HEAD_DIM] / acc_s[HEAD_DIM:HEAD_DIM + 1]

    gates = jnp.concatenate([gt_ref[x] for x in range(Q_TILE // K_BLOCK)], axis=1)
    outs = []
    for r in range(REP):
        outs.append(gates[r:r + 1, :] * head(o_c, r)
                    + gates[REP + r:REP + r + 1, :] * head(o_s, r)
                    + gates[2 * REP + r:2 * REP + r + 1, :] * head(o_w, r))
    o_ref[...] = jnp.concatenate(outs, axis=0).T.astype(BF16)


def _attn(proj, ksel, kwin, vt, kc, vct, gt, sf, wm, dm, ov_t, wq_b):
    q0 = Z_COLS // (REP * HEAD_DIM)
    kvspec = lambda shape: pl.BlockSpec((None, None) + shape, lambda b, g, qt: (b, g) + (0,) * len(shape))
    const = lambda shape: pl.BlockSpec(shape, lambda b, g, qt: (0,) * len(shape))
    return pl.pallas_call(
        _attn_kernel,
        grid=(BATCH, KV_GROUPS, N_QT),
        in_specs=[
            pl.BlockSpec((Q_TILE, REP * HEAD_DIM), lambda b, g, qt: (b * N_QT + qt, q0 + g)),
            kvspec((N_KB, K_BLOCK, LANES)),
            kvspec((N_KB, K_BLOCK, LANES)),
            kvspec((N_KB, 2 * VT_ROWS, K_BLOCK)),
            kvspec((N_CMP_PAD, LANES)),
            kvspec((HEAD_DIM, N_CMP_PAD)),
            pl.BlockSpec((None, Q_TILE // K_BLOCK, GATE_ROWS_PER_GROUP, K_BLOCK), lambda b, g, qt: (b, qt, g, 0)),
            pl.BlockSpec((None, 8, REP * Q_TILE), lambda b, g, qt: (g, 0, 0)),
            const((WIN_KEYS, Q_TILE)),
            const((SEL_TILE_KEYS // Q_TILE, SEL_TILE_KEYS, Q_TILE)),
            const((N_SEL, N_CMP_PAD)),
            const((HEAD_DIM, Q_TILE)),
        ],
        out_specs=pl.BlockSpec((Q_TILE, REP * HEAD_DIM), lambda b, g, qt: (b * N_QT + qt, g)),
        out_shape=jax.ShapeDtypeStruct((N_TOK, NSA_WIDTH), BF16),
        scratch_shapes=[pltpu.VMEM((N_SEL, Q_TILE), F32), pltpu.VMEM((LANES, REP * Q_TILE), BF16),
                        pltpu.VMEM((SEL_TILE_KEYS, REP * Q_TILE), F32),
                        pltpu.VMEM((SEL_TILE_KEYS, REP * Q_TILE), F32),
                        pltpu.SMEM((SEQ // SEL_TILE_KEYS + 1,), jnp.int32)],
        compiler_params=_cparams("arbitrary", "arbitrary", "arbitrary"),
        name="attn",
    )(proj, ksel, kwin, vt, kc, vct, gt, sf, wm, dm, ov_t, wq_b)


def _spatial_gating(z, lnw, lnb, sw_ref, sbx):
    ge = jax.nn.gelu(z)
    u = ge[:, :GMLP_WIDTH]
    v = ge[:, GMLP_WIDTH:]
    mu = jnp.mean(v, axis=-1, keepdims=True)
    var = jnp.mean(jnp.square(v - mu), axis=-1, keepdims=True)
    vn = ((v - mu) * lax.rsqrt(var + LN_EPS) * lnw + lnb).astype(BF16)
    ti = lax.broadcasted_iota(jnp.int32, (GMLP_CHUNK, GMLP_CHUNK), 0)
    si = lax.broadcasted_iota(jnp.int32, (GMLP_CHUNK, GMLP_CHUNK), 1)
    ws = [jnp.where(ti >= si, sw_ref[gg], 0.0).astype(BF16) for gg in range(GMLP_GROUPS)]
    gd = GMLP_WIDTH // GMLP_GROUPS
    outs = []
    for c in range(z.shape[0] // GMLP_CHUNK):
        rows = slice(c * GMLP_CHUNK, (c + 1) * GMLP_CHUNK)
        mix = jnp.concatenate(
            [_dot(ws[gg], vn[rows, gg * gd:(gg + 1) * gd]) for gg in range(GMLP_GROUPS)], axis=1)
        outs.append((u[rows, :] * (mix + sbx)).astype(BF16))
    return jnp.concatenate(outs, axis=0)


def _oproj_kernel(a_ref, z_ref, x_ref, wa_ref, wb_ref, nw_ref, lnw_ref, lnb_ref, sw_ref, sbx_ref,
                  x1_ref, xn_ref):
    attn_part = _dot(a_ref[...], wa_ref[...])
    b = _spatial_gating(z_ref[...], lnw_ref[...], lnb_ref[...], sw_ref, sbx_ref[...])
    y = x_ref[...] + (attn_part + _dot(b, wb_ref[...]))
    x1_ref[...] = y
    ms = jnp.mean(y * y, axis=-1, keepdims=True)
    xn_ref[...] = (y * lax.rsqrt(ms + NORM_EPS) * nw_ref[...]).astype(BF16)


def _oproj(a, proj, x2, w_out, nw, lnw, lnb, sw, sbx):
    return pl.pallas_call(
        _oproj_kernel,
        grid=(N_TOK // OPROJ_TM,),
        in_specs=[
            pl.BlockSpec((OPROJ_TM, NSA_WIDTH), lambda i: (i, 0)),
            pl.BlockSpec((OPROJ_TM, Z_COLS), lambda i: (i, 0)),
            pl.BlockSpec((OPROJ_TM, D_MODEL), lambda i: (i, 0)),
            pl.BlockSpec((NSA_WIDTH, D_MODEL), lambda i: (0, 0)),
            pl.BlockSpec((GMLP_WIDTH, D_MODEL), lambda i: (1, 0)),
            pl.BlockSpec((1, D_MODEL), lambda i: (0, 0)),
            pl.BlockSpec((1, GMLP_WIDTH), lambda i: (0, 0)),
            pl.BlockSpec((1, GMLP_WIDTH), lambda i: (0, 0)),
            pl.BlockSpec((GMLP_GROUPS, GMLP_CHUNK, GMLP_CHUNK), lambda i: (0, 0, 0)),
            pl.BlockSpec((GMLP_CHUNK, GMLP_WIDTH), lambda i: (0, 0)),
        ],
        out_specs=[
            pl.BlockSpec((OPROJ_TM, D_MODEL), lambda i: (i, 0)),
            pl.BlockSpec((OPROJ_TM, D_MODEL), lambda i: (i, 0)),
        ],
        out_shape=[
            jax.ShapeDtypeStruct((N_TOK, D_MODEL), F32),
            jax.ShapeDtypeStruct((N_TOK, D_MODEL), BF16),
        ],
        compiler_params=_cparams("arbitrary"),
        name="oproj",
    )(a, proj, x2, w_out, w_out, nw, lnw, lnb, sw, sbx)


def _ffn_kernel(xn_ref, x1_ref, wg_ref, wu_ref, cwg_ref, cwu_ref, cbg_ref, cbu_ref, wd_ref,
                cwg_p_ref, cwu_p_ref, cbg_p_ref, cbu_p_ref, o_ref, h0_ref, h1_ref, carry_ref, wd_p_ref):
    i = pl.program_id(0)
    q = pl.program_id(1)
    nq = pl.num_programs(1)
    half = FFN_TF // 2

    @pl.when((i == 0) & (q == 0))
    def _():
        carry_ref[...] = jnp.zeros_like(carry_ref)

    seq_start = i % (SEQ // FFN_TM) == 0
    xn = xn_ref[...]

    def produce(c, h_ref):
        h_ref[0, 8:, :] = _dot(xn, wg_ref[:, c * half:(c + 1) * half])
        h_ref[1, 8:, :] = _dot(xn, wu_ref[:, c * half:(c + 1) * half])

    def conv(h_ref, part, tile, cw, cb):
        h_ref[part, 0:8, :] = jnp.where(seq_start, 0.0, carry_ref[tile, part])
        carry_ref[tile, part] = h_ref[part, FFN_TM:FFN_TM + 8, :]
        taps = [h_ref[part, 8 - k:8 - k + FFN_TM, :] for k in (2, 1, 0)]
        return cb + ((cw[0:1, :] * taps[0] + cw[1:2, :] * taps[1]) + cw[2:3, :] * taps[2])

    def gate(h_ref, tile, c, cwg, cwu, cbg, cbu):
        cols = slice(c * half, (c + 1) * half)
        cg = conv(h_ref, 0, tile, cwg[:, cols], cbg[:, cols])
        cu = conv(h_ref, 1, tile, cwu[:, cols], cbu[:, cols])
        return (jax.nn.silu(cg) * cu).astype(BF16)

    def first_step():
        produce(0, h0_ref)
        act_cur = gate(h0_ref, 2 * q, 0, cwg_ref, cwu_ref, cbg_ref, cbu_ref)
        produce(1, h1_ref)
        o_ref[...] = x1_ref[...] + _dot(act_cur, wd_ref[0:half, :])

    def later_step():
        act_prev = gate(h1_ref, 2 * q - 1, 1, cwg_p_ref, cwu_p_ref, cbg_p_ref, cbu_p_ref)
        produce(0, h0_ref)
        down_prev = _dot(act_prev, wd_p_ref[...])
        act_cur = gate(h0_ref, 2 * q, 0, cwg_ref, cwu_ref, cbg_ref, cbu_ref)
        produce(1, h1_ref)
        o_ref[...] += down_prev + _dot(act_cur, wd_ref[0:half, :])

    lax.cond(q == 0, first_step, later_step)
    wd_p_ref[...] = wd_ref[half:, :]

    @pl.when(q == nq - 1)
    def _():
        act_last = gate(h1_ref, 2 * q + 1, 1, cwg_ref, cwu_ref, cbg_ref, cbu_ref)
        o_ref[...] += _dot(act_last, wd_ref[half:, :])


def _ffn(xn, x1, w_up, conv_w, conv_b, w_down):
    nj = D_FF // FFN_TF
    prev = lambda q: jnp.maximum(q - 1, 0)
    return pl.pallas_call(
        _ffn_kernel,
        grid=(N_TOK // FFN_TM, nj),
        in_specs=[
            pl.BlockSpec((FFN_TM, D_MODEL), lambda i, q: (i, 0)),
            pl.BlockSpec((FFN_TM, D_MODEL), lambda i, q: (i, 0)),
            pl.BlockSpec((D_MODEL, FFN_TF), lambda i, q: (0, q)),
            pl.BlockSpec((D_MODEL, FFN_TF), lambda i, q: (0, nj + q)),
            pl.BlockSpec((CONV_WIDTH, FFN_TF), lambda i, q: (0, q)),
            pl.BlockSpec((CONV_WIDTH, FFN_TF), lambda i, q: (0, nj + q)),
            pl.BlockSpec((1, FFN_TF), lambda i, q: (0, q)),
            pl.BlockSpec((1, FFN_TF), lambda i, q: (0, nj + q)),
            pl.BlockSpec((FFN_TF, D_MODEL), lambda i, q: (q, 0)),
            pl.BlockSpec((CONV_WIDTH, FFN_TF), lambda i, q: (0, prev(q))),
            pl.BlockSpec((CONV_WIDTH, FFN_TF), lambda i, q: (0, nj + prev(q))),
            pl.BlockSpec((1, FFN_TF), lambda i, q: (0, prev(q))),
            pl.BlockSpec((1, FFN_TF), lambda i, q: (0, nj + prev(q))),
        ],
        out_specs=pl.BlockSpec((FFN_TM, D_MODEL), lambda i, q: (i, 0)),
        out_shape=jax.ShapeDtypeStruct((N_TOK, D_MODEL), F32),
        scratch_shapes=[pltpu.VMEM((2, FFN_TM + 8, FFN_TF // 2), F32), pltpu.VMEM((2, FFN_TM + 8, FFN_TF // 2), F32),
                        pltpu.VMEM((2 * nj, 2, 8, FFN_TF // 2), F32),
                        pltpu.VMEM((FFN_TF // 2, D_MODEL), BF16)],
        compiler_params=_cparams("arbitrary", "arbitrary"),
        name="ffn",
    )(xn, x1, w_up, w_up, conv_w, conv_w, conv_b, conv_b, w_down, conv_w, conv_w, conv_b, conv_b)


def _overlap_t():
    start = np.arange(N_CMP_PAD)[None, :] * CMP_STRIDE
    s0 = np.arange(N_SEL)[:, None] * SEL_BLOCK
    ov = (start < s0 + SEL_BLOCK) & (start + CMP_BLOCK > s0) & (np.arange(N_CMP_PAD)[None, :] < N_CMP_PAD - 1)
    return ov.astype(np.float32)


def _bf16_round(a):
    return np.asarray(a, np.float32).astype(BF16).astype(np.float32)


def _slope_features():
    sl = (np.power(2.0, -8.0 * np.arange(1, NSA_HEADS + 1) / NSA_HEADS).astype(np.float32)
          * np.float32(LOG2E)).astype(np.float32)
    s1 = _bf16_round(sl)
    s2 = _bf16_round(sl - s1)
    s3 = _bf16_round(sl - s1 - s2)
    rows = np.stack([s1, s2, s3, s1, s2, s3, -sl, np.zeros_like(sl)], axis=0)
    rows = rows.reshape(8, KV_GROUPS, REP).transpose(1, 0, 2)
    return np.repeat(rows, Q_TILE, axis=2).astype(np.float32)


def _window_mask():
    kl = np.arange(WIN_KEYS)[:, None]
    ql = np.arange(Q_TILE)[None, :]
    dist = ql + WINDOW - kl
    return np.where((dist >= 0) & (dist < WINDOW), 0.0, NEG_BIG).astype(np.float32)


def _diag_masks():
    kl = np.arange(SEL_TILE_KEYS)[None, :, None]
    ql = np.arange(Q_TILE)[None, None, :]
    off = np.arange(SEL_TILE_KEYS // Q_TILE)[:, None, None] * Q_TILE
    return np.where(kl > ql + off, NEG_BIG, 0.0).astype(np.float32)


def _layout_w_in(w_in):
    q_end = NSA_WIDTH
    kv_end = q_end + 6 * KV_COLS
    g_end = kv_end + NSA_HEADS * N_BRANCH
    w_in = w_in.astype(BF16)
    w_q = w_in[:, :q_end]
    w_kv = w_in[:, q_end:kv_end].reshape(D_MODEL, 6, KV_GROUPS, HEAD_DIM)
    w_kv = jnp.stack([w_kv[:, s] for s in (2, 4, 0, 1, 3, 5)], axis=2).reshape(D_MODEL, 6 * KV_COLS)
    w_z = w_in[:, g_end:]
    w_main = jnp.concatenate([w_z, w_q, w_kv], axis=1)
    w_g = w_in[:, kv_end:g_end].reshape(D_MODEL, KV_GROUPS, REP, N_BRANCH).transpose(0, 1, 3, 2)
    w_g = w_g.reshape(D_MODEL, KV_GROUPS, N_BRANCH * REP)
    w_g = jnp.pad(w_g, ((0, 0), (0, 0), (0, GATE_ROWS_PER_GROUP - N_BRANCH * REP)))
    w_g = jnp.pad(w_g.reshape(D_MODEL, KV_GROUPS * GATE_ROWS_PER_GROUP),
                  ((0, 0), (0, GATE_PAD - KV_GROUPS * GATE_ROWS_PER_GROUP)))
    return w_main, w_g


def _layout_compress(cmp_pos, cmp_w1, cmp_w2):
    pos2 = jnp.concatenate([cmp_pos[0], cmp_pos[1]], axis=1)
    w1 = cmp_w1.reshape(2, CMP_BLOCK, HEAD_DIM, CMP_HIDDEN)
    zero = jnp.zeros_like(w1[0])
    w1k = jnp.concatenate([w1[0], zero], axis=1)
    w1v = jnp.concatenate([zero, w1[1]], axis=1)
    w1p = jnp.concatenate([w1k, w1v], axis=2).astype(BF16)
    w1p = w1p.reshape(CMP_BLOCK // 2, 2 * LANES, 2 * CMP_HIDDEN)
    w2p = jnp.pad(cmp_w2, ((0, 0), (0, 0), (0, LANES - HEAD_DIM))).astype(BF16)
    return pos2, w1p, w2p


def kernel(x, attn_norm_w, w_in, q_norm_w, k_norm_w, cmp_pos, cmp_w1, cmp_w2, gmlp_ln_w, gmlp_ln_b,
           spatial_w, spatial_b, w_out, ffn_norm_w, w_up, conv_w, conv_b, w_down):
    x2 = x.reshape(N_TOK, D_MODEL)
    w_main, w_gate = _layout_w_in(w_in)
    proj, gl = _proj(x2, attn_norm_w.reshape(1, D_MODEL), w_main, w_gate)

    knw2 = jnp.concatenate([k_norm_w[1], k_norm_w[2]]).reshape(1, LANES)
    kcw = jnp.concatenate([k_norm_w[0], jnp.zeros((LANES - HEAD_DIM,), F32)]).reshape(1, LANES)
    pos2, w1p, w2p = _layout_compress(cmp_pos, cmp_w1, cmp_w2)
    ksel, kwin, vt, kc, vct, gt = _kprep(proj, gl, knw2, kcw, pos2, w1p, w2p)

    wq_b = jnp.broadcast_to(q_norm_w.reshape(HEAD_DIM, 1), (HEAD_DIM, Q_TILE))
    a = _attn(proj, ksel, kwin, vt, kc, vct, gt, jnp.asarray(_slope_features()),
              jnp.asarray(_window_mask()), jnp.asarray(_diag_masks()),
              jnp.asarray(_overlap_t(), dtype=BF16), wq_b)

    sbx = jnp.repeat(spatial_b.T, GMLP_WIDTH // GMLP_GROUPS, axis=1)
    x1, xn = _oproj(a, proj, x2, w_out.astype(BF16), ffn_norm_w.reshape(1, D_MODEL),
                    gmlp_ln_w.reshape(1, GMLP_WIDTH), gmlp_ln_b.reshape(1, GMLP_WIDTH), spatial_w, sbx)

    out = _ffn(xn, x1, w_up.astype(BF16), conv_w, conv_b.reshape(1, 2 * D_FF), w_down.astype(BF16))
    return out.reshape(BATCH, SEQ, D_MODEL)
```

```python
import numpy as np
import jax
import jax.numpy as jnp
from jax import lax
from jax.experimental import pallas as pl
from jax.experimental.pallas import tpu as pltpu

F32 = jnp.float32
BF16 = jnp.bfloat16

D_MODEL = 2048
BATCH = 4
SEQ = 4096
N_TOK = BATCH * SEQ
NSA_HEADS = 16
KV_GROUPS = 4
REP = NSA_HEADS // KV_GROUPS
HEAD_DIM = 64
NSA_WIDTH = NSA_HEADS * HEAD_DIM
GMLP_WIDTH = D_MODEL - NSA_WIDTH
GMLP_GROUPS = 8
GMLP_CHUNK = 128
CMP_BLOCK = 32
CMP_STRIDE = 16
CMP_HIDDEN = 256
N_CMP_PAD = SEQ // CMP_STRIDE
SEL_BLOCK = 64
N_SEL = SEQ // SEL_BLOCK
SEL_TOPK = 16
WINDOW = 512
K_BLOCK = 128
N_KB = SEQ // K_BLOCK
Q_TILE = 256
N_QT = SEQ // Q_TILE
N_BRANCH = 3
KV_COLS = KV_GROUPS * HEAD_DIM
D_FF = 5632
CONV_WIDTH = 3
NORM_EPS = 1e-6
LN_EPS = 1e-5
NEG_BIG = -1e30
SEL_BIG = 1e9

LANES = 128
VMEM_LIMIT_BYTES = 56 * 1024 * 1024

Z_COLS = 2 * GMLP_WIDTH
PROJ_COLS = Z_COLS + NSA_WIDTH + 6 * KV_COLS
GATE_PAD = LANES
GATE_ROWS_PER_GROUP = 16

PROJ_TM = 1024
PROJ_TN = 1536
OPROJ_TM = 512
FFN_TM = 512
FFN_TF = 512
SEL_TILE_BLOCKS = 4
SEL_TILE_KEYS = SEL_TILE_BLOCKS * K_BLOCK
MASK_LANE0 = 80
SUM_ROWS = 16
VT_ROWS = HEAD_DIM + SUM_ROWS
LOG2E = 1.4426950408889634
WIN_BLOCKS = (WINDOW + Q_TILE) // K_BLOCK
WIN_KEYS = WIN_BLOCKS * K_BLOCK


def _cparams(*sem):
    return pltpu.CompilerParams(dimension_semantics=sem, vmem_limit_bytes=VMEM_LIMIT_BYTES)


def _dot(a, b):
    return jnp.dot(a, b, preferred_element_type=F32)


def _proj_kernel(x_ref, nw_ref, w_ref, wg_ref, o_ref, og_ref, h_ref):
    @pl.when(pl.program_id(1) == 0)
    def _():
        x = x_ref[...]
        ms = jnp.mean(x * x, axis=-1, keepdims=True)
        h = (x * lax.rsqrt(ms + NORM_EPS) * nw_ref[...]).astype(BF16)
        h_ref[...] = h
        og_ref[...] = _dot(h, wg_ref[...])

    o_ref[...] = _dot(h_ref[...], w_ref[...])


def _proj(x2, nw, w_main, w_gate):
    return pl.pallas_call(
        _proj_kernel,
        grid=(N_TOK // PROJ_TM, PROJ_COLS // PROJ_TN),
        in_specs=[
            pl.BlockSpec((PROJ_TM, D_MODEL), lambda i, j: (i, 0)),
            pl.BlockSpec((1, D_MODEL), lambda i, j: (0, 0)),
            pl.BlockSpec((D_MODEL, PROJ_TN), lambda i, j: (0, j)),
            pl.BlockSpec((D_MODEL, GATE_PAD), lambda i, j: (0, 0)),
        ],
        out_specs=[
            pl.BlockSpec((PROJ_TM, PROJ_TN), lambda i, j: (i, j)),
            pl.BlockSpec((PROJ_TM, GATE_PAD), lambda i, j: (i, 0)),
        ],
        out_shape=[
            jax.ShapeDtypeStruct((N_TOK, PROJ_COLS), F32),
            jax.ShapeDtypeStruct((N_TOK, GATE_PAD), F32),
        ],
        scratch_shapes=[pltpu.VMEM((PROJ_TM, D_MODEL), BF16)],
        compiler_params=_cparams("arbitrary", "arbitrary"),
        name="proj",
    )(x2, nw, w_main, w_gate)


def _pos_features(lane, first, f1, f2):
    return jnp.where(lane < first + 3, f1, jnp.where(lane < first + 6, f2,
                                                     jnp.where(lane == first + 6, 1.0, 0.0)))


def _kprep_kernel(s0_ref, s1_ref, s2_ref, gl_ref, knw_ref, kcw_ref, pos_ref, w1_ref, w2_ref,
                  ksel_ref, kwin_ref, vt_ref, kc_ref, vct_ref, gt_ref, ha_ref, hb_ref):
    lane = lax.broadcasted_iota(jnp.int32, (K_BLOCK, LANES), 1)
    row = lax.broadcasted_iota(jnp.int32, (K_BLOCK, LANES), 0)
    lo = lane < HEAD_DIM
    inv_d = 1.0 / HEAD_DIM
    ones_rows = jnp.where(lax.broadcasted_iota(jnp.int32, (SUM_ROWS, K_BLOCK), 0) == 0, 1.0, 0.0)

    def chunk(c, carry):
        r0 = pl.multiple_of(c * K_BLOCK, K_BLOCK)
        x = s0_ref[pl.ds(r0, K_BLOCK), :]
        sq = x * x
        s_lo = jnp.sum(jnp.where(lo, sq, 0.0), axis=1, keepdims=True)
        s_hi = jnp.sum(jnp.where(lo, 0.0, sq), axis=1, keepdims=True)
        ms = jnp.where(lo, s_lo, s_hi) * inv_d
        y = x * lax.rsqrt(ms + NORM_EPS) * knw_ref[...]
        pos = r0 + row
        blk = lax.shift_right_logical(pos, 6)
        f1 = (blk * SEL_BLOCK).astype(F32)
        f2 = (pos & (SEL_BLOCK - 1)).astype(F32)
        onehot = jnp.where(lane - MASK_LANE0 == (blk & (SEL_TILE_BLOCKS * 2 - 1)), 1.0, 0.0)
        feat_sel = jnp.where(lane < MASK_LANE0, _pos_features(lane, HEAD_DIM, f1, f2), onehot)
        ksel_ref[c] = jnp.where(lo, y, feat_sel).astype(BF16)
        kwin_ref[c] = jnp.where(lo, _pos_features(lane, 0, f1, f2), y).astype(BF16)
        v_t = s2_ref[pl.ds(r0, K_BLOCK), :].T
        vt_ref[c] = jnp.concatenate(
            [v_t[0:HEAD_DIM], ones_rows, v_t[HEAD_DIM:], ones_rows], axis=0).astype(BF16)
        return carry

    lax.fori_loop(0, N_KB, chunk, 0, unroll=2)

    @pl.when(pl.program_id(1) == 0)
    def _():
        def gchunk(c, carry):
            r0 = pl.multiple_of(c * K_BLOCK, K_BLOCK)
            gt_ref[c] = jax.nn.sigmoid(gl_ref[pl.ds(r0, K_BLOCK), :]).T
            return carry

        lax.fori_loop(0, N_KB, gchunk, 0, unroll=2)

    ha_ref[...] = jnp.zeros_like(ha_ref)
    hb_ref[...] = jnp.zeros_like(hb_ref)
    for tp in range(CMP_STRIDE // 2):
        rows = [s1_ref[pl.ds(2 * tp + e, N_CMP_PAD, stride=CMP_STRIDE), :] for e in range(2)]
        xa = jnp.concatenate([(rows[e] + pos_ref[2 * tp + e:2 * tp + e + 1, :]).astype(BF16)
                              for e in range(2)], axis=1)
        xb = jnp.concatenate([(rows[e] + pos_ref[CMP_STRIDE + 2 * tp + e:CMP_STRIDE + 2 * tp + e + 1, :])
                              .astype(BF16) for e in range(2)], axis=1)
        ha_ref[...] += _dot(xa, w1_ref[tp])
        hb_ref[0:N_CMP_PAD, :] += _dot(xb, w1_ref[CMP_STRIDE // 2 + tp])
    hid = jax.nn.gelu(ha_ref[...] + hb_ref[1:N_CMP_PAD + 1, :]).astype(BF16)
    kc = _dot(hid[:, :CMP_HIDDEN], w2_ref[0])
    vc = _dot(hid[:, CMP_HIDDEN:], w2_ref[1])
    ms = jnp.sum(kc * kc, axis=1, keepdims=True) * inv_d
    kcn = kc * lax.rsqrt(ms + NORM_EPS) * kcw_ref[...]
    ci = lax.broadcasted_iota(jnp.int32, (N_CMP_PAD, LANES), 0)
    cl = lax.broadcasted_iota(jnp.int32, (N_CMP_PAD, LANES), 1)
    c1 = (lax.shift_right_logical(ci, 2) * SEL_BLOCK).astype(F32)
    c2 = ((ci & 3) * CMP_STRIDE).astype(F32) + (CMP_BLOCK - 1) * 0.5
    kc_ref[...] = jnp.where(cl < HEAD_DIM, kcn, _pos_features(cl, HEAD_DIM, c1, c2)).astype(BF16)
    vct_ref[...] = vc.T[0:HEAD_DIM, :].astype(BF16)


def _kprep(proj, gl, knw2, kcw, pos2, w1p, w2p):
    kv0 = (Z_COLS + NSA_WIDTH) // LANES
    slab = lambda s: pl.BlockSpec((SEQ, LANES), lambda b, g, s=s: (b, kv0 + 3 * g + s))
    full = lambda shape: pl.BlockSpec(shape, lambda b, g: (0,) * len(shape))
    per_bg = lambda shape: pl.BlockSpec((None, None) + shape, lambda b, g: (b, g) + (0,) * len(shape))
    bg_shape = lambda shape, dt: jax.ShapeDtypeStruct((BATCH, KV_GROUPS) + shape, dt)
    return pl.pallas_call(
        _kprep_kernel,
        grid=(BATCH, KV_GROUPS),
        in_specs=[
            slab(0), slab(1), slab(2),
            pl.BlockSpec((SEQ, GATE_PAD), lambda b, g: (b, 0)),
            full((1, LANES)), full((1, LANES)), full((CMP_BLOCK, LANES)),
            full((CMP_BLOCK // 2, 2 * LANES, 2 * CMP_HIDDEN)), full((2, CMP_HIDDEN, LANES)),
        ],
        out_specs=[
            per_bg((N_KB, K_BLOCK, LANES)),
            per_bg((N_KB, K_BLOCK, LANES)),
            per_bg((N_KB, 2 * VT_ROWS, K_BLOCK)),
            per_bg((N_CMP_PAD, LANES)),
            per_bg((HEAD_DIM, N_CMP_PAD)),
            pl.BlockSpec((None, N_KB, GATE_PAD, K_BLOCK), lambda b, g: (b, 0, 0, 0)),
        ],
        out_shape=[
            bg_shape((N_KB, K_BLOCK, LANES), BF16),
            bg_shape((N_KB, K_BLOCK, LANES), BF16),
            bg_shape((N_KB, 2 * VT_ROWS, K_BLOCK), BF16),
            bg_shape((N_CMP_PAD, LANES), BF16),
            bg_shape((HEAD_DIM, N_CMP_PAD), BF16),
            jax.ShapeDtypeStruct((BATCH, N_KB, GATE_PAD, K_BLOCK), F32),
        ],
        scratch_shapes=[
            pltpu.VMEM((N_CMP_PAD, 2 * CMP_HIDDEN), F32),
            pltpu.VMEM((N_CMP_PAD + 8, 2 * CMP_HIDDEN), F32),
        ],
        compiler_params=_cparams("arbitrary", "arbitrary"),
        name="kprep",
    )(proj, proj, proj, gl, knw2, kcw, pos2, w1p, w2p)


def _attn_kernel(q_ref, ksel_ref, kwin_ref, vt_ref, kc_ref, vct_ref, gt_ref, sf_ref, wm_ref, dm_ref,
                 ov_ref, wq_ref, o_ref, sel_ref, qs_ref, sa_ref, sb_ref, tiles_ref):
    qt = pl.program_id(2)
    t_i = qt * Q_TILE + lax.broadcasted_iota(jnp.int32, (1, Q_TILE), 1)
    head = lambda a, r: a[:, r * Q_TILE:(r + 1) * Q_TILE]
    heads = lambda parts: jnp.concatenate(parts, axis=1)

    q_t = q_ref[...].T
    parts = []
    for r in range(REP):
        xq = q_t[r * HEAD_DIM:(r + 1) * HEAD_DIM, :]
        ms = jnp.sum(xq * xq, axis=0, keepdims=True) * (1.0 / HEAD_DIM)
        parts.append(xq * lax.rsqrt(ms + NORM_EPS) * wq_ref[...] * (HEAD_DIM ** -0.5) * LOG2E)
    qn = heads(parts)
    sf = sf_ref[...]
    sf = jnp.where(lax.broadcasted_iota(jnp.int32, sf.shape, 0) == 6, sf * (qt * Q_TILE).astype(F32), sf)
    wide = REP * Q_TILE
    pad = jnp.zeros((HEAD_DIM - 8, wide), F32)
    qs_ref[...] = jnp.concatenate([qn, sf, pad], axis=0).astype(BF16)
    q_win = jnp.concatenate([sf, pad, qn], axis=0).astype(BF16)

    sc = _dot(kc_ref[...], qs_ref[...])
    ci = lax.broadcasted_iota(jnp.int32, (N_CMP_PAD, Q_TILE), 0)
    cm = jnp.where(ci * CMP_STRIDE + (CMP_BLOCK - 1) <= t_i, 0.0, NEG_BIG)
    any_valid = t_i >= CMP_BLOCK - 1
    pcs = []
    for r in range(REP):
        s = head(sc, r) + cm
        e = jnp.exp2(s - jnp.max(s, axis=0, keepdims=True))
        d = jnp.sum(e, axis=0, keepdims=True)
        pcs.append((e * jnp.where(any_valid, 1.0 / d, 0.0)).astype(BF16))
    oc_imp = _dot(jnp.concatenate([vct_ref[...], ov_ref[...]], axis=0), heads(pcs))
    o_c = oc_imp[0:HEAD_DIM]
    imp4 = oc_imp[HEAD_DIM:]
    imp = (head(imp4, 0) + head(imp4, 1)) + (head(imp4, 2) + head(imp4, 3))

    jj = lax.broadcasted_iota(jnp.int32, (N_SEL, Q_TILE), 0)
    cur = lax.shift_right_logical(t_i, 6)
    valid = jj <= cur
    sel_ref[...] = jnp.where(valid, 1.0, 0.0)
    forced = (jj == 0) | (jj == cur) | (jj == cur - 1)
    score = jnp.where(valid, jnp.where(forced, SEL_BIG, imp), -SEL_BIG)

    def rank_rows(n8):
        def branch():
            rows8 = [score[8 * v:8 * v + 8, :] for v in range(n8)]
            j8 = lax.broadcasted_iota(jnp.int32, (8, Q_TILE), 0)
            ranks = [jnp.zeros((8, Q_TILE), F32) for _ in rows8]
            for jp in range(8 * n8):
                row = score[jp:jp + 1, :]
                for v, blk in enumerate(rows8):
                    if 8 * v > jp:
                        ahead = jnp.where(row >= blk, 1.0, 0.0)
                    elif 8 * v + 7 <= jp:
                        ahead = jnp.where(row > blk, 1.0, 0.0)
                    else:
                        ahead = jnp.where(j8 + 8 * v > jp, jnp.where(row >= blk, 1.0, 0.0),
                                          jnp.where(row > blk, 1.0, 0.0))
                    ranks[v] = ranks[v] + ahead
            rank = jnp.concatenate(ranks, axis=0)
            sel_ref[0:8 * n8, :] = jnp.where(valid[0:8 * n8] & (rank < SEL_TOPK), 1.0, 0.0)
        return branch

    n8_needed = (qt * Q_TILE + Q_TILE - 1) // (8 * SEL_BLOCK) + 1
    lax.switch(n8_needed - 1, [(lambda: None) if n8 * 8 <= SEL_TOPK else rank_rows(n8)
                               for n8 in range(1, N_SEL // 8 + 1)])

    sub = SEL_TILE_KEYS // SEL_BLOCK

    def scores_into(kt, s_ref):
        member = sel_ref[pl.ds(kt * sub, sub), :]
        mrows = (member - 1.0) * (-NEG_BIG)
        qs_ref[MASK_LANE0:MASK_LANE0 + 16, :] = jnp.concatenate(
            [heads([mrows] * REP), jnp.zeros((16 - sub, wide), F32)], axis=0).astype(BF16)
        k_tile = ksel_ref[pl.ds(kt * SEL_TILE_BLOCKS, SEL_TILE_BLOCKS)].reshape(SEL_TILE_KEYS, LANES)
        s_ref[...] = _dot(k_tile, qs_ref[...])

    def absorb(kt, s_ref, carry, diagonal):
        m_run, acc = carry
        kb0 = kt * SEL_TILE_BLOCKS
        if diagonal:
            causal = dm_ref[qt % (SEL_TILE_KEYS // Q_TILE)]
        v_t = jnp.concatenate([vt_ref[kb0 + x, 0:VT_ROWS, :] for x in range(SEL_TILE_BLOCKS)], axis=1)
        ms_, accs = [], []
        for r in range(REP):
            sr = s_ref[:, r * Q_TILE:(r + 1) * Q_TILE]
            if diagonal:
                sr = sr + causal
            m_old = head(m_run, r)
            m_new = jnp.maximum(m_old, jnp.max(sr, axis=0, keepdims=True))
            p = jnp.exp2(sr - m_new).astype(BF16)
            accs.append(head(acc, r) * jnp.exp2(m_old - m_new) + _dot(v_t, p))
            ms_.append(m_new)
        return heads(ms_), heads(accs)

    carry = (jnp.full((1, wide), NEG_BIG, F32), jnp.zeros((VT_ROWS, wide), F32))
    n_full = (qt * Q_TILE) // SEL_TILE_KEYS

    n_sel_tiles = SEQ // SEL_TILE_KEYS
    tr = lax.broadcasted_iota(jnp.int32, (n_sel_tiles, N_SEL), 0)
    tc = lax.broadcasted_iota(jnp.int32, (n_sel_tiles, N_SEL), 1)
    tile_of_block = jnp.where(lax.shift_right_logical(tc, 3) == tr, 1.0, 0.0).astype(BF16)
    per_query = _dot(tile_of_block, sel_ref[...].astype(BF16))
    per_tile = _dot(per_query.astype(BF16), jnp.ones((Q_TILE, LANES), BF16))
    bit = lax.shift_left(1, lax.broadcasted_iota(jnp.int32, (n_sel_tiles, LANES), 0)).astype(F32)
    used_bits = jnp.sum(jnp.where(per_tile > 0.5, bit, 0.0), axis=0, keepdims=True)[0, 0].astype(jnp.int32)

    kb_first = qt * (Q_TILE // K_BLOCK) - WINDOW // K_BLOCK
    kbs = [jnp.maximum(kb_first + x, 0) for x in range(WIN_BLOCKS)]
    sw = _dot(jnp.concatenate([kwin_ref[kb] for kb in kbs], axis=0), q_win)

    n_visit = jnp.int32(0)
    for kt in range(n_sel_tiles):
        tiles_ref[n_visit] = kt
        take = (kt < n_full) & ((lax.shift_right_logical(used_bits, kt) & 1) == 1)
        n_visit = n_visit + jnp.where(take, 1, 0)
    tiles_ref[n_visit] = n_full

    scores_into(tiles_ref[0], sa_ref)
    wm = jnp.concatenate(
        [wm_ref[x * K_BLOCK:(x + 1) * K_BLOCK, :] + jnp.where(kb_first + x < 0, NEG_BIG, 0.0)
         for x in range(WIN_BLOCKS)], axis=0)
    pws = []
    for r in range(REP):
        s = head(sw, r) + wm
        pws.append(jnp.exp2(s - jnp.max(s, axis=0, keepdims=True)).astype(BF16))
    vw_t = jnp.concatenate([vt_ref[kb, VT_ROWS:2 * VT_ROWS, :] for kb in kbs], axis=1)
    acc_w = _dot(vw_t, heads(pws))
    o_w = acc_w[0:HEAD_DIM] / acc_w[HEAD_DIM:HEAD_DIM + 1]

    def tile_pair(i, carry):
        scores_into(tiles_ref[2 * i + 1], sb_ref)
        carry = absorb(tiles_ref[2 * i], sa_ref, carry, False)
        scores_into(tiles_ref[2 * i + 2], sa_ref)
        return absorb(tiles_ref[2 * i + 1], sb_ref, carry, False)

    carry = lax.fori_loop(0, n_visit // 2, tile_pair, carry)
    last = 2 * (n_visit // 2)

    def odd_tail(carry):
        scores_into(n_full, sb_ref)
        return absorb(n_full, sb_ref, absorb(tiles_ref[last], sa_ref, carry, False), True)

    def even_tail(carry):
        return absorb(n_full, sa_ref, carry, True)

    _, acc_s = lax.cond(n_visit % 2 == 1, odd_tail, even_tail, carry)
    o_s = acc_s[0:HEAD_DIM] / acc_s[HEAD_DIM:HEAD_DIM + 1]

    gates = jnp.concatenate([gt_ref[x] for x in range(Q_TILE // K_BLOCK)], axis=1)
    outs = []
    for r in range(REP):
        outs.append(gates[r:r + 1, :] * head(o_c, r)
                    + gates[REP + r:REP + r + 1, :] * head(o_s, r)
                    + gates[2 * REP + r:2 * REP + r + 1, :] * head(o_w, r))
    o_ref[...] = jnp.concatenate(outs, axis=0).T.astype(BF16)


def _attn(proj, ksel, kwin, vt, kc, vct, gt, sf, wm, dm, ov_t, wq_b):
    q0 = Z_COLS // (REP * HEAD_DIM)
    kvspec = lambda shape: pl.BlockSpec((None, None) + shape, lambda b, g, qt: (b, g) + (0,) * len(shape))
    const = lambda shape: pl.BlockSpec(shape, lambda b, g, qt: (0,) * len(shape))
    return pl.pallas_call(
        _attn_kernel,
        grid=(BATCH, KV_GROUPS, N_QT),
        in_specs=[
            pl.BlockSpec((Q_TILE, REP * HEAD_DIM), lambda b, g, qt: (b * N_QT + qt, q0 + g)),
            kvspec((N_KB, K_BLOCK, LANES)),
            kvspec((N_KB, K_BLOCK, LANES)),
            kvspec((N_KB, 2 * VT_ROWS, K_BLOCK)),
            kvspec((N_CMP_PAD, LANES)),
            kvspec((HEAD_DIM, N_CMP_PAD)),
            pl.BlockSpec((None, Q_TILE // K_BLOCK, GATE_ROWS_PER_GROUP, K_BLOCK), lambda b, g, qt: (b, qt, g, 0)),
            pl.BlockSpec((None, 8, REP * Q_TILE), lambda b, g, qt: (g, 0, 0)),
            const((WIN_KEYS, Q_TILE)),
            const((SEL_TILE_KEYS // Q_TILE, SEL_TILE_KEYS, Q_TILE)),
            const((N_SEL, N_CMP_PAD)),
            const((HEAD_DIM, Q_TILE)),
        ],
        out_specs=pl.BlockSpec((Q_TILE, REP * HEAD_DIM), lambda b, g, qt: (b * N_QT + qt, g)),
        out_shape=jax.ShapeDtypeStruct((N_TOK, NSA_WIDTH), BF16),
        scratch_shapes=[pltpu.VMEM((N_SEL, Q_TILE), F32), pltpu.VMEM((LANES, REP * Q_TILE), BF16),
                        pltpu.VMEM((SEL_TILE_KEYS, REP * Q_TILE), F32),
                        pltpu.VMEM((SEL_TILE_KEYS, REP * Q_TILE), F32),
                        pltpu.SMEM((SEQ // SEL_TILE_KEYS + 1,), jnp.int32)],
        compiler_params=_cparams("arbitrary", "arbitrary", "arbitrary"),
        name="attn",
    )(proj, ksel, kwin, vt, kc, vct, gt, sf, wm, dm, ov_t, wq_b)


def _spatial_gating(z, lnw, lnb, sw_ref, sbx):
    ge = jax.nn.gelu(z)
    u = ge[:, :GMLP_WIDTH]
    v = ge[:, GMLP_WIDTH:]
    mu = jnp.mean(v, axis=-1, keepdims=True)
    var = jnp.mean(jnp.square(v - mu), axis=-1, keepdims=True)
    vn = ((v - mu) * lax.rsqrt(var + LN_EPS) * lnw + lnb).astype(BF16)
    ti = lax.broadcasted_iota(jnp.int32, (GMLP_CHUNK, GMLP_CHUNK), 0)
    si = lax.broadcasted_iota(jnp.int32, (GMLP_CHUNK, GMLP_CHUNK), 1)
    ws = [jnp.where(ti >= si, sw_ref[gg], 0.0).astype(BF16) for gg in range(GMLP_GROUPS)]
    gd = GMLP_WIDTH // GMLP_GROUPS
    outs = []
    for c in range(z.shape[0] // GMLP_CHUNK):
        rows = slice(c * GMLP_CHUNK, (c + 1) * GMLP_CHUNK)
        mix = jnp.concatenate(
            [_dot(ws[gg], vn[rows, gg * gd:(gg + 1) * gd]) for gg in range(GMLP_GROUPS)], axis=1)
        outs.append((u[rows, :] * (mix + sbx)).astype(BF16))
    return jnp.concatenate(outs, axis=0)


def _oproj_kernel(a_ref, z_ref, x_ref, wa_ref, wb_ref, nw_ref, lnw_ref, lnb_ref, sw_ref, sbx_ref,
                  x1_ref, xn_ref):
    attn_part = _dot(a_ref[...], wa_ref[...])
    b = _spatial_gating(z_ref[...], lnw_ref[...], lnb_ref[...], sw_ref, sbx_ref[...])
    y = x_ref[...] + (attn_part + _dot(b, wb_ref[...]))
    x1_ref[...] = y
    ms = jnp.mean(y * y, axis=-1, keepdims=True)
    xn_ref[...] = (y * lax.rsqrt(ms + NORM_EPS) * nw_ref[...]).astype(BF16)


def _oproj(a, proj, x2, w_out, nw, lnw, lnb, sw, sbx):
    return pl.pallas_call(
        _oproj_kernel,
        grid=(N_TOK // OPROJ_TM,),
        in_specs=[
            pl.BlockSpec((OPROJ_TM, NSA_WIDTH), lambda i: (i, 0)),
            pl.BlockSpec((OPROJ_TM, Z_COLS), lambda i: (i, 0)),
            pl.BlockSpec((OPROJ_TM, D_MODEL), lambda i: (i, 0)),
            pl.BlockSpec((NSA_WIDTH, D_MODEL), lambda i: (0, 0)),
            pl.BlockSpec((GMLP_WIDTH, D_MODEL), lambda i: (1, 0)),
            pl.BlockSpec((1, D_MODEL), lambda i: (0, 0)),
            pl.BlockSpec((1, GMLP_WIDTH), lambda i: (0, 0)),
            pl.BlockSpec((1, GMLP_WIDTH), lambda i: (0, 0)),
            pl.BlockSpec((GMLP_GROUPS, GMLP_CHUNK, GMLP_CHUNK), lambda i: (0, 0, 0)),
            pl.BlockSpec((GMLP_CHUNK, GMLP_WIDTH), lambda i: (0, 0)),
        ],
        out_specs=[
            pl.BlockSpec((OPROJ_TM, D_MODEL), lambda i: (i, 0)),
            pl.BlockSpec((OPROJ_TM, D_MODEL), lambda i: (i, 0)),
        ],
        out_shape=[
            jax.ShapeDtypeStruct((N_TOK, D_MODEL), F32),
            jax.ShapeDtypeStruct((N_TOK, D_MODEL), BF16),
        ],
        compiler_params=_cparams("arbitrary"),
        name="oproj",
    )(a, proj, x2, w_out, w_out, nw, lnw, lnb, sw, sbx)


def _ffn_kernel(xn_ref, x1_ref, wg_ref, wu_ref, cwg_ref, cwu_ref, cbg_ref, cbu_ref, wd_ref,
                cwg_p_ref, cwu_p_ref, cbg_p_ref, cbu_p_ref, wd_p_ref, o_ref, h0_ref, h1_ref, carry_ref):
    i = pl.program_id(0)
    q = pl.program_id(1)
    nq = pl.num_programs(1)
    half = FFN_TF // 2

    @pl.when((i == 0) & (q == 0))
    def _():
        carry_ref[...] = jnp.zeros_like(carry_ref)
        h1_ref[...] = jnp.zeros_like(h1_ref)

    @pl.when(q == 0)
    def _():
        o_ref[...] = x1_ref[...]

    seq_start = i % (SEQ // FFN_TM) == 0
    xn = xn_ref[...]

    def produce(c, h_ref):
        h_ref[0, 8:, :] = _dot(xn, wg_ref[:, c * half:(c + 1) * half])
        h_ref[1, 8:, :] = _dot(xn, wu_ref[:, c * half:(c + 1) * half])

    def conv(h_ref, part, tile, cw, cb):
        h_ref[part, 0:8, :] = jnp.where(seq_start, 0.0, carry_ref[tile, part])
        carry_ref[tile, part] = h_ref[part, FFN_TM:FFN_TM + 8, :]
        taps = [h_ref[part, 8 - k:8 - k + FFN_TM, :] for k in (2, 1, 0)]
        return cb + ((cw[0:1, :] * taps[0] + cw[1:2, :] * taps[1]) + cw[2:3, :] * taps[2])

    def gate(h_ref, tile, c, cwg, cwu, cbg, cbu, scale=None):
        cols = slice(c * half, (c + 1) * half)
        cg = conv(h_ref, 0, tile, cwg[:, cols], cbg[:, cols])
        cu = conv(h_ref, 1, tile, cwu[:, cols], cbu[:, cols])
        act = jax.nn.silu(cg) * cu
        if scale is not None:
            act = act * scale
        return act.astype(BF16)

    act_prev = gate(h1_ref, jnp.where(q == 0, 2 * nq - 1, 2 * q - 1), 1, cwg_p_ref, cwu_p_ref, cbg_p_ref,
                    cbu_p_ref, scale=jnp.where(q == 0, 0.0, 1.0))
    produce(0, h0_ref)
    down_prev = _dot(act_prev, wd_p_ref[half:, :])
    act_cur = gate(h0_ref, 2 * q, 0, cwg_ref, cwu_ref, cbg_ref, cbu_ref)
    produce(1, h1_ref)
    o_ref[...] += down_prev + _dot(act_cur, wd_ref[0:half, :])

    @pl.when(q == nq - 1)
    def _():
        act_last = gate(h1_ref, 2 * q + 1, 1, cwg_ref, cwu_ref, cbg_ref, cbu_ref)
        o_ref[...] += _dot(act_last, wd_ref[half:, :])


def _ffn(xn, x1, w_up, conv_w, conv_b, w_down):
    nj = D_FF // FFN_TF
    prev = lambda q: jnp.maximum(q - 1, 0)
    return pl.pallas_call(
        _ffn_kernel,
        grid=(N_TOK // FFN_TM, nj),
        in_specs=[
            pl.BlockSpec((FFN_TM, D_MODEL), lambda i, q: (i, 0)),
            pl.BlockSpec((FFN_TM, D_MODEL), lambda i, q: (i, 0)),
            pl.BlockSpec((D_MODEL, FFN_TF), lambda i, q: (0, q)),
            pl.BlockSpec((D_MODEL, FFN_TF), lambda i, q: (0, nj + q)),
            pl.BlockSpec((CONV_WIDTH, FFN_TF), lambda i, q: (0, q)),
            pl.BlockSpec((CONV_WIDTH, FFN_TF), lambda i, q: (0, nj + q)),
            pl.BlockSpec((1, FFN_TF), lambda i, q: (0, q)),
            pl.BlockSpec((1, FFN_TF), lambda i, q: (0, nj + q)),
            pl.BlockSpec((FFN_TF, D_MODEL), lambda i, q: (q, 0)),
            pl.BlockSpec((CONV_WIDTH, FFN_TF), lambda i, q: (0, prev(q))),
            pl.BlockSpec((CONV_WIDTH, FFN_TF), lambda i, q: (0, nj + prev(q))),
            pl.BlockSpec((1, FFN_TF), lambda i, q: (0, prev(q))),
            pl.BlockSpec((1, FFN_TF), lambda i, q: (0, nj + prev(q))),
            pl.BlockSpec((FFN_TF, D_MODEL), lambda i, q: (prev(q), 0)),
        ],
        out_specs=pl.BlockSpec((FFN_TM, D_MODEL), lambda i, q: (i, 0)),
        out_shape=jax.ShapeDtypeStruct((N_TOK, D_MODEL), F32),
        scratch_shapes=[pltpu.VMEM((2, FFN_TM + 8, FFN_TF // 2), F32), pltpu.VMEM((2, FFN_TM + 8, FFN_TF // 2), F32),
                        pltpu.VMEM((2 * nj, 2, 8, FFN_TF // 2), F32)],
        compiler_params=_cparams("arbitrary", "arbitrary"),
        name="ffn",
    )(xn, x1, w_up, w_up, conv_w, conv_w, conv_b, conv_b, w_down, conv_w, conv_w, conv_b, conv_b, w_down)


def _overlap_t():
    start = np.arange(N_CMP_PAD)[None, :] * CMP_STRIDE
    s0 = np.arange(N_SEL)[:, None] * SEL_BLOCK
    ov = (start < s0 + SEL_BLOCK) & (start + CMP_BLOCK > s0) & (np.arange(N_CMP_PAD)[None, :] < N_CMP_PAD - 1)
    return ov.astype(np.float32)


def _bf16_round(a):
    return np.asarray(a, np.float32).astype(BF16).astype(np.float32)


def _slope_features():
    sl = (np.power(2.0, -8.0 * np.arange(1, NSA_HEADS + 1) / NSA_HEADS).astype(np.float32)
          * np.float32(LOG2E)).astype(np.float32)
    s1 = _bf16_round(sl)
    s2 = _bf16_round(sl - s1)
    s3 = _bf16_round(sl - s1 - s2)
    rows = np.stack([s1, s2, s3, s1, s2, s3, -sl, np.zeros_like(sl)], axis=0)
    rows = rows.reshape(8, KV_GROUPS, REP).transpose(1, 0, 2)
    return np.repeat(rows, Q_TILE, axis=2).astype(np.float32)


def _window_mask():
    kl = np.arange(WIN_KEYS)[:, None]
    ql = np.arange(Q_TILE)[None, :]
    dist = ql + WINDOW - kl
    return np.where((dist >= 0) & (dist < WINDOW), 0.0, NEG_BIG).astype(np.float32)


def _diag_masks():
    kl = np.arange(SEL_TILE_KEYS)[None, :, None]
    ql = np.arange(Q_TILE)[None, None, :]
    off = np.arange(SEL_TILE_KEYS // Q_TILE)[:, None, None] * Q_TILE
    return np.where(kl > ql + off, NEG_BIG, 0.0).astype(np.float32)


def _layout_w_in(w_in):
    q_end = NSA_WIDTH
    kv_end = q_end + 6 * KV_COLS
    g_end = kv_end + NSA_HEADS * N_BRANCH
    w_in = w_in.astype(BF16)
    w_q = w_in[:, :q_end]
    w_kv = w_in[:, q_end:kv_end].reshape(D_MODEL, 6, KV_GROUPS, HEAD_DIM)
    w_kv = jnp.stack([w_kv[:, s] for s in (2, 4, 0, 1, 3, 5)], axis=2).reshape(D_MODEL, 6 * KV_COLS)
    w_z = w_in[:, g_end:]
    w_main = jnp.concatenate([w_z, w_q, w_kv], axis=1)
    w_g = w_in[:, kv_end:g_end].reshape(D_MODEL, KV_GROUPS, REP, N_BRANCH).transpose(0, 1, 3, 2)
    w_g = w_g.reshape(D_MODEL, KV_GROUPS, N_BRANCH * REP)
    w_g = jnp.pad(w_g, ((0, 0), (0, 0), (0, GATE_ROWS_PER_GROUP - N_BRANCH * REP)))
    w_g = jnp.pad(w_g.reshape(D_MODEL, KV_GROUPS * GATE_ROWS_PER_GROUP),
                  ((0, 0), (0, GATE_PAD - KV_GROUPS * GATE_ROWS_PER_GROUP)))
    return w_main, w_g


def _layout_compress(cmp_pos, cmp_w1, cmp_w2):
    pos2 = jnp.concatenate([cmp_pos[0], cmp_pos[1]], axis=1)
    w1 = cmp_w1.reshape(2, CMP_BLOCK, HEAD_DIM, CMP_HIDDEN)
    zero = jnp.zeros_like(w1[0])
    w1k = jnp.concatenate([w1[0], zero], axis=1)
    w1v = jnp.concatenate([zero, w1[1]], axis=1)
    w1p = jnp.concatenate([w1k, w1v], axis=2).astype(BF16)
    w1p = w1p.reshape(CMP_BLOCK // 2, 2 * LANES, 2 * CMP_HIDDEN)
    w2p = jnp.pad(cmp_w2, ((0, 0), (0, 0), (0, LANES - HEAD_DIM))).astype(BF16)
    return pos2, w1p, w2p


def kernel(x, attn_norm_w, w_in, q_norm_w, k_norm_w, cmp_pos, cmp_w1, cmp_w2, gmlp_ln_w, gmlp_ln_b,
           spatial_w, spatial_b, w_out, ffn_norm_w, w_up, conv_w, conv_b, w_down):
    x2 = x.reshape(N_TOK, D_MODEL)
    w_main, w_gate = _layout_w_in(w_in)
    proj, gl = _proj(x2, attn_norm_w.reshape(1, D_MODEL), w_main, w_gate)

    knw2 = jnp.concatenate([k_norm_w[1], k_norm_w[2]]).reshape(1, LANES)
    kcw = jnp.concatenate([k_norm_w[0], jnp.zeros((LANES - HEAD_DIM,), F32)]).reshape(1, LANES)
    pos2, w1p, w2p = _layout_compress(cmp_pos, cmp_w1, cmp_w2)
    ksel, kwin, vt, kc, vct, gt = _kprep(proj, gl, knw2, kcw, pos2, w1p, w2p)

    wq_b = jnp.broadcast_to(q_norm_w.reshape(HEAD_DIM, 1), (HEAD_DIM, Q_TILE))
    a = _attn(proj, ksel, kwin, vt, kc, vct, gt, jnp.asarray(_slope_features()),
              jnp.asarray(_window_mask()), jnp.asarray(_diag_masks()),
              jnp.asarray(_overlap_t(), dtype=BF16), wq_b)

    sbx = jnp.repeat(spatial_b.T, GMLP_WIDTH // GMLP_GROUPS, axis=1)
    x1, xn = _oproj(a, proj, x2, w_out.astype(BF16), ffn_norm_w.reshape(1, D_MODEL),
                    gmlp_ln_w.reshape(1, GMLP_WIDTH), gmlp_ln_b.reshape(1, GMLP_WIDTH), spatial_w, sbx)

    out = _ffn(xn, x1, w_up.astype(BF16), conv_w, conv_b.reshape(1, 2 * D_FF), w_down.astype(BF16))
    return out.reshape(BATCH, SEQ, D_MODEL)
```

```python
import numpy as np
import jax
import jax.numpy as jnp
from jax import lax
from jax.experimental import pallas as pl
from jax.experimental.pallas import tpu as pltpu

F32 = jnp.float32
BF16 = jnp.bfloat16

D_MODEL = 2048
BATCH = 4
SEQ = 4096
N_TOK = BATCH * SEQ
NSA_HEADS = 16
KV_GROUPS = 4
REP = NSA_HEADS // KV_GROUPS
HEAD_DIM = 64
NSA_WIDTH = NSA_HEADS * HEAD_DIM
GMLP_WIDTH = D_MODEL - NSA_WIDTH
GMLP_GROUPS = 8
GMLP_CHUNK = 128
CMP_BLOCK = 32
CMP_STRIDE = 16
CMP_HIDDEN = 256
N_CMP_PAD = SEQ // CMP_STRIDE
SEL_BLOCK = 64
N_SEL = SEQ // SEL_BLOCK
SEL_TOPK = 16
WINDOW = 512
K_BLOCK = 128
N_KB = SEQ // K_BLOCK
Q_TILE = 256
N_QT = SEQ // Q_TILE
N_BRANCH = 3
KV_COLS = KV_GROUPS * HEAD_DIM
D_FF = 5632
CONV_WIDTH = 3
NORM_EPS = 1e-6
LN_EPS = 1e-5
NEG_BIG = -1e30
SEL_BIG = 1e9

LANES = 128
VMEM_LIMIT_BYTES = 56 * 1024 * 1024

Z_COLS = 2 * GMLP_WIDTH
PROJ_COLS = Z_COLS + NSA_WIDTH + 6 * KV_COLS
GATE_PAD = LANES
GATE_ROWS_PER_GROUP = 16

PROJ_TM = 1024
PROJ_TN = 1536
OPROJ_TM = 512
FFN_TM = 512
FFN_TF = 512
SEL_TILE_BLOCKS = 4
SEL_TILE_KEYS = SEL_TILE_BLOCKS * K_BLOCK
MASK_LANE0 = 80
SUM_ROWS = 16
VT_ROWS = HEAD_DIM + SUM_ROWS
LOG2E = 1.4426950408889634
WIN_BLOCKS = (WINDOW + Q_TILE) // K_BLOCK
WIN_KEYS = WIN_BLOCKS * K_BLOCK


def _cparams(*sem):
    return pltpu.CompilerParams(dimension_semantics=sem, vmem_limit_bytes=VMEM_LIMIT_BYTES)


def _dot(a, b):
    return jnp.dot(a, b, preferred_element_type=F32)


def _proj_kernel(x_ref, nw_ref, w_ref, wg_ref, o_ref, og_ref, h_ref):
    @pl.when(pl.program_id(1) == 0)
    def _():
        x = x_ref[...]
        ms = jnp.mean(x * x, axis=-1, keepdims=True)
        h = (x * lax.rsqrt(ms + NORM_EPS) * nw_ref[...]).astype(BF16)
        h_ref[...] = h
        og_ref[...] = _dot(h, wg_ref[...])

    o_ref[...] = _dot(h_ref[...], w_ref[...])


def _proj(x2, nw, w_main, w_gate):
    return pl.pallas_call(
        _proj_kernel,
        grid=(N_TOK // PROJ_TM, PROJ_COLS // PROJ_TN),
        in_specs=[
            pl.BlockSpec((PROJ_TM, D_MODEL), lambda i, j: (i, 0)),
            pl.BlockSpec((1, D_MODEL), lambda i, j: (0, 0)),
            pl.BlockSpec((D_MODEL, PROJ_TN), lambda i, j: (0, j)),
            pl.BlockSpec((D_MODEL, GATE_PAD), lambda i, j: (0, 0)),
        ],
        out_specs=[
            pl.BlockSpec((PROJ_TM, PROJ_TN), lambda i, j: (i, j)),
            pl.BlockSpec((PROJ_TM, GATE_PAD), lambda i, j: (i, 0)),
        ],
        out_shape=[
            jax.ShapeDtypeStruct((N_TOK, PROJ_COLS), F32),
            jax.ShapeDtypeStruct((N_TOK, GATE_PAD), F32),
        ],
        scratch_shapes=[pltpu.VMEM((PROJ_TM, D_MODEL), BF16)],
        compiler_params=_cparams("arbitrary", "arbitrary"),
        name="proj",
    )(x2, nw, w_main, w_gate)


def _pos_features(lane, first, f1, f2):
    return jnp.where(lane < first + 3, f1, jnp.where(lane < first + 6, f2,
                                                     jnp.where(lane == first + 6, 1.0, 0.0)))


def _kprep_kernel(s0_ref, s1_ref, s2_ref, gl_ref, knw_ref, kcw_ref, pos_ref, w1_ref, w2_ref,
                  ksel_ref, kwin_ref, vt_ref, kc_ref, vct_ref, gt_ref, ha_ref, hb_ref):
    lane = lax.broadcasted_iota(jnp.int32, (K_BLOCK, LANES), 1)
    row = lax.broadcasted_iota(jnp.int32, (K_BLOCK, LANES), 0)
    lo = lane < HEAD_DIM
    inv_d = 1.0 / HEAD_DIM
    ones_rows = jnp.where(lax.broadcasted_iota(jnp.int32, (SUM_ROWS, K_BLOCK), 0) == 0, 1.0, 0.0)

    def chunk(c, carry):
        r0 = pl.multiple_of(c * K_BLOCK, K_BLOCK)
        x = s0_ref[pl.ds(r0, K_BLOCK), :]
        sq = x * x
        s_lo = jnp.sum(jnp.where(lo, sq, 0.0), axis=1, keepdims=True)
        s_hi = jnp.sum(jnp.where(lo, 0.0, sq), axis=1, keepdims=True)
        ms = jnp.where(lo, s_lo, s_hi) * inv_d
        y = x * lax.rsqrt(ms + NORM_EPS) * knw_ref[...]
        pos = r0 + row
        blk = lax.shift_right_logical(pos, 6)
        f1 = (blk * SEL_BLOCK).astype(F32)
        f2 = (pos & (SEL_BLOCK - 1)).astype(F32)
        onehot = jnp.where(lane - MASK_LANE0 == (blk & (SEL_TILE_BLOCKS * 2 - 1)), 1.0, 0.0)
        feat_sel = jnp.where(lane < MASK_LANE0, _pos_features(lane, HEAD_DIM, f1, f2), onehot)
        ksel_ref[c] = jnp.where(lo, y, feat_sel).astype(BF16)
        kwin_ref[c] = jnp.where(lo, _pos_features(lane, 0, f1, f2), y).astype(BF16)
        v_t = s2_ref[pl.ds(r0, K_BLOCK), :].T
        vt_ref[c] = jnp.concatenate(
            [v_t[0:HEAD_DIM], ones_rows, v_t[HEAD_DIM:], ones_rows], axis=0).astype(BF16)
        return carry

    lax.fori_loop(0, N_KB, chunk, 0, unroll=4)

    @pl.when(pl.program_id(1) == 0)
    def _():
        def gchunk(c, carry):
            r0 = pl.multiple_of(c * K_BLOCK, K_BLOCK)
            gt_ref[c] = jax.nn.sigmoid(gl_ref[pl.ds(r0, K_BLOCK), :]).T
            return carry

        lax.fori_loop(0, N_KB, gchunk, 0, unroll=4)

    ha_ref[...] = jnp.zeros_like(ha_ref)
    hb_ref[...] = jnp.zeros_like(hb_ref)
    for tp in range(CMP_STRIDE // 2):
        rows = [s1_ref[pl.ds(2 * tp + e, N_CMP_PAD, stride=CMP_STRIDE), :] for e in range(2)]
        xa = jnp.concatenate([(rows[e] + pos_ref[2 * tp + e:2 * tp + e + 1, :]).astype(BF16)
                              for e in range(2)], axis=1)
        xb = jnp.concatenate([(rows[e] + pos_ref[CMP_STRIDE + 2 * tp + e:CMP_STRIDE + 2 * tp + e + 1, :])
                              .astype(BF16) for e in range(2)], axis=1)
        ha_ref[...] += _dot(xa, w1_ref[tp])
        hb_ref[0:N_CMP_PAD, :] += _dot(xb, w1_ref[CMP_STRIDE // 2 + tp])
    hid = jax.nn.gelu(ha_ref[...] + hb_ref[1:N_CMP_PAD + 1, :]).astype(BF16)
    kc = _dot(hid[:, :CMP_HIDDEN], w2_ref[0])
    vc = _dot(hid[:, CMP_HIDDEN:], w2_ref[1])
    ms = jnp.sum(kc * kc, axis=1, keepdims=True) * inv_d
    kcn = kc * lax.rsqrt(ms + NORM_EPS) * kcw_ref[...]
    ci = lax.broadcasted_iota(jnp.int32, (N_CMP_PAD, LANES), 0)
    cl = lax.broadcasted_iota(jnp.int32, (N_CMP_PAD, LANES), 1)
    c1 = (lax.shift_right_logical(ci, 2) * SEL_BLOCK).astype(F32)
    c2 = ((ci & 3) * CMP_STRIDE).astype(F32) + (CMP_BLOCK - 1) * 0.5
    kc_ref[...] = jnp.where(cl < HEAD_DIM, kcn, _pos_features(cl, HEAD_DIM, c1, c2)).astype(BF16)
    vct_ref[...] = vc.T[0:HEAD_DIM, :].astype(BF16)


def _kprep(proj, gl, knw2, kcw, pos2, w1p, w2p):
    kv0 = (Z_COLS + NSA_WIDTH) // LANES
    slab = lambda s: pl.BlockSpec((SEQ, LANES), lambda b, g, s=s: (b, kv0 + 3 * g + s))
    full = lambda shape: pl.BlockSpec(shape, lambda b, g: (0,) * len(shape))
    per_bg = lambda shape: pl.BlockSpec((None, None) + shape, lambda b, g: (b, g) + (0,) * len(shape))
    bg_shape = lambda shape, dt: jax.ShapeDtypeStruct((BATCH, KV_GROUPS) + shape, dt)
    return pl.pallas_call(
        _kprep_kernel,
        grid=(BATCH, KV_GROUPS),
        in_specs=[
            slab(0), slab(1), slab(2),
            pl.BlockSpec((SEQ, GATE_PAD), lambda b, g: (b, 0)),
            full((1, LANES)), full((1, LANES)), full((CMP_BLOCK, LANES)),
            full((CMP_BLOCK // 2, 2 * LANES, 2 * CMP_HIDDEN)), full((2, CMP_HIDDEN, LANES)),
        ],
        out_specs=[
            per_bg((N_KB, K_BLOCK, LANES)),
            per_bg((N_KB, K_BLOCK, LANES)),
            per_bg((N_KB, 2 * VT_ROWS, K_BLOCK)),
            per_bg((N_CMP_PAD, LANES)),
            per_bg((HEAD_DIM, N_CMP_PAD)),
            pl.BlockSpec((None, N_KB, GATE_PAD, K_BLOCK), lambda b, g: (b, 0, 0, 0)),
        ],
        out_shape=[
            bg_shape((N_KB, K_BLOCK, LANES), BF16),
            bg_shape((N_KB, K_BLOCK, LANES), BF16),
            bg_shape((N_KB, 2 * VT_ROWS, K_BLOCK), BF16),
            bg_shape((N_CMP_PAD, LANES), BF16),
            bg_shape((HEAD_DIM, N_CMP_PAD), BF16),
            jax.ShapeDtypeStruct((BATCH, N_KB, GATE_PAD, K_BLOCK), F32),
        ],
        scratch_shapes=[
            pltpu.VMEM((N_CMP_PAD, 2 * CMP_HIDDEN), F32),
            pltpu.VMEM((N_CMP_PAD + 8, 2 * CMP_HIDDEN), F32),
        ],
        compiler_params=_cparams("arbitrary", "arbitrary"),
        name="kprep",
    )(proj, proj, proj, gl, knw2, kcw, pos2, w1p, w2p)


def _attn_kernel(q_ref, ksel_ref, kwin_ref, vt_ref, kc_ref, vct_ref, gt_ref, sf_ref, wm_ref, dm_ref,
                 ov_ref, wq_ref, o_ref, sel_ref, qs_ref, sa_ref, sb_ref, tiles_ref):
    qt = pl.program_id(2)
    t_i = qt * Q_TILE + lax.broadcasted_iota(jnp.int32, (1, Q_TILE), 1)
    head = lambda a, r: a[:, r * Q_TILE:(r + 1) * Q_TILE]
    heads = lambda parts: jnp.concatenate(parts, axis=1)

    q_t = q_ref[...].T
    parts = []
    for r in range(REP):
        xq = q_t[r * HEAD_DIM:(r + 1) * HEAD_DIM, :]
        ms = jnp.sum(xq * xq, axis=0, keepdims=True) * (1.0 / HEAD_DIM)
        parts.append(xq * lax.rsqrt(ms + NORM_EPS) * wq_ref[...] * (HEAD_DIM ** -0.5) * LOG2E)
    qn = heads(parts)
    sf = sf_ref[...]
    sf = jnp.where(lax.broadcasted_iota(jnp.int32, sf.shape, 0) == 6, sf * (qt * Q_TILE).astype(F32), sf)
    wide = REP * Q_TILE
    pad = jnp.zeros((HEAD_DIM - 8, wide), F32)
    qs_ref[...] = jnp.concatenate([qn, sf, pad], axis=0).astype(BF16)
    q_win = jnp.concatenate([sf, pad, qn], axis=0).astype(BF16)

    sc = _dot(kc_ref[...], qs_ref[...])
    ci = lax.broadcasted_iota(jnp.int32, (N_CMP_PAD, Q_TILE), 0)
    cm = jnp.where(ci * CMP_STRIDE + (CMP_BLOCK - 1) <= t_i, 0.0, NEG_BIG)
    any_valid = t_i >= CMP_BLOCK - 1
    pcs = []
    for r in range(REP):
        s = head(sc, r) + cm
        e = jnp.exp2(s - jnp.max(s, axis=0, keepdims=True))
        d = jnp.sum(e, axis=0, keepdims=True)
        pcs.append((e * jnp.where(any_valid, 1.0 / d, 0.0)).astype(BF16))
    oc_imp = _dot(jnp.concatenate([vct_ref[...], ov_ref[...]], axis=0), heads(pcs))
    o_c = oc_imp[0:HEAD_DIM]
    imp4 = oc_imp[HEAD_DIM:]
    imp = (head(imp4, 0) + head(imp4, 1)) + (head(imp4, 2) + head(imp4, 3))

    jj = lax.broadcasted_iota(jnp.int32, (N_SEL, Q_TILE), 0)
    cur = lax.shift_right_logical(t_i, 6)
    valid = jj <= cur
    sel_ref[...] = jnp.where(valid, 1.0, 0.0)
    forced = (jj == 0) | (jj == cur) | (jj == cur - 1)
    score = jnp.where(valid, jnp.where(forced, SEL_BIG, imp), -SEL_BIG)

    def rank_rows(n8):
        def branch():
            rows8 = [score[8 * v:8 * v + 8, :] for v in range(n8)]
            j8 = lax.broadcasted_iota(jnp.int32, (8, Q_TILE), 0)
            ranks = [jnp.zeros((8, Q_TILE), F32) for _ in rows8]
            for jp in range(8 * n8):
                row = score[jp:jp + 1, :]
                for v, blk in enumerate(rows8):
                    if 8 * v > jp:
                        ahead = jnp.where(row >= blk, 1.0, 0.0)
                    elif 8 * v + 7 <= jp:
                        ahead = jnp.where(row > blk, 1.0, 0.0)
                    else:
                        ahead = jnp.where(j8 + 8 * v > jp, jnp.where(row >= blk, 1.0, 0.0),
                                          jnp.where(row > blk, 1.0, 0.0))
                    ranks[v] = ranks[v] + ahead
            rank = jnp.concatenate(ranks, axis=0)
            sel_ref[0:8 * n8, :] = jnp.where(valid[0:8 * n8] & (rank < SEL_TOPK), 1.0, 0.0)
        return branch

    n8_needed = (qt * Q_TILE + Q_TILE - 1) // (8 * SEL_BLOCK) + 1
    lax.switch(n8_needed - 1, [(lambda: None) if n8 * 8 <= SEL_TOPK else rank_rows(n8)
                               for n8 in range(1, N_SEL // 8 + 1)])

    sub = SEL_TILE_KEYS // SEL_BLOCK

    def scores_into(kt, s_ref):
        member = sel_ref[pl.ds(kt * sub, sub), :]
        mrows = (member - 1.0) * (-NEG_BIG)
        qs_ref[MASK_LANE0:MASK_LANE0 + 16, :] = jnp.concatenate(
            [heads([mrows] * REP), jnp.zeros((16 - sub, wide), F32)], axis=0).astype(BF16)
        k_tile = ksel_ref[pl.ds(kt * SEL_TILE_BLOCKS, SEL_TILE_BLOCKS)].reshape(SEL_TILE_KEYS, LANES)
        s_ref[...] = _dot(k_tile, qs_ref[...])

    def absorb(kt, s_ref, carry, diagonal):
        m_run, acc = carry
        kb0 = kt * SEL_TILE_BLOCKS
        if diagonal:
            causal = dm_ref[qt % (SEL_TILE_KEYS // Q_TILE)]
        v_t = jnp.concatenate([vt_ref[kb0 + x, 0:VT_ROWS, :] for x in range(SEL_TILE_BLOCKS)], axis=1)
        ms_, accs = [], []
        for r in range(REP):
            sr = s_ref[:, r * Q_TILE:(r + 1) * Q_TILE]
            if diagonal:
                sr = sr + causal
            m_old = head(m_run, r)
            m_new = jnp.maximum(m_old, jnp.max(sr, axis=0, keepdims=True))
            p = jnp.exp2(sr - m_new).astype(BF16)
            accs.append(head(acc, r) * jnp.exp2(m_old - m_new) + _dot(v_t, p))
            ms_.append(m_new)
        return heads(ms_), heads(accs)

    carry = (jnp.full((1, wide), NEG_BIG, F32), jnp.zeros((VT_ROWS, wide), F32))
    n_full = (qt * Q_TILE) // SEL_TILE_KEYS

    n_sel_tiles = SEQ // SEL_TILE_KEYS
    tr = lax.broadcasted_iota(jnp.int32, (n_sel_tiles, N_SEL), 0)
    tc = lax.broadcasted_iota(jnp.int32, (n_sel_tiles, N_SEL), 1)
    tile_of_block = jnp.where(lax.shift_right_logical(tc, 3) == tr, 1.0, 0.0).astype(BF16)
    per_query = _dot(tile_of_block, sel_ref[...].astype(BF16))
    per_tile = _dot(per_query.astype(BF16), jnp.ones((Q_TILE, LANES), BF16))
    bit = lax.shift_left(1, lax.broadcasted_iota(jnp.int32, (n_sel_tiles, LANES), 0)).astype(F32)
    used_bits = jnp.sum(jnp.where(per_tile > 0.5, bit, 0.0), axis=0, keepdims=True)[0, 0].astype(jnp.int32)

    kb_first = qt * (Q_TILE // K_BLOCK) - WINDOW // K_BLOCK
    kbs = [jnp.maximum(kb_first + x, 0) for x in range(WIN_BLOCKS)]
    sw = _dot(jnp.concatenate([kwin_ref[kb] for kb in kbs], axis=0), q_win)

    n_visit = jnp.int32(0)
    for kt in range(n_sel_tiles):
        tiles_ref[n_visit] = kt
        take = (kt < n_full) & ((lax.shift_right_logical(used_bits, kt) & 1) == 1)
        n_visit = n_visit + jnp.where(take, 1, 0)
    tiles_ref[n_visit] = n_full

    scores_into(tiles_ref[0], sa_ref)
    wm = jnp.concatenate(
        [wm_ref[x * K_BLOCK:(x + 1) * K_BLOCK, :] + jnp.where(kb_first + x < 0, NEG_BIG, 0.0)
         for x in range(WIN_BLOCKS)], axis=0)
    pws = []
    for r in range(REP):
        s = head(sw, r) + wm
        pws.append(jnp.exp2(s - jnp.max(s, axis=0, keepdims=True)).astype(BF16))
    vw_t = jnp.concatenate([vt_ref[kb, VT_ROWS:2 * VT_ROWS, :] for kb in kbs], axis=1)
    acc_w = _dot(vw_t, heads(pws))
    o_w = acc_w[0:HEAD_DIM] / acc_w[HEAD_DIM:HEAD_DIM + 1]

    def tile_pair(i, carry):
        scores_into(tiles_ref[2 * i + 1], sb_ref)
        carry = absorb(tiles_ref[2 * i], sa_ref, carry, False)
        scores_into(tiles_ref[2 * i + 2], sa_ref)
        return absorb(tiles_ref[2 * i + 1], sb_ref, carry, False)

    carry = lax.fori_loop(0, n_visit // 2, tile_pair, carry)
    last = 2 * (n_visit // 2)

    def odd_tail(carry):
        scores_into(n_full, sb_ref)
        return absorb(n_full, sb_ref, absorb(tiles_ref[last], sa_ref, carry, False), True)

    def even_tail(carry):
        return absorb(n_full, sa_ref, carry, True)

    _, acc_s = lax.cond(n_visit % 2 == 1, odd_tail, even_tail, carry)
    o_s = acc_s[0:HEAD_DIM] / acc_s[HEAD_DIM:HEAD_DIM + 1]

    gates = jnp.concatenate([gt_ref[x] for x in range(Q_TILE // K_BLOCK)], axis=1)
    outs = []
    for r in range(REP):
        outs.append(gates[r:r + 1, :] * head(o_c, r)
                    + gates[REP + r:REP + r + 1, :] * head(o_s, r)
                    + gates[2 * REP + r:2 * REP + r + 1, :] * head(o_w, r))
    o_ref[...] = jnp.concatenate(outs, axis=0).T.astype(BF16)


def _attn(proj, ksel, kwin, vt, kc, vct, gt, sf, wm, dm, ov_t, wq_b):
    q0 = Z_COLS // (REP * HEAD_DIM)
    kvspec = lambda shape: pl.BlockSpec((None, None) + shape, lambda b, g, qt: (b, g) + (0,) * len(shape))
    const = lambda shape: pl.BlockSpec(shape, lambda b, g, qt: (0,) * len(shape))
    return pl.pallas_call(
        _attn_kernel,
        grid=(BATCH, KV_GROUPS, N_QT),
        in_specs=[
            pl.BlockSpec((Q_TILE, REP * HEAD_DIM), lambda b, g, qt: (b * N_QT + qt, q0 + g)),
            kvspec((N_KB, K_BLOCK, LANES)),
            kvspec((N_KB, K_BLOCK, LANES)),
            kvspec((N_KB, 2 * VT_ROWS, K_BLOCK)),
            kvspec((N_CMP_PAD, LANES)),
            kvspec((HEAD_DIM, N_CMP_PAD)),
            pl.BlockSpec((None, Q_TILE // K_BLOCK, GATE_ROWS_PER_GROUP, K_BLOCK), lambda b, g, qt: (b, qt, g, 0)),
            pl.BlockSpec((None, 8, REP * Q_TILE), lambda b, g, qt: (g, 0, 0)),
            const((WIN_KEYS, Q_TILE)),
            const((SEL_TILE_KEYS // Q_TILE, SEL_TILE_KEYS, Q_TILE)),
            const((N_SEL, N_CMP_PAD)),
            const((HEAD_DIM, Q_TILE)),
        ],
        out_specs=pl.BlockSpec((Q_TILE, REP * HEAD_DIM), lambda b, g, qt: (b * N_QT + qt, g)),
        out_shape=jax.ShapeDtypeStruct((N_TOK, NSA_WIDTH), BF16),
        scratch_shapes=[pltpu.VMEM((N_SEL, Q_TILE), F32), pltpu.VMEM((LANES, REP * Q_TILE), BF16),
                        pltpu.VMEM((SEL_TILE_KEYS, REP * Q_TILE), F32),
                        pltpu.VMEM((SEL_TILE_KEYS, REP * Q_TILE), F32),
                        pltpu.SMEM((SEQ // SEL_TILE_KEYS + 1,), jnp.int32)],
        compiler_params=_cparams("arbitrary", "arbitrary", "arbitrary"),
        name="attn",
    )(proj, ksel, kwin, vt, kc, vct, gt, sf, wm, dm, ov_t, wq_b)


def _spatial_gating(z, lnw, lnb, sw_ref, sbx):
    ge = jax.nn.gelu(z)
    u = ge[:, :GMLP_WIDTH]
    v = ge[:, GMLP_WIDTH:]
    mu = jnp.mean(v, axis=-1, keepdims=True)
    var = jnp.mean(jnp.square(v - mu), axis=-1, keepdims=True)
    vn = ((v - mu) * lax.rsqrt(var + LN_EPS) * lnw + lnb).astype(BF16)
    ti = lax.broadcasted_iota(jnp.int32, (GMLP_CHUNK, GMLP_CHUNK), 0)
    si = lax.broadcasted_iota(jnp.int32, (GMLP_CHUNK, GMLP_CHUNK), 1)
    ws = [jnp.where(ti >= si, sw_ref[gg], 0.0).astype(BF16) for gg in range(GMLP_GROUPS)]
    gd = GMLP_WIDTH // GMLP_GROUPS
    outs = []
    for c in range(z.shape[0] // GMLP_CHUNK):
        rows = slice(c * GMLP_CHUNK, (c + 1) * GMLP_CHUNK)
        mix = jnp.concatenate(
            [_dot(ws[gg], vn[rows, gg * gd:(gg + 1) * gd]) for gg in range(GMLP_GROUPS)], axis=1)
        outs.append((u[rows, :] * (mix + sbx)).astype(BF16))
    return jnp.concatenate(outs, axis=0)


def _oproj_kernel(a_ref, z0_ref, z1_ref, z2_ref, x_ref, wa_ref, wb_ref, nw_ref, lnw_ref, lnb_ref, sw_ref,
                  sbx_ref, x1_ref, xn_ref, b0_ref, b1_ref):
    half = OPROJ_TM // 2
    gating = lambda z_ref: _spatial_gating(z_ref[...], lnw_ref[...], lnb_ref[...], sw_ref, sbx_ref[...])

    @pl.when(pl.program_id(0) == 0)
    def _():
        b0_ref[...] = gating(z0_ref)

    def finish(rows, b_ref):
        y = x_ref[rows, :] + (_dot(a_ref[rows, :], wa_ref[...]) + _dot(b_ref[...], wb_ref[...]))
        x1_ref[rows, :] = y
        ms = jnp.mean(y * y, axis=-1, keepdims=True)
        xn_ref[rows, :] = (y * lax.rsqrt(ms + NORM_EPS) * nw_ref[...]).astype(BF16)

    b1_ref[...] = gating(z1_ref)
    finish(slice(0, half), b0_ref)
    b0_ref[...] = gating(z2_ref)
    finish(slice(half, OPROJ_TM), b1_ref)


def _oproj(a, proj, x2, w_out, nw, lnw, lnb, sw, sbx):
    half = OPROJ_TM // 2
    last = N_TOK // half - 1
    return pl.pallas_call(
        _oproj_kernel,
        grid=(N_TOK // OPROJ_TM,),
        in_specs=[
            pl.BlockSpec((OPROJ_TM, NSA_WIDTH), lambda i: (i, 0)),
            pl.BlockSpec((half, Z_COLS), lambda i: (0, 0)),
            pl.BlockSpec((half, Z_COLS), lambda i: (2 * i + 1, 0)),
            pl.BlockSpec((half, Z_COLS), lambda i: (jnp.minimum(2 * i + 2, last), 0)),
            pl.BlockSpec((OPROJ_TM, D_MODEL), lambda i: (i, 0)),
            pl.BlockSpec((NSA_WIDTH, D_MODEL), lambda i: (0, 0)),
            pl.BlockSpec((GMLP_WIDTH, D_MODEL), lambda i: (1, 0)),
            pl.BlockSpec((1, D_MODEL), lambda i: (0, 0)),
            pl.BlockSpec((1, GMLP_WIDTH), lambda i: (0, 0)),
            pl.BlockSpec((1, GMLP_WIDTH), lambda i: (0, 0)),
            pl.BlockSpec((GMLP_GROUPS, GMLP_CHUNK, GMLP_CHUNK), lambda i: (0, 0, 0)),
            pl.BlockSpec((GMLP_CHUNK, GMLP_WIDTH), lambda i: (0, 0)),
        ],
        out_specs=[
            pl.BlockSpec((OPROJ_TM, D_MODEL), lambda i: (i, 0)),
            pl.BlockSpec((OPROJ_TM, D_MODEL), lambda i: (i, 0)),
        ],
        out_shape=[
            jax.ShapeDtypeStruct((N_TOK, D_MODEL), F32),
            jax.ShapeDtypeStruct((N_TOK, D_MODEL), BF16),
        ],
        scratch_shapes=[pltpu.VMEM((OPROJ_TM // 2, GMLP_WIDTH), BF16), pltpu.VMEM((OPROJ_TM // 2, GMLP_WIDTH), BF16)],
        compiler_params=_cparams("arbitrary"),
        name="oproj",
    )(a, proj, proj, proj, x2, w_out, w_out, nw, lnw, lnb, sw, sbx)


def _ffn_kernel(xn_ref, x1_ref, wg_ref, wu_ref, cwg_ref, cwu_ref, cbg_ref, cbu_ref, wd_ref,
                cwg_p_ref, cwu_p_ref, cbg_p_ref, cbu_p_ref, wd_p_ref, o_ref, h0_ref, h1_ref, carry_ref):
    i = pl.program_id(0)
    q = pl.program_id(1)
    nq = pl.num_programs(1)
    half = FFN_TF // 2

    @pl.when((i == 0) & (q == 0))
    def _():
        carry_ref[...] = jnp.zeros_like(carry_ref)
        h1_ref[...] = jnp.zeros_like(h1_ref)

    @pl.when(q == 0)
    def _():
        o_ref[...] = x1_ref[...]

    seq_start = i % (SEQ // FFN_TM) == 0
    xn = xn_ref[...]

    def produce(c, h_ref):
        h_ref[0, 8:, :] = _dot(xn, wg_ref[:, c * half:(c + 1) * half])
        h_ref[1, 8:, :] = _dot(xn, wu_ref[:, c * half:(c + 1) * half])

    def conv(h_ref, part, tile, cw, cb):
        h_ref[part, 0:8, :] = jnp.where(seq_start, 0.0, carry_ref[tile, part])
        carry_ref[tile, part] = h_ref[part, FFN_TM:FFN_TM + 8, :]
        taps = [h_ref[part, 8 - k:8 - k + FFN_TM, :] for k in (2, 1, 0)]
        return cb + ((cw[0:1, :] * taps[0] + cw[1:2, :] * taps[1]) + cw[2:3, :] * taps[2])

    def gate(h_ref, tile, c, cwg, cwu, cbg, cbu, scale=None):
        cols = slice(c * half, (c + 1) * half)
        cg = conv(h_ref, 0, tile, cwg[:, cols], cbg[:, cols])
        cu = conv(h_ref, 1, tile, cwu[:, cols], cbu[:, cols])
        act = jax.nn.silu(cg) * cu
        if scale is not None:
            act = act * scale
        return act.astype(BF16)

    act_prev = gate(h1_ref, jnp.where(q == 0, 2 * nq - 1, 2 * q - 1), 1, cwg_p_ref, cwu_p_ref, cbg_p_ref,
                    cbu_p_ref, scale=jnp.where(q == 0, 0.0, 1.0))
    produce(0, h0_ref)
    down_prev = _dot(act_prev, wd_p_ref[half:, :])
    act_cur = gate(h0_ref, 2 * q, 0, cwg_ref, cwu_ref, cbg_ref, cbu_ref)
    produce(1, h1_ref)
    o_ref[...] += down_prev + _dot(act_cur, wd_ref[0:half, :])

    @pl.when(q == nq - 1)
    def _():
        act_last = gate(h1_ref, 2 * q + 1, 1, cwg_ref, cwu_ref, cbg_ref, cbu_ref)
        o_ref[...] += _dot(act_last, wd_ref[half:, :])


def _ffn(xn, x1, w_up, conv_w, conv_b, w_down):
    nj = D_FF // FFN_TF
    prev = lambda q: jnp.maximum(q - 1, 0)
    return pl.pallas_call(
        _ffn_kernel,
        grid=(N_TOK // FFN_TM, nj),
        in_specs=[
            pl.BlockSpec((FFN_TM, D_MODEL), lambda i, q: (i, 0)),
            pl.BlockSpec((FFN_TM, D_MODEL), lambda i, q: (i, 0)),
            pl.BlockSpec((D_MODEL, FFN_TF), lambda i, q: (0, q)),
            pl.BlockSpec((D_MODEL, FFN_TF), lambda i, q: (0, nj + q)),
            pl.BlockSpec((CONV_WIDTH, FFN_TF), lambda i, q: (0, q)),
            pl.BlockSpec((CONV_WIDTH, FFN_TF), lambda i, q: (0, nj + q)),
            pl.BlockSpec((1, FFN_TF), lambda i, q: (0, q)),
            pl.BlockSpec((1, FFN_TF), lambda i, q: (0, nj + q)),
            pl.BlockSpec((FFN_TF, D_MODEL), lambda i, q: (q, 0)),
            pl.BlockSpec((CONV_WIDTH, FFN_TF), lambda i, q: (0, prev(q))),
            pl.BlockSpec((CONV_WIDTH, FFN_TF), lambda i, q: (0, nj + prev(q))),
            pl.BlockSpec((1, FFN_TF), lambda i, q: (0, prev(q))),
            pl.BlockSpec((1, FFN_TF), lambda i, q: (0, nj + prev(q))),
            pl.BlockSpec((FFN_TF, D_MODEL), lambda i, q: (prev(q), 0)),
        ],
        out_specs=pl.BlockSpec((FFN_TM, D_MODEL), lambda i, q: (i, 0)),
        out_shape=jax.ShapeDtypeStruct((N_TOK, D_MODEL), F32),
        scratch_shapes=[pltpu.VMEM((2, FFN_TM + 8, FFN_TF // 2), F32), pltpu.VMEM((2, FFN_TM + 8, FFN_TF // 2), F32),
                        pltpu.VMEM((2 * nj, 2, 8, FFN_TF // 2), F32)],
        compiler_params=_cparams("arbitrary", "arbitrary"),
        name="ffn",
    )(xn, x1, w_up, w_up, conv_w, conv_w, conv_b, conv_b, w_down, conv_w, conv_w, conv_b, conv_b, w_down)


def _overlap_t():
    start = np.arange(N_CMP_PAD)[None, :] * CMP_STRIDE
    s0 = np.arange(N_SEL)[:, None] * SEL_BLOCK
    ov = (start < s0 + SEL_BLOCK) & (start + CMP_BLOCK > s0) & (np.arange(N_CMP_PAD)[None, :] < N_CMP_PAD - 1)
    return ov.astype(np.float32)


def _bf16_round(a):
    return np.asarray(a, np.float32).astype(BF16).astype(np.float32)


def _slope_features():
    sl = (np.power(2.0, -8.0 * np.arange(1, NSA_HEADS + 1) / NSA_HEADS).astype(np.float32)
          * np.float32(LOG2E)).astype(np.float32)
    s1 = _bf16_round(sl)
    s2 = _bf16_round(sl - s1)
    s3 = _bf16_round(sl - s1 - s2)
    rows = np.stack([s1, s2, s3, s1, s2, s3, -sl, np.zeros_like(sl)], axis=0)
    rows = rows.reshape(8, KV_GROUPS, REP).transpose(1, 0, 2)
    return np.repeat(rows, Q_TILE, axis=2).astype(np.float32)


def _window_mask():
    kl = np.arange(WIN_KEYS)[:, None]
    ql = np.arange(Q_TILE)[None, :]
    dist = ql + WINDOW - kl
    return np.where((dist >= 0) & (dist < WINDOW), 0.0, NEG_BIG).astype(np.float32)


def _diag_masks():
    kl = np.arange(SEL_TILE_KEYS)[None, :, None]
    ql = np.arange(Q_TILE)[None, None, :]
    off = np.arange(SEL_TILE_KEYS // Q_TILE)[:, None, None] * Q_TILE
    return np.where(kl > ql + off, NEG_BIG, 0.0).astype(np.float32)


def _layout_w_in(w_in):
    q_end = NSA_WIDTH
    kv_end = q_end + 6 * KV_COLS
    g_end = kv_end + NSA_HEADS * N_BRANCH
    w_in = w_in.astype(BF16)
    w_q = w_in[:, :q_end]
    w_kv = w_in[:, q_end:kv_end].reshape(D_MODEL, 6, KV_GROUPS, HEAD_DIM)
    w_kv = jnp.stack([w_kv[:, s] for s in (2, 4, 0, 1, 3, 5)], axis=2).reshape(D_MODEL, 6 * KV_COLS)
    w_z = w_in[:, g_end:]
    w_main = jnp.concatenate([w_z, w_q, w_kv], axis=1)
    w_g = w_in[:, kv_end:g_end].reshape(D_MODEL, KV_GROUPS, REP, N_BRANCH).transpose(0, 1, 3, 2)
    w_g = w_g.reshape(D_MODEL, KV_GROUPS, N_BRANCH * REP)
    w_g = jnp.pad(w_g, ((0, 0), (0, 0), (0, GATE_ROWS_PER_GROUP - N_BRANCH * REP)))
    w_g = jnp.pad(w_g.reshape(D_MODEL, KV_GROUPS * GATE_ROWS_PER_GROUP),
                  ((0, 0), (0, GATE_PAD - KV_GROUPS * GATE_ROWS_PER_GROUP)))
    return w_main, w_g


def _layout_compress(cmp_pos, cmp_w1, cmp_w2):
    pos2 = jnp.concatenate([cmp_pos[0], cmp_pos[1]], axis=1)
    w1 = cmp_w1.reshape(2, CMP_BLOCK, HEAD_DIM, CMP_HIDDEN)
    zero = jnp.zeros_like(w1[0])
    w1k = jnp.concatenate([w1[0], zero], axis=1)
    w1v = jnp.concatenate([zero, w1[1]], axis=1)
    w1p = jnp.concatenate([w1k, w1v], axis=2).astype(BF16)
    w1p = w1p.reshape(CMP_BLOCK // 2, 2 * LANES, 2 * CMP_HIDDEN)
    w2p = jnp.pad(cmp_w2, ((0, 0), (0, 0), (0, LANES - HEAD_DIM))).astype(BF16)
    return pos2, w1p, w2p


def kernel(x, attn_norm_w, w_in, q_norm_w, k_norm_w, cmp_pos, cmp_w1, cmp_w2, gmlp_ln_w, gmlp_ln_b,
           spatial_w, spatial_b, w_out, ffn_norm_w, w_up, conv_w, conv_b, w_down):
    x2 = x.reshape(N_TOK, D_MODEL)
    w_main, w_gate = _layout_w_in(w_in)
    proj, gl = _proj(x2, attn_norm_w.reshape(1, D_MODEL), w_main, w_gate)

    knw2 = jnp.concatenate([k_norm_w[1], k_norm_w[2]]).reshape(1, LANES)
    kcw = jnp.concatenate([k_norm_w[0], jnp.zeros((LANES - HEAD_DIM,), F32)]).reshape(1, LANES)
    pos2, w1p, w2p = _layout_compress(cmp_pos, cmp_w1, cmp_w2)
    ksel, kwin, vt, kc, vct, gt = _kprep(proj, gl, knw2, kcw, pos2, w1p, w2p)

    wq_b = jnp.broadcast_to(q_norm_w.reshape(HEAD_DIM, 1), (HEAD_DIM, Q_TILE))
    a = _attn(proj, ksel, kwin, vt, kc, vct, gt, jnp.asarray(_slope_features()),
              jnp.asarray(_window_mask()), jnp.asarray(_diag_masks()),
              jnp.asarray(_overlap_t(), dtype=BF16), wq_b)

    sbx = jnp.repeat(spatial_b.T, GMLP_WIDTH // GMLP_GROUPS, axis=1)
    x1, xn = _oproj(a, proj, x2, w_out.astype(BF16), ffn_norm_w.reshape(1, D_MODEL),
                    gmlp_ln_w.reshape(1, GMLP_WIDTH), gmlp_ln_b.reshape(1, GMLP_WIDTH), spatial_w, sbx)

    out = _ffn(xn, x1, w_up.astype(BF16), conv_w, conv_b.reshape(1, 2 * D_FF), w_down.astype(BF16))
    return out.reshape(BATCH, SEQ, D_MODEL)
```

```python
import numpy as np
import jax
import jax.numpy as jnp
from jax import lax
from jax.experimental import pallas as pl
from jax.experimental.pallas import tpu as pltpu

F32 = jnp.float32
BF16 = jnp.bfloat16

D_MODEL = 2048
BATCH = 4
SEQ = 4096
N_TOK = BATCH * SEQ
NSA_HEADS = 16
KV_GROUPS = 4
REP = NSA_HEADS // KV_GROUPS
HEAD_DIM = 64
NSA_WIDTH = NSA_HEADS * HEAD_DIM
GMLP_WIDTH = D_MODEL - NSA_WIDTH
GMLP_GROUPS = 8
GMLP_CHUNK = 128
CMP_BLOCK = 32
CMP_STRIDE = 16
CMP_HIDDEN = 256
N_CMP_PAD = SEQ // CMP_STRIDE
SEL_BLOCK = 64
N_SEL = SEQ // SEL_BLOCK
SEL_TOPK = 16
WINDOW = 512
K_BLOCK = 128
N_KB = SEQ // K_BLOCK
Q_TILE = 256
N_QT = SEQ // Q_TILE
N_BRANCH = 3
KV_COLS = KV_GROUPS * HEAD_DIM
D_FF = 5632
CONV_WIDTH = 3
NORM_EPS = 1e-6
LN_EPS = 1e-5
NEG_BIG = -1e30
SEL_BIG = 1e9

LANES = 128
VMEM_LIMIT_BYTES = 56 * 1024 * 1024

Z_COLS = 2 * GMLP_WIDTH
PROJ_COLS = Z_COLS + NSA_WIDTH + 6 * KV_COLS
GATE_PAD = LANES
GATE_ROWS_PER_GROUP = 16

PROJ_TM = 1024
PROJ_TN = 1536
OPROJ_TM = 512
FFN_TM = 512
FFN_TF = 512
SEL_TILE_BLOCKS = 4
SEL_TILE_KEYS = SEL_TILE_BLOCKS * K_BLOCK
MASK_LANE0 = 80
SUM_ROWS = 16
VT_ROWS = HEAD_DIM + SUM_ROWS
LOG2E = 1.4426950408889634
WIN_BLOCKS = (WINDOW + Q_TILE) // K_BLOCK
WIN_KEYS = WIN_BLOCKS * K_BLOCK


def _cparams(*sem):
    return pltpu.CompilerParams(dimension_semantics=sem, vmem_limit_bytes=VMEM_LIMIT_BYTES)


def _dot(a, b):
    return jnp.dot(a, b, preferred_element_type=F32)


def _proj_kernel(x_ref, nw_ref, w_ref, wg_ref, o_ref, og_ref, h_ref):
    @pl.when(pl.program_id(1) == 0)
    def _():
        x = x_ref[...]
        ms = jnp.mean(x * x, axis=-1, keepdims=True)
        h = (x * lax.rsqrt(ms + NORM_EPS) * nw_ref[...]).astype(BF16)
        h_ref[...] = h
        og_ref[...] = _dot(h, wg_ref[...])

    o_ref[...] = _dot(h_ref[...], w_ref[...])


def _proj(x2, nw, w_main, w_gate):
    return pl.pallas_call(
        _proj_kernel,
        grid=(N_TOK // PROJ_TM, PROJ_COLS // PROJ_TN),
        in_specs=[
            pl.BlockSpec((PROJ_TM, D_MODEL), lambda i, j: (i, 0)),
            pl.BlockSpec((1, D_MODEL), lambda i, j: (0, 0)),
            pl.BlockSpec((D_MODEL, PROJ_TN), lambda i, j: (0, j)),
            pl.BlockSpec((D_MODEL, GATE_PAD), lambda i, j: (0, 0)),
        ],
        out_specs=[
            pl.BlockSpec((PROJ_TM, PROJ_TN), lambda i, j: (i, j)),
            pl.BlockSpec((PROJ_TM, GATE_PAD), lambda i, j: (i, 0)),
        ],
        out_shape=[
            jax.ShapeDtypeStruct((N_TOK, PROJ_COLS), F32),
            jax.ShapeDtypeStruct((N_TOK, GATE_PAD), F32),
        ],
        scratch_shapes=[pltpu.VMEM((PROJ_TM, D_MODEL), BF16)],
        compiler_params=_cparams("arbitrary", "arbitrary"),
        name="proj",
    )(x2, nw, w_main, w_gate)


def _pos_features(lane, first, f1, f2):
    return jnp.where(lane < first + 3, f1, jnp.where(lane < first + 6, f2,
                                                     jnp.where(lane == first + 6, 1.0, 0.0)))


def _kprep_kernel(s0_ref, s1_ref, s2_ref, gl_ref, knw_ref, kcw_ref, pos_ref, w1_ref, w2_ref,
                  ksel_ref, kwin_ref, vt_ref, kc_ref, vct_ref, gt_ref, ha_ref, hb_ref):
    lane = lax.broadcasted_iota(jnp.int32, (K_BLOCK, LANES), 1)
    row = lax.broadcasted_iota(jnp.int32, (K_BLOCK, LANES), 0)
    lo = lane < HEAD_DIM
    inv_d = 1.0 / HEAD_DIM
    ones_rows = jnp.where(lax.broadcasted_iota(jnp.int32, (SUM_ROWS, K_BLOCK), 0) == 0, 1.0, 0.0)

    def chunk(c, carry):
        r0 = pl.multiple_of(c * K_BLOCK, K_BLOCK)
        x = s0_ref[pl.ds(r0, K_BLOCK), :]
        sq = x * x
        s_lo = jnp.sum(jnp.where(lo, sq, 0.0), axis=1, keepdims=True)
        s_hi = jnp.sum(jnp.where(lo, 0.0, sq), axis=1, keepdims=True)
        ms = jnp.where(lo, s_lo, s_hi) * inv_d
        y = x * lax.rsqrt(ms + NORM_EPS) * knw_ref[...]
        pos = r0 + row
        blk = lax.shift_right_logical(pos, 6)
        f1 = (blk * SEL_BLOCK).astype(F32)
        f2 = (pos & (SEL_BLOCK - 1)).astype(F32)
        onehot = jnp.where(lane - MASK_LANE0 == (blk & (SEL_TILE_BLOCKS * 2 - 1)), 1.0, 0.0)
        feat_sel = jnp.where(lane < MASK_LANE0, _pos_features(lane, HEAD_DIM, f1, f2), onehot)
        ksel_ref[c] = jnp.where(lo, y, feat_sel).astype(BF16)
        kwin_ref[c] = jnp.where(lo, _pos_features(lane, 0, f1, f2), y).astype(BF16)
        v_t = s2_ref[pl.ds(r0, K_BLOCK), :].T
        vt_ref[c] = jnp.concatenate(
            [v_t[0:HEAD_DIM], ones_rows, v_t[HEAD_DIM:], ones_rows], axis=0).astype(BF16)
        return carry

    lax.fori_loop(0, N_KB, chunk, 0, unroll=4)

    @pl.when(pl.program_id(1) == 0)
    def _():
        def gchunk(c, carry):
            r0 = pl.multiple_of(c * K_BLOCK, K_BLOCK)
            gt_ref[c] = jax.nn.sigmoid(gl_ref[pl.ds(r0, K_BLOCK), :]).T
            return carry

        lax.fori_loop(0, N_KB, gchunk, 0, unroll=4)

    ha_ref[...] = jnp.zeros_like(ha_ref)
    hb_ref[...] = jnp.zeros_like(hb_ref)
    for tp in range(CMP_STRIDE // 2):
        rows = [s1_ref[pl.ds(2 * tp + e, N_CMP_PAD, stride=CMP_STRIDE), :] for e in range(2)]
        xa = jnp.concatenate([(rows[e] + pos_ref[2 * tp + e:2 * tp + e + 1, :]).astype(BF16)
                              for e in range(2)], axis=1)
        xb = jnp.concatenate([(rows[e] + pos_ref[CMP_STRIDE + 2 * tp + e:CMP_STRIDE + 2 * tp + e + 1, :])
                              .astype(BF16) for e in range(2)], axis=1)
        ha_ref[...] += _dot(xa, w1_ref[tp])
        hb_ref[0:N_CMP_PAD, :] += _dot(xb, w1_ref[CMP_STRIDE // 2 + tp])
    hid = jax.nn.gelu(ha_ref[...] + hb_ref[1:N_CMP_PAD + 1, :]).astype(BF16)
    kc = _dot(hid[:, :CMP_HIDDEN], w2_ref[0])
    vc = _dot(hid[:, CMP_HIDDEN:], w2_ref[1])
    ms = jnp.sum(kc * kc, axis=1, keepdims=True) * inv_d
    kcn = kc * lax.rsqrt(ms + NORM_EPS) * kcw_ref[...]
    ci = lax.broadcasted_iota(jnp.int32, (N_CMP_PAD, LANES), 0)
    cl = lax.broadcasted_iota(jnp.int32, (N_CMP_PAD, LANES), 1)
    c1 = (lax.shift_right_logical(ci, 2) * SEL_BLOCK).astype(F32)
    c2 = ((ci & 3) * CMP_STRIDE).astype(F32) + (CMP_BLOCK - 1) * 0.5
    kc_ref[...] = jnp.where(cl < HEAD_DIM, kcn, _pos_features(cl, HEAD_DIM, c1, c2)).astype(BF16)
    vct_ref[...] = vc.T[0:HEAD_DIM, :].astype(BF16)


def _kprep(proj, gl, knw2, kcw, pos2, w1p, w2p):
    kv0 = (Z_COLS + NSA_WIDTH) // LANES
    slab = lambda s: pl.BlockSpec((SEQ, LANES), lambda b, g, s=s: (b, kv0 + 3 * g + s))
    full = lambda shape: pl.BlockSpec(shape, lambda b, g: (0,) * len(shape))
    per_bg = lambda shape: pl.BlockSpec((None, None) + shape, lambda b, g: (b, g) + (0,) * len(shape))
    bg_shape = lambda shape, dt: jax.ShapeDtypeStruct((BATCH, KV_GROUPS) + shape, dt)
    return pl.pallas_call(
        _kprep_kernel,
        grid=(BATCH, KV_GROUPS),
        in_specs=[
            slab(0), slab(1), slab(2),
            pl.BlockSpec((SEQ, GATE_PAD), lambda b, g: (b, 0)),
            full((1, LANES)), full((1, LANES)), full((CMP_BLOCK, LANES)),
            full((CMP_BLOCK // 2, 2 * LANES, 2 * CMP_HIDDEN)), full((2, CMP_HIDDEN, LANES)),
        ],
        out_specs=[
            per_bg((N_KB, K_BLOCK, LANES)),
            per_bg((N_KB, K_BLOCK, LANES)),
            per_bg((N_KB, 2 * VT_ROWS, K_BLOCK)),
            per_bg((N_CMP_PAD, LANES)),
            per_bg((HEAD_DIM, N_CMP_PAD)),
            pl.BlockSpec((None, N_KB, GATE_PAD, K_BLOCK), lambda b, g: (b, 0, 0, 0)),
        ],
        out_shape=[
            bg_shape((N_KB, K_BLOCK, LANES), BF16),
            bg_shape((N_KB, K_BLOCK, LANES), BF16),
            bg_shape((N_KB, 2 * VT_ROWS, K_BLOCK), BF16),
            bg_shape((N_CMP_PAD, LANES), BF16),
            bg_shape((HEAD_DIM, N_CMP_PAD), BF16),
            jax.ShapeDtypeStruct((BATCH, N_KB, GATE_PAD, K_BLOCK), F32),
        ],
        scratch_shapes=[
            pltpu.VMEM((N_CMP_PAD, 2 * CMP_HIDDEN), F32),
            pltpu.VMEM((N_CMP_PAD + 8, 2 * CMP_HIDDEN), F32),
        ],
        compiler_params=_cparams("arbitrary", "arbitrary"),
        name="kprep",
    )(proj, proj, proj, gl, knw2, kcw, pos2, w1p, w2p)


def _attn_kernel(q_ref, ksel_ref, kwin_ref, vt_ref, kc_ref, vct_ref, gt_ref, sf_ref, wm_ref, dm_ref,
                 ov_ref, wq_ref, o_ref, sel_ref, qs_ref, sa_ref, sb_ref, tiles_ref):
    qt = pl.program_id(2)
    t_i = qt * Q_TILE + lax.broadcasted_iota(jnp.int32, (1, Q_TILE), 1)
    head = lambda a, r: a[:, r * Q_TILE:(r + 1) * Q_TILE]
    heads = lambda parts: jnp.concatenate(parts, axis=1)

    q_t = q_ref[...].T
    parts = []
    for r in range(REP):
        xq = q_t[r * HEAD_DIM:(r + 1) * HEAD_DIM, :]
        ms = jnp.sum(xq * xq, axis=0, keepdims=True) * (1.0 / HEAD_DIM)
        parts.append(xq * lax.rsqrt(ms + NORM_EPS) * wq_ref[...] * (HEAD_DIM ** -0.5) * LOG2E)
    qn = heads(parts)
    sf = sf_ref[...]
    sf = jnp.where(lax.broadcasted_iota(jnp.int32, sf.shape, 0) == 6, sf * (qt * Q_TILE).astype(F32), sf)
    wide = REP * Q_TILE
    pad = jnp.zeros((HEAD_DIM - 8, wide), F32)
    qs_ref[...] = jnp.concatenate([qn, sf, pad], axis=0).astype(BF16)
    q_win = jnp.concatenate([sf, pad, qn], axis=0).astype(BF16)

    sc = _dot(kc_ref[...], qs_ref[...])
    ci = lax.broadcasted_iota(jnp.int32, (N_CMP_PAD, Q_TILE), 0)
    cm = jnp.where(ci * CMP_STRIDE + (CMP_BLOCK - 1) <= t_i, 0.0, NEG_BIG)
    any_valid = t_i >= CMP_BLOCK - 1
    pcs = []
    for r in range(REP):
        s = head(sc, r) + cm
        e = jnp.exp2(s - jnp.max(s, axis=0, keepdims=True))
        d = jnp.sum(e, axis=0, keepdims=True)
        pcs.append((e * jnp.where(any_valid, 1.0 / d, 0.0)).astype(BF16))
    oc_imp = _dot(jnp.concatenate([vct_ref[...], ov_ref[...]], axis=0), heads(pcs))
    o_c = oc_imp[0:HEAD_DIM]
    imp4 = oc_imp[HEAD_DIM:]
    imp = (head(imp4, 0) + head(imp4, 1)) + (head(imp4, 2) + head(imp4, 3))

    jj = lax.broadcasted_iota(jnp.int32, (N_SEL, Q_TILE), 0)
    cur = lax.shift_right_logical(t_i, 6)
    valid = jj <= cur
    sel_ref[...] = jnp.where(valid, 1.0, 0.0)
    forced = (jj == 0) | (jj == cur) | (jj == cur - 1)
    score = jnp.where(valid, jnp.where(forced, SEL_BIG, imp), -SEL_BIG)

    def rank_rows(n8):
        def branch():
            rows8 = [score[8 * v:8 * v + 8, :] for v in range(n8)]
            j8 = lax.broadcasted_iota(jnp.int32, (8, Q_TILE), 0)
            ranks = [jnp.zeros((8, Q_TILE), F32) for _ in rows8]
            for jp in range(8 * n8):
                row = score[jp:jp + 1, :]
                for v, blk in enumerate(rows8):
                    if 8 * v > jp:
                        ahead = jnp.where(row >= blk, 1.0, 0.0)
                    elif 8 * v + 7 <= jp:
                        ahead = jnp.where(row > blk, 1.0, 0.0)
                    else:
                        ahead = jnp.where(j8 + 8 * v > jp, jnp.where(row >= blk, 1.0, 0.0),
                                          jnp.where(row > blk, 1.0, 0.0))
                    ranks[v] = ranks[v] + ahead
            rank = jnp.concatenate(ranks, axis=0)
            sel_ref[0:8 * n8, :] = jnp.where(valid[0:8 * n8] & (rank < SEL_TOPK), 1.0, 0.0)
        return branch

    n8_needed = (qt * Q_TILE + Q_TILE - 1) // (8 * SEL_BLOCK) + 1
    lax.switch(n8_needed - 1, [(lambda: None) if n8 * 8 <= SEL_TOPK else rank_rows(n8)
                               for n8 in range(1, N_SEL // 8 + 1)])

    sub = SEL_TILE_KEYS // SEL_BLOCK

    def scores_into(kt, s_ref):
        member = sel_ref[pl.ds(kt * sub, sub), :]
        mrows = (member - 1.0) * (-NEG_BIG)
        qs_ref[MASK_LANE0:MASK_LANE0 + 16, :] = jnp.concatenate(
            [heads([mrows] * REP), jnp.zeros((16 - sub, wide), F32)], axis=0).astype(BF16)
        k_tile = ksel_ref[pl.ds(kt * SEL_TILE_BLOCKS, SEL_TILE_BLOCKS)].reshape(SEL_TILE_KEYS, LANES)
        s_ref[...] = _dot(k_tile, qs_ref[...])

    def absorb(kt, s_ref, carry, diagonal):
        m_run, acc = carry
        kb0 = kt * SEL_TILE_BLOCKS
        if diagonal:
            causal = dm_ref[qt % (SEL_TILE_KEYS // Q_TILE)]
        v_t = jnp.concatenate([vt_ref[kb0 + x, 0:VT_ROWS, :] for x in range(SEL_TILE_BLOCKS)], axis=1)
        ms_, accs = [], []
        for r in range(REP):
            sr = s_ref[:, r * Q_TILE:(r + 1) * Q_TILE]
            if diagonal:
                sr = sr + causal
            m_old = head(m_run, r)
            m_new = jnp.maximum(m_old, jnp.max(sr, axis=0, keepdims=True))
            p = jnp.exp2(sr - m_new).astype(BF16)
            accs.append(head(acc, r) * jnp.exp2(m_old - m_new) + _dot(v_t, p))
            ms_.append(m_new)
        return heads(ms_), heads(accs)

    carry = (jnp.full((1, wide), NEG_BIG, F32), jnp.zeros((VT_ROWS, wide), F32))
    n_full = (qt * Q_TILE) // SEL_TILE_KEYS

    n_sel_tiles = SEQ // SEL_TILE_KEYS
    tr = lax.broadcasted_iota(jnp.int32, (n_sel_tiles, N_SEL), 0)
    tc = lax.broadcasted_iota(jnp.int32, (n_sel_tiles, N_SEL), 1)
    tile_of_block = jnp.where(lax.shift_right_logical(tc, 3) == tr, 1.0, 0.0).astype(BF16)
    per_query = _dot(tile_of_block, sel_ref[...].astype(BF16))
    per_tile = _dot(per_query.astype(BF16), jnp.ones((Q_TILE, LANES), BF16))
    bit = lax.shift_left(1, lax.broadcasted_iota(jnp.int32, (n_sel_tiles, LANES), 0)).astype(F32)
    used_bits = jnp.sum(jnp.where(per_tile > 0.5, bit, 0.0), axis=0, keepdims=True)[0, 0].astype(jnp.int32)

    kb_first = qt * (Q_TILE // K_BLOCK) - WINDOW // K_BLOCK
    kbs = [jnp.maximum(kb_first + x, 0) for x in range(WIN_BLOCKS)]
    sw = _dot(jnp.concatenate([kwin_ref[kb] for kb in kbs], axis=0), q_win)

    n_visit = jnp.int32(0)
    for kt in range(n_sel_tiles):
        tiles_ref[n_visit] = kt
        take = (kt < n_full) & ((lax.shift_right_logical(used_bits, kt) & 1) == 1)
        n_visit = n_visit + jnp.where(take, 1, 0)
    tiles_ref[n_visit] = n_full

    scores_into(tiles_ref[0], sa_ref)
    wm = jnp.concatenate(
        [wm_ref[x * K_BLOCK:(x + 1) * K_BLOCK, :] + jnp.where(kb_first + x < 0, NEG_BIG, 0.0)
         for x in range(WIN_BLOCKS)], axis=0)
    pws = []
    for r in range(REP):
        s = head(sw, r) + wm
        pws.append(jnp.exp2(s - jnp.max(s, axis=0, keepdims=True)).astype(BF16))
    vw_t = jnp.concatenate([vt_ref[kb, VT_ROWS:2 * VT_ROWS, :] for kb in kbs], axis=1)
    acc_w = _dot(vw_t, heads(pws))
    o_w = acc_w[0:HEAD_DIM] / acc_w[HEAD_DIM:HEAD_DIM + 1]

    def tile_pair(i, carry):
        scores_into(tiles_ref[2 * i + 1], sb_ref)
        carry = absorb(tiles_ref[2 * i], sa_ref, carry, False)
        scores_into(tiles_ref[2 * i + 2], sa_ref)
        return absorb(tiles_ref[2 * i + 1], sb_ref, carry, False)

    carry = lax.fori_loop(0, n_visit // 2, tile_pair, carry)
    last = 2 * (n_visit // 2)

    def odd_tail(carry):
        scores_into(n_full, sb_ref)
        return absorb(n_full, sb_ref, absorb(tiles_ref[last], sa_ref, carry, False), True)

    def even_tail(carry):
        return absorb(n_full, sa_ref, carry, True)

    _, acc_s = lax.cond(n_visit % 2 == 1, odd_tail, even_tail, carry)
    o_s = acc_s[0:HEAD_DIM] / acc_s[HEAD_DIM:HEAD_DIM + 1]

    gates = jnp.concatenate([gt_ref[x] for x in range(Q_TILE // K_BLOCK)], axis=1)
    outs = []
    for r in range(REP):
        outs.append(gates[r:r + 1, :] * head(o_c, r)
                    + gates[REP + r:REP + r + 1, :] * head(o_s, r)
                    + gates[2 * REP + r:2 * REP + r + 1, :] * head(o_w, r))
    o_ref[...] = jnp.concatenate(outs, axis=0).T.astype(BF16)


def _attn(proj, ksel, kwin, vt, kc, vct, gt, sf, wm, dm, ov_t, wq_b):
    q0 = Z_COLS // (REP * HEAD_DIM)
    kvspec = lambda shape: pl.BlockSpec((None, None) + shape, lambda b, g, qt: (b, g) + (0,) * len(shape))
    const = lambda shape: pl.BlockSpec(shape, lambda b, g, qt: (0,) * len(shape))
    return pl.pallas_call(
        _attn_kernel,
        grid=(BATCH, KV_GROUPS, N_QT),
        in_specs=[
            pl.BlockSpec((Q_TILE, REP * HEAD_DIM), lambda b, g, qt: (b * N_QT + qt, q0 + g)),
            kvspec((N_KB, K_BLOCK, LANES)),
            kvspec((N_KB, K_BLOCK, LANES)),
            kvspec((N_KB, 2 * VT_ROWS, K_BLOCK)),
            kvspec((N_CMP_PAD, LANES)),
            kvspec((HEAD_DIM, N_CMP_PAD)),
            pl.BlockSpec((None, Q_TILE // K_BLOCK, GATE_ROWS_PER_GROUP, K_BLOCK), lambda b, g, qt: (b, qt, g, 0)),
            pl.BlockSpec((None, 8, REP * Q_TILE), lambda b, g, qt: (g, 0, 0)),
            const((WIN_KEYS, Q_TILE)),
            const((SEL_TILE_KEYS // Q_TILE, SEL_TILE_KEYS, Q_TILE)),
            const((N_SEL, N_CMP_PAD)),
            const((HEAD_DIM, Q_TILE)),
        ],
        out_specs=pl.BlockSpec((Q_TILE, REP * HEAD_DIM), lambda b, g, qt: (b * N_QT + qt, g)),
        out_shape=jax.ShapeDtypeStruct((N_TOK, NSA_WIDTH), BF16),
        scratch_shapes=[pltpu.VMEM((N_SEL, Q_TILE), F32), pltpu.VMEM((LANES, REP * Q_TILE), BF16),
                        pltpu.VMEM((SEL_TILE_KEYS, REP * Q_TILE), F32),
                        pltpu.VMEM((SEL_TILE_KEYS, REP * Q_TILE), F32),
                        pltpu.SMEM((SEQ // SEL_TILE_KEYS + 1,), jnp.int32)],
        compiler_params=_cparams("arbitrary", "arbitrary", "arbitrary"),
        name="attn",
    )(proj, ksel, kwin, vt, kc, vct, gt, sf, wm, dm, ov_t, wq_b)


def _spatial_gating(z, lnw, lnb, sw_ref, sbx):
    ge = jax.nn.gelu(z)
    u = ge[:, :GMLP_WIDTH]
    v = ge[:, GMLP_WIDTH:]
    mu = jnp.mean(v, axis=-1, keepdims=True)
    var = jnp.mean(jnp.square(v - mu), axis=-1, keepdims=True)
    vn = ((v - mu) * lax.rsqrt(var + LN_EPS) * lnw + lnb).astype(BF16)
    ti = lax.broadcasted_iota(jnp.int32, (GMLP_CHUNK, GMLP_CHUNK), 0)
    si = lax.broadcasted_iota(jnp.int32, (GMLP_CHUNK, GMLP_CHUNK), 1)
    ws = [jnp.where(ti >= si, sw_ref[gg], 0.0).astype(BF16) for gg in range(GMLP_GROUPS)]
    gd = GMLP_WIDTH // GMLP_GROUPS
    outs = []
    for c in range(z.shape[0] // GMLP_CHUNK):
        rows = slice(c * GMLP_CHUNK, (c + 1) * GMLP_CHUNK)
        mix = jnp.concatenate(
            [_dot(ws[gg], vn[rows, gg * gd:(gg + 1) * gd]) for gg in range(GMLP_GROUPS)], axis=1)
        outs.append((u[rows, :] * (mix + sbx)).astype(BF16))
    return jnp.concatenate(outs, axis=0)


def _oproj_kernel(a_ref, z_ref, x_ref, wa_ref, wb_ref, nw_ref, lnw_ref, lnb_ref, sw_ref, sbx_ref,
                  x1_ref, xn_ref):
    attn_part = _dot(a_ref[...], wa_ref[...])
    b = _spatial_gating(z_ref[...], lnw_ref[...], lnb_ref[...], sw_ref, sbx_ref[...])
    y = x_ref[...] + (attn_part + _dot(b, wb_ref[...]))
    x1_ref[...] = y
    ms = jnp.mean(y * y, axis=-1, keepdims=True)
    xn_ref[...] = (y * lax.rsqrt(ms + NORM_EPS) * nw_ref[...]).astype(BF16)


def _oproj(a, proj, x2, w_out, nw, lnw, lnb, sw, sbx):
    return pl.pallas_call(
        _oproj_kernel,
        grid=(N_TOK // OPROJ_TM,),
        in_specs=[
            pl.BlockSpec((OPROJ_TM, NSA_WIDTH), lambda i: (i, 0)),
            pl.BlockSpec((OPROJ_TM, Z_COLS), lambda i: (i, 0)),
            pl.BlockSpec((OPROJ_TM, D_MODEL), lambda i: (i, 0)),
            pl.BlockSpec((NSA_WIDTH, D_MODEL), lambda i: (0, 0)),
            pl.BlockSpec((GMLP_WIDTH, D_MODEL), lambda i: (1, 0)),
            pl.BlockSpec((1, D_MODEL), lambda i: (0, 0)),
            pl.BlockSpec((1, GMLP_WIDTH), lambda i: (0, 0)),
            pl.BlockSpec((1, GMLP_WIDTH), lambda i: (0, 0)),
            pl.BlockSpec((GMLP_GROUPS, GMLP_CHUNK, GMLP_CHUNK), lambda i: (0, 0, 0)),
            pl.BlockSpec((GMLP_CHUNK, GMLP_WIDTH), lambda i: (0, 0)),
        ],
        out_specs=[
            pl.BlockSpec((OPROJ_TM, D_MODEL), lambda i: (i, 0)),
            pl.BlockSpec((OPROJ_TM, D_MODEL), lambda i: (i, 0)),
        ],
        out_shape=[
            jax.ShapeDtypeStruct((N_TOK, D_MODEL), F32),
            jax.ShapeDtypeStruct((N_TOK, D_MODEL), BF16),
        ],
        compiler_params=_cparams("arbitrary"),
        name="oproj",
    )(a, proj, x2, w_out, w_out, nw, lnw, lnb, sw, sbx)


def _ffn_kernel(xn_ref, x1_ref, wg_ref, wu_ref, cwg_ref, cwu_ref, cbg_ref, cbu_ref, wd_ref,
                cwg_p_ref, cwu_p_ref, cbg_p_ref, cbu_p_ref, wd_p_ref, o_ref, h0_ref, h1_ref, carry_ref):
    i = pl.program_id(0)
    q = pl.program_id(1)
    nq = pl.num_programs(1)
    half = FFN_TF // 2

    @pl.when((i == 0) & (q == 0))
    def _():
        carry_ref[...] = jnp.zeros_like(carry_ref)
        h1_ref[...] = jnp.zeros_like(h1_ref)

    @pl.when(q == 0)
    def _():
        o_ref[...] = x1_ref[...]

    seq_start = i % (SEQ // FFN_TM) == 0
    xn = xn_ref[...]

    def produce(c, h_ref):
        h_ref[0, 8:, :] = _dot(xn, wg_ref[:, c * half:(c + 1) * half])
        h_ref[1, 8:, :] = _dot(xn, wu_ref[:, c * half:(c + 1) * half])

    def conv(h_ref, part, tile, cw, cb):
        h_ref[part, 0:8, :] = jnp.where(seq_start, 0.0, carry_ref[tile, part])
        carry_ref[tile, part] = h_ref[part, FFN_TM:FFN_TM + 8, :]
        taps = [h_ref[part, 8 - k:8 - k + FFN_TM, :] for k in (2, 1, 0)]
        return cb + ((cw[0:1, :] * taps[0] + cw[1:2, :] * taps[1]) + cw[2:3, :] * taps[2])

    def gate(h_ref, tile, c, cwg, cwu, cbg, cbu, scale=None):
        cols = slice(c * half, (c + 1) * half)
        cg = conv(h_ref, 0, tile, cwg[:, cols], cbg[:, cols])
        cu = conv(h_ref, 1, tile, cwu[:, cols], cbu[:, cols])
        act = jax.nn.silu(cg) * cu
        if scale is not None:
            act = act * scale
        return act.astype(BF16)

    act_prev = gate(h1_ref, jnp.where(q == 0, 2 * nq - 1, 2 * q - 1), 1, cwg_p_ref, cwu_p_ref, cbg_p_ref,
                    cbu_p_ref, scale=jnp.where(q == 0, 0.0, 1.0))
    produce(0, h0_ref)
    down_prev = _dot(act_prev, wd_p_ref[half:, :])
    act_cur = gate(h0_ref, 2 * q, 0, cwg_ref, cwu_ref, cbg_ref, cbu_ref)
    produce(1, h1_ref)
    o_ref[...] += down_prev + _dot(act_cur, wd_ref[0:half, :])

    @pl.when(q == nq - 1)
    def _():
        act_last = gate(h1_ref, 2 * q + 1, 1, cwg_ref, cwu_ref, cbg_ref, cbu_ref)
        o_ref[...] += _dot(act_last, wd_ref[half:, :])


def _ffn(xn, x1, w_up, conv_w, conv_b, w_down):
    nj = D_FF // FFN_TF
    prev = lambda q: jnp.maximum(q - 1, 0)
    return pl.pallas_call(
        _ffn_kernel,
        grid=(N_TOK // FFN_TM, nj),
        in_specs=[
            pl.BlockSpec((FFN_TM, D_MODEL), lambda i, q: (i, 0)),
            pl.BlockSpec((FFN_TM, D_MODEL), lambda i, q: (i, 0)),
            pl.BlockSpec((D_MODEL, FFN_TF), lambda i, q: (0, q)),
            pl.BlockSpec((D_MODEL, FFN_TF), lambda i, q: (0, nj + q)),
            pl.BlockSpec((CONV_WIDTH, FFN_TF), lambda i, q: (0, q)),
            pl.BlockSpec((CONV_WIDTH, FFN_TF), lambda i, q: (0, nj + q)),
            pl.BlockSpec((1, FFN_TF), lambda i, q: (0, q)),
            pl.BlockSpec((1, FFN_TF), lambda i, q: (0, nj + q)),
            pl.BlockSpec((FFN_TF, D_MODEL), lambda i, q: (q, 0)),
            pl.BlockSpec((CONV_WIDTH, FFN_TF), lambda i, q: (0, prev(q))),
            pl.BlockSpec((CONV_WIDTH, FFN_TF), lambda i, q: (0, nj + prev(q))),
            pl.BlockSpec((1, FFN_TF), lambda i, q: (0, prev(q))),
            pl.BlockSpec((1, FFN_TF), lambda i, q: (0, nj + prev(q))),
            pl.BlockSpec((FFN_TF, D_MODEL), lambda i, q: (prev(q), 0)),
        ],
        out_specs=pl.BlockSpec((FFN_TM, D_MODEL), lambda i, q: (i, 0)),
        out_shape=jax.ShapeDtypeStruct((N_TOK, D_MODEL), F32),
        scratch_shapes=[pltpu.VMEM((2, FFN_TM + 8, FFN_TF // 2), F32), pltpu.VMEM((2, FFN_TM + 8, FFN_TF // 2), F32),
                        pltpu.VMEM((2 * nj, 2, 8, FFN_TF // 2), F32)],
        compiler_params=_cparams("arbitrary", "arbitrary"),
        name="ffn",
    )(xn, x1, w_up, w_up, conv_w, conv_w, conv_b, conv_b, w_down, conv_w, conv_w, conv_b, conv_b, w_down)


def _overlap_t():
    start = np.arange(N_CMP_PAD)[None, :] * CMP_STRIDE
    s0 = np.arange(N_SEL)[:, None] * SEL_BLOCK
    ov = (start < s0 + SEL_BLOCK) & (start + CMP_BLOCK > s0) & (np.arange(N_CMP_PAD)[None, :] < N_CMP_PAD - 1)
    return ov.astype(np.float32)


def _bf16_round(a):
    return np.asarray(a, np.float32).astype(BF16).astype(np.float32)


def _slope_features():
    sl = (np.power(2.0, -8.0 * np.arange(1, NSA_HEADS + 1) / NSA_HEADS).astype(np.float32)
          * np.float32(LOG2E)).astype(np.float32)
    s1 = _bf16_round(sl)
    s2 = _bf16_round(sl - s1)
    s3 = _bf16_round(sl - s1 - s2)
    rows = np.stack([s1, s2, s3, s1, s2, s3, -sl, np.zeros_like(sl)], axis=0)
    rows = rows.reshape(8, KV_GROUPS, REP).transpose(1, 0, 2)
    return np.repeat(rows, Q_TILE, axis=2).astype(np.float32)


def _window_mask():
    kl = np.arange(WIN_KEYS)[:, None]
    ql = np.arange(Q_TILE)[None, :]
    dist = ql + WINDOW - kl
    return np.where((dist >= 0) & (dist < WINDOW), 0.0, NEG_BIG).astype(np.float32)


def _diag_masks():
    kl = np.arange(SEL_TILE_KEYS)[None, :, None]
    ql = np.arange(Q_TILE)[None, None, :]
    off = np.arange(SEL_TILE_KEYS // Q_TILE)[:, None, None] * Q_TILE
    return np.where(kl > ql + off, NEG_BIG, 0.0).astype(np.float32)


def _layout_w_in(w_in):
    q_end = NSA_WIDTH
    kv_end = q_end + 6 * KV_COLS
    g_end = kv_end + NSA_HEADS * N_BRANCH
    w_in = w_in.astype(BF16)
    w_q = w_in[:, :q_end]
    w_kv = w_in[:, q_end:kv_end].reshape(D_MODEL, 6, KV_GROUPS, HEAD_DIM)
    w_kv = jnp.stack([w_kv[:, s] for s in (2, 4, 0, 1, 3, 5)], axis=2).reshape(D_MODEL, 6 * KV_COLS)
    w_z = w_in[:, g_end:]
    w_main = jnp.concatenate([w_z, w_q, w_kv], axis=1)
    w_g = w_in[:, kv_end:g_end].reshape(D_MODEL, KV_GROUPS, REP, N_BRANCH).transpose(0, 1, 3, 2)
    w_g = w_g.reshape(D_MODEL, KV_GROUPS, N_BRANCH * REP)
    w_g = jnp.pad(w_g, ((0, 0), (0, 0), (0, GATE_ROWS_PER_GROUP - N_BRANCH * REP)))
    w_g = jnp.pad(w_g.reshape(D_MODEL, KV_GROUPS * GATE_ROWS_PER_GROUP),
                  ((0, 0), (0, GATE_PAD - KV_GROUPS * GATE_ROWS_PER_GROUP)))
    return w_main, w_g


def _layout_compress(cmp_pos, cmp_w1, cmp_w2):
    pos2 = jnp.concatenate([cmp_pos[0], cmp_pos[1]], axis=1)
    w1 = cmp_w1.reshape(2, CMP_BLOCK, HEAD_DIM, CMP_HIDDEN)
    zero = jnp.zeros_like(w1[0])
    w1k = jnp.concatenate([w1[0], zero], axis=1)
    w1v = jnp.concatenate([zero, w1[1]], axis=1)
    w1p = jnp.concatenate([w1k, w1v], axis=2).astype(BF16)
    w1p = w1p.reshape(CMP_BLOCK // 2, 2 * LANES, 2 * CMP_HIDDEN)
    w2p = jnp.pad(cmp_w2, ((0, 0), (0, 0), (0, LANES - HEAD_DIM))).astype(BF16)
    return pos2, w1p, w2p


def kernel(x, attn_norm_w, w_in, q_norm_w, k_norm_w, cmp_pos, cmp_w1, cmp_w2, gmlp_ln_w, gmlp_ln_b,
           spatial_w, spatial_b, w_out, ffn_norm_w, w_up, conv_w, conv_b, w_down):
    x2 = x.reshape(N_TOK, D_MODEL)
    w_main, w_gate = _layout_w_in(w_in)
    proj, gl = _proj(x2, attn_norm_w.reshape(1, D_MODEL), w_main, w_gate)

    knw2 = jnp.concatenate([k_norm_w[1], k_norm_w[2]]).reshape(1, LANES)
    kcw = jnp.concatenate([k_norm_w[0], jnp.zeros((LANES - HEAD_DIM,), F32)]).reshape(1, LANES)
    pos2, w1p, w2p = _layout_compress(cmp_pos, cmp_w1, cmp_w2)
    ksel, kwin, vt, kc, vct, gt = _kprep(proj, gl, knw2, kcw, pos2, w1p, w2p)

    wq_b = jnp.broadcast_to(q_norm_w.reshape(HEAD_DIM, 1), (HEAD_DIM, Q_TILE))
    a = _attn(proj, ksel, kwin, vt, kc, vct, gt, jnp.asarray(_slope_features()),
              jnp.asarray(_window_mask()), jnp.asarray(_diag_masks()),
              jnp.asarray(_overlap_t(), dtype=BF16), wq_b)

    sbx = jnp.repeat(spatial_b.T, GMLP_WIDTH // GMLP_GROUPS, axis=1)
    x1, xn = _oproj(a, proj, x2, w_out.astype(BF16), ffn_norm_w.reshape(1, D_MODEL),
                    gmlp_ln_w.reshape(1, GMLP_WIDTH), gmlp_ln_b.reshape(1, GMLP_WIDTH), spatial_w, sbx)

    out = _ffn(xn, x1, w_up.astype(BF16), conv_w, conv_b.reshape(1, 2 * D_FF), w_down.astype(BF16))
    return out.reshape(BATCH, SEQ, D_MODEL)
```

```python
import numpy as np
import jax
import jax.numpy as jnp
from jax import lax
from jax.experimental import pallas as pl
from jax.experimental.pallas import tpu as pltpu

F32 = jnp.float32
BF16 = jnp.bfloat16

D_MODEL = 2048
BATCH = 4
SEQ = 4096
N_TOK = BATCH * SEQ
NSA_HEADS = 16
KV_GROUPS = 4
REP = NSA_HEADS // KV_GROUPS
HEAD_DIM = 64
NSA_WIDTH = NSA_HEADS * HEAD_DIM
GMLP_WIDTH = D_MODEL - NSA_WIDTH
GMLP_GROUPS = 8
GMLP_CHUNK = 128
CMP_BLOCK = 32
CMP_STRIDE = 16
CMP_HIDDEN = 256
N_CMP_PAD = SEQ // CMP_STRIDE
SEL_BLOCK = 64
N_SEL = SEQ // SEL_BLOCK
SEL_TOPK = 16
WINDOW = 512
K_BLOCK = 128
N_KB = SEQ // K_BLOCK
Q_TILE = 256
N_QT = SEQ // Q_TILE
N_BRANCH = 3
KV_COLS = KV_GROUPS * HEAD_DIM
D_FF = 5632
CONV_WIDTH = 3
NORM_EPS = 1e-6
LN_EPS = 1e-5
NEG_BIG = -1e30
SEL_BIG = 1e9

LANES = 128
VMEM_LIMIT_BYTES = 56 * 1024 * 1024

Z_COLS = 2 * GMLP_WIDTH
PROJ_COLS = Z_COLS + NSA_WIDTH + 6 * KV_COLS
GATE_PAD = LANES
GATE_ROWS_PER_GROUP = 16

PROJ_TM = 1024
PROJ_TN = 1536
OPROJ_TM = 512
FFN_TM = 512
FFN_TF = 512
SEL_TILE_BLOCKS = 4
SEL_TILE_KEYS = SEL_TILE_BLOCKS * K_BLOCK
MASK_LANE0 = 80
SUM_ROWS = 16
VT_ROWS = HEAD_DIM + SUM_ROWS
LOG2E = 1.4426950408889634
WIN_BLOCKS = (WINDOW + Q_TILE) // K_BLOCK
WIN_KEYS = WIN_BLOCKS * K_BLOCK


def _cparams(*sem):
    return pltpu.CompilerParams(dimension_semantics=sem, vmem_limit_bytes=VMEM_LIMIT_BYTES)


def _dot(a, b):
    return jnp.dot(a, b, preferred_element_type=F32)


def _proj_kernel(x_ref, nw_ref, w_ref, wg_ref, o_ref, og_ref, h_ref):
    @pl.when(pl.program_id(1) == 0)
    def _():
        x = x_ref[...]
        ms = jnp.mean(x * x, axis=-1, keepdims=True)
        h = (x * lax.rsqrt(ms + NORM_EPS) * nw_ref[...]).astype(BF16)
        h_ref[...] = h
        og_ref[...] = _dot(h, wg_ref[...])

    o_ref[...] = _dot(h_ref[...], w_ref[...])


def _proj(x2, nw, w_main, w_gate):
    return pl.pallas_call(
        _proj_kernel,
        grid=(N_TOK // PROJ_TM, PROJ_COLS // PROJ_TN),
        in_specs=[
            pl.BlockSpec((PROJ_TM, D_MODEL), lambda i, j: (i, 0)),
            pl.BlockSpec((1, D_MODEL), lambda i, j: (0, 0)),
            pl.BlockSpec((D_MODEL, PROJ_TN), lambda i, j: (0, j)),
            pl.BlockSpec((D_MODEL, GATE_PAD), lambda i, j: (0, 0)),
        ],
        out_specs=[
            pl.BlockSpec((PROJ_TM, PROJ_TN), lambda i, j: (i, j)),
            pl.BlockSpec((PROJ_TM, GATE_PAD), lambda i, j: (i, 0)),
        ],
        out_shape=[
            jax.ShapeDtypeStruct((N_TOK, PROJ_COLS), F32),
            jax.ShapeDtypeStruct((N_TOK, GATE_PAD), F32),
        ],
        scratch_shapes=[pltpu.VMEM((PROJ_TM, D_MODEL), BF16)],
        compiler_params=_cparams("arbitrary", "arbitrary"),
        name="proj",
    )(x2, nw, w_main, w_gate)


def _pos_features(lane, first, f1, f2):
    return jnp.where(lane < first + 3, f1, jnp.where(lane < first + 6, f2,
                                                     jnp.where(lane == first + 6, 1.0, 0.0)))


def _kprep_kernel(s0_ref, s1_ref, s2_ref, gl_ref, knw_ref, kcw_ref, pos_ref, w1_ref, w2_ref,
                  ksel_ref, kwin_ref, vt_ref, kc_ref, vct_ref, gt_ref, ha_ref, hb_ref):
    lane = lax.broadcasted_iota(jnp.int32, (K_BLOCK, LANES), 1)
    row = lax.broadcasted_iota(jnp.int32, (K_BLOCK, LANES), 0)
    lo = lane < HEAD_DIM
    inv_d = 1.0 / HEAD_DIM
    ones_rows = jnp.where(lax.broadcasted_iota(jnp.int32, (SUM_ROWS, K_BLOCK), 0) == 0, 1.0, 0.0)

    def chunk(c, carry):
        r0 = pl.multiple_of(c * K_BLOCK, K_BLOCK)
        x = s0_ref[pl.ds(r0, K_BLOCK), :]
        sq = x * x
        s_lo = jnp.sum(jnp.where(lo, sq, 0.0), axis=1, keepdims=True)
        s_hi = jnp.sum(jnp.where(lo, 0.0, sq), axis=1, keepdims=True)
        ms = jnp.where(lo, s_lo, s_hi) * inv_d
        y = x * lax.rsqrt(ms + NORM_EPS) * knw_ref[...]
        pos = r0 + row
        blk = lax.shift_right_logical(pos, 6)
        f1 = (blk * SEL_BLOCK).astype(F32)
        f2 = (pos & (SEL_BLOCK - 1)).astype(F32)
        onehot = jnp.where(lane - MASK_LANE0 == (blk & (SEL_TILE_BLOCKS * 2 - 1)), 1.0, 0.0)
        feat_sel = jnp.where(lane < MASK_LANE0, _pos_features(lane, HEAD_DIM, f1, f2), onehot)
        ksel_ref[c] = jnp.where(lo, y, feat_sel).astype(BF16)
        kwin_ref[c] = jnp.where(lo, _pos_features(lane, 0, f1, f2), y).astype(BF16)
        v_t = s2_ref[pl.ds(r0, K_BLOCK), :].T
        vt_ref[c] = jnp.concatenate(
            [v_t[0:HEAD_DIM], ones_rows, v_t[HEAD_DIM:], ones_rows], axis=0).astype(BF16)
        return carry

    lax.fori_loop(0, N_KB, chunk, 0, unroll=4)

    @pl.when(pl.program_id(1) == 0)
    def _():
        def gchunk(c, carry):
            r0 = pl.multiple_of(c * K_BLOCK, K_BLOCK)
            gt_ref[c] = jax.nn.sigmoid(gl_ref[pl.ds(r0, K_BLOCK), :]).T
            return carry

        lax.fori_loop(0, N_KB, gchunk, 0, unroll=4)

    ha_ref[...] = jnp.zeros_like(ha_ref)
    hb_ref[...] = jnp.zeros_like(hb_ref)
    for tp in range(CMP_STRIDE // 2):
        rows = [s1_ref[pl.ds(2 * tp + e, N_CMP_PAD, stride=CMP_STRIDE), :] for e in range(2)]
        xa = jnp.concatenate([(rows[e] + pos_ref[2 * tp + e:2 * tp + e + 1, :]).astype(BF16)
                              for e in range(2)], axis=1)
        xb = jnp.concatenate([(rows[e] + pos_ref[CMP_STRIDE + 2 * tp + e:CMP_STRIDE + 2 * tp + e + 1, :])
                              .astype(BF16) for e in range(2)], axis=1)
        ha_ref[...] += _dot(xa, w1_ref[tp])
        hb_ref[0:N_CMP_PAD, :] += _dot(xb, w1_ref[CMP_STRIDE // 2 + tp])
    hid = jax.nn.gelu(ha_ref[...] + hb_ref[1:N_CMP_PAD + 1, :]).astype(BF16)
    kc = _dot(hid[:, :CMP_HIDDEN], w2_ref[0])
    vc = _dot(hid[:, CMP_HIDDEN:], w2_ref[1])
    ms = jnp.sum(kc * kc, axis=1, keepdims=True) * inv_d
    kcn = kc * lax.rsqrt(ms + NORM_EPS) * kcw_ref[...]
    ci = lax.broadcasted_iota(jnp.int32, (N_CMP_PAD, LANES), 0)
    cl = lax.broadcasted_iota(jnp.int32, (N_CMP_PAD, LANES), 1)
    c1 = (lax.shift_right_logical(ci, 2) * SEL_BLOCK).astype(F32)
    c2 = ((ci & 3) * CMP_STRIDE).astype(F32) + (CMP_BLOCK - 1) * 0.5
    kc_ref[...] = jnp.where(cl < HEAD_DIM, kcn, _pos_features(cl, HEAD_DIM, c1, c2)).astype(BF16)
    vct_ref[...] = vc.T[0:HEAD_DIM, :].astype(BF16)


def _kprep(proj, gl, knw2, kcw, pos2, w1p, w2p):
    kv0 = (Z_COLS + NSA_WIDTH) // LANES
    slab = lambda s: pl.BlockSpec((SEQ, LANES), lambda b, g, s=s: (b, kv0 + 3 * g + s))
    full = lambda shape: pl.BlockSpec(shape, lambda b, g: (0,) * len(shape))
    per_bg = lambda shape: pl.BlockSpec((None, None) + shape, lambda b, g: (b, g) + (0,) * len(shape))
    bg_shape = lambda shape, dt: jax.ShapeDtypeStruct((BATCH, KV_GROUPS) + shape, dt)
    return pl.pallas_call(
        _kprep_kernel,
        grid=(BATCH, KV_GROUPS),
        in_specs=[
            slab(0), slab(1), slab(2),
            pl.BlockSpec((SEQ, GATE_PAD), lambda b, g: (b, 0)),
            full((1, LANES)), full((1, LANES)), full((CMP_BLOCK, LANES)),
            full((CMP_BLOCK // 2, 2 * LANES, 2 * CMP_HIDDEN)), full((2, CMP_HIDDEN, LANES)),
        ],
        out_specs=[
            per_bg((N_KB, K_BLOCK, LANES)),
            per_bg((N_KB, K_BLOCK, LANES)),
            per_bg((N_KB, 2 * VT_ROWS, K_BLOCK)),
            per_bg((N_CMP_PAD, LANES)),
            per_bg((HEAD_DIM, N_CMP_PAD)),
            pl.BlockSpec((None, N_KB, GATE_PAD, K_BLOCK), lambda b, g: (b, 0, 0, 0)),
        ],
        out_shape=[
            bg_shape((N_KB, K_BLOCK, LANES), BF16),
            bg_shape((N_KB, K_BLOCK, LANES), BF16),
            bg_shape((N_KB, 2 * VT_ROWS, K_BLOCK), BF16),
            bg_shape((N_CMP_PAD, LANES), BF16),
            bg_shape((HEAD_DIM, N_CMP_PAD), BF16),
            jax.ShapeDtypeStruct((BATCH, N_KB, GATE_PAD, K_BLOCK), F32),
        ],
        scratch_shapes=[
            pltpu.VMEM((N_CMP_PAD, 2 * CMP_HIDDEN), F32),
            pltpu.VMEM((N_CMP_PAD + 8, 2 * CMP_HIDDEN), F32),
        ],
        compiler_params=_cparams("arbitrary", "arbitrary"),
        name="kprep",
    )(proj, proj, proj, gl, knw2, kcw, pos2, w1p, w2p)


def _attn_kernel(q_ref, ksel_ref, kwin_ref, vt_ref, kc_ref, vct_ref, gt_ref, sf_ref, wm_ref, dm_ref,
                 ov_ref, wq_ref, o_ref, sel_ref, qs_ref, sa_ref, sb_ref, tiles_ref):
    qt = pl.program_id(2)
    t_i = qt * Q_TILE + lax.broadcasted_iota(jnp.int32, (1, Q_TILE), 1)
    head = lambda a, r: a[:, r * Q_TILE:(r + 1) * Q_TILE]
    heads = lambda parts: jnp.concatenate(parts, axis=1)

    q_t = q_ref[...].T
    parts = []
    for r in range(REP):
        xq = q_t[r * HEAD_DIM:(r + 1) * HEAD_DIM, :]
        ms = jnp.sum(xq * xq, axis=0, keepdims=True) * (1.0 / HEAD_DIM)
        parts.append(xq * lax.rsqrt(ms + NORM_EPS) * wq_ref[...] * (HEAD_DIM ** -0.5) * LOG2E)
    qn = heads(parts)
    sf = sf_ref[...]
    sf = jnp.where(lax.broadcasted_iota(jnp.int32, sf.shape, 0) == 6, sf * (qt * Q_TILE).astype(F32), sf)
    wide = REP * Q_TILE
    pad = jnp.zeros((HEAD_DIM - 8, wide), F32)
    qs_ref[...] = jnp.concatenate([qn, sf, pad], axis=0).astype(BF16)
    q_win = jnp.concatenate([sf, pad, qn], axis=0).astype(BF16)

    sc = _dot(kc_ref[...], qs_ref[...])
    ci = lax.broadcasted_iota(jnp.int32, (N_CMP_PAD, Q_TILE), 0)
    cm = jnp.where(ci * CMP_STRIDE + (CMP_BLOCK - 1) <= t_i, 0.0, NEG_BIG)
    any_valid = t_i >= CMP_BLOCK - 1
    pcs = []
    for r in range(REP):
        s = head(sc, r) + cm
        e = jnp.exp2(s - jnp.max(s, axis=0, keepdims=True))
        d = jnp.sum(e, axis=0, keepdims=True)
        pcs.append((e * jnp.where(any_valid, 1.0 / d, 0.0)).astype(BF16))
    oc_imp = _dot(jnp.concatenate([vct_ref[...], ov_ref[...]], axis=0), heads(pcs))
    o_c = oc_imp[0:HEAD_DIM]
    imp4 = oc_imp[HEAD_DIM:]
    imp = (head(imp4, 0) + head(imp4, 1)) + (head(imp4, 2) + head(imp4, 3))

    jj = lax.broadcasted_iota(jnp.int32, (N_SEL, Q_TILE), 0)
    cur = lax.shift_right_logical(t_i, 6)
    valid = jj <= cur
    sel_ref[...] = jnp.where(valid, 1.0, 0.0)
    forced = (jj == 0) | (jj == cur) | (jj == cur - 1)
    score = jnp.where(valid, jnp.where(forced, SEL_BIG, imp), -SEL_BIG)

    def rank_rows(n8):
        def branch():
            rows8 = [score[8 * v:8 * v + 8, :] for v in range(n8)]
            j8 = lax.broadcasted_iota(jnp.int32, (8, Q_TILE), 0)
            ranks = [jnp.zeros((8, Q_TILE), F32) for _ in rows8]
            for jp in range(8 * n8):
                row = score[jp:jp + 1, :]
                for v, blk in enumerate(rows8):
                    if 8 * v > jp:
                        ahead = jnp.where(row >= blk, 1.0, 0.0)
                    elif 8 * v + 7 <= jp:
                        ahead = jnp.where(row > blk, 1.0, 0.0)
                    else:
                        ahead = jnp.where(j8 + 8 * v > jp, jnp.where(row >= blk, 1.0, 0.0),
                                          jnp.where(row > blk, 1.0, 0.0))
                    ranks[v] = ranks[v] + ahead
            rank = jnp.concatenate(ranks, axis=0)
            sel_ref[0:8 * n8, :] = jnp.where(valid[0:8 * n8] & (rank < SEL_TOPK), 1.0, 0.0)
        return branch

    n8_needed = (qt * Q_TILE + Q_TILE - 1) // (8 * SEL_BLOCK) + 1
    lax.switch(n8_needed - 1, [(lambda: None) if n8 * 8 <= SEL_TOPK else rank_rows(n8)
                               for n8 in range(1, N_SEL // 8 + 1)])

    sub = SEL_TILE_KEYS // SEL_BLOCK

    def scores_into(kt, s_ref):
        member = sel_ref[pl.ds(kt * sub, sub), :]
        mrows = (member - 1.0) * (-NEG_BIG)
        qs_ref[MASK_LANE0:MASK_LANE0 + 16, :] = jnp.concatenate(
            [heads([mrows] * REP), jnp.zeros((16 - sub, wide), F32)], axis=0).astype(BF16)
        k_tile = ksel_ref[pl.ds(kt * SEL_TILE_BLOCKS, SEL_TILE_BLOCKS)].reshape(SEL_TILE_KEYS, LANES)
        s_ref[...] = _dot(k_tile, qs_ref[...])

    def absorb(kt, s_ref, carry, diagonal):
        m_run, acc = carry
        kb0 = kt * SEL_TILE_BLOCKS
        if diagonal:
            causal = dm_ref[qt % (SEL_TILE_KEYS // Q_TILE)]
        v_t = jnp.concatenate([vt_ref[kb0 + x, 0:VT_ROWS, :] for x in range(SEL_TILE_BLOCKS)], axis=1)
        ms_, accs = [], []
        for r in range(REP):
            sr = s_ref[:, r * Q_TILE:(r + 1) * Q_TILE]
            if diagonal:
                sr = sr + causal
            m_old = head(m_run, r)
            m_new = jnp.maximum(m_old, jnp.max(sr, axis=0, keepdims=True))
            p = jnp.exp2(sr - m_new).astype(BF16)
            accs.append(head(acc, r) * jnp.exp2(m_old - m_new) + _dot(v_t, p))
            ms_.append(m_new)
        return heads(ms_), heads(accs)

    carry = (jnp.full((1, wide), NEG_BIG, F32), jnp.zeros((VT_ROWS, wide), F32))
    n_full = (qt * Q_TILE) // SEL_TILE_KEYS

    n_sel_tiles = SEQ // SEL_TILE_KEYS
    tr = lax.broadcasted_iota(jnp.int32, (n_sel_tiles, N_SEL), 0)
    tc = lax.broadcasted_iota(jnp.int32, (n_sel_tiles, N_SEL), 1)
    tile_of_block = jnp.where(lax.shift_right_logical(tc, 3) == tr, 1.0, 0.0).astype(BF16)
    per_query = _dot(tile_of_block, sel_ref[...].astype(BF16))
    per_tile = _dot(per_query.astype(BF16), jnp.ones((Q_TILE, LANES), BF16))
    bit = lax.shift_left(1, lax.broadcasted_iota(jnp.int32, (n_sel_tiles, LANES), 0)).astype(F32)
    used_bits = jnp.sum(jnp.where(per_tile > 0.5, bit, 0.0), axis=0, keepdims=True)[0, 0].astype(jnp.int32)

    kb_first = qt * (Q_TILE // K_BLOCK) - WINDOW // K_BLOCK
    kbs = [jnp.maximum(kb_first + x, 0) for x in range(WIN_BLOCKS)]
    sw = _dot(jnp.concatenate([kwin_ref[kb] for kb in kbs], axis=0), q_win)

    n_visit = jnp.int32(0)
    for kt in range(n_sel_tiles):
        tiles_ref[n_visit] = kt
        take = (kt < n_full) & ((lax.shift_right_logical(used_bits, kt) & 1) == 1)
        n_visit = n_visit + jnp.where(take, 1, 0)
    tiles_ref[n_visit] = n_full

    scores_into(tiles_ref[0], sa_ref)
    wm = jnp.concatenate(
        [wm_ref[x * K_BLOCK:(x + 1) * K_BLOCK, :] + jnp.where(kb_first + x < 0, NEG_BIG, 0.0)
         for x in range(WIN_BLOCKS)], axis=0)
    pws = []
    for r in range(REP):
        s = head(sw, r) + wm
        pws.append(jnp.exp2(s - jnp.max(s, axis=0, keepdims=True)).astype(BF16))
    vw_t = jnp.concatenate([vt_ref[kb, VT_ROWS:2 * VT_ROWS, :] for kb in kbs], axis=1)
    acc_w = _dot(vw_t, heads(pws))
    o_w = acc_w[0:HEAD_DIM] / acc_w[HEAD_DIM:HEAD_DIM + 1]

    def tile_pair(i, carry):
        scores_into(tiles_ref[2 * i + 1], sb_ref)
        carry = absorb(tiles_ref[2 * i], sa_ref, carry, False)
        scores_into(tiles_ref[2 * i + 2], sa_ref)
        return absorb(tiles_ref[2 * i + 1], sb_ref, carry, False)

    carry = lax.fori_loop(0, n_visit // 2, tile_pair, carry)
    last = 2 * (n_visit // 2)

    def odd_tail(carry):
        scores_into(n_full, sb_ref)
        return absorb(n_full, sb_ref, absorb(tiles_ref[last], sa_ref, carry, False), True)

    def even_tail(carry):
        return absorb(n_full, sa_ref, carry, True)

    _, acc_s = lax.cond(n_visit % 2 == 1, odd_tail, even_tail, carry)
    o_s = acc_s[0:HEAD_DIM] / acc_s[HEAD_DIM:HEAD_DIM + 1]

    gates = jnp.concatenate([gt_ref[x] for x in range(Q_TILE // K_BLOCK)], axis=1)
    outs = []
    for r in range(REP):
        outs.append(gates[r:r + 1, :] * head(o_c, r)
                    + gates[REP + r:REP + r + 1, :] * head(o_s, r)
                    + gates[2 * REP + r:2 * REP + r + 1, :] * head(o_w, r))
    o_ref[...] = jnp.concatenate(outs, axis=0).T.astype(BF16)


def _attn(proj, ksel, kwin, vt, kc, vct, gt, sf, wm, dm, ov_t, wq_b):
    q0 = Z_COLS // (REP * HEAD_DIM)
    kvspec = lambda shape: pl.BlockSpec((None, None) + shape, lambda b, g, qt: (b, g) + (0,) * len(shape))
    const = lambda shape: pl.BlockSpec(shape, lambda b, g, qt: (0,) * len(shape))
    return pl.pallas_call(
        _attn_kernel,
        grid=(BATCH, KV_GROUPS, N_QT),
        in_specs=[
            pl.BlockSpec((Q_TILE, REP * HEAD_DIM), lambda b, g, qt: (b * N_QT + qt, q0 + g)),
            kvspec((N_KB, K_BLOCK, LANES)),
            kvspec((N_KB, K_BLOCK, LANES)),
            kvspec((N_KB, 2 * VT_ROWS, K_BLOCK)),
            kvspec((N_CMP_PAD, LANES)),
            kvspec((HEAD_DIM, N_CMP_PAD)),
            pl.BlockSpec((None, Q_TILE // K_BLOCK, GATE_ROWS_PER_GROUP, K_BLOCK), lambda b, g, qt: (b, qt, g, 0)),
            pl.BlockSpec((None, 8, REP * Q_TILE), lambda b, g, qt: (g, 0, 0)),
            const((WIN_KEYS, Q_TILE)),
            const((SEL_TILE_KEYS // Q_TILE, SEL_TILE_KEYS, Q_TILE)),
            const((N_SEL, N_CMP_PAD)),
            const((HEAD_DIM, Q_TILE)),
        ],
        out_specs=pl.BlockSpec((Q_TILE, REP * HEAD_DIM), lambda b, g, qt: (b * N_QT + qt, g)),
        out_shape=jax.ShapeDtypeStruct((N_TOK, NSA_WIDTH), BF16),
        scratch_shapes=[pltpu.VMEM((N_SEL, Q_TILE), F32), pltpu.VMEM((LANES, REP * Q_TILE), BF16),
                        pltpu.VMEM((SEL_TILE_KEYS, REP * Q_TILE), F32),
                        pltpu.VMEM((SEL_TILE_KEYS, REP * Q_TILE), F32),
                        pltpu.SMEM((SEQ // SEL_TILE_KEYS + 1,), jnp.int32)],
        compiler_params=_cparams("arbitrary", "arbitrary", "arbitrary"),
        name="attn",
    )(proj, ksel, kwin, vt, kc, vct, gt, sf, wm, dm, ov_t, wq_b)


def _spatial_gating(z, lnw, lnb, sw_ref, sbx):
    ge = jax.nn.gelu(z)
    u = ge[:, :GMLP_WIDTH]
    v = ge[:, GMLP_WIDTH:]
    mu = jnp.mean(v, axis=-1, keepdims=True)
    var = jnp.mean(jnp.square(v - mu), axis=-1, keepdims=True)
    vn = ((v - mu) * lax.rsqrt(var + LN_EPS) * lnw + lnb).astype(BF16)
    ti = lax.broadcasted_iota(jnp.int32, (GMLP_CHUNK, GMLP_CHUNK), 0)
    si = lax.broadcasted_iota(jnp.int32, (GMLP_CHUNK, GMLP_CHUNK), 1)
    ws = [jnp.where(ti >= si, sw_ref[gg], 0.0).astype(BF16) for gg in range(GMLP_GROUPS)]
    gd = GMLP_WIDTH // GMLP_GROUPS
    outs = []
    for c in range(z.shape[0] // GMLP_CHUNK):
        rows = slice(c * GMLP_CHUNK, (c + 1) * GMLP_CHUNK)
        mix = jnp.concatenate(
            [_dot(ws[gg], vn[rows, gg * gd:(gg + 1) * gd]) for gg in range(GMLP_GROUPS)], axis=1)
        outs.append((u[rows, :] * (mix + sbx)).astype(BF16))
    return jnp.concatenate(outs, axis=0)


def _oproj_kernel(a_ref, z_ref, x_ref, wa_ref, wb_ref, nw_ref, lnw_ref, lnb_ref, sw_ref, sbx_ref,
                  x1_ref, xn_ref):
    attn_part = _dot(a_ref[...], wa_ref[...])
    b = _spatial_gating(z_ref[...], lnw_ref[...], lnb_ref[...], sw_ref, sbx_ref[...])
    y = x_ref[...] + (attn_part + _dot(b, wb_ref[...]))
    x1_ref[...] = y
    ms = jnp.mean(y * y, axis=-1, keepdims=True)
    xn_ref[...] = (y * lax.rsqrt(ms + NORM_EPS) * nw_ref[...]).astype(BF16)


def _oproj(a, proj, x2, w_out, nw, lnw, lnb, sw, sbx):
    return pl.pallas_call(
        _oproj_kernel,
        grid=(N_TOK // OPROJ_TM,),
        in_specs=[
            pl.BlockSpec((OPROJ_TM, NSA_WIDTH), lambda i: (i, 0)),
            pl.BlockSpec((OPROJ_TM, Z_COLS), lambda i: (i, 0)),
            pl.BlockSpec((OPROJ_TM, D_MODEL), lambda i: (i, 0)),
            pl.BlockSpec((NSA_WIDTH, D_MODEL), lambda i: (0, 0)),
            pl.BlockSpec((GMLP_WIDTH, D_MODEL), lambda i: (1, 0)),
            pl.BlockSpec((1, D_MODEL), lambda i: (0, 0)),
            pl.BlockSpec((1, GMLP_WIDTH), lambda i: (0, 0)),
            pl.BlockSpec((1, GMLP_WIDTH), lambda i: (0, 0)),
            pl.BlockSpec((GMLP_GROUPS, GMLP_CHUNK, GMLP_CHUNK), lambda i: (0, 0, 0)),
            pl.BlockSpec((GMLP_CHUNK, GMLP_WIDTH), lambda i: (0, 0)),
        ],
        out_specs=[
            pl.BlockSpec((OPROJ_TM, D_MODEL), lambda i: (i, 0)),
            pl.BlockSpec((OPROJ_TM, D_MODEL), lambda i: (i, 0)),
        ],
        out_shape=[
            jax.ShapeDtypeStruct((N_TOK, D_MODEL), F32),
            jax.ShapeDtypeStruct((N_TOK, D_MODEL), BF16),
        ],
        compiler_params=_cparams("arbitrary"),
        name="oproj",
    )(a, proj, x2, w_out, w_out, nw, lnw, lnb, sw, sbx)


def _ffn_kernel(xn_ref, x1_ref, wg_ref, wu_ref, cwg_ref, cwu_ref, cbg_ref, cbu_ref, wd_ref,
                cwg_p_ref, cwu_p_ref, cbg_p_ref, cbu_p_ref, o_ref, h0_ref, h1_ref, carry_ref, wd_p_ref):
    i = pl.program_id(0)
    q = pl.program_id(1)
    nq = pl.num_programs(1)
    half = FFN_TF // 2

    @pl.when((i == 0) & (q == 0))
    def _():
        carry_ref[...] = jnp.zeros_like(carry_ref)
        h1_ref[...] = jnp.zeros_like(h1_ref)
        wd_p_ref[...] = jnp.zeros_like(wd_p_ref)

    @pl.when(q == 0)
    def _():
        o_ref[...] = x1_ref[...]

    seq_start = i % (SEQ // FFN_TM) == 0
    xn = xn_ref[...]

    def produce(c, h_ref):
        h_ref[0, 8:, :] = _dot(xn, wg_ref[:, c * half:(c + 1) * half])
        h_ref[1, 8:, :] = _dot(xn, wu_ref[:, c * half:(c + 1) * half])

    def conv(h_ref, part, tile, cw, cb):
        h_ref[part, 0:8, :] = jnp.where(seq_start, 0.0, carry_ref[tile, part])
        carry_ref[tile, part] = h_ref[part, FFN_TM:FFN_TM + 8, :]
        taps = [h_ref[part, 8 - k:8 - k + FFN_TM, :] for k in (2, 1, 0)]
        return cb + ((cw[0:1, :] * taps[0] + cw[1:2, :] * taps[1]) + cw[2:3, :] * taps[2])

    def gate(h_ref, tile, c, cwg, cwu, cbg, cbu, scale=None):
        cols = slice(c * half, (c + 1) * half)
        cg = conv(h_ref, 0, tile, cwg[:, cols], cbg[:, cols])
        cu = conv(h_ref, 1, tile, cwu[:, cols], cbu[:, cols])
        act = jax.nn.silu(cg) * cu
        if scale is not None:
            act = act * scale
        return act.astype(BF16)

    act_prev = gate(h1_ref, jnp.where(q == 0, 2 * nq - 1, 2 * q - 1), 1, cwg_p_ref, cwu_p_ref, cbg_p_ref,
                    cbu_p_ref, scale=jnp.where(q == 0, 0.0, 1.0))
    produce(0, h0_ref)
    down_prev = _dot(act_prev, wd_p_ref[...])
    act_cur = gate(h0_ref, 2 * q, 0, cwg_ref, cwu_ref, cbg_ref, cbu_ref)
    produce(1, h1_ref)
    o_ref[...] += down_prev + _dot(act_cur, wd_ref[0:half, :])
    wd_p_ref[...] = wd_ref[half:, :]

    @pl.when(q == nq - 1)
    def _():
        act_last = gate(h1_ref, 2 * q + 1, 1, cwg_ref, cwu_ref, cbg_ref, cbu_ref)
        o_ref[...] += _dot(act_last, wd_ref[half:, :])


def _ffn(xn, x1, w_up, conv_w, conv_b, w_down):
    nj = D_FF // FFN_TF
    prev = lambda q: jnp.maximum(q - 1, 0)
    return pl.pallas_call(
        _ffn_kernel,
        grid=(N_TOK // FFN_TM, nj),
        in_specs=[
            pl.BlockSpec((FFN_TM, D_MODEL), lambda i, q: (i, 0)),
            pl.BlockSpec((FFN_TM, D_MODEL), lambda i, q: (i, 0)),
            pl.BlockSpec((D_MODEL, FFN_TF), lambda i, q: (0, q)),
            pl.BlockSpec((D_MODEL, FFN_TF), lambda i, q: (0, nj + q)),
            pl.BlockSpec((CONV_WIDTH, FFN_TF), lambda i, q: (0, q)),
            pl.BlockSpec((CONV_WIDTH, FFN_TF), lambda i, q: (0, nj + q)),
            pl.BlockSpec((1, FFN_TF), lambda i, q: (0, q)),
            pl.BlockSpec((1, FFN_TF), lambda i, q: (0, nj + q)),
            pl.BlockSpec((FFN_TF, D_MODEL), lambda i, q: (q, 0)),
            pl.BlockSpec((CONV_WIDTH, FFN_TF), lambda i, q: (0, prev(q))),
            pl.BlockSpec((CONV_WIDTH, FFN_TF), lambda i, q: (0, nj + prev(q))),
            pl.BlockSpec((1, FFN_TF), lambda i, q: (0, prev(q))),
            pl.BlockSpec((1, FFN_TF), lambda i, q: (0, nj + prev(q))),
        ],
        out_specs=pl.BlockSpec((FFN_TM, D_MODEL), lambda i, q: (i, 0)),
        out_shape=jax.ShapeDtypeStruct((N_TOK, D_MODEL), F32),
        scratch_shapes=[pltpu.VMEM((2, FFN_TM + 8, FFN_TF // 2), F32), pltpu.VMEM((2, FFN_TM + 8, FFN_TF // 2), F32),
                        pltpu.VMEM((2 * nj, 2, 8, FFN_TF // 2), F32),
                        pltpu.VMEM((FFN_TF // 2, D_MODEL), BF16)],
        compiler_params=_cparams("arbitrary", "arbitrary"),
        name="ffn",
    )(xn, x1, w_up, w_up, conv_w, conv_w, conv_b, conv_b, w_down, conv_w, conv_w, conv_b, conv_b)


def _overlap_t():
    start = np.arange(N_CMP_PAD)[None, :] * CMP_STRIDE
    s0 = np.arange(N_SEL)[:, None] * SEL_BLOCK
    ov = (start < s0 + SEL_BLOCK) & (start + CMP_BLOCK > s0) & (np.arange(N_CMP_PAD)[None, :] < N_CMP_PAD - 1)
    return ov.astype(np.float32)


def _bf16_round(a):
    return np.asarray(a, np.float32).astype(BF16).astype(np.float32)


def _slope_features():
    sl = (np.power(2.0, -8.0 * np.arange(1, NSA_HEADS + 1) / NSA_HEADS).astype(np.float32)
          * np.float32(LOG2E)).astype(np.float32)
    s1 = _bf16_round(sl)
    s2 = _bf16_round(sl - s1)
    s3 = _bf16_round(sl - s1 - s2)
    rows = np.stack([s1, s2, s3, s1, s2, s3, -sl, np.zeros_like(sl)], axis=0)
    rows = rows.reshape(8, KV_GROUPS, REP).transpose(1, 0, 2)
    return np.repeat(rows, Q_TILE, axis=2).astype(np.float32)


def _window_mask():
    kl = np.arange(WIN_KEYS)[:, None]
    ql = np.arange(Q_TILE)[None, :]
    dist = ql + WINDOW - kl
    return np.where((dist >= 0) & (dist < WINDOW), 0.0, NEG_BIG).astype(np.float32)


def _diag_masks():
    kl = np.arange(SEL_TILE_KEYS)[None, :, None]
    ql = np.arange(Q_TILE)[None, None, :]
    off = np.arange(SEL_TILE_KEYS // Q_TILE)[:, None, None] * Q_TILE
    return np.where(kl > ql + off, NEG_BIG, 0.0).astype(np.float32)


def _layout_w_in(w_in):
    q_end = NSA_WIDTH
    kv_end = q_end + 6 * KV_COLS
    g_end = kv_end + NSA_HEADS * N_BRANCH
    w_in = w_in.astype(BF16)
    w_q = w_in[:, :q_end]
    w_kv = w_in[:, q_end:kv_end].reshape(D_MODEL, 6, KV_GROUPS, HEAD_DIM)
    w_kv = jnp.stack([w_kv[:, s] for s in (2, 4, 0, 1, 3, 5)], axis=2).reshape(D_MODEL, 6 * KV_COLS)
    w_z = w_in[:, g_end:]
    w_main = jnp.concatenate([w_z, w_q, w_kv], axis=1)
    w_g = w_in[:, kv_end:g_end].reshape(D_MODEL, KV_GROUPS, REP, N_BRANCH).transpose(0, 1, 3, 2)
    w_g = w_g.reshape(D_MODEL, KV_GROUPS, N_BRANCH * REP)
    w_g = jnp.pad(w_g, ((0, 0), (0, 0), (0, GATE_ROWS_PER_GROUP - N_BRANCH * REP)))
    w_g = jnp.pad(w_g.reshape(D_MODEL, KV_GROUPS * GATE_ROWS_PER_GROUP),
                  ((0, 0), (0, GATE_PAD - KV_GROUPS * GATE_ROWS_PER_GROUP)))
    return w_main, w_g


def _layout_compress(cmp_pos, cmp_w1, cmp_w2):
    pos2 = jnp.concatenate([cmp_pos[0], cmp_pos[1]], axis=1)
    w1 = cmp_w1.reshape(2, CMP_BLOCK, HEAD_DIM, CMP_HIDDEN)
    zero = jnp.zeros_like(w1[0])
    w1k = jnp.concatenate([w1[0], zero], axis=1)
    w1v = jnp.concatenate([zero, w1[1]], axis=1)
    w1p = jnp.concatenate([w1k, w1v], axis=2).astype(BF16)
    w1p = w1p.reshape(CMP_BLOCK // 2, 2 * LANES, 2 * CMP_HIDDEN)
    w2p = jnp.pad(cmp_w2, ((0, 0), (0, 0), (0, LANES - HEAD_DIM))).astype(BF16)
    return pos2, w1p, w2p


def kernel(x, attn_norm_w, w_in, q_norm_w, k_norm_w, cmp_pos, cmp_w1, cmp_w2, gmlp_ln_w, gmlp_ln_b,
           spatial_w, spatial_b, w_out, ffn_norm_w, w_up, conv_w, conv_b, w_down):
    x2 = x.reshape(N_TOK, D_MODEL)
    w_main, w_gate = _layout_w_in(w_in)
    proj, gl = _proj(x2, attn_norm_w.reshape(1, D_MODEL), w_main, w_gate)

    knw2 = jnp.concatenate([k_norm_w[1], k_norm_w[2]]).reshape(1, LANES)
    kcw = jnp.concatenate([k_norm_w[0], jnp.zeros((LANES - HEAD_DIM,), F32)]).reshape(1, LANES)
    pos2, w1p, w2p = _layout_compress(cmp_pos, cmp_w1, cmp_w2)
    ksel, kwin, vt, kc, vct, gt = _kprep(proj, gl, knw2, kcw, pos2, w1p, w2p)

    wq_b = jnp.broadcast_to(q_norm_w.reshape(HEAD_DIM, 1), (HEAD_DIM, Q_TILE))
    a = _attn(proj, ksel, kwin, vt, kc, vct, gt, jnp.asarray(_slope_features()),
              jnp.asarray(_window_mask()), jnp.asarray(_diag_masks()),
              jnp.asarray(_overlap_t(), dtype=BF16), wq_b)

    sbx = jnp.repeat(spatial_b.T, GMLP_WIDTH // GMLP_GROUPS, axis=1)
    x1, xn = _oproj(a, proj, x2, w_out.astype(BF16), ffn_norm_w.reshape(1, D_MODEL),
                    gmlp_ln_w.reshape(1, GMLP_WIDTH), gmlp_ln_b.reshape(1, GMLP_WIDTH), spatial_w, sbx)

    out = _ffn(xn, x1, w_up.astype(BF16), conv_w, conv_b.reshape(1, 2 * D_FF), w_down.astype(BF16))
    return out.reshape(BATCH, SEQ, D_MODEL)
```

```python
import numpy as np
import jax
import jax.numpy as jnp
from jax import lax
from jax.experimental import pallas as pl
from jax.experimental.pallas import tpu as pltpu

F32 = jnp.float32
BF16 = jnp.bfloat16

D_MODEL = 2048
BATCH = 4
SEQ = 4096
N_TOK = BATCH * SEQ
NSA_HEADS = 16
KV_GROUPS = 4
REP = NSA_HEADS // KV_GROUPS
HEAD_DIM = 64
NSA_WIDTH = NSA_HEADS * HEAD_DIM
GMLP_WIDTH = D_MODEL - NSA_WIDTH
GMLP_GROUPS = 8
GMLP_CHUNK = 128
CMP_BLOCK = 32
CMP_STRIDE = 16
CMP_HIDDEN = 256
N_CMP_PAD = SEQ // CMP_STRIDE
SEL_BLOCK = 64
N_SEL = SEQ // SEL_BLOCK
SEL_TOPK = 16
WINDOW = 512
K_BLOCK = 128
N_KB = SEQ // K_BLOCK
Q_TILE = 256
N_QT = SEQ // Q_TILE
N_BRANCH = 3
KV_COLS = KV_GROUPS * HEAD_DIM
D_FF = 5632
CONV_WIDTH = 3
NORM_EPS = 1e-6
LN_EPS = 1e-5
NEG_BIG = -1e30
SEL_BIG = 1e9

LANES = 128
VMEM_LIMIT_BYTES = 56 * 1024 * 1024

Z_COLS = 2 * GMLP_WIDTH
PROJ_COLS = Z_COLS + NSA_WIDTH + 6 * KV_COLS
GATE_PAD = LANES
GATE_ROWS_PER_GROUP = 16

PROJ_TM = 1024
PROJ_TN = 1536
OPROJ_TM = 512
FFN_TM = 512
FFN_TF = 512
SEL_TILE_BLOCKS = 4
SEL_TILE_KEYS = SEL_TILE_BLOCKS * K_BLOCK
MASK_LANE0 = 80
SUM_ROWS = 16
VT_ROWS = HEAD_DIM + SUM_ROWS
LOG2E = 1.4426950408889634
WIN_BLOCKS = (WINDOW + Q_TILE) // K_BLOCK
WIN_KEYS = WIN_BLOCKS * K_BLOCK
CAST_UP_ROWS = 16
CAST_DOWN_ROWS = 32


def _cparams(*sem):
    return pltpu.CompilerParams(dimension_semantics=sem, vmem_limit_bytes=VMEM_LIMIT_BYTES)


def _dot(a, b):
    return jnp.dot(a, b, preferred_element_type=F32)


def _proj_kernel(x_ref, nw_ref, w_ref, wg_ref, o_ref, og_ref, h_ref):
    @pl.when(pl.program_id(1) == 0)
    def _():
        x = x_ref[...]
        ms = jnp.mean(x * x, axis=-1, keepdims=True)
        h = (x * lax.rsqrt(ms + NORM_EPS) * nw_ref[...]).astype(BF16)
        h_ref[...] = h
        og_ref[...] = _dot(h, wg_ref[...])

    o_ref[...] = _dot(h_ref[...], w_ref[...])


def _proj(x2, nw, w_main, w_gate):
    return pl.pallas_call(
        _proj_kernel,
        grid=(N_TOK // PROJ_TM, PROJ_COLS // PROJ_TN),
        in_specs=[
            pl.BlockSpec((PROJ_TM, D_MODEL), lambda i, j: (i, 0)),
            pl.BlockSpec((1, D_MODEL), lambda i, j: (0, 0)),
            pl.BlockSpec((D_MODEL, PROJ_TN), lambda i, j: (0, j)),
            pl.BlockSpec((D_MODEL, GATE_PAD), lambda i, j: (0, 0)),
        ],
        out_specs=[
            pl.BlockSpec((PROJ_TM, PROJ_TN), lambda i, j: (i, j)),
            pl.BlockSpec((PROJ_TM, GATE_PAD), lambda i, j: (i, 0)),
        ],
        out_shape=[
            jax.ShapeDtypeStruct((N_TOK, PROJ_COLS), F32),
            jax.ShapeDtypeStruct((N_TOK, GATE_PAD), F32),
        ],
        scratch_shapes=[pltpu.VMEM((PROJ_TM, D_MODEL), BF16)],
        compiler_params=_cparams("arbitrary", "arbitrary"),
        name="proj",
    )(x2, nw, w_main, w_gate)


def _pos_features(lane, first, f1, f2):
    return jnp.where(lane < first + 3, f1, jnp.where(lane < first + 6, f2,
                                                     jnp.where(lane == first + 6, 1.0, 0.0)))


def _kprep_kernel(s0_ref, s1_ref, s2_ref, gl_ref, knw_ref, kcw_ref, pos_ref, w1_ref, w2_ref,
                  ksel_ref, kwin_ref, vt_ref, kc_ref, vct_ref, gt_ref, ha_ref, hb_ref):
    lane = lax.broadcasted_iota(jnp.int32, (K_BLOCK, LANES), 1)
    row = lax.broadcasted_iota(jnp.int32, (K_BLOCK, LANES), 0)
    lo = lane < HEAD_DIM
    inv_d = 1.0 / HEAD_DIM
    ones_rows = jnp.where(lax.broadcasted_iota(jnp.int32, (SUM_ROWS, K_BLOCK), 0) == 0, 1.0, 0.0)

    def chunk(c, carry):
        r0 = pl.multiple_of(c * K_BLOCK, K_BLOCK)
        x = s0_ref[pl.ds(r0, K_BLOCK), :]
        sq = x * x
        s_lo = jnp.sum(jnp.where(lo, sq, 0.0), axis=1, keepdims=True)
        s_hi = jnp.sum(jnp.where(lo, 0.0, sq), axis=1, keepdims=True)
        ms = jnp.where(lo, s_lo, s_hi) * inv_d
        y = x * lax.rsqrt(ms + NORM_EPS) * knw_ref[...]
        pos = r0 + row
        blk = lax.shift_right_logical(pos, 6)
        f1 = (blk * SEL_BLOCK).astype(F32)
        f2 = (pos & (SEL_BLOCK - 1)).astype(F32)
        onehot = jnp.where(lane - MASK_LANE0 == (blk & (SEL_TILE_BLOCKS * 2 - 1)), 1.0, 0.0)
        feat_sel = jnp.where(lane < MASK_LANE0, _pos_features(lane, HEAD_DIM, f1, f2), onehot)
        ksel_ref[c] = jnp.where(lo, y, feat_sel).astype(BF16)
        kwin_ref[c] = jnp.where(lo, _pos_features(lane, 0, f1, f2), y).astype(BF16)
        v_t = s2_ref[pl.ds(r0, K_BLOCK), :].T
        vt_ref[c] = jnp.concatenate(
            [v_t[0:HEAD_DIM], ones_rows, v_t[HEAD_DIM:], ones_rows], axis=0).astype(BF16)
        return carry

    lax.fori_loop(0, N_KB, chunk, 0, unroll=4)

    @pl.when(pl.program_id(1) == 0)
    def _():
        def gchunk(c, carry):
            r0 = pl.multiple_of(c * K_BLOCK, K_BLOCK)
            gt_ref[c] = jax.nn.sigmoid(gl_ref[pl.ds(r0, K_BLOCK), :]).T
            return carry

        lax.fori_loop(0, N_KB, gchunk, 0, unroll=4)

    ha_ref[...] = jnp.zeros_like(ha_ref)
    hb_ref[...] = jnp.zeros_like(hb_ref)
    for tp in range(CMP_STRIDE // 2):
        rows = [s1_ref[pl.ds(2 * tp + e, N_CMP_PAD, stride=CMP_STRIDE), :] for e in range(2)]
        xa = jnp.concatenate([(rows[e] + pos_ref[2 * tp + e:2 * tp + e + 1, :]).astype(BF16)
                              for e in range(2)], axis=1)
        xb = jnp.concatenate([(rows[e] + pos_ref[CMP_STRIDE + 2 * tp + e:CMP_STRIDE + 2 * tp + e + 1, :])
                              .astype(BF16) for e in range(2)], axis=1)
        ha_ref[...] += _dot(xa, w1_ref[tp])
        hb_ref[0:N_CMP_PAD, :] += _dot(xb, w1_ref[CMP_STRIDE // 2 + tp])
    hid = jax.nn.gelu(ha_ref[...] + hb_ref[1:N_CMP_PAD + 1, :]).astype(BF16)
    kc = _dot(hid[:, :CMP_HIDDEN], w2_ref[0])
    vc = _dot(hid[:, CMP_HIDDEN:], w2_ref[1])
    ms = jnp.sum(kc * kc, axis=1, keepdims=True) * inv_d
    kcn = kc * lax.rsqrt(ms + NORM_EPS) * kcw_ref[...]
    ci = lax.broadcasted_iota(jnp.int32, (N_CMP_PAD, LANES), 0)
    cl = lax.broadcasted_iota(jnp.int32, (N_CMP_PAD, LANES), 1)
    c1 = (lax.shift_right_logical(ci, 2) * SEL_BLOCK).astype(F32)
    c2 = ((ci & 3) * CMP_STRIDE).astype(F32) + (CMP_BLOCK - 1) * 0.5
    kc_ref[...] = jnp.where(cl < HEAD_DIM, kcn, _pos_features(cl, HEAD_DIM, c1, c2)).astype(BF16)
    vct_ref[...] = vc.T[0:HEAD_DIM, :].astype(BF16)


def _kprep(proj, gl, knw2, kcw, pos2, w1p, w2p):
    kv0 = (Z_COLS + NSA_WIDTH) // LANES
    slab = lambda s: pl.BlockSpec((SEQ, LANES), lambda b, g, s=s: (b, kv0 + 3 * g + s))
    full = lambda shape: pl.BlockSpec(shape, lambda b, g: (0,) * len(shape))
    per_bg = lambda shape: pl.BlockSpec((None, None) + shape, lambda b, g: (b, g) + (0,) * len(shape))
    bg_shape = lambda shape, dt: jax.ShapeDtypeStruct((BATCH, KV_GROUPS) + shape, dt)
    return pl.pallas_call(
        _kprep_kernel,
        grid=(BATCH, KV_GROUPS),
        in_specs=[
            slab(0), slab(1), slab(2),
            pl.BlockSpec((SEQ, GATE_PAD), lambda b, g: (b, 0)),
            full((1, LANES)), full((1, LANES)), full((CMP_BLOCK, LANES)),
            full((CMP_BLOCK // 2, 2 * LANES, 2 * CMP_HIDDEN)), full((2, CMP_HIDDEN, LANES)),
        ],
        out_specs=[
            per_bg((N_KB, K_BLOCK, LANES)),
            per_bg((N_KB, K_BLOCK, LANES)),
            per_bg((N_KB, 2 * VT_ROWS, K_BLOCK)),
            per_bg((N_CMP_PAD, LANES)),
            per_bg((HEAD_DIM, N_CMP_PAD)),
            pl.BlockSpec((None, N_KB, GATE_PAD, K_BLOCK), lambda b, g: (b, 0, 0, 0)),
        ],
        out_shape=[
            bg_shape((N_KB, K_BLOCK, LANES), BF16),
            bg_shape((N_KB, K_BLOCK, LANES), BF16),
            bg_shape((N_KB, 2 * VT_ROWS, K_BLOCK), BF16),
            bg_shape((N_CMP_PAD, LANES), BF16),
            bg_shape((HEAD_DIM, N_CMP_PAD), BF16),
            jax.ShapeDtypeStruct((BATCH, N_KB, GATE_PAD, K_BLOCK), F32),
        ],
        scratch_shapes=[
            pltpu.VMEM((N_CMP_PAD, 2 * CMP_HIDDEN), F32),
            pltpu.VMEM((N_CMP_PAD + 8, 2 * CMP_HIDDEN), F32),
        ],
        compiler_params=_cparams("arbitrary", "arbitrary"),
        name="kprep",
    )(proj, proj, proj, gl, knw2, kcw, pos2, w1p, w2p)


def _attn_kernel(q_ref, ksel_ref, kwin_ref, vt_ref, kc_ref, vct_ref, gt_ref, sf_ref, wm_ref, dm_ref,
                 ov_ref, wq_ref, wup_ref, wdn_ref, o_ref, wup_o_ref, wdn_o_ref,
                 sel_ref, qs_ref, sa_ref, sb_ref, tiles_ref):
    wup_o_ref[...] = wup_ref[...].astype(BF16)
    wdn_o_ref[...] = wdn_ref[...].astype(BF16)

    qt = pl.program_id(2)
    t_i = qt * Q_TILE + lax.broadcasted_iota(jnp.int32, (1, Q_TILE), 1)
    head = lambda a, r: a[:, r * Q_TILE:(r + 1) * Q_TILE]
    heads = lambda parts: jnp.concatenate(parts, axis=1)

    q_t = q_ref[...].T
    parts = []
    for r in range(REP):
        xq = q_t[r * HEAD_DIM:(r + 1) * HEAD_DIM, :]
        ms = jnp.sum(xq * xq, axis=0, keepdims=True) * (1.0 / HEAD_DIM)
        parts.append(xq * lax.rsqrt(ms + NORM_EPS) * wq_ref[...] * (HEAD_DIM ** -0.5) * LOG2E)
    qn = heads(parts)
    sf = sf_ref[...]
    sf = jnp.where(lax.broadcasted_iota(jnp.int32, sf.shape, 0) == 6, sf * (qt * Q_TILE).astype(F32), sf)
    wide = REP * Q_TILE
    pad = jnp.zeros((HEAD_DIM - 8, wide), F32)
    qs_ref[...] = jnp.concatenate([qn, sf, pad], axis=0).astype(BF16)
    q_win = jnp.concatenate([sf, pad, qn], axis=0).astype(BF16)

    sc = _dot(kc_ref[...], qs_ref[...])
    ci = lax.broadcasted_iota(jnp.int32, (N_CMP_PAD, Q_TILE), 0)
    cm = jnp.where(ci * CMP_STRIDE + (CMP_BLOCK - 1) <= t_i, 0.0, NEG_BIG)
    any_valid = t_i >= CMP_BLOCK - 1
    pcs = []
    for r in range(REP):
        s = head(sc, r) + cm
        e = jnp.exp2(s - jnp.max(s, axis=0, keepdims=True))
        d = jnp.sum(e, axis=0, keepdims=True)
        pcs.append((e * jnp.where(any_valid, 1.0 / d, 0.0)).astype(BF16))
    oc_imp = _dot(jnp.concatenate([vct_ref[...], ov_ref[...]], axis=0), heads(pcs))
    o_c = oc_imp[0:HEAD_DIM]
    imp4 = oc_imp[HEAD_DIM:]
    imp = (head(imp4, 0) + head(imp4, 1)) + (head(imp4, 2) + head(imp4, 3))

    jj = lax.broadcasted_iota(jnp.int32, (N_SEL, Q_TILE), 0)
    cur = lax.shift_right_logical(t_i, 6)
    valid = jj <= cur
    sel_ref[...] = jnp.where(valid, 1.0, 0.0)
    forced = (jj == 0) | (jj == cur) | (jj == cur - 1)
    score = jnp.where(valid, jnp.where(forced, SEL_BIG, imp), -SEL_BIG)

    def rank_rows(n8):
        def branch():
            rows8 = [score[8 * v:8 * v + 8, :] for v in range(n8)]
            j8 = lax.broadcasted_iota(jnp.int32, (8, Q_TILE), 0)
            ranks = [jnp.zeros((8, Q_TILE), F32) for _ in rows8]
            for jp in range(8 * n8):
                row = score[jp:jp + 1, :]
                for v, blk in enumerate(rows8):
                    if 8 * v > jp:
                        ahead = jnp.where(row >= blk, 1.0, 0.0)
                    elif 8 * v + 7 <= jp:
                        ahead = jnp.where(row > blk, 1.0, 0.0)
                    else:
                        ahead = jnp.where(j8 + 8 * v > jp, jnp.where(row >= blk, 1.0, 0.0),
                                          jnp.where(row > blk, 1.0, 0.0))
                    ranks[v] = ranks[v] + ahead
            rank = jnp.concatenate(ranks, axis=0)
            sel_ref[0:8 * n8, :] = jnp.where(valid[0:8 * n8] & (rank < SEL_TOPK), 1.0, 0.0)
        return branch

    n8_needed = (qt * Q_TILE + Q_TILE - 1) // (8 * SEL_BLOCK) + 1
    lax.switch(n8_needed - 1, [(lambda: None) if n8 * 8 <= SEL_TOPK else rank_rows(n8)
                               for n8 in range(1, N_SEL // 8 + 1)])

    sub = SEL_TILE_KEYS // SEL_BLOCK

    def scores_into(kt, s_ref):
        member = sel_ref[pl.ds(kt * sub, sub), :]
        mrows = (member - 1.0) * (-NEG_BIG)
        qs_ref[MASK_LANE0:MASK_LANE0 + 16, :] = jnp.concatenate(
            [heads([mrows] * REP), jnp.zeros((16 - sub, wide), F32)], axis=0).astype(BF16)
        k_tile = ksel_ref[pl.ds(kt * SEL_TILE_BLOCKS, SEL_TILE_BLOCKS)].reshape(SEL_TILE_KEYS, LANES)
        s_ref[...] = _dot(k_tile, qs_ref[...])

    def absorb(kt, s_ref, carry, diagonal):
        m_run, acc = carry
        kb0 = kt * SEL_TILE_BLOCKS
        if diagonal:
            causal = dm_ref[qt % (SEL_TILE_KEYS // Q_TILE)]
        v_t = jnp.concatenate([vt_ref[kb0 + x, 0:VT_ROWS, :] for x in range(SEL_TILE_BLOCKS)], axis=1)
        ms_, accs = [], []
        for r in range(REP):
            sr = s_ref[:, r * Q_TILE:(r + 1) * Q_TILE]
            if diagonal:
                sr = sr + causal
            m_old = head(m_run, r)
            m_new = jnp.maximum(m_old, jnp.max(sr, axis=0, keepdims=True))
            p = jnp.exp2(sr - m_new).astype(BF16)
            accs.append(head(acc, r) * jnp.exp2(m_old - m_new) + _dot(v_t, p))
            ms_.append(m_new)
        return heads(ms_), heads(accs)

    carry = (jnp.full((1, wide), NEG_BIG, F32), jnp.zeros((VT_ROWS, wide), F32))
    n_full = (qt * Q_TILE) // SEL_TILE_KEYS

    n_sel_tiles = SEQ // SEL_TILE_KEYS
    tr = lax.broadcasted_iota(jnp.int32, (n_sel_tiles, N_SEL), 0)
    tc = lax.broadcasted_iota(jnp.int32, (n_sel_tiles, N_SEL), 1)
    tile_of_block = jnp.where(lax.shift_right_logical(tc, 3) == tr, 1.0, 0.0).astype(BF16)
    per_query = _dot(tile_of_block, sel_ref[...].astype(BF16))
    per_tile = _dot(per_query.astype(BF16), jnp.ones((Q_TILE, LANES), BF16))
    bit = lax.shift_left(1, lax.broadcasted_iota(jnp.int32, (n_sel_tiles, LANES), 0)).astype(F32)
    used_bits = jnp.sum(jnp.where(per_tile > 0.5, bit, 0.0), axis=0, keepdims=True)[0, 0].astype(jnp.int32)

    kb_first = qt * (Q_TILE // K_BLOCK) - WINDOW // K_BLOCK
    kbs = [jnp.maximum(kb_first + x, 0) for x in range(WIN_BLOCKS)]
    sw = _dot(jnp.concatenate([kwin_ref[kb] for kb in kbs], axis=0), q_win)

    n_visit = jnp.int32(0)
    for kt in range(n_sel_tiles):
        tiles_ref[n_visit] = kt
        take = (kt < n_full) & ((lax.shift_right_logical(used_bits, kt) & 1) == 1)
        n_visit = n_visit + jnp.where(take, 1, 0)
    tiles_ref[n_visit] = n_full

    scores_into(tiles_ref[0], sa_ref)
    wm = jnp.concatenate(
        [wm_ref[x * K_BLOCK:(x + 1) * K_BLOCK, :] + jnp.where(kb_first + x < 0, NEG_BIG, 0.0)
         for x in range(WIN_BLOCKS)], axis=0)
    pws = []
    for r in range(REP):
        s = head(sw, r) + wm
        pws.append(jnp.exp2(s - jnp.max(s, axis=0, keepdims=True)).astype(BF16))
    vw_t = jnp.concatenate([vt_ref[kb, VT_ROWS:2 * VT_ROWS, :] for kb in kbs], axis=1)
    acc_w = _dot(vw_t, heads(pws))
    o_w = acc_w[0:HEAD_DIM] / acc_w[HEAD_DIM:HEAD_DIM + 1]

    def tile_pair(i, carry):
        scores_into(tiles_ref[2 * i + 1], sb_ref)
        carry = absorb(tiles_ref[2 * i], sa_ref, carry, False)
        scores_into(tiles_ref[2 * i + 2], sa_ref)
        return absorb(tiles_ref[2 * i + 1], sb_ref, carry, False)

    carry = lax.fori_loop(0, n_visit // 2, tile_pair, carry)
    last = 2 * (n_visit // 2)

    def odd_tail(carry):
        scores_into(n_full, sb_ref)
        return absorb(n_full, sb_ref, absorb(tiles_ref[last], sa_ref, carry, False), True)

    def even_tail(carry):
        return absorb(n_full, sa_ref, carry, True)

    _, acc_s = lax.cond(n_visit % 2 == 1, odd_tail, even_tail, carry)
    o_s = acc_s[0:HEAD_DIM] / acc_s[HEAD_DIM:HEAD_DIM + 1]

    gates = jnp.concatenate([gt_ref[x] for x in range(Q_TILE // K_BLOCK)], axis=1)
    outs = []
    for r in range(REP):
        outs.append(gates[r:r + 1, :] * head(o_c, r)
                    + gates[REP + r:REP + r + 1, :] * head(o_s, r)
                    + gates[2 * REP + r:2 * REP + r + 1, :] * head(o_w, r))
    o_ref[...] = jnp.concatenate(outs, axis=0).T.astype(BF16)


def _attn(proj, ksel, kwin, vt, kc, vct, gt, sf, wm, dm, ov_t, wq_b, w_up, w_down):
    q0 = Z_COLS // (REP * HEAD_DIM)
    kvspec = lambda shape: pl.BlockSpec((None, None) + shape, lambda b, g, qt: (b, g) + (0,) * len(shape))
    const = lambda shape: pl.BlockSpec(shape, lambda b, g, qt: (0,) * len(shape))
    step = lambda b, g, qt: (b * KV_GROUPS + g) * N_QT + qt
    wup_spec = pl.BlockSpec((CAST_UP_ROWS, 2 * D_FF), lambda b, g, qt: (
        jnp.minimum(step(b, g, qt), D_MODEL // CAST_UP_ROWS - 1), 0))
    wdn_spec = pl.BlockSpec((CAST_DOWN_ROWS, D_MODEL), lambda b, g, qt: (
        jnp.minimum(step(b, g, qt), D_FF // CAST_DOWN_ROWS - 1), 0))
    return pl.pallas_call(
        _attn_kernel,
        grid=(BATCH, KV_GROUPS, N_QT),
        in_specs=[
            pl.BlockSpec((Q_TILE, REP * HEAD_DIM), lambda b, g, qt: (b * N_QT + qt, q0 + g)),
            kvspec((N_KB, K_BLOCK, LANES)),
            kvspec((N_KB, K_BLOCK, LANES)),
            kvspec((N_KB, 2 * VT_ROWS, K_BLOCK)),
            kvspec((N_CMP_PAD, LANES)),
            kvspec((HEAD_DIM, N_CMP_PAD)),
            pl.BlockSpec((None, Q_TILE // K_BLOCK, GATE_ROWS_PER_GROUP, K_BLOCK), lambda b, g, qt: (b, qt, g, 0)),
            pl.BlockSpec((None, 8, REP * Q_TILE), lambda b, g, qt: (g, 0, 0)),
            const((WIN_KEYS, Q_TILE)),
            const((SEL_TILE_KEYS // Q_TILE, SEL_TILE_KEYS, Q_TILE)),
            const((N_SEL, N_CMP_PAD)),
            const((HEAD_DIM, Q_TILE)),
            wup_spec,
            wdn_spec,
        ],
        out_specs=[pl.BlockSpec((Q_TILE, REP * HEAD_DIM), lambda b, g, qt: (b * N_QT + qt, g)),
                   wup_spec, wdn_spec],
        out_shape=[jax.ShapeDtypeStruct((N_TOK, NSA_WIDTH), BF16),
                   jax.ShapeDtypeStruct((D_MODEL, 2 * D_FF), BF16),
                   jax.ShapeDtypeStruct((D_FF, D_MODEL), BF16)],
        scratch_shapes=[pltpu.VMEM((N_SEL, Q_TILE), F32), pltpu.VMEM((LANES, REP * Q_TILE), BF16),
                        pltpu.VMEM((SEL_TILE_KEYS, REP * Q_TILE), F32),
                        pltpu.VMEM((SEL_TILE_KEYS, REP * Q_TILE), F32),
                        pltpu.SMEM((SEQ // SEL_TILE_KEYS + 1,), jnp.int32)],
        compiler_params=_cparams("arbitrary", "arbitrary", "arbitrary"),
        name="attn",
    )(proj, ksel, kwin, vt, kc, vct, gt, sf, wm, dm, ov_t, wq_b, w_up, w_down)


def _spatial_gating(z, lnw, lnb, sw_ref, sbx):
    ge = jax.nn.gelu(z)
    u = ge[:, :GMLP_WIDTH]
    v = ge[:, GMLP_WIDTH:]
    mu = jnp.mean(v, axis=-1, keepdims=True)
    var = jnp.mean(jnp.square(v - mu), axis=-1, keepdims=True)
    vn = ((v - mu) * lax.rsqrt(var + LN_EPS) * lnw + lnb).astype(BF16)
    ti = lax.broadcasted_iota(jnp.int32, (GMLP_CHUNK, GMLP_CHUNK), 0)
    si = lax.broadcasted_iota(jnp.int32, (GMLP_CHUNK, GMLP_CHUNK), 1)
    ws = [jnp.where(ti >= si, sw_ref[gg], 0.0).astype(BF16) for gg in range(GMLP_GROUPS)]
    gd = GMLP_WIDTH // GMLP_GROUPS
    outs = []
    for c in range(z.shape[0] // GMLP_CHUNK):
        rows = slice(c * GMLP_CHUNK, (c + 1) * GMLP_CHUNK)
        mix = jnp.concatenate(
            [_dot(ws[gg], vn[rows, gg * gd:(gg + 1) * gd]) for gg in range(GMLP_GROUPS)], axis=1)
        outs.append((u[rows, :] * (mix + sbx)).astype(BF16))
    return jnp.concatenate(outs, axis=0)


def _oproj_kernel(a_ref, z_ref, x_ref, wa_ref, wb_ref, nw_ref, lnw_ref, lnb_ref, sw_ref, sbx_ref,
                  x1_ref, xn_ref):
    attn_part = _dot(a_ref[...], wa_ref[...])
    b = _spatial_gating(z_ref[...], lnw_ref[...], lnb_ref[...], sw_ref, sbx_ref[...])
    y = x_ref[...] + (attn_part + _dot(b, wb_ref[...]))
    x1_ref[...] = y
    ms = jnp.mean(y * y, axis=-1, keepdims=True)
    xn_ref[...] = (y * lax.rsqrt(ms + NORM_EPS) * nw_ref[...]).astype(BF16)


def _oproj(a, proj, x2, w_out, nw, lnw, lnb, sw, sbx):
    return pl.pallas_call(
        _oproj_kernel,
        grid=(N_TOK // OPROJ_TM,),
        in_specs=[
            pl.BlockSpec((OPROJ_TM, NSA_WIDTH), lambda i: (i, 0)),
            pl.BlockSpec((OPROJ_TM, Z_COLS), lambda i: (i, 0)),
            pl.BlockSpec((OPROJ_TM, D_MODEL), lambda i: (i, 0)),
            pl.BlockSpec((NSA_WIDTH, D_MODEL), lambda i: (0, 0)),
            pl.BlockSpec((GMLP_WIDTH, D_MODEL), lambda i: (1, 0)),
            pl.BlockSpec((1, D_MODEL), lambda i: (0, 0)),
            pl.BlockSpec((1, GMLP_WIDTH), lambda i: (0, 0)),
            pl.BlockSpec((1, GMLP_WIDTH), lambda i: (0, 0)),
            pl.BlockSpec((GMLP_GROUPS, GMLP_CHUNK, GMLP_CHUNK), lambda i: (0, 0, 0)),
            pl.BlockSpec((GMLP_CHUNK, GMLP_WIDTH), lambda i: (0, 0)),
        ],
        out_specs=[
            pl.BlockSpec((OPROJ_TM, D_MODEL), lambda i: (i, 0)),
            pl.BlockSpec((OPROJ_TM, D_MODEL), lambda i: (i, 0)),
        ],
        out_shape=[
            jax.ShapeDtypeStruct((N_TOK, D_MODEL), F32),
            jax.ShapeDtypeStruct((N_TOK, D_MODEL), BF16),
        ],
        compiler_params=_cparams("arbitrary"),
        name="oproj",
    )(a, proj, x2, w_out, w_out, nw, lnw, lnb, sw, sbx)


def _ffn_kernel(xn_ref, x1_ref, wg_ref, wu_ref, cwg_ref, cwu_ref, cbg_ref, cbu_ref, wd_ref,
                cwg_p_ref, cwu_p_ref, cbg_p_ref, cbu_p_ref, o_ref, h0_ref, h1_ref, carry_ref, wd_p_ref):
    i = pl.program_id(0)
    q = pl.program_id(1)
    nq = pl.num_programs(1)
    half = FFN_TF // 2

    @pl.when((i == 0) & (q == 0))
    def _():
        carry_ref[...] = jnp.zeros_like(carry_ref)
        h1_ref[...] = jnp.zeros_like(h1_ref)
        wd_p_ref[...] = jnp.zeros_like(wd_p_ref)

    @pl.when(q == 0)
    def _():
        o_ref[...] = x1_ref[...]

    seq_start = i % (SEQ // FFN_TM) == 0
    xn = xn_ref[...]

    def produce(c, h_ref):
        h_ref[0, 8:, :] = _dot(xn, wg_ref[:, c * half:(c + 1) * half])
        h_ref[1, 8:, :] = _dot(xn, wu_ref[:, c * half:(c + 1) * half])

    def conv(h_ref, part, tile, cw, cb):
        h_ref[part, 0:8, :] = jnp.where(seq_start, 0.0, carry_ref[tile, part])
        carry_ref[tile, part] = h_ref[part, FFN_TM:FFN_TM + 8, :]
        taps = [h_ref[part, 8 - k:8 - k + FFN_TM, :] for k in (2, 1, 0)]
        return cb + ((cw[0:1, :] * taps[0] + cw[1:2, :] * taps[1]) + cw[2:3, :] * taps[2])

    def gate(h_ref, tile, c, cwg, cwu, cbg, cbu, scale=None):
        cols = slice(c * half, (c + 1) * half)
        cg = conv(h_ref, 0, tile, cwg[:, cols], cbg[:, cols])
        cu = conv(h_ref, 1, tile, cwu[:, cols], cbu[:, cols])
        act = jax.nn.silu(cg) * cu
        if scale is not None:
            act = act * scale
        return act.astype(BF16)

    act_prev = gate(h1_ref, jnp.where(q == 0, 2 * nq - 1, 2 * q - 1), 1, cwg_p_ref, cwu_p_ref, cbg_p_ref,
                    cbu_p_ref, scale=jnp.where(q == 0, 0.0, 1.0))
    produce(0, h0_ref)
    down_prev = _dot(act_prev, wd_p_ref[...])
    act_cur = gate(h0_ref, 2 * q, 0, cwg_ref, cwu_ref, cbg_ref, cbu_ref)
    produce(1, h1_ref)
    o_ref[...] += down_prev + _dot(act_cur, wd_ref[0:half, :])
    wd_p_ref[...] = wd_ref[half:, :]

    @pl.when(q == nq - 1)
    def _():
        act_last = gate(h1_ref, 2 * q + 1, 1, cwg_ref, cwu_ref, cbg_ref, cbu_ref)
        o_ref[...] += _dot(act_last, wd_ref[half:, :])


def _ffn(xn, x1, w_up, conv_w, conv_b, w_down):
    nj = D_FF // FFN_TF
    prev = lambda q: jnp.maximum(q - 1, 0)
    return pl.pallas_call(
        _ffn_kernel,
        grid=(N_TOK // FFN_TM, nj),
        in_specs=[
            pl.BlockSpec((FFN_TM, D_MODEL), lambda i, q: (i, 0)),
            pl.BlockSpec((FFN_TM, D_MODEL), lambda i, q: (i, 0)),
            pl.BlockSpec((D_MODEL, FFN_TF), lambda i, q: (0, q)),
            pl.BlockSpec((D_MODEL, FFN_TF), lambda i, q: (0, nj + q)),
            pl.BlockSpec((CONV_WIDTH, FFN_TF), lambda i, q: (0, q)),
            pl.BlockSpec((CONV_WIDTH, FFN_TF), lambda i, q: (0, nj + q)),
            pl.BlockSpec((1, FFN_TF), lambda i, q: (0, q)),
            pl.BlockSpec((1, FFN_TF), lambda i, q: (0, nj + q)),
            pl.BlockSpec((FFN_TF, D_MODEL), lambda i, q: (q, 0)),
            pl.BlockSpec((CONV_WIDTH, FFN_TF), lambda i, q: (0, prev(q))),
            pl.BlockSpec((CONV_WIDTH, FFN_TF), lambda i, q: (0, nj + prev(q))),
            pl.BlockSpec((1, FFN_TF), lambda i, q: (0, prev(q))),
            pl.BlockSpec((1, FFN_TF), lambda i, q: (0, nj + prev(q))),
        ],
        out_specs=pl.BlockSpec((FFN_TM, D_MODEL), lambda i, q: (i, 0)),
        out_shape=jax.ShapeDtypeStruct((N_TOK, D_MODEL), F32),
        scratch_shapes=[pltpu.VMEM((2, FFN_TM + 8, FFN_TF // 2), F32), pltpu.VMEM((2, FFN_TM + 8, FFN_TF // 2), F32),
                        pltpu.VMEM((2 * nj, 2, 8, FFN_TF // 2), F32),
                        pltpu.VMEM((FFN_TF // 2, D_MODEL), BF16)],
        compiler_params=_cparams("arbitrary", "arbitrary"),
        name="ffn",
    )(xn, x1, w_up, w_up, conv_w, conv_w, conv_b, conv_b, w_down, conv_w, conv_w, conv_b, conv_b)


def _overlap_t():
    start = np.arange(N_CMP_PAD)[None, :] * CMP_STRIDE
    s0 = np.arange(N_SEL)[:, None] * SEL_BLOCK
    ov = (start < s0 + SEL_BLOCK) & (start + CMP_BLOCK > s0) & (np.arange(N_CMP_PAD)[None, :] < N_CMP_PAD - 1)
    return ov.astype(np.float32)


def _bf16_round(a):
    return np.asarray(a, np.float32).astype(BF16).astype(np.float32)


def _slope_features():
    sl = (np.power(2.0, -8.0 * np.arange(1, NSA_HEADS + 1) / NSA_HEADS).astype(np.float32)
          * np.float32(LOG2E)).astype(np.float32)
    s1 = _bf16_round(sl)
    s2 = _bf16_round(sl - s1)
    s3 = _bf16_round(sl - s1 - s2)
    rows = np.stack([s1, s2, s3, s1, s2, s3, -sl, np.zeros_like(sl)], axis=0)
    rows = rows.reshape(8, KV_GROUPS, REP).transpose(1, 0, 2)
    return np.repeat(rows, Q_TILE, axis=2).astype(np.float32)


def _window_mask():
    kl = np.arange(WIN_KEYS)[:, None]
    ql = np.arange(Q_TILE)[None, :]
    dist = ql + WINDOW - kl
    return np.where((dist >= 0) & (dist < WINDOW), 0.0, NEG_BIG).astype(np.float32)


def _diag_masks():
    kl = np.arange(SEL_TILE_KEYS)[None, :, None]
    ql = np.arange(Q_TILE)[None, None, :]
    off = np.arange(SEL_TILE_KEYS // Q_TILE)[:, None, None] * Q_TILE
    return np.where(kl > ql + off, NEG_BIG, 0.0).astype(np.float32)


def _layout_w_in(w_in):
    q_end = NSA_WIDTH
    kv_end = q_end + 6 * KV_COLS
    g_end = kv_end + NSA_HEADS * N_BRANCH
    w_in = w_in.astype(BF16)
    w_q = w_in[:, :q_end]
    w_kv = w_in[:, q_end:kv_end].reshape(D_MODEL, 6, KV_GROUPS, HEAD_DIM)
    w_kv = jnp.stack([w_kv[:, s] for s in (2, 4, 0, 1, 3, 5)], axis=2).reshape(D_MODEL, 6 * KV_COLS)
    w_z = w_in[:, g_end:]
    w_main = jnp.concatenate([w_z, w_q, w_kv], axis=1)
    w_g = w_in[:, kv_end:g_end].reshape(D_MODEL, KV_GROUPS, REP, N_BRANCH).transpose(0, 1, 3, 2)
    w_g = w_g.reshape(D_MODEL, KV_GROUPS, N_BRANCH * REP)
    w_g = jnp.pad(w_g, ((0, 0), (0, 0), (0, GATE_ROWS_PER_GROUP - N_BRANCH * REP)))
    w_g = jnp.pad(w_g.reshape(D_MODEL, KV_GROUPS * GATE_ROWS_PER_GROUP),
                  ((0, 0), (0, GATE_PAD - KV_GROUPS * GATE_ROWS_PER_GROUP)))
    return w_main, w_g


def _layout_compress(cmp_pos, cmp_w1, cmp_w2):
    pos2 = jnp.concatenate([cmp_pos[0], cmp_pos[1]], axis=1)
    w1 = cmp_w1.reshape(2, CMP_BLOCK, HEAD_DIM, CMP_HIDDEN)
    zero = jnp.zeros_like(w1[0])
    w1k = jnp.concatenate([w1[0], zero], axis=1)
    w1v = jnp.concatenate([zero, w1[1]], axis=1)
    w1p = jnp.concatenate([w1k, w1v], axis=2).astype(BF16)
    w1p = w1p.reshape(CMP_BLOCK // 2, 2 * LANES, 2 * CMP_HIDDEN)
    w2p = jnp.pad(cmp_w2, ((0, 0), (0, 0), (0, LANES - HEAD_DIM))).astype(BF16)
    return pos2, w1p, w2p


def kernel(x, attn_norm_w, w_in, q_norm_w, k_norm_w, cmp_pos, cmp_w1, cmp_w2, gmlp_ln_w, gmlp_ln_b,
           spatial_w, spatial_b, w_out, ffn_norm_w, w_up, conv_w, conv_b, w_down):
    x2 = x.reshape(N_TOK, D_MODEL)
    w_main, w_gate = _layout_w_in(w_in)
    proj, gl = _proj(x2, attn_norm_w.reshape(1, D_MODEL), w_main, w_gate)

    knw2 = jnp.concatenate([k_norm_w[1], k_norm_w[2]]).reshape(1, LANES)
    kcw = jnp.concatenate([k_norm_w[0], jnp.zeros((LANES - HEAD_DIM,), F32)]).reshape(1, LANES)
    pos2, w1p, w2p = _layout_compress(cmp_pos, cmp_w1, cmp_w2)
    ksel, kwin, vt, kc, vct, gt = _kprep(proj, gl, knw2, kcw, pos2, w1p, w2p)

    wq_b = jnp.broadcast_to(q_norm_w.reshape(HEAD_DIM, 1), (HEAD_DIM, Q_TILE))
    a, w_up_b, w_down_b = _attn(proj, ksel, kwin, vt, kc, vct, gt, jnp.asarray(_slope_features()),
                                jnp.asarray(_window_mask()), jnp.asarray(_diag_masks()),
                                jnp.asarray(_overlap_t(), dtype=BF16), wq_b, w_up, w_down)

    sbx = jnp.repeat(spatial_b.T, GMLP_WIDTH // GMLP_GROUPS, axis=1)
    x1, xn = _oproj(a, proj, x2, w_out.astype(BF16), ffn_norm_w.reshape(1, D_MODEL),
                    gmlp_ln_w.reshape(1, GMLP_WIDTH), gmlp_ln_b.reshape(1, GMLP_WIDTH), spatial_w, sbx)

    out = _ffn(xn, x1, w_up_b, conv_w, conv_b.reshape(1, 2 * D_FF), w_down_b)
    return out.reshape(BATCH, SEQ, D_MODEL)
```

```python
import numpy as np
import jax
import jax.numpy as jnp
from jax import lax
from jax.experimental import pallas as pl
from jax.experimental.pallas import tpu as pltpu

F32 = jnp.float32
BF16 = jnp.bfloat16

D_MODEL = 2048
BATCH = 4
SEQ = 4096
N_TOK = BATCH * SEQ
NSA_HEADS = 16
KV_GROUPS = 4
REP = NSA_HEADS // KV_GROUPS
HEAD_DIM = 64
NSA_WIDTH = NSA_HEADS * HEAD_DIM
GMLP_WIDTH = D_MODEL - NSA_WIDTH
GMLP_GROUPS = 8
GMLP_CHUNK = 128
CMP_BLOCK = 32
CMP_STRIDE = 16
CMP_HIDDEN = 256
N_CMP_PAD = SEQ // CMP_STRIDE
SEL_BLOCK = 64
N_SEL = SEQ // SEL_BLOCK
SEL_TOPK = 16
WINDOW = 512
K_BLOCK = 128
N_KB = SEQ // K_BLOCK
Q_TILE = 256
N_QT = SEQ // Q_TILE
N_BRANCH = 3
KV_COLS = KV_GROUPS * HEAD_DIM
D_FF = 5632
CONV_WIDTH = 3
NORM_EPS = 1e-6
LN_EPS = 1e-5
NEG_BIG = -1e30
SEL_BIG = 1e9

LANES = 128
VMEM_LIMIT_BYTES = 56 * 1024 * 1024

Z_COLS = 2 * GMLP_WIDTH
PROJ_COLS = Z_COLS + NSA_WIDTH + 6 * KV_COLS
GATE_PAD = LANES
GATE_ROWS_PER_GROUP = 16

PROJ_TM = 1024
PROJ_TN = 1536
OPROJ_TM = 512
FFN_TM = 512
FFN_TF = 512
SEL_TILE_BLOCKS = 4
SEL_TILE_KEYS = SEL_TILE_BLOCKS * K_BLOCK
MASK_LANE0 = 80
SUM_ROWS = 16
VT_ROWS = HEAD_DIM + SUM_ROWS
LOG2E = 1.4426950408889634
WIN_BLOCKS = (WINDOW + Q_TILE) // K_BLOCK
WIN_KEYS = WIN_BLOCKS * K_BLOCK
CAST_UP_ROWS = 16
CAST_DOWN_ROWS = 32


def _cparams(*sem):
    return pltpu.CompilerParams(dimension_semantics=sem, vmem_limit_bytes=VMEM_LIMIT_BYTES)


def _dot(a, b):
    return jnp.dot(a, b, preferred_element_type=F32)


def _proj_kernel(x_ref, nw_ref, w_ref, wg_ref, o_ref, og_ref, h_ref):
    @pl.when(pl.program_id(1) == 0)
    def _():
        x = x_ref[...]
        ms = jnp.mean(x * x, axis=-1, keepdims=True)
        h = (x * lax.rsqrt(ms + NORM_EPS) * nw_ref[...]).astype(BF16)
        h_ref[...] = h
        og_ref[...] = _dot(h, wg_ref[...])

    o_ref[...] = _dot(h_ref[...], w_ref[...])


def _proj(x2, nw, w_main, w_gate):
    return pl.pallas_call(
        _proj_kernel,
        grid=(N_TOK // PROJ_TM, PROJ_COLS // PROJ_TN),
        in_specs=[
            pl.BlockSpec((PROJ_TM, D_MODEL), lambda i, j: (i, 0)),
            pl.BlockSpec((1, D_MODEL), lambda i, j: (0, 0)),
            pl.BlockSpec((D_MODEL, PROJ_TN), lambda i, j: (0, j)),
            pl.BlockSpec((D_MODEL, GATE_PAD), lambda i, j: (0, 0)),
        ],
        out_specs=[
            pl.BlockSpec((PROJ_TM, PROJ_TN), lambda i, j: (i, j)),
            pl.BlockSpec((PROJ_TM, GATE_PAD), lambda i, j: (i, 0)),
        ],
        out_shape=[
            jax.ShapeDtypeStruct((N_TOK, PROJ_COLS), F32),
            jax.ShapeDtypeStruct((N_TOK, GATE_PAD), F32),
        ],
        scratch_shapes=[pltpu.VMEM((PROJ_TM, D_MODEL), BF16)],
        compiler_params=_cparams("arbitrary", "arbitrary"),
        name="proj",
    )(x2, nw, w_main, w_gate)


def _pos_features(lane, first, f1, f2):
    return jnp.where(lane < first + 3, f1, jnp.where(lane < first + 6, f2,
                                                     jnp.where(lane == first + 6, 1.0, 0.0)))


def _kprep_kernel(s0_ref, s1_ref, s2_ref, gl_ref, knw_ref, kcw_ref, pos_ref, w1_ref, w2_ref,
                  ksel_ref, kwin_ref, vt_ref, kc_ref, vct_ref, gt_ref, ha_ref, hb_ref):
    lane = lax.broadcasted_iota(jnp.int32, (K_BLOCK, LANES), 1)
    row = lax.broadcasted_iota(jnp.int32, (K_BLOCK, LANES), 0)
    lo = lane < HEAD_DIM
    inv_d = 1.0 / HEAD_DIM
    ones_rows = jnp.where(lax.broadcasted_iota(jnp.int32, (SUM_ROWS, K_BLOCK), 0) == 0, 1.0, 0.0)

    def chunk(c, carry):
        r0 = pl.multiple_of(c * K_BLOCK, K_BLOCK)
        x = s0_ref[pl.ds(r0, K_BLOCK), :]
        sq = x * x
        s_lo = jnp.sum(jnp.where(lo, sq, 0.0), axis=1, keepdims=True)
        s_hi = jnp.sum(jnp.where(lo, 0.0, sq), axis=1, keepdims=True)
        ms = jnp.where(lo, s_lo, s_hi) * inv_d
        y = x * lax.rsqrt(ms + NORM_EPS) * knw_ref[...]
        pos = r0 + row
        blk = lax.shift_right_logical(pos, 6)
        f1 = (blk * SEL_BLOCK).astype(F32)
        f2 = (pos & (SEL_BLOCK - 1)).astype(F32)
        onehot = jnp.where(lane - MASK_LANE0 == (blk & (SEL_TILE_BLOCKS * 2 - 1)), 1.0, 0.0)
        feat_sel = jnp.where(lane < MASK_LANE0, _pos_features(lane, HEAD_DIM, f1, f2), onehot)
        ksel_ref[c] = jnp.where(lo, y, feat_sel).astype(BF16)
        kwin_ref[c] = jnp.where(lo, _pos_features(lane, 0, f1, f2), y).astype(BF16)
        v_t = s2_ref[pl.ds(r0, K_BLOCK), :].T
        vt_ref[c] = jnp.concatenate(
            [v_t[0:HEAD_DIM], ones_rows, v_t[HEAD_DIM:], ones_rows], axis=0).astype(BF16)
        return carry

    lax.fori_loop(0, N_KB, chunk, 0, unroll=4)

    @pl.when(pl.program_id(1) == 0)
    def _():
        def gchunk(c, carry):
            r0 = pl.multiple_of(c * K_BLOCK, K_BLOCK)
            gt_ref[c] = jax.nn.sigmoid(gl_ref[pl.ds(r0, K_BLOCK), :]).T
            return carry

        lax.fori_loop(0, N_KB, gchunk, 0, unroll=4)

    ha_ref[...] = jnp.zeros_like(ha_ref)
    hb_ref[...] = jnp.zeros_like(hb_ref)
    for tp in range(CMP_STRIDE // 2):
        rows = [s1_ref[pl.ds(2 * tp + e, N_CMP_PAD, stride=CMP_STRIDE), :] for e in range(2)]
        xa = jnp.concatenate([(rows[e] + pos_ref[2 * tp + e:2 * tp + e + 1, :]).astype(BF16)
                              for e in range(2)], axis=1)
        xb = jnp.concatenate([(rows[e] + pos_ref[CMP_STRIDE + 2 * tp + e:CMP_STRIDE + 2 * tp + e + 1, :])
                              .astype(BF16) for e in range(2)], axis=1)
        ha_ref[...] += _dot(xa, w1_ref[tp])
        hb_ref[0:N_CMP_PAD, :] += _dot(xb, w1_ref[CMP_STRIDE // 2 + tp])
    hid = jax.nn.gelu(ha_ref[...] + hb_ref[1:N_CMP_PAD + 1, :]).astype(BF16)
    kc = _dot(hid[:, :CMP_HIDDEN], w2_ref[0])
    vc = _dot(hid[:, CMP_HIDDEN:], w2_ref[1])
    ms = jnp.sum(kc * kc, axis=1, keepdims=True) * inv_d
    kcn = kc * lax.rsqrt(ms + NORM_EPS) * kcw_ref[...]
    ci = lax.broadcasted_iota(jnp.int32, (N_CMP_PAD, LANES), 0)
    cl = lax.broadcasted_iota(jnp.int32, (N_CMP_PAD, LANES), 1)
    c1 = (lax.shift_right_logical(ci, 2) * SEL_BLOCK).astype(F32)
    c2 = ((ci & 3) * CMP_STRIDE).astype(F32) + (CMP_BLOCK - 1) * 0.5
    kc_ref[...] = jnp.where(cl < HEAD_DIM, kcn, _pos_features(cl, HEAD_DIM, c1, c2)).astype(BF16)
    vct_ref[...] = vc.T[0:HEAD_DIM, :].astype(BF16)


def _kprep(proj, gl, knw2, kcw, pos2, w1p, w2p):
    kv0 = (Z_COLS + NSA_WIDTH) // LANES
    slab = lambda s: pl.BlockSpec((SEQ, LANES), lambda b, g, s=s: (b, kv0 + 3 * g + s))
    full = lambda shape: pl.BlockSpec(shape, lambda b, g: (0,) * len(shape))
    per_bg = lambda shape: pl.BlockSpec((None, None) + shape, lambda b, g: (b, g) + (0,) * len(shape))
    bg_shape = lambda shape, dt: jax.ShapeDtypeStruct((BATCH, KV_GROUPS) + shape, dt)
    return pl.pallas_call(
        _kprep_kernel,
        grid=(BATCH, KV_GROUPS),
        in_specs=[
            slab(0), slab(1), slab(2),
            pl.BlockSpec((SEQ, GATE_PAD), lambda b, g: (b, 0)),
            full((1, LANES)), full((1, LANES)), full((CMP_BLOCK, LANES)),
            full((CMP_BLOCK // 2, 2 * LANES, 2 * CMP_HIDDEN)), full((2, CMP_HIDDEN, LANES)),
        ],
        out_specs=[
            per_bg((N_KB, K_BLOCK, LANES)),
            per_bg((N_KB, K_BLOCK, LANES)),
            per_bg((N_KB, 2 * VT_ROWS, K_BLOCK)),
            per_bg((N_CMP_PAD, LANES)),
            per_bg((HEAD_DIM, N_CMP_PAD)),
            pl.BlockSpec((None, N_KB, GATE_PAD, K_BLOCK), lambda b, g: (b, 0, 0, 0)),
        ],
        out_shape=[
            bg_shape((N_KB, K_BLOCK, LANES), BF16),
            bg_shape((N_KB, K_BLOCK, LANES), BF16),
            bg_shape((N_KB, 2 * VT_ROWS, K_BLOCK), BF16),
            bg_shape((N_CMP_PAD, LANES), BF16),
            bg_shape((HEAD_DIM, N_CMP_PAD), BF16),
            jax.ShapeDtypeStruct((BATCH, N_KB, GATE_PAD, K_BLOCK), F32),
        ],
        scratch_shapes=[
            pltpu.VMEM((N_CMP_PAD, 2 * CMP_HIDDEN), F32),
            pltpu.VMEM((N_CMP_PAD + 8, 2 * CMP_HIDDEN), F32),
        ],
        compiler_params=_cparams("arbitrary", "arbitrary"),
        name="kprep",
    )(proj, proj, proj, gl, knw2, kcw, pos2, w1p, w2p)


def _attn_kernel(q_ref, ksel_ref, kwin_ref, vt_ref, kc_ref, vct_ref, gt_ref, sf_ref, wm_ref, dm_ref,
                 ov_ref, wq_ref, wup_ref, wdn_ref, o_ref, wup_o_ref, wdn_o_ref,
                 sel_ref, qs_ref, sa_ref, sb_ref, tiles_ref):
    wup_o_ref[...] = wup_ref[...].astype(BF16)
    wdn_o_ref[...] = wdn_ref[...].astype(BF16)

    qt = pl.program_id(2)
    t_i = qt * Q_TILE + lax.broadcasted_iota(jnp.int32, (1, Q_TILE), 1)
    head = lambda a, r: a[:, r * Q_TILE:(r + 1) * Q_TILE]
    heads = lambda parts: jnp.concatenate(parts, axis=1)

    q_t = q_ref[...].T
    parts = []
    for r in range(REP):
        xq = q_t[r * HEAD_DIM:(r + 1) * HEAD_DIM, :]
        ms = jnp.sum(xq * xq, axis=0, keepdims=True) * (1.0 / HEAD_DIM)
        parts.append(xq * lax.rsqrt(ms + NORM_EPS) * wq_ref[...] * (HEAD_DIM ** -0.5) * LOG2E)
    qn = heads(parts)
    sf = sf_ref[...]
    sf = jnp.where(lax.broadcasted_iota(jnp.int32, sf.shape, 0) == 6, sf * (qt * Q_TILE).astype(F32), sf)
    wide = REP * Q_TILE
    pad = jnp.zeros((HEAD_DIM - 8, wide), F32)
    qs_ref[...] = jnp.concatenate([qn, sf, pad], axis=0).astype(BF16)
    q_win = jnp.concatenate([sf, pad, qn], axis=0).astype(BF16)

    sc = _dot(kc_ref[...], qs_ref[...])
    ci = lax.broadcasted_iota(jnp.int32, (N_CMP_PAD, Q_TILE), 0)
    cm = jnp.where(ci * CMP_STRIDE + (CMP_BLOCK - 1) <= t_i, 0.0, NEG_BIG)
    any_valid = t_i >= CMP_BLOCK - 1
    pcs = []
    for r in range(REP):
        s = head(sc, r) + cm
        e = jnp.exp2(s - jnp.max(s, axis=0, keepdims=True))
        d = jnp.sum(e, axis=0, keepdims=True)
        pcs.append((e * jnp.where(any_valid, 1.0 / d, 0.0)).astype(BF16))
    oc_imp = _dot(jnp.concatenate([vct_ref[...], ov_ref[...]], axis=0), heads(pcs))
    o_c = oc_imp[0:HEAD_DIM]
    imp4 = oc_imp[HEAD_DIM:]
    imp = (head(imp4, 0) + head(imp4, 1)) + (head(imp4, 2) + head(imp4, 3))

    jj = lax.broadcasted_iota(jnp.int32, (N_SEL, Q_TILE), 0)
    cur = lax.shift_right_logical(t_i, 6)
    valid = jj <= cur
    sel_ref[...] = jnp.where(valid, 1.0, 0.0)
    forced = (jj == 0) | (jj == cur) | (jj == cur - 1)
    score = jnp.where(valid, jnp.where(forced, SEL_BIG, imp), -SEL_BIG)

    def rank_rows(n8):
        def branch():
            rows8 = [score[8 * v:8 * v + 8, :] for v in range(n8)]
            j8 = lax.broadcasted_iota(jnp.int32, (8, Q_TILE), 0)
            ranks = [jnp.zeros((8, Q_TILE), F32) for _ in rows8]
            for jp in range(8 * n8):
                row = score[jp:jp + 1, :]
                for v, blk in enumerate(rows8):
                    if 8 * v > jp:
                        ahead = jnp.where(row >= blk, 1.0, 0.0)
                    elif 8 * v + 7 <= jp:
                        ahead = jnp.where(row > blk, 1.0, 0.0)
                    else:
                        ahead = jnp.where(j8 + 8 * v > jp, jnp.where(row >= blk, 1.0, 0.0),
                                          jnp.where(row > blk, 1.0, 0.0))
                    ranks[v] = ranks[v] + ahead
            rank = jnp.concatenate(ranks, axis=0)
            sel_ref[0:8 * n8, :] = jnp.where(valid[0:8 * n8] & (rank < SEL_TOPK), 1.0, 0.0)
        return branch

    n8_needed = (qt * Q_TILE + Q_TILE - 1) // (8 * SEL_BLOCK) + 1
    lax.switch(n8_needed - 1, [(lambda: None) if n8 * 8 <= SEL_TOPK else rank_rows(n8)
                               for n8 in range(1, N_SEL // 8 + 1)])

    sub = SEL_TILE_KEYS // SEL_BLOCK

    def scores_into(kt, s_ref):
        member = sel_ref[pl.ds(kt * sub, sub), :]
        mrows = (member - 1.0) * (-NEG_BIG)
        qs_ref[MASK_LANE0:MASK_LANE0 + 16, :] = jnp.concatenate(
            [heads([mrows] * REP), jnp.zeros((16 - sub, wide), F32)], axis=0).astype(BF16)
        k_tile = ksel_ref[pl.ds(kt * SEL_TILE_BLOCKS, SEL_TILE_BLOCKS)].reshape(SEL_TILE_KEYS, LANES)
        s_ref[...] = _dot(k_tile, qs_ref[...])

    def absorb(kt, s_ref, carry, diagonal):
        m_run, acc = carry
        kb0 = kt * SEL_TILE_BLOCKS
        if diagonal:
            causal = dm_ref[qt % (SEL_TILE_KEYS // Q_TILE)]
        v_t = jnp.concatenate([vt_ref[kb0 + x, 0:VT_ROWS, :] for x in range(SEL_TILE_BLOCKS)], axis=1)
        ms_, accs = [], []
        for r in range(REP):
            sr = s_ref[:, r * Q_TILE:(r + 1) * Q_TILE]
            if diagonal:
                sr = sr + causal
            m_old = head(m_run, r)
            m_new = jnp.maximum(m_old, jnp.max(sr, axis=0, keepdims=True))
            p = jnp.exp2(sr - m_new).astype(BF16)
            accs.append(head(acc, r) * jnp.exp2(m_old - m_new) + _dot(v_t, p))
            ms_.append(m_new)
        return heads(ms_), heads(accs)

    carry = (jnp.full((1, wide), NEG_BIG, F32), jnp.zeros((VT_ROWS, wide), F32))
    n_full = (qt * Q_TILE) // SEL_TILE_KEYS

    n_sel_tiles = SEQ // SEL_TILE_KEYS
    tr = lax.broadcasted_iota(jnp.int32, (n_sel_tiles, N_SEL), 0)
    tc = lax.broadcasted_iota(jnp.int32, (n_sel_tiles, N_SEL), 1)
    tile_of_block = jnp.where(lax.shift_right_logical(tc, 3) == tr, 1.0, 0.0).astype(BF16)
    per_query = _dot(tile_of_block, sel_ref[...].astype(BF16))
    per_tile = _dot(per_query.astype(BF16), jnp.ones((Q_TILE, LANES), BF16))
    bit = lax.shift_left(1, lax.broadcasted_iota(jnp.int32, (n_sel_tiles, LANES), 0)).astype(F32)
    used_bits = jnp.sum(jnp.where(per_tile > 0.5, bit, 0.0), axis=0, keepdims=True)[0, 0].astype(jnp.int32)

    kb_first = qt * (Q_TILE // K_BLOCK) - WINDOW // K_BLOCK
    kbs = [jnp.maximum(kb_first + x, 0) for x in range(WIN_BLOCKS)]
    sw = _dot(jnp.concatenate([kwin_ref[kb] for kb in kbs], axis=0), q_win)

    n_visit = jnp.int32(0)
    for kt in range(n_sel_tiles):
        tiles_ref[n_visit] = kt
        take = (kt < n_full) & ((lax.shift_right_logical(used_bits, kt) & 1) == 1)
        n_visit = n_visit + jnp.where(take, 1, 0)
    tiles_ref[n_visit] = n_full

    scores_into(tiles_ref[0], sa_ref)
    wm = jnp.concatenate(
        [wm_ref[x * K_BLOCK:(x + 1) * K_BLOCK, :] + jnp.where(kb_first + x < 0, NEG_BIG, 0.0)
         for x in range(WIN_BLOCKS)], axis=0)
    pws = []
    for r in range(REP):
        s = head(sw, r) + wm
        pws.append(jnp.exp2(s - jnp.max(s, axis=0, keepdims=True)).astype(BF16))
    vw_t = jnp.concatenate([vt_ref[kb, VT_ROWS:2 * VT_ROWS, :] for kb in kbs], axis=1)
    acc_w = _dot(vw_t, heads(pws))
    o_w = acc_w[0:HEAD_DIM] / acc_w[HEAD_DIM:HEAD_DIM + 1]

    def tile_pair(i, carry):
        scores_into(tiles_ref[2 * i + 1], sb_ref)
        carry = absorb(tiles_ref[2 * i], sa_ref, carry, False)
        scores_into(tiles_ref[2 * i + 2], sa_ref)
        return absorb(tiles_ref[2 * i + 1], sb_ref, carry, False)

    carry = lax.fori_loop(0, n_visit // 2, tile_pair, carry)
    last = 2 * (n_visit // 2)

    def odd_tail(carry):
        scores_into(n_full, sb_ref)
        return absorb(n_full, sb_ref, absorb(tiles_ref[last], sa_ref, carry, False), True)

    def even_tail(carry):
        return absorb(n_full, sa_ref, carry, True)

    _, acc_s = lax.cond(n_visit % 2 == 1, odd_tail, even_tail, carry)
    o_s = acc_s[0:HEAD_DIM] / acc_s[HEAD_DIM:HEAD_DIM + 1]

    gates = jnp.concatenate([gt_ref[x] for x in range(Q_TILE // K_BLOCK)], axis=1)
    outs = []
    for r in range(REP):
        outs.append(gates[r:r + 1, :] * head(o_c, r)
                    + gates[REP + r:REP + r + 1, :] * head(o_s, r)
                    + gates[2 * REP + r:2 * REP + r + 1, :] * head(o_w, r))
    o_ref[...] = jnp.concatenate(outs, axis=0).T.astype(BF16)


def _attn(proj, ksel, kwin, vt, kc, vct, gt, sf, wm, dm, ov_t, wq_b, w_up, w_down):
    q0 = Z_COLS // (REP * HEAD_DIM)
    kvspec = lambda shape: pl.BlockSpec((None, None) + shape, lambda b, g, qt: (b, g) + (0,) * len(shape))
    const = lambda shape: pl.BlockSpec(shape, lambda b, g, qt: (0,) * len(shape))
    n_steps = BATCH * KV_GROUPS * N_QT
    assert 2 * (D_MODEL // CAST_UP_ROWS) == n_steps and D_FF // CAST_DOWN_ROWS <= n_steps
    step = lambda b, g, qt: (b * KV_GROUPS + g) * N_QT + qt
    wup_spec = pl.BlockSpec((CAST_UP_ROWS, D_FF), lambda b, g, qt: (step(b, g, qt) // 2, step(b, g, qt) % 2))
    wdn_spec = pl.BlockSpec((CAST_DOWN_ROWS, D_MODEL), lambda b, g, qt: (
        jnp.minimum(step(b, g, qt), D_FF // CAST_DOWN_ROWS - 1), 0))
    return pl.pallas_call(
        _attn_kernel,
        grid=(BATCH, KV_GROUPS, N_QT),
        in_specs=[
            pl.BlockSpec((Q_TILE, REP * HEAD_DIM), lambda b, g, qt: (b * N_QT + qt, q0 + g)),
            kvspec((N_KB, K_BLOCK, LANES)),
            kvspec((N_KB, K_BLOCK, LANES)),
            kvspec((N_KB, 2 * VT_ROWS, K_BLOCK)),
            kvspec((N_CMP_PAD, LANES)),
            kvspec((HEAD_DIM, N_CMP_PAD)),
            pl.BlockSpec((None, Q_TILE // K_BLOCK, GATE_ROWS_PER_GROUP, K_BLOCK), lambda b, g, qt: (b, qt, g, 0)),
            pl.BlockSpec((None, 8, REP * Q_TILE), lambda b, g, qt: (g, 0, 0)),
            const((WIN_KEYS, Q_TILE)),
            const((SEL_TILE_KEYS // Q_TILE, SEL_TILE_KEYS, Q_TILE)),
            const((N_SEL, N_CMP_PAD)),
            const((HEAD_DIM, Q_TILE)),
            wup_spec,
            wdn_spec,
        ],
        out_specs=[pl.BlockSpec((Q_TILE, REP * HEAD_DIM), lambda b, g, qt: (b * N_QT + qt, g)),
                   wup_spec, wdn_spec],
        out_shape=[jax.ShapeDtypeStruct((N_TOK, NSA_WIDTH), BF16),
                   jax.ShapeDtypeStruct((D_MODEL, 2 * D_FF), BF16),
                   jax.ShapeDtypeStruct((D_FF, D_MODEL), BF16)],
        scratch_shapes=[pltpu.VMEM((N_SEL, Q_TILE), F32), pltpu.VMEM((LANES, REP * Q_TILE), BF16),
                        pltpu.VMEM((SEL_TILE_KEYS, REP * Q_TILE), F32),
                        pltpu.VMEM((SEL_TILE_KEYS, REP * Q_TILE), F32),
                        pltpu.SMEM((SEQ // SEL_TILE_KEYS + 1,), jnp.int32)],
        compiler_params=_cparams("arbitrary", "arbitrary", "arbitrary"),
        name="attn",
    )(proj, ksel, kwin, vt, kc, vct, gt, sf, wm, dm, ov_t, wq_b, w_up, w_down)


def _spatial_gating(z, lnw, lnb, sw_ref, sbx):
    ge = jax.nn.gelu(z)
    u = ge[:, :GMLP_WIDTH]
    v = ge[:, GMLP_WIDTH:]
    mu = jnp.mean(v, axis=-1, keepdims=True)
    var = jnp.mean(jnp.square(v - mu), axis=-1, keepdims=True)
    vn = ((v - mu) * lax.rsqrt(var + LN_EPS) * lnw + lnb).astype(BF16)
    ti = lax.broadcasted_iota(jnp.int32, (GMLP_CHUNK, GMLP_CHUNK), 0)
    si = lax.broadcasted_iota(jnp.int32, (GMLP_CHUNK, GMLP_CHUNK), 1)
    ws = [jnp.where(ti >= si, sw_ref[gg], 0.0).astype(BF16) for gg in range(GMLP_GROUPS)]
    gd = GMLP_WIDTH // GMLP_GROUPS
    outs = []
    for c in range(z.shape[0] // GMLP_CHUNK):
        rows = slice(c * GMLP_CHUNK, (c + 1) * GMLP_CHUNK)
        mix = jnp.concatenate(
            [_dot(ws[gg], vn[rows, gg * gd:(gg + 1) * gd]) for gg in range(GMLP_GROUPS)], axis=1)
        outs.append((u[rows, :] * (mix + sbx)).astype(BF16))
    return jnp.concatenate(outs, axis=0)


def _oproj_kernel(a_ref, z_ref, x_ref, wa_ref, wb_ref, nw_ref, lnw_ref, lnb_ref, sw_ref, sbx_ref,
                  x1_ref, xn_ref):
    attn_part = _dot(a_ref[...], wa_ref[...])
    b = _spatial_gating(z_ref[...], lnw_ref[...], lnb_ref[...], sw_ref, sbx_ref[...])
    y = x_ref[...] + (attn_part + _dot(b, wb_ref[...]))
    x1_ref[...] = y
    ms = jnp.mean(y * y, axis=-1, keepdims=True)
    xn_ref[...] = (y * lax.rsqrt(ms + NORM_EPS) * nw_ref[...]).astype(BF16)


def _oproj(a, proj, x2, w_out, nw, lnw, lnb, sw, sbx):
    return pl.pallas_call(
        _oproj_kernel,
        grid=(N_TOK // OPROJ_TM,),
        in_specs=[
            pl.BlockSpec((OPROJ_TM, NSA_WIDTH), lambda i: (i, 0)),
            pl.BlockSpec((OPROJ_TM, Z_COLS), lambda i: (i, 0)),
            pl.BlockSpec((OPROJ_TM, D_MODEL), lambda i: (i, 0)),
            pl.BlockSpec((NSA_WIDTH, D_MODEL), lambda i: (0, 0)),
            pl.BlockSpec((GMLP_WIDTH, D_MODEL), lambda i: (1, 0)),
            pl.BlockSpec((1, D_MODEL), lambda i: (0, 0)),
            pl.BlockSpec((1, GMLP_WIDTH), lambda i: (0, 0)),
            pl.BlockSpec((1, GMLP_WIDTH), lambda i: (0, 0)),
            pl.BlockSpec((GMLP_GROUPS, GMLP_CHUNK, GMLP_CHUNK), lambda i: (0, 0, 0)),
            pl.BlockSpec((GMLP_CHUNK, GMLP_WIDTH), lambda i: (0, 0)),
        ],
        out_specs=[
            pl.BlockSpec((OPROJ_TM, D_MODEL), lambda i: (i, 0)),
            pl.BlockSpec((OPROJ_TM, D_MODEL), lambda i: (i, 0)),
        ],
        out_shape=[
            jax.ShapeDtypeStruct((N_TOK, D_MODEL), F32),
            jax.ShapeDtypeStruct((N_TOK, D_MODEL), BF16),
        ],
        compiler_params=_cparams("arbitrary"),
        name="oproj",
    )(a, proj, x2, w_out, w_out, nw, lnw, lnb, sw, sbx)


def _ffn_kernel(xn_ref, x1_ref, wg_ref, wu_ref, cwg_ref, cwu_ref, cbg_ref, cbu_ref, wd_ref,
                cwg_p_ref, cwu_p_ref, cbg_p_ref, cbu_p_ref, o_ref, h0_ref, h1_ref, carry_ref, wd_p_ref):
    i = pl.program_id(0)
    q = pl.program_id(1)
    nq = pl.num_programs(1)
    half = FFN_TF // 2

    @pl.when((i == 0) & (q == 0))
    def _():
        carry_ref[...] = jnp.zeros_like(carry_ref)
        h1_ref[...] = jnp.zeros_like(h1_ref)
        wd_p_ref[...] = jnp.zeros_like(wd_p_ref)

    @pl.when(q == 0)
    def _():
        o_ref[...] = x1_ref[...]

    seq_start = i % (SEQ // FFN_TM) == 0
    xn = xn_ref[...]

    def produce(c, h_ref):
        h_ref[0, 8:, :] = _dot(xn, wg_ref[:, c * half:(c + 1) * half])
        h_ref[1, 8:, :] = _dot(xn, wu_ref[:, c * half:(c + 1) * half])

    def conv(h_ref, part, tile, cw, cb):
        h_ref[part, 0:8, :] = jnp.where(seq_start, 0.0, carry_ref[tile, part])
        carry_ref[tile, part] = h_ref[part, FFN_TM:FFN_TM + 8, :]
        taps = [h_ref[part, 8 - k:8 - k + FFN_TM, :] for k in (2, 1, 0)]
        return cb + ((cw[0:1, :] * taps[0] + cw[1:2, :] * taps[1]) + cw[2:3, :] * taps[2])

    def gate(h_ref, tile, c, cwg, cwu, cbg, cbu, scale=None):
        cols = slice(c * half, (c + 1) * half)
        cg = conv(h_ref, 0, tile, cwg[:, cols], cbg[:, cols])
        cu = conv(h_ref, 1, tile, cwu[:, cols], cbu[:, cols])
        act = jax.nn.silu(cg) * cu
        if scale is not None:
            act = act * scale
        return act.astype(BF16)

    act_prev = gate(h1_ref, jnp.where(q == 0, 2 * nq - 1, 2 * q - 1), 1, cwg_p_ref, cwu_p_ref, cbg_p_ref,
                    cbu_p_ref, scale=jnp.where(q == 0, 0.0, 1.0))
    produce(0, h0_ref)
    down_prev = _dot(act_prev, wd_p_ref[...])
    act_cur = gate(h0_ref, 2 * q, 0, cwg_ref, cwu_ref, cbg_ref, cbu_ref)
    produce(1, h1_ref)
    o_ref[...] += down_prev + _dot(act_cur, wd_ref[0:half, :])
    wd_p_ref[...] = wd_ref[half:, :]

    @pl.when(q == nq - 1)
    def _():
        act_last = gate(h1_ref, 2 * q + 1, 1, cwg_ref, cwu_ref, cbg_ref, cbu_ref)
        o_ref[...] += _dot(act_last, wd_ref[half:, :])


def _ffn(xn, x1, w_up, conv_w, conv_b, w_down):
    nj = D_FF // FFN_TF
    prev = lambda q: jnp.maximum(q - 1, 0)
    return pl.pallas_call(
        _ffn_kernel,
        grid=(N_TOK // FFN_TM, nj),
        in_specs=[
            pl.BlockSpec((FFN_TM, D_MODEL), lambda i, q: (i, 0)),
            pl.BlockSpec((FFN_TM, D_MODEL), lambda i, q: (i, 0)),
            pl.BlockSpec((D_MODEL, FFN_TF), lambda i, q: (0, q)),
            pl.BlockSpec((D_MODEL, FFN_TF), lambda i, q: (0, nj + q)),
            pl.BlockSpec((CONV_WIDTH, FFN_TF), lambda i, q: (0, q)),
            pl.BlockSpec((CONV_WIDTH, FFN_TF), lambda i, q: (0, nj + q)),
            pl.BlockSpec((1, FFN_TF), lambda i, q: (0, q)),
            pl.BlockSpec((1, FFN_TF), lambda i, q: (0, nj + q)),
            pl.BlockSpec((FFN_TF, D_MODEL), lambda i, q: (q, 0)),
            pl.BlockSpec((CONV_WIDTH, FFN_TF), lambda i, q: (0, prev(q))),
            pl.BlockSpec((CONV_WIDTH, FFN_TF), lambda i, q: (0, nj + prev(q))),
            pl.BlockSpec((1, FFN_TF), lambda i, q: (0, prev(q))),
            pl.BlockSpec((1, FFN_TF), lambda i, q: (0, nj + prev(q))),
        ],
        out_specs=pl.BlockSpec((FFN_TM, D_MODEL), lambda i, q: (i, 0)),
        out_shape=jax.ShapeDtypeStruct((N_TOK, D_MODEL), F32),
        scratch_shapes=[pltpu.VMEM((2, FFN_TM + 8, FFN_TF // 2), F32), pltpu.VMEM((2, FFN_TM + 8, FFN_TF // 2), F32),
                        pltpu.VMEM((2 * nj, 2, 8, FFN_TF // 2), F32),
                        pltpu.VMEM((FFN_TF // 2, D_MODEL), BF16)],
        compiler_params=_cparams("arbitrary", "arbitrary"),
        name="ffn",
    )(xn, x1, w_up, w_up, conv_w, conv_w, conv_b, conv_b, w_down, conv_w, conv_w, conv_b, conv_b)


def _overlap_t():
    start = np.arange(N_CMP_PAD)[None, :] * CMP_STRIDE
    s0 = np.arange(N_SEL)[:, None] * SEL_BLOCK
    ov = (start < s0 + SEL_BLOCK) & (start + CMP_BLOCK > s0) & (np.arange(N_CMP_PAD)[None, :] < N_CMP_PAD - 1)
    return ov.astype(np.float32)


def _bf16_round(a):
    return np.asarray(a, np.float32).astype(BF16).astype(np.float32)


def _slope_features():
    sl = (np.power(2.0, -8.0 * np.arange(1, NSA_HEADS + 1) / NSA_HEADS).astype(np.float32)
          * np.float32(LOG2E)).astype(np.float32)
    s1 = _bf16_round(sl)
    s2 = _bf16_round(sl - s1)
    s3 = _bf16_round(sl - s1 - s2)
    rows = np.stack([s1, s2, s3, s1, s2, s3, -sl, np.zeros_like(sl)], axis=0)
    rows = rows.reshape(8, KV_GROUPS, REP).transpose(1, 0, 2)
    return np.repeat(rows, Q_TILE, axis=2).astype(np.float32)


def _window_mask():
    kl = np.arange(WIN_KEYS)[:, None]
    ql = np.arange(Q_TILE)[None, :]
    dist = ql + WINDOW - kl
    return np.where((dist >= 0) & (dist < WINDOW), 0.0, NEG_BIG).astype(np.float32)


def _diag_masks():
    kl = np.arange(SEL_TILE_KEYS)[None, :, None]
    ql = np.arange(Q_TILE)[None, None, :]
    off = np.arange(SEL_TILE_KEYS // Q_TILE)[:, None, None] * Q_TILE
    return np.where(kl > ql + off, NEG_BIG, 0.0).astype(np.float32)


def _layout_w_in(w_in):
    q_end = NSA_WIDTH
    kv_end = q_end + 6 * KV_COLS
    g_end = kv_end + NSA_HEADS * N_BRANCH
    w_in = w_in.astype(BF16)
    w_q = w_in[:, :q_end]
    w_kv = w_in[:, q_end:kv_end].reshape(D_MODEL, 6, KV_GROUPS, HEAD_DIM)
    w_kv = jnp.stack([w_kv[:, s] for s in (2, 4, 0, 1, 3, 5)], axis=2).reshape(D_MODEL, 6 * KV_COLS)
    w_z = w_in[:, g_end:]
    w_main = jnp.concatenate([w_z, w_q, w_kv], axis=1)
    w_g = w_in[:, kv_end:g_end].reshape(D_MODEL, KV_GROUPS, REP, N_BRANCH).transpose(0, 1, 3, 2)
    w_g = w_g.reshape(D_MODEL, KV_GROUPS, N_BRANCH * REP)
    w_g = jnp.pad(w_g, ((0, 0), (0, 0), (0, GATE_ROWS_PER_GROUP - N_BRANCH * REP)))
    w_g = jnp.pad(w_g.reshape(D_MODEL, KV_GROUPS * GATE_ROWS_PER_GROUP),
                  ((0, 0), (0, GATE_PAD - KV_GROUPS * GATE_ROWS_PER_GROUP)))
    return w_main, w_g


def _layout_compress(cmp_pos, cmp_w1, cmp_w2):
    pos2 = jnp.concatenate([cmp_pos[0], cmp_pos[1]], axis=1)
    w1 = cmp_w1.reshape(2, CMP_BLOCK, HEAD_DIM, CMP_HIDDEN)
    zero = jnp.zeros_like(w1[0])
    w1k = jnp.concatenate([w1[0], zero], axis=1)
    w1v = jnp.concatenate([zero, w1[1]], axis=1)
    w1p = jnp.concatenate([w1k, w1v], axis=2).astype(BF16)
    w1p = w1p.reshape(CMP_BLOCK // 2, 2 * LANES, 2 * CMP_HIDDEN)
    w2p = jnp.pad(cmp_w2, ((0, 0), (0, 0), (0, LANES - HEAD_DIM))).astype(BF16)
    return pos2, w1p, w2p


def kernel(x, attn_norm_w, w_in, q_norm_w, k_norm_w, cmp_pos, cmp_w1, cmp_w2, gmlp_ln_w, gmlp_ln_b,
           spatial_w, spatial_b, w_out, ffn_norm_w, w_up, conv_w, conv_b, w_down):
    x2 = x.reshape(N_TOK, D_MODEL)
    w_main, w_gate = _layout_w_in(w_in)
    proj, gl = _proj(x2, attn_norm_w.reshape(1, D_MODEL), w_main, w_gate)

    knw2 = jnp.concatenate([k_norm_w[1], k_norm_w[2]]).reshape(1, LANES)
    kcw = jnp.concatenate([k_norm_w[0], jnp.zeros((LANES - HEAD_DIM,), F32)]).reshape(1, LANES)
    pos2, w1p, w2p = _layout_compress(cmp_pos, cmp_w1, cmp_w2)
    ksel, kwin, vt, kc, vct, gt = _kprep(proj, gl, knw2, kcw, pos2, w1p, w2p)

    wq_b = jnp.broadcast_to(q_norm_w.reshape(HEAD_DIM, 1), (HEAD_DIM, Q_TILE))
    a, w_up_b, w_down_b = _attn(proj, ksel, kwin, vt, kc, vct, gt, jnp.asarray(_slope_features()),
                                jnp.asarray(_window_mask()), jnp.asarray(_diag_masks()),
                                jnp.asarray(_overlap_t(), dtype=BF16), wq_b, w_up, w_down)

    sbx = jnp.repeat(spatial_b.T, GMLP_WIDTH // GMLP_GROUPS, axis=1)
    x1, xn = _oproj(a, proj, x2, w_out.astype(BF16), ffn_norm_w.reshape(1, D_MODEL),
                    gmlp_ln_w.reshape(1, GMLP_WIDTH), gmlp_ln_b.reshape(1, GMLP_WIDTH), spatial_w, sbx)

    out = _ffn(xn, x1, w_up_b, conv_w, conv_b.reshape(1, 2 * D_FF), w_down_b)
    return out.reshape(BATCH, SEQ, D_MODEL)
```

```python
import numpy as np
import jax
import jax.numpy as jnp
from jax import lax
from jax.experimental import pallas as pl
from jax.experimental.pallas import tpu as pltpu

F32 = jnp.float32
BF16 = jnp.bfloat16

D_MODEL = 2048
BATCH = 4
SEQ = 4096
N_TOK = BATCH * SEQ
NSA_HEADS = 16
KV_GROUPS = 4
REP = NSA_HEADS // KV_GROUPS
HEAD_DIM = 64
NSA_WIDTH = NSA_HEADS * HEAD_DIM
GMLP_WIDTH = D_MODEL - NSA_WIDTH
GMLP_GROUPS = 8
GMLP_CHUNK = 128
CMP_BLOCK = 32
CMP_STRIDE = 16
CMP_HIDDEN = 256
N_CMP_PAD = SEQ // CMP_STRIDE
SEL_BLOCK = 64
N_SEL = SEQ // SEL_BLOCK
SEL_TOPK = 16
WINDOW = 512
K_BLOCK = 128
N_KB = SEQ // K_BLOCK
Q_TILE = 256
N_QT = SEQ // Q_TILE
N_BRANCH = 3
KV_COLS = KV_GROUPS * HEAD_DIM
D_FF = 5632
CONV_WIDTH = 3
NORM_EPS = 1e-6
LN_EPS = 1e-5
NEG_BIG = -1e30
SEL_BIG = 1e9

LANES = 128
VMEM_LIMIT_BYTES = 56 * 1024 * 1024

Z_COLS = 2 * GMLP_WIDTH
PROJ_COLS = Z_COLS + NSA_WIDTH + 6 * KV_COLS
GATE_PAD = LANES
GATE_ROWS_PER_GROUP = 16

PROJ_TM = 1024
PROJ_TN = 1536
OPROJ_TM = 512
FFN_TM = 512
FFN_TF = 512
SEL_TILE_BLOCKS = 4
SEL_TILE_KEYS = SEL_TILE_BLOCKS * K_BLOCK
MASK_LANE0 = 80
SUM_ROWS = 16
VT_ROWS = HEAD_DIM + SUM_ROWS
LOG2E = 1.4426950408889634
WIN_BLOCKS = (WINDOW + Q_TILE) // K_BLOCK
WIN_KEYS = WIN_BLOCKS * K_BLOCK
CAST_UP_ROWS = 16
CAST_DOWN_ROWS = 32


def _cparams(*sem):
    return pltpu.CompilerParams(dimension_semantics=sem, vmem_limit_bytes=VMEM_LIMIT_BYTES)


def _dot(a, b):
    return jnp.dot(a, b, preferred_element_type=F32)


def _proj_kernel(x_ref, nw_ref, w_ref, wg_ref, o_ref, og_ref, h_ref):
    @pl.when(pl.program_id(1) == 0)
    def _():
        x = x_ref[...]
        ms = jnp.mean(x * x, axis=-1, keepdims=True)
        h = (x * lax.rsqrt(ms + NORM_EPS) * nw_ref[...]).astype(BF16)
        h_ref[...] = h
        og_ref[...] = _dot(h, wg_ref[...])

    o_ref[...] = _dot(h_ref[...], w_ref[...])


def _proj(x2, nw, w_main, w_gate):
    return pl.pallas_call(
        _proj_kernel,
        grid=(N_TOK // PROJ_TM, PROJ_COLS // PROJ_TN),
        in_specs=[
            pl.BlockSpec((PROJ_TM, D_MODEL), lambda i, j: (i, 0)),
            pl.BlockSpec((1, D_MODEL), lambda i, j: (0, 0)),
            pl.BlockSpec((D_MODEL, PROJ_TN), lambda i, j: (0, j)),
            pl.BlockSpec((D_MODEL, GATE_PAD), lambda i, j: (0, 0)),
        ],
        out_specs=[
            pl.BlockSpec((PROJ_TM, PROJ_TN), lambda i, j: (i, j)),
            pl.BlockSpec((PROJ_TM, GATE_PAD), lambda i, j: (i, 0)),
        ],
        out_shape=[
            jax.ShapeDtypeStruct((N_TOK, PROJ_COLS), F32),
            jax.ShapeDtypeStruct((N_TOK, GATE_PAD), F32),
        ],
        scratch_shapes=[pltpu.VMEM((PROJ_TM, D_MODEL), BF16)],
        compiler_params=_cparams("arbitrary", "arbitrary"),
        name="proj",
    )(x2, nw, w_main, w_gate)


def _pos_features(lane, first, f1, f2):
    return jnp.where(lane < first + 3, f1, jnp.where(lane < first + 6, f2,
                                                     jnp.where(lane == first + 6, 1.0, 0.0)))


def _kprep_kernel(s0_ref, s1_ref, s2_ref, gl_ref, knw_ref, kcw_ref, pos_ref, w1_ref, w2_ref,
                  ksel_ref, kwin_ref, vt_ref, kc_ref, vct_ref, gt_ref, ha_ref, hb_ref):
    lane = lax.broadcasted_iota(jnp.int32, (K_BLOCK, LANES), 1)
    row = lax.broadcasted_iota(jnp.int32, (K_BLOCK, LANES), 0)
    lo = lane < HEAD_DIM
    inv_d = 1.0 / HEAD_DIM
    ones_rows = jnp.where(lax.broadcasted_iota(jnp.int32, (SUM_ROWS, K_BLOCK), 0) == 0, 1.0, 0.0)

    def chunk(c, carry):
        r0 = pl.multiple_of(c * K_BLOCK, K_BLOCK)
        x = s0_ref[pl.ds(r0, K_BLOCK), :]
        sq = x * x
        s_lo = jnp.sum(jnp.where(lo, sq, 0.0), axis=1, keepdims=True)
        s_hi = jnp.sum(jnp.where(lo, 0.0, sq), axis=1, keepdims=True)
        ms = jnp.where(lo, s_lo, s_hi) * inv_d
        y = x * lax.rsqrt(ms + NORM_EPS) * knw_ref[...]
        pos = r0 + row
        blk = lax.shift_right_logical(pos, 6)
        f1 = (blk * SEL_BLOCK).astype(F32)
        f2 = (pos & (SEL_BLOCK - 1)).astype(F32)
        onehot = jnp.where(lane - MASK_LANE0 == (blk & (SEL_TILE_BLOCKS * 2 - 1)), 1.0, 0.0)
        feat_sel = jnp.where(lane < MASK_LANE0, _pos_features(lane, HEAD_DIM, f1, f2), onehot)
        ksel_ref[c] = jnp.where(lo, y, feat_sel).astype(BF16)
        kwin_ref[c] = jnp.where(lo, _pos_features(lane, 0, f1, f2), y).astype(BF16)
        v_t = s2_ref[pl.ds(r0, K_BLOCK), :].T
        vt_ref[c] = jnp.concatenate(
            [v_t[0:HEAD_DIM], ones_rows, v_t[HEAD_DIM:], ones_rows], axis=0).astype(BF16)
        return carry

    lax.fori_loop(0, N_KB, chunk, 0, unroll=4)

    @pl.when(pl.program_id(1) == 0)
    def _():
        def gchunk(c, carry):
            r0 = pl.multiple_of(c * K_BLOCK, K_BLOCK)
            gt_ref[c] = jax.nn.sigmoid(gl_ref[pl.ds(r0, K_BLOCK), :]).T
            return carry

        lax.fori_loop(0, N_KB, gchunk, 0, unroll=4)

    ha_ref[...] = jnp.zeros_like(ha_ref)
    hb_ref[...] = jnp.zeros_like(hb_ref)
    for tp in range(CMP_STRIDE // 2):
        rows = [s1_ref[pl.ds(2 * tp + e, N_CMP_PAD, stride=CMP_STRIDE), :] for e in range(2)]
        xa = jnp.concatenate([(rows[e] + pos_ref[2 * tp + e:2 * tp + e + 1, :]).astype(BF16)
                              for e in range(2)], axis=1)
        xb = jnp.concatenate([(rows[e] + pos_ref[CMP_STRIDE + 2 * tp + e:CMP_STRIDE + 2 * tp + e + 1, :])
                              .astype(BF16) for e in range(2)], axis=1)
        ha_ref[...] += _dot(xa, w1_ref[tp])
        hb_ref[0:N_CMP_PAD, :] += _dot(xb, w1_ref[CMP_STRIDE // 2 + tp])
    hid = jax.nn.gelu(ha_ref[...] + hb_ref[1:N_CMP_PAD + 1, :]).astype(BF16)
    kc = _dot(hid[:, :CMP_HIDDEN], w2_ref[0])
    vc = _dot(hid[:, CMP_HIDDEN:], w2_ref[1])
    ms = jnp.sum(kc * kc, axis=1, keepdims=True) * inv_d
    kcn = kc * lax.rsqrt(ms + NORM_EPS) * kcw_ref[...]
    ci = lax.broadcasted_iota(jnp.int32, (N_CMP_PAD, LANES), 0)
    cl = lax.broadcasted_iota(jnp.int32, (N_CMP_PAD, LANES), 1)
    c1 = (lax.shift_right_logical(ci, 2) * SEL_BLOCK).astype(F32)
    c2 = ((ci & 3) * CMP_STRIDE).astype(F32) + (CMP_BLOCK - 1) * 0.5
    kc_ref[...] = jnp.where(cl < HEAD_DIM, kcn, _pos_features(cl, HEAD_DIM, c1, c2)).astype(BF16)
    vct_ref[...] = vc.T[0:HEAD_DIM, :].astype(BF16)


def _kprep(proj, gl, knw2, kcw, pos2, w1p, w2p):
    kv0 = (Z_COLS + NSA_WIDTH) // LANES
    slab = lambda s: pl.BlockSpec((SEQ, LANES), lambda b, g, s=s: (b, kv0 + 3 * g + s))
    full = lambda shape: pl.BlockSpec(shape, lambda b, g: (0,) * len(shape))
    per_bg = lambda shape: pl.BlockSpec((None, None) + shape, lambda b, g: (b, g) + (0,) * len(shape))
    bg_shape = lambda shape, dt: jax.ShapeDtypeStruct((BATCH, KV_GROUPS) + shape, dt)
    return pl.pallas_call(
        _kprep_kernel,
        grid=(BATCH, KV_GROUPS),
        in_specs=[
            slab(0), slab(1), slab(2),
            pl.BlockSpec((SEQ, GATE_PAD), lambda b, g: (b, 0)),
            full((1, LANES)), full((1, LANES)), full((CMP_BLOCK, LANES)),
            full((CMP_BLOCK // 2, 2 * LANES, 2 * CMP_HIDDEN)), full((2, CMP_HIDDEN, LANES)),
        ],
        out_specs=[
            per_bg((N_KB, K_BLOCK, LANES)),
            per_bg((N_KB, K_BLOCK, LANES)),
            per_bg((N_KB, 2 * VT_ROWS, K_BLOCK)),
            per_bg((N_CMP_PAD, LANES)),
            per_bg((HEAD_DIM, N_CMP_PAD)),
            pl.BlockSpec((None, N_KB, GATE_PAD, K_BLOCK), lambda b, g: (b, 0, 0, 0)),
        ],
        out_shape=[
            bg_shape((N_KB, K_BLOCK, LANES), BF16),
            bg_shape((N_KB, K_BLOCK, LANES), BF16),
            bg_shape((N_KB, 2 * VT_ROWS, K_BLOCK), BF16),
            bg_shape((N_CMP_PAD, LANES), BF16),
            bg_shape((HEAD_DIM, N_CMP_PAD), BF16),
            jax.ShapeDtypeStruct((BATCH, N_KB, GATE_PAD, K_BLOCK), F32),
        ],
        scratch_shapes=[
            pltpu.VMEM((N_CMP_PAD, 2 * CMP_HIDDEN), F32),
            pltpu.VMEM((N_CMP_PAD + 8, 2 * CMP_HIDDEN), F32),
        ],
        compiler_params=_cparams("arbitrary", "arbitrary"),
        name="kprep",
    )(proj, proj, proj, gl, knw2, kcw, pos2, w1p, w2p)


def _attn_kernel(q_ref, ksel_ref, kwin_ref, vt_ref, kc_ref, vct_ref, gt_ref, sf_ref, wm_ref, dm_ref,
                 ov_ref, wq_ref, wup_ref, wdn_ref, o_ref, wup_o_ref, wdn_o_ref,
                 sel_ref, qs_ref, sa_ref, sb_ref, tiles_ref):
    wup_o_ref[...] = wup_ref[...].astype(BF16)
    wdn_o_ref[...] = wdn_ref[...].astype(BF16)

    qt = pl.program_id(2)
    t_i = qt * Q_TILE + lax.broadcasted_iota(jnp.int32, (1, Q_TILE), 1)
    head = lambda a, r: a[:, r * Q_TILE:(r + 1) * Q_TILE]
    heads = lambda parts: jnp.concatenate(parts, axis=1)

    q_t = q_ref[...].T
    parts = []
    for r in range(REP):
        xq = q_t[r * HEAD_DIM:(r + 1) * HEAD_DIM, :]
        ms = jnp.sum(xq * xq, axis=0, keepdims=True) * (1.0 / HEAD_DIM)
        parts.append(xq * lax.rsqrt(ms + NORM_EPS) * wq_ref[...] * (HEAD_DIM ** -0.5) * LOG2E)
    qn = heads(parts)
    sf = sf_ref[...]
    sf = jnp.where(lax.broadcasted_iota(jnp.int32, sf.shape, 0) == 6, sf * (qt * Q_TILE).astype(F32), sf)
    wide = REP * Q_TILE
    pad = jnp.zeros((HEAD_DIM - 8, wide), F32)
    qs_ref[...] = jnp.concatenate([qn, sf, pad], axis=0).astype(BF16)
    q_win = jnp.concatenate([sf, pad, qn], axis=0).astype(BF16)

    sc = _dot(kc_ref[...], qs_ref[...])
    ci = lax.broadcasted_iota(jnp.int32, (N_CMP_PAD, Q_TILE), 0)
    cm = jnp.where(ci * CMP_STRIDE + (CMP_BLOCK - 1) <= t_i, 0.0, NEG_BIG)
    any_valid = t_i >= CMP_BLOCK - 1
    pcs = []
    for r in range(REP):
        s = head(sc, r) + cm
        e = jnp.exp2(s - jnp.max(s, axis=0, keepdims=True))
        d = jnp.sum(e, axis=0, keepdims=True)
        pcs.append((e * jnp.where(any_valid, 1.0 / d, 0.0)).astype(BF16))
    oc_imp = _dot(jnp.concatenate([vct_ref[...], ov_ref[...]], axis=0), heads(pcs))
    o_c = oc_imp[0:HEAD_DIM]
    imp4 = oc_imp[HEAD_DIM:]
    imp = (head(imp4, 0) + head(imp4, 1)) + (head(imp4, 2) + head(imp4, 3))

    jj = lax.broadcasted_iota(jnp.int32, (N_SEL, Q_TILE), 0)
    cur = lax.shift_right_logical(t_i, 6)
    valid = jj <= cur
    sel_ref[...] = jnp.where(valid, 1.0, 0.0)
    forced = (jj == 0) | (jj == cur) | (jj == cur - 1)
    score = jnp.where(valid, jnp.where(forced, SEL_BIG, imp), -SEL_BIG)

    def rank_rows(n8):
        def branch():
            rows8 = [score[8 * v:8 * v + 8, :] for v in range(n8)]
            j8 = lax.broadcasted_iota(jnp.int32, (8, Q_TILE), 0)
            ranks = [jnp.zeros((8, Q_TILE), F32) for _ in rows8]
            for jp in range(8 * n8):
                row = score[jp:jp + 1, :]
                for v, blk in enumerate(rows8):
                    if 8 * v > jp:
                        ahead = jnp.where(row >= blk, 1.0, 0.0)
                    elif 8 * v + 7 <= jp:
                        ahead = jnp.where(row > blk, 1.0, 0.0)
                    else:
                        ahead = jnp.where(j8 + 8 * v > jp, jnp.where(row >= blk, 1.0, 0.0),
                                          jnp.where(row > blk, 1.0, 0.0))
                    ranks[v] = ranks[v] + ahead
            rank = jnp.concatenate(ranks, axis=0)
            sel_ref[0:8 * n8, :] = jnp.where(valid[0:8 * n8] & (rank < SEL_TOPK), 1.0, 0.0)
        return branch

    n8_needed = (qt * Q_TILE + Q_TILE - 1) // (8 * SEL_BLOCK) + 1
    lax.switch(n8_needed - 1, [(lambda: None) if n8 * 8 <= SEL_TOPK else rank_rows(n8)
                               for n8 in range(1, N_SEL // 8 + 1)])

    sub = SEL_TILE_KEYS // SEL_BLOCK

    def scores_into(kt, s_ref):
        member = sel_ref[pl.ds(kt * sub, sub), :]
        mrows = (member - 1.0) * (-NEG_BIG)
        qs_ref[MASK_LANE0:MASK_LANE0 + 16, :] = jnp.concatenate(
            [heads([mrows] * REP), jnp.zeros((16 - sub, wide), F32)], axis=0).astype(BF16)
        k_tile = ksel_ref[pl.ds(kt * SEL_TILE_BLOCKS, SEL_TILE_BLOCKS)].reshape(SEL_TILE_KEYS, LANES)
        s_ref[...] = _dot(k_tile, qs_ref[...])

    def absorb(kt, s_ref, carry, diagonal):
        m_run, acc = carry
        kb0 = kt * SEL_TILE_BLOCKS
        if diagonal:
            causal = dm_ref[qt % (SEL_TILE_KEYS // Q_TILE)]
        v_t = jnp.concatenate([vt_ref[kb0 + x, 0:VT_ROWS, :] for x in range(SEL_TILE_BLOCKS)], axis=1)
        ms_, accs = [], []
        for r in range(REP):
            sr = s_ref[:, r * Q_TILE:(r + 1) * Q_TILE]
            if diagonal:
                sr = sr + causal
            m_old = head(m_run, r)
            m_new = jnp.maximum(m_old, jnp.max(sr, axis=0, keepdims=True))
            p = jnp.exp2(sr - m_new).astype(BF16)
            accs.append(head(acc, r) * jnp.exp2(m_old - m_new) + _dot(v_t, p))
            ms_.append(m_new)
        return heads(ms_), heads(accs)

    carry = (jnp.full((1, wide), NEG_BIG, F32), jnp.zeros((VT_ROWS, wide), F32))
    n_full = (qt * Q_TILE) // SEL_TILE_KEYS

    n_sel_tiles = SEQ // SEL_TILE_KEYS
    tr = lax.broadcasted_iota(jnp.int32, (n_sel_tiles, N_SEL), 0)
    tc = lax.broadcasted_iota(jnp.int32, (n_sel_tiles, N_SEL), 1)
    tile_of_block = jnp.where(lax.shift_right_logical(tc, 3) == tr, 1.0, 0.0).astype(BF16)
    per_query = _dot(tile_of_block, sel_ref[...].astype(BF16))
    per_tile = _dot(per_query.astype(BF16), jnp.ones((Q_TILE, LANES), BF16))
    bit = lax.shift_left(1, lax.broadcasted_iota(jnp.int32, (n_sel_tiles, LANES), 0)).astype(F32)
    used_bits = jnp.sum(jnp.where(per_tile > 0.5, bit, 0.0), axis=0, keepdims=True)[0, 0].astype(jnp.int32)

    kb_first = qt * (Q_TILE // K_BLOCK) - WINDOW // K_BLOCK
    kbs = [jnp.maximum(kb_first + x, 0) for x in range(WIN_BLOCKS)]
    sw = _dot(jnp.concatenate([kwin_ref[kb] for kb in kbs], axis=0), q_win)

    n_visit = jnp.int32(0)
    for kt in range(n_sel_tiles):
        tiles_ref[n_visit] = kt
        take = (kt < n_full) & ((lax.shift_right_logical(used_bits, kt) & 1) == 1)
        n_visit = n_visit + jnp.where(take, 1, 0)
    tiles_ref[n_visit] = n_full

    scores_into(tiles_ref[0], sa_ref)
    wm = jnp.concatenate(
        [wm_ref[x * K_BLOCK:(x + 1) * K_BLOCK, :] + jnp.where(kb_first + x < 0, NEG_BIG, 0.0)
         for x in range(WIN_BLOCKS)], axis=0)
    pws = []
    for r in range(REP):
        s = head(sw, r) + wm
        pws.append(jnp.exp2(s - jnp.max(s, axis=0, keepdims=True)).astype(BF16))
    vw_t = jnp.concatenate([vt_ref[kb, VT_ROWS:2 * VT_ROWS, :] for kb in kbs], axis=1)
    acc_w = _dot(vw_t, heads(pws))
    o_w = acc_w[0:HEAD_DIM] / acc_w[HEAD_DIM:HEAD_DIM + 1]

    def tile_pair(i, carry):
        scores_into(tiles_ref[2 * i + 1], sb_ref)
        carry = absorb(tiles_ref[2 * i], sa_ref, carry, False)
        scores_into(tiles_ref[2 * i + 2], sa_ref)
        return absorb(tiles_ref[2 * i + 1], sb_ref, carry, False)

    carry = lax.fori_loop(0, n_visit // 2, tile_pair, carry)
    last = 2 * (n_visit // 2)

    def odd_tail(carry):
        scores_into(n_full, sb_ref)
        return absorb(n_full, sb_ref, absorb(tiles_ref[last], sa_ref, carry, False), True)

    def even_tail(carry):
        return absorb(n_full, sa_ref, carry, True)

    _, acc_s = lax.cond(n_visit % 2 == 1, odd_tail, even_tail, carry)
    o_s = acc_s[0:HEAD_DIM] / acc_s[HEAD_DIM:HEAD_DIM + 1]

    gates = jnp.concatenate([gt_ref[x] for x in range(Q_TILE // K_BLOCK)], axis=1)
    outs = []
    for r in range(REP):
        outs.append(gates[r:r + 1, :] * head(o_c, r)
                    + gates[REP + r:REP + r + 1, :] * head(o_s, r)
                    + gates[2 * REP + r:2 * REP + r + 1, :] * head(o_w, r))
    o_ref[...] = jnp.concatenate(outs, axis=0).T.astype(BF16)


def _attn(proj, ksel, kwin, vt, kc, vct, gt, sf, wm, dm, ov_t, wq_b, w_up, w_down):
    q0 = Z_COLS // (REP * HEAD_DIM)
    kvspec = lambda shape: pl.BlockSpec((None, None) + shape, lambda b, g, qt: (b, g) + (0,) * len(shape))
    const = lambda shape: pl.BlockSpec(shape, lambda b, g, qt: (0,) * len(shape))
    n_steps = BATCH * KV_GROUPS * N_QT
    assert 2 * (D_MODEL // CAST_UP_ROWS) == n_steps and D_FF // CAST_DOWN_ROWS <= n_steps
    step = lambda b, g, qt: (b * KV_GROUPS + g) * N_QT + qt
    wup_spec = pl.BlockSpec((CAST_UP_ROWS, D_FF), lambda b, g, qt: (step(b, g, qt) // 2, step(b, g, qt) % 2))
    wdn_spec = pl.BlockSpec((CAST_DOWN_ROWS, D_MODEL), lambda b, g, qt: (
        jnp.minimum(step(b, g, qt), D_FF // CAST_DOWN_ROWS - 1), 0))
    return pl.pallas_call(
        _attn_kernel,
        grid=(BATCH, KV_GROUPS, N_QT),
        in_specs=[
            pl.BlockSpec((Q_TILE, REP * HEAD_DIM), lambda b, g, qt: (b * N_QT + qt, q0 + g)),
            kvspec((N_KB, K_BLOCK, LANES)),
            kvspec((N_KB, K_BLOCK, LANES)),
            kvspec((N_KB, 2 * VT_ROWS, K_BLOCK)),
            kvspec((N_CMP_PAD, LANES)),
            kvspec((HEAD_DIM, N_CMP_PAD)),
            pl.BlockSpec((None, Q_TILE // K_BLOCK, GATE_ROWS_PER_GROUP, K_BLOCK), lambda b, g, qt: (b, qt, g, 0)),
            pl.BlockSpec((None, 8, REP * Q_TILE), lambda b, g, qt: (g, 0, 0)),
            const((WIN_KEYS, Q_TILE)),
            const((SEL_TILE_KEYS // Q_TILE, SEL_TILE_KEYS, Q_TILE)),
            const((N_SEL, N_CMP_PAD)),
            const((HEAD_DIM, Q_TILE)),
            wup_spec,
            wdn_spec,
        ],
        out_specs=[pl.BlockSpec((Q_TILE, REP * HEAD_DIM), lambda b, g, qt: (b * N_QT + qt, g)),
                   wup_spec, wdn_spec],
        out_shape=[jax.ShapeDtypeStruct((N_TOK, NSA_WIDTH), BF16),
                   jax.ShapeDtypeStruct((D_MODEL, 2 * D_FF), BF16),
                   jax.ShapeDtypeStruct((D_FF, D_MODEL), BF16)],
        scratch_shapes=[pltpu.VMEM((N_SEL, Q_TILE), F32), pltpu.VMEM((LANES, REP * Q_TILE), BF16),
                        pltpu.VMEM((SEL_TILE_KEYS, REP * Q_TILE), F32),
                        pltpu.VMEM((SEL_TILE_KEYS, REP * Q_TILE), F32),
                        pltpu.SMEM((SEQ // SEL_TILE_KEYS + 1,), jnp.int32)],
        compiler_params=_cparams("arbitrary", "arbitrary", "arbitrary"),
        name="attn",
    )(proj, ksel, kwin, vt, kc, vct, gt, sf, wm, dm, ov_t, wq_b, w_up, w_down)


def _spatial_gating(z, lnw, lnb, sw_ref, sbx):
    ge = jax.nn.gelu(z)
    u = ge[:, :GMLP_WIDTH]
    v = ge[:, GMLP_WIDTH:]
    mu = jnp.mean(v, axis=-1, keepdims=True)
    var = jnp.mean(jnp.square(v - mu), axis=-1, keepdims=True)
    vn = ((v - mu) * lax.rsqrt(var + LN_EPS) * lnw + lnb).astype(BF16)
    ti = lax.broadcasted_iota(jnp.int32, (GMLP_CHUNK, GMLP_CHUNK), 0)
    si = lax.broadcasted_iota(jnp.int32, (GMLP_CHUNK, GMLP_CHUNK), 1)
    ws = [jnp.where(ti >= si, sw_ref[gg], 0.0).astype(BF16) for gg in range(GMLP_GROUPS)]
    gd = GMLP_WIDTH // GMLP_GROUPS
    outs = []
    for c in range(z.shape[0] // GMLP_CHUNK):
        rows = slice(c * GMLP_CHUNK, (c + 1) * GMLP_CHUNK)
        mix = jnp.concatenate(
            [_dot(ws[gg], vn[rows, gg * gd:(gg + 1) * gd]) for gg in range(GMLP_GROUPS)], axis=1)
        outs.append((u[rows, :] * (mix + sbx)).astype(BF16))
    return jnp.concatenate(outs, axis=0)


def _oproj_kernel(a_ref, z_ref, x_ref, wa_ref, wb_ref, nw_ref, lnw_ref, lnb_ref, sw_ref, sbx_ref,
                  x1_ref, xn_ref):
    attn_part = _dot(a_ref[...], wa_ref[...])
    b = _spatial_gating(z_ref[...], lnw_ref[...], lnb_ref[...], sw_ref, sbx_ref[...])
    y = x_ref[...] + (attn_part + _dot(b, wb_ref[...]))
    x1_ref[...] = y
    ms = jnp.mean(y * y, axis=-1, keepdims=True)
    xn_ref[...] = (y * lax.rsqrt(ms + NORM_EPS) * nw_ref[...]).astype(BF16)


def _oproj(a, proj, x2, w_out, nw, lnw, lnb, sw, sbx):
    return pl.pallas_call(
        _oproj_kernel,
        grid=(N_TOK // OPROJ_TM,),
        in_specs=[
            pl.BlockSpec((OPROJ_TM, NSA_WIDTH), lambda i: (i, 0)),
            pl.BlockSpec((OPROJ_TM, Z_COLS), lambda i: (i, 0)),
            pl.BlockSpec((OPROJ_TM, D_MODEL), lambda i: (i, 0)),
            pl.BlockSpec((NSA_WIDTH, D_MODEL), lambda i: (0, 0)),
            pl.BlockSpec((GMLP_WIDTH, D_MODEL), lambda i: (1, 0)),
            pl.BlockSpec((1, D_MODEL), lambda i: (0, 0)),
            pl.BlockSpec((1, GMLP_WIDTH), lambda i: (0, 0)),
            pl.BlockSpec((1, GMLP_WIDTH), lambda i: (0, 0)),
            pl.BlockSpec((GMLP_GROUPS, GMLP_CHUNK, GMLP_CHUNK), lambda i: (0, 0, 0)),
            pl.BlockSpec((GMLP_CHUNK, GMLP_WIDTH), lambda i: (0, 0)),
        ],
        out_specs=[
            pl.BlockSpec((OPROJ_TM, D_MODEL), lambda i: (i, 0)),
            pl.BlockSpec((OPROJ_TM, D_MODEL), lambda i: (i, 0)),
        ],
        out_shape=[
            jax.ShapeDtypeStruct((N_TOK, D_MODEL), F32),
            jax.ShapeDtypeStruct((N_TOK, D_MODEL), BF16),
        ],
        compiler_params=_cparams("arbitrary"),
        name="oproj",
    )(a, proj, x2, w_out, w_out, nw, lnw, lnb, sw, sbx)


def _ffn_kernel(xn_ref, x1_ref, wg_ref, wu_ref, cwg_ref, cwu_ref, cbg_ref, cbu_ref, wd_ref,
                cwg_p_ref, cwu_p_ref, cbg_p_ref, cbu_p_ref, o_ref, h0_ref, h1_ref, carry_ref, wd_p_ref):
    i = pl.program_id(0)
    q = pl.program_id(1)
    nq = pl.num_programs(1)
    half = FFN_TF // 2

    @pl.when((i == 0) & (q == 0))
    def _():
        carry_ref[...] = jnp.zeros_like(carry_ref)
        h1_ref[...] = jnp.zeros_like(h1_ref)
        wd_p_ref[...] = jnp.zeros_like(wd_p_ref)

    @pl.when(q == 0)
    def _():
        o_ref[...] = x1_ref[...]

    seq_start = i % (SEQ // FFN_TM) == 0
    xn = xn_ref[...]

    def produce(c, h_ref):
        h_ref[0, 8:, :] = _dot(xn, wg_ref[:, c * half:(c + 1) * half])
        h_ref[1, 8:, :] = _dot(xn, wu_ref[:, c * half:(c + 1) * half])

    def conv(h_ref, part, tile, cw, cb):
        h_ref[part, 0:8, :] = jnp.where(seq_start, 0.0, carry_ref[tile, part])
        carry_ref[tile, part] = h_ref[part, FFN_TM:FFN_TM + 8, :]
        taps = [h_ref[part, 8 - k:8 - k + FFN_TM, :] for k in (2, 1, 0)]
        return cb + ((cw[0:1, :] * taps[0] + cw[1:2, :] * taps[1]) + cw[2:3, :] * taps[2])

    def gate(h_ref, tile, c, cwg, cwu, cbg, cbu, scale=None):
        cols = slice(c * half, (c + 1) * half)
        cg = conv(h_ref, 0, tile, cwg[:, cols], cbg[:, cols])
        cu = conv(h_ref, 1, tile, cwu[:, cols], cbu[:, cols])
        act = jax.nn.silu(cg) * cu
        if scale is not None:
            act = act * scale
        return act.astype(BF16)

    act_prev = gate(h1_ref, jnp.where(q == 0, 2 * nq - 1, 2 * q - 1), 1, cwg_p_ref, cwu_p_ref, cbg_p_ref,
                    cbu_p_ref, scale=jnp.where(q == 0, 0.0, 1.0))
    produce(0, h0_ref)
    down_prev = _dot(act_prev, wd_p_ref[...])
    act_cur = gate(h0_ref, 2 * q, 0, cwg_ref, cwu_ref, cbg_ref, cbu_ref)
    produce(1, h1_ref)
    o_ref[...] += down_prev + _dot(act_cur, wd_ref[0:half, :])
    wd_p_ref[...] = wd_ref[half:, :]

    @pl.when(q == nq - 1)
    def _():
        act_last = gate(h1_ref, 2 * q + 1, 1, cwg_ref, cwu_ref, cbg_ref, cbu_ref)
        o_ref[...] += _dot(act_last, wd_ref[half:, :])


def _ffn(xn, x1, w_up, conv_w, conv_b, w_down):
    nj = D_FF // FFN_TF
    prev = lambda q: jnp.maximum(q - 1, 0)
    return pl.pallas_call(
        _ffn_kernel,
        grid=(N_TOK // FFN_TM, nj),
        in_specs=[
            pl.BlockSpec((FFN_TM, D_MODEL), lambda i, q: (i, 0)),
            pl.BlockSpec((FFN_TM, D_MODEL), lambda i, q: (i, 0)),
            pl.BlockSpec((D_MODEL, FFN_TF), lambda i, q: (0, q)),
            pl.BlockSpec((D_MODEL, FFN_TF), lambda i, q: (0, nj + q)),
            pl.BlockSpec((CONV_WIDTH, FFN_TF), lambda i, q: (0, q)),
            pl.BlockSpec((CONV_WIDTH, FFN_TF), lambda i, q: (0, nj + q)),
            pl.BlockSpec((1, FFN_TF), lambda i, q: (0, q)),
            pl.BlockSpec((1, FFN_TF), lambda i, q: (0, nj + q)),
            pl.BlockSpec((FFN_TF, D_MODEL), lambda i, q: (q, 0)),
            pl.BlockSpec((CONV_WIDTH, FFN_TF), lambda i, q: (0, prev(q))),
            pl.BlockSpec((CONV_WIDTH, FFN_TF), lambda i, q: (0, nj + prev(q))),
            pl.BlockSpec((1, FFN_TF), lambda i, q: (0, prev(q))),
            pl.BlockSpec((1, FFN_TF), lambda i, q: (0, nj + prev(q))),
        ],
        out_specs=pl.BlockSpec((FFN_TM, D_MODEL), lambda i, q: (i, 0)),
        out_shape=jax.ShapeDtypeStruct((N_TOK, D_MODEL), F32),
        scratch_shapes=[pltpu.VMEM((2, FFN_TM + 8, FFN_TF // 2), F32), pltpu.VMEM((2, FFN_TM + 8, FFN_TF // 2), F32),
                        pltpu.VMEM((2 * nj, 2, 8, FFN_TF // 2), F32),
                        pltpu.VMEM((FFN_TF // 2, D_MODEL), BF16)],
        compiler_params=_cparams("arbitrary", "arbitrary"),
        name="ffn",
    )(xn, x1, w_up, w_up, conv_w, conv_w, conv_b, conv_b, w_down, conv_w, conv_w, conv_b, conv_b)


def _overlap_t():
    start = np.arange(N_CMP_PAD)[None, :] * CMP_STRIDE
    s0 = np.arange(N_SEL)[:, None] * SEL_BLOCK
    ov = (start < s0 + SEL_BLOCK) & (start + CMP_BLOCK > s0) & (np.arange(N_CMP_PAD)[None, :] < N_CMP_PAD - 1)
    return ov.astype(np.float32)


def _bf16_round(a):
    return np.asarray(a, np.float32).astype(BF16).astype(np.float32)


def _slope_features():
    sl = (np.power(2.0, -8.0 * np.arange(1, NSA_HEADS + 1) / NSA_HEADS).astype(np.float32)
          * np.float32(LOG2E)).astype(np.float32)
    s1 = _bf16_round(sl)
    s2 = _bf16_round(sl - s1)
    s3 = _bf16_round(sl - s1 - s2)
    rows = np.stack([s1, s2, s3, s1, s2, s3, -sl, np.zeros_like(sl)], axis=0)
    rows = rows.reshape(8, KV_GROUPS, REP).transpose(1, 0, 2)
    return np.repeat(rows, Q_TILE, axis=2).astype(np.float32)


def _window_mask():
    kl = np.arange(WIN_KEYS)[:, None]
    ql = np.arange(Q_TILE)[None, :]
    dist = ql + WINDOW - kl
    return np.where((dist >= 0) & (dist < WINDOW), 0.0, NEG_BIG).astype(np.float32)


def _diag_masks():
    kl = np.arange(SEL_TILE_KEYS)[None, :, None]
    ql = np.arange(Q_TILE)[None, None, :]
    off = np.arange(SEL_TILE_KEYS // Q_TILE)[:, None, None] * Q_TILE
    return np.where(kl > ql + off, NEG_BIG, 0.0).astype(np.float32)


W_IN_KV_END = NSA_WIDTH + 6 * KV_COLS
W_IN_Z0 = W_IN_KV_END + NSA_HEADS * N_BRANCH
W_IN_COLS = W_IN_Z0 + Z_COLS
W_IN_COLS_PAD = -(-W_IN_COLS // LANES) * LANES
WPREP_ROWS = 256


def _wprep_kernel(w_ref, o_ref):
    lane = lax.broadcasted_iota(jnp.int32, (WPREP_ROWS, LANES), 1)
    blk = lambda c: w_ref[:, c * LANES:(c + 1) * LANES]
    z_blk, z_off = W_IN_Z0 // LANES, W_IN_Z0 % LANES
    for c in range(Z_COLS // LANES):
        v = jnp.where(lane >= z_off, blk(z_blk + c), blk(z_blk + c + 1))
        o_ref[:, c * LANES:(c + 1) * LANES] = pltpu.roll(v, LANES - z_off, axis=1).astype(BF16)
    o_ref[:, Z_COLS:Z_COLS + NSA_WIDTH] = w_ref[:, 0:NSA_WIDTH].astype(BF16)

    def half(s, g, high):
        col = NSA_WIDTH + s * KV_COLS + g * HEAD_DIM
        v = blk(col // LANES)
        return v if (col % LANES != 0) == high else pltpu.roll(v, HEAD_DIM, axis=1)

    for g in range(KV_GROUPS):
        for k, (s_lo, s_hi) in enumerate(((2, 4), (0, 1), (3, 5))):
            c0 = Z_COLS + NSA_WIDTH + (3 * g + k) * LANES
            o_ref[:, c0:c0 + LANES] = jnp.where(lane < HEAD_DIM, half(s_lo, g, False), half(s_hi, g, True)).astype(BF16)


def _wprep(w_in):
    return pl.pallas_call(
        _wprep_kernel,
        grid=(D_MODEL // WPREP_ROWS,),
        in_specs=[pl.BlockSpec((WPREP_ROWS, W_IN_COLS_PAD), lambda i: (i, 0))],
        out_specs=pl.BlockSpec((WPREP_ROWS, PROJ_COLS), lambda i: (i, 0)),
        out_shape=jax.ShapeDtypeStruct((D_MODEL, PROJ_COLS), BF16),
        compiler_params=_cparams("arbitrary"),
    )(w_in)


def _layout_w_in(w_in):
    assert w_in.shape == (D_MODEL, W_IN_COLS) and W_IN_Z0 % LANES != 0
    w_main = _wprep(w_in)
    w_g = w_in[:, W_IN_KV_END:W_IN_Z0].astype(BF16)
    w_g = w_g.reshape(D_MODEL, KV_GROUPS, REP, N_BRANCH).transpose(0, 1, 3, 2)
    w_g = w_g.reshape(D_MODEL, KV_GROUPS, N_BRANCH * REP)
    w_g = jnp.pad(w_g, ((0, 0), (0, 0), (0, GATE_ROWS_PER_GROUP - N_BRANCH * REP)))
    w_g = jnp.pad(w_g.reshape(D_MODEL, KV_GROUPS * GATE_ROWS_PER_GROUP),
                  ((0, 0), (0, GATE_PAD - KV_GROUPS * GATE_ROWS_PER_GROUP)))
    return w_main, w_g


def _layout_compress(cmp_pos, cmp_w1, cmp_w2):
    pos2 = jnp.concatenate([cmp_pos[0], cmp_pos[1]], axis=1)
    w1 = cmp_w1.reshape(2, CMP_BLOCK, HEAD_DIM, CMP_HIDDEN)
    zero = jnp.zeros_like(w1[0])
    w1k = jnp.concatenate([w1[0], zero], axis=1)
    w1v = jnp.concatenate([zero, w1[1]], axis=1)
    w1p = jnp.concatenate([w1k, w1v], axis=2).astype(BF16)
    w1p = w1p.reshape(CMP_BLOCK // 2, 2 * LANES, 2 * CMP_HIDDEN)
    w2p = jnp.pad(cmp_w2, ((0, 0), (0, 0), (0, LANES - HEAD_DIM))).astype(BF16)
    return pos2, w1p, w2p


def kernel(x, attn_norm_w, w_in, q_norm_w, k_norm_w, cmp_pos, cmp_w1, cmp_w2, gmlp_ln_w, gmlp_ln_b,
           spatial_w, spatial_b, w_out, ffn_norm_w, w_up, conv_w, conv_b, w_down):
    x2 = x.reshape(N_TOK, D_MODEL)
    w_main, w_gate = _layout_w_in(w_in)
    proj, gl = _proj(x2, attn_norm_w.reshape(1, D_MODEL), w_main, w_gate)

    knw2 = jnp.concatenate([k_norm_w[1], k_norm_w[2]]).reshape(1, LANES)
    kcw = jnp.concatenate([k_norm_w[0], jnp.zeros((LANES - HEAD_DIM,), F32)]).reshape(1, LANES)
    pos2, w1p, w2p = _layout_compress(cmp_pos, cmp_w1, cmp_w2)
    ksel, kwin, vt, kc, vct, gt = _kprep(proj, gl, knw2, kcw, pos2, w1p, w2p)

    wq_b = jnp.broadcast_to(q_norm_w.reshape(HEAD_DIM, 1), (HEAD_DIM, Q_TILE))
    a, w_up_b, w_down_b = _attn(proj, ksel, kwin, vt, kc, vct, gt, jnp.asarray(_slope_features()),
                                jnp.asarray(_window_mask()), jnp.asarray(_diag_masks()),
                                jnp.asarray(_overlap_t(), dtype=BF16), wq_b, w_up, w_down)

    sbx = jnp.repeat(spatial_b.T, GMLP_WIDTH // GMLP_GROUPS, axis=1)
    x1, xn = _oproj(a, proj, x2, w_out.astype(BF16), ffn_norm_w.reshape(1, D_MODEL),
                    gmlp_ln_w.reshape(1, GMLP_WIDTH), gmlp_ln_b.reshape(1, GMLP_WIDTH), spatial_w, sbx)

    out = _ffn(xn, x1, w_up_b, conv_w, conv_b.reshape(1, 2 * D_FF), w_down_b)
    return out.reshape(BATCH, SEQ, D_MODEL)
```

```python
import numpy as np
import jax
import jax.numpy as jnp
from jax import lax
from jax.experimental import pallas as pl
from jax.experimental.pallas import tpu as pltpu

F32 = jnp.float32
BF16 = jnp.bfloat16

D_MODEL = 2048
BATCH = 4
SEQ = 4096
N_TOK = BATCH * SEQ
NSA_HEADS = 16
KV_GROUPS = 4
REP = NSA_HEADS // KV_GROUPS
HEAD_DIM = 64
NSA_WIDTH = NSA_HEADS * HEAD_DIM
GMLP_WIDTH = D_MODEL - NSA_WIDTH
GMLP_GROUPS = 8
GMLP_CHUNK = 128
CMP_BLOCK = 32
CMP_STRIDE = 16
CMP_HIDDEN = 256
N_CMP_PAD = SEQ // CMP_STRIDE
SEL_BLOCK = 64
N_SEL = SEQ // SEL_BLOCK
SEL_TOPK = 16
WINDOW = 512
K_BLOCK = 128
N_KB = SEQ // K_BLOCK
Q_TILE = 256
N_QT = SEQ // Q_TILE
N_BRANCH = 3
KV_COLS = KV_GROUPS * HEAD_DIM
D_FF = 5632
CONV_WIDTH = 3
NORM_EPS = 1e-6
LN_EPS = 1e-5
NEG_BIG = -1e30
SEL_BIG = 1e9

LANES = 128
VMEM_LIMIT_BYTES = 56 * 1024 * 1024

Z_COLS = 2 * GMLP_WIDTH
PROJ_COLS = Z_COLS + NSA_WIDTH + 6 * KV_COLS
GATE_PAD = LANES
GATE_ROWS_PER_GROUP = 16

PROJ_TM = 1024
PROJ_TN = 1536
OPROJ_TM = 512
FFN_TM = 512
FFN_TF = 512
SEL_TILE_BLOCKS = 4
SEL_TILE_KEYS = SEL_TILE_BLOCKS * K_BLOCK
MASK_LANE0 = 80
SUM_ROWS = 16
VT_ROWS = HEAD_DIM + SUM_ROWS
LOG2E = 1.4426950408889634
WIN_BLOCKS = (WINDOW + Q_TILE) // K_BLOCK
WIN_KEYS = WIN_BLOCKS * K_BLOCK
CAST_UP_ROWS = 16
CAST_DOWN_ROWS = 32


def _cparams(*sem):
    return pltpu.CompilerParams(dimension_semantics=sem, vmem_limit_bytes=VMEM_LIMIT_BYTES)


def _dot(a, b):
    return jnp.dot(a, b, preferred_element_type=F32)


def _proj_kernel(x_ref, nw_ref, w_ref, wg_ref, o_ref, og_ref, h_ref):
    @pl.when(pl.program_id(1) == 0)
    def _():
        x = x_ref[...]
        ms = jnp.mean(x * x, axis=-1, keepdims=True)
        h = (x * lax.rsqrt(ms + NORM_EPS) * nw_ref[...]).astype(BF16)
        h_ref[...] = h
        og_ref[...] = _dot(h, wg_ref[...])

    o_ref[...] = _dot(h_ref[...], w_ref[...])


def _proj(x2, nw, w_main, w_gate):
    return pl.pallas_call(
        _proj_kernel,
        grid=(N_TOK // PROJ_TM, PROJ_COLS // PROJ_TN),
        in_specs=[
            pl.BlockSpec((PROJ_TM, D_MODEL), lambda i, j: (i, 0)),
            pl.BlockSpec((1, D_MODEL), lambda i, j: (0, 0)),
            pl.BlockSpec((D_MODEL, PROJ_TN), lambda i, j: (0, j)),
            pl.BlockSpec((D_MODEL, GATE_PAD), lambda i, j: (0, 0)),
        ],
        out_specs=[
            pl.BlockSpec((PROJ_TM, PROJ_TN), lambda i, j: (i, j)),
            pl.BlockSpec((PROJ_TM, GATE_PAD), lambda i, j: (i, 0)),
        ],
        out_shape=[
            jax.ShapeDtypeStruct((N_TOK, PROJ_COLS), F32),
            jax.ShapeDtypeStruct((N_TOK, GATE_PAD), F32),
        ],
        scratch_shapes=[pltpu.VMEM((PROJ_TM, D_MODEL), BF16)],
        compiler_params=_cparams("arbitrary", "arbitrary"),
        name="proj",
    )(x2, nw, w_main, w_gate)


def _pos_features(lane, first, f1, f2):
    return jnp.where(lane < first + 3, f1, jnp.where(lane < first + 6, f2,
                                                     jnp.where(lane == first + 6, 1.0, 0.0)))


def _kprep_kernel(s0_ref, s1_ref, s2_ref, gl_ref, knw_ref, kcw_ref, pos_ref, w1_ref, w2_ref,
                  ksel_ref, kwin_ref, vt_ref, kc_ref, vct_ref, gt_ref, ha_ref, hb_ref):
    lane = lax.broadcasted_iota(jnp.int32, (K_BLOCK, LANES), 1)
    row = lax.broadcasted_iota(jnp.int32, (K_BLOCK, LANES), 0)
    lo = lane < HEAD_DIM
    inv_d = 1.0 / HEAD_DIM
    ones_rows = jnp.where(lax.broadcasted_iota(jnp.int32, (SUM_ROWS, K_BLOCK), 0) == 0, 1.0, 0.0)

    def chunk(c, carry):
        r0 = pl.multiple_of(c * K_BLOCK, K_BLOCK)
        x = s0_ref[pl.ds(r0, K_BLOCK), :]
        sq = x * x
        s_lo = jnp.sum(jnp.where(lo, sq, 0.0), axis=1, keepdims=True)
        s_hi = jnp.sum(jnp.where(lo, 0.0, sq), axis=1, keepdims=True)
        ms = jnp.where(lo, s_lo, s_hi) * inv_d
        y = x * lax.rsqrt(ms + NORM_EPS) * knw_ref[...]
        pos = r0 + row
        blk = lax.shift_right_logical(pos, 6)
        f1 = (blk * SEL_BLOCK).astype(F32)
        f2 = (pos & (SEL_BLOCK - 1)).astype(F32)
        onehot = jnp.where(lane - MASK_LANE0 == (blk & (SEL_TILE_BLOCKS * 2 - 1)), 1.0, 0.0)
        feat_sel = jnp.where(lane < MASK_LANE0, _pos_features(lane, HEAD_DIM, f1, f2), onehot)
        ksel_ref[c] = jnp.where(lo, y, feat_sel).astype(BF16)
        kwin_ref[c] = jnp.where(lo, _pos_features(lane, 0, f1, f2), y).astype(BF16)
        v_t = s2_ref[pl.ds(r0, K_BLOCK), :].T
        vt_ref[c] = jnp.concatenate(
            [v_t[0:HEAD_DIM], ones_rows, v_t[HEAD_DIM:], ones_rows], axis=0).astype(BF16)
        return carry

    lax.fori_loop(0, N_KB, chunk, 0, unroll=4)

    @pl.when(pl.program_id(1) == 0)
    def _():
        def gchunk(c, carry):
            r0 = pl.multiple_of(c * K_BLOCK, K_BLOCK)
            gt_ref[c] = jax.nn.sigmoid(gl_ref[pl.ds(r0, K_BLOCK), :]).T
            return carry

        lax.fori_loop(0, N_KB, gchunk, 0, unroll=4)

    ha_ref[...] = jnp.zeros_like(ha_ref)
    hb_ref[...] = jnp.zeros_like(hb_ref)
    for tp in range(CMP_STRIDE // 2):
        rows = [s1_ref[pl.ds(2 * tp + e, N_CMP_PAD, stride=CMP_STRIDE), :] for e in range(2)]
        xa = jnp.concatenate([(rows[e] + pos_ref[2 * tp + e:2 * tp + e + 1, :]).astype(BF16)
                              for e in range(2)], axis=1)
        xb = jnp.concatenate([(rows[e] + pos_ref[CMP_STRIDE + 2 * tp + e:CMP_STRIDE + 2 * tp + e + 1, :])
                              .astype(BF16) for e in range(2)], axis=1)
        ha_ref[...] += _dot(xa, w1_ref[tp])
        hb_ref[0:N_CMP_PAD, :] += _dot(xb, w1_ref[CMP_STRIDE // 2 + tp])
    hid = jax.nn.gelu(ha_ref[...] + hb_ref[1:N_CMP_PAD + 1, :]).astype(BF16)
    kc = _dot(hid[:, :CMP_HIDDEN], w2_ref[0])
    vc = _dot(hid[:, CMP_HIDDEN:], w2_ref[1])
    ms = jnp.sum(kc * kc, axis=1, keepdims=True) * inv_d
    kcn = kc * lax.rsqrt(ms + NORM_EPS) * kcw_ref[...]
    ci = lax.broadcasted_iota(jnp.int32, (N_CMP_PAD, LANES), 0)
    cl = lax.broadcasted_iota(jnp.int32, (N_CMP_PAD, LANES), 1)
    c1 = (lax.shift_right_logical(ci, 2) * SEL_BLOCK).astype(F32)
    c2 = ((ci & 3) * CMP_STRIDE).astype(F32) + (CMP_BLOCK - 1) * 0.5
    kc_ref[...] = jnp.where(cl < HEAD_DIM, kcn, _pos_features(cl, HEAD_DIM, c1, c2)).astype(BF16)
    vct_ref[...] = vc.T[0:HEAD_DIM, :].astype(BF16)


def _kprep(proj, gl, knw2, kcw, pos2, w1p, w2p):
    kv0 = (Z_COLS + NSA_WIDTH) // LANES
    slab = lambda s: pl.BlockSpec((SEQ, LANES), lambda b, g, s=s: (b, kv0 + 3 * g + s))
    full = lambda shape: pl.BlockSpec(shape, lambda b, g: (0,) * len(shape))
    per_bg = lambda shape: pl.BlockSpec((None, None) + shape, lambda b, g: (b, g) + (0,) * len(shape))
    bg_shape = lambda shape, dt: jax.ShapeDtypeStruct((BATCH, KV_GROUPS) + shape, dt)
    return pl.pallas_call(
        _kprep_kernel,
        grid=(BATCH, KV_GROUPS),
        in_specs=[
            slab(0), slab(1), slab(2),
            pl.BlockSpec((SEQ, GATE_PAD), lambda b, g: (b, 0)),
            full((1, LANES)), full((1, LANES)), full((CMP_BLOCK, LANES)),
            full((CMP_BLOCK // 2, 2 * LANES, 2 * CMP_HIDDEN)), full((2, CMP_HIDDEN, LANES)),
        ],
        out_specs=[
            per_bg((N_KB, K_BLOCK, LANES)),
            per_bg((N_KB, K_BLOCK, LANES)),
            per_bg((N_KB, 2 * VT_ROWS, K_BLOCK)),
            per_bg((N_CMP_PAD, LANES)),
            per_bg((HEAD_DIM, N_CMP_PAD)),
            pl.BlockSpec((None, N_KB, GATE_PAD, K_BLOCK), lambda b, g: (b, 0, 0, 0)),
        ],
        out_shape=[
            bg_shape((N_KB, K_BLOCK, LANES), BF16),
            bg_shape((N_KB, K_BLOCK, LANES), BF16),
            bg_shape((N_KB, 2 * VT_ROWS, K_BLOCK), BF16),
            bg_shape((N_CMP_PAD, LANES), BF16),
            bg_shape((HEAD_DIM, N_CMP_PAD), BF16),
            jax.ShapeDtypeStruct((BATCH, N_KB, GATE_PAD, K_BLOCK), F32),
        ],
        scratch_shapes=[
            pltpu.VMEM((N_CMP_PAD, 2 * CMP_HIDDEN), F32),
            pltpu.VMEM((N_CMP_PAD + 8, 2 * CMP_HIDDEN), F32),
        ],
        compiler_params=_cparams("arbitrary", "arbitrary"),
        name="kprep",
    )(proj, proj, proj, gl, knw2, kcw, pos2, w1p, w2p)


def _attn_kernel(q_ref, ksel_ref, kwin_ref, vt_ref, kc_ref, vct_ref, gt_ref, sf_ref, wm_ref, dm_ref,
                 ov_ref, wq_ref, wup_ref, wdn_ref, o_ref, wup_o_ref, wdn_o_ref,
                 sel_ref, qs_ref, sa_ref, sb_ref, tiles_ref):
    wup_o_ref[...] = wup_ref[...].astype(BF16)
    wdn_o_ref[...] = wdn_ref[...].astype(BF16)

    qt = pl.program_id(2)
    t_i = qt * Q_TILE + lax.broadcasted_iota(jnp.int32, (1, Q_TILE), 1)
    head = lambda a, r: a[:, r * Q_TILE:(r + 1) * Q_TILE]
    heads = lambda parts: jnp.concatenate(parts, axis=1)

    q_t = q_ref[...].T
    parts = []
    for r in range(REP):
        xq = q_t[r * HEAD_DIM:(r + 1) * HEAD_DIM, :]
        ms = jnp.sum(xq * xq, axis=0, keepdims=True) * (1.0 / HEAD_DIM)
        parts.append(xq * lax.rsqrt(ms + NORM_EPS) * wq_ref[...] * (HEAD_DIM ** -0.5) * LOG2E)
    qn = heads(parts)
    sf = sf_ref[...]
    sf = jnp.where(lax.broadcasted_iota(jnp.int32, sf.shape, 0) == 6, sf * (qt * Q_TILE).astype(F32), sf)
    wide = REP * Q_TILE
    pad = jnp.zeros((HEAD_DIM - 8, wide), F32)
    qs_ref[...] = jnp.concatenate([qn, sf, pad], axis=0).astype(BF16)
    q_win = jnp.concatenate([sf, pad, qn], axis=0).astype(BF16)

    sc = _dot(kc_ref[...], qs_ref[...])
    ci = lax.broadcasted_iota(jnp.int32, (N_CMP_PAD, Q_TILE), 0)
    cm = jnp.where(ci * CMP_STRIDE + (CMP_BLOCK - 1) <= t_i, 0.0, NEG_BIG)
    any_valid = t_i >= CMP_BLOCK - 1
    pcs = []
    for r in range(REP):
        s = head(sc, r) + cm
        e = jnp.exp2(s - jnp.max(s, axis=0, keepdims=True))
        d = jnp.sum(e, axis=0, keepdims=True)
        pcs.append((e * jnp.where(any_valid, 1.0 / d, 0.0)).astype(BF16))
    oc_imp = _dot(jnp.concatenate([vct_ref[...], ov_ref[...]], axis=0), heads(pcs))
    o_c = oc_imp[0:HEAD_DIM]
    imp4 = oc_imp[HEAD_DIM:]
    imp = (head(imp4, 0) + head(imp4, 1)) + (head(imp4, 2) + head(imp4, 3))

    jj = lax.broadcasted_iota(jnp.int32, (N_SEL, Q_TILE), 0)
    cur = lax.shift_right_logical(t_i, 6)
    valid = jj <= cur
    sel_ref[...] = jnp.where(valid, 1.0, 0.0)
    forced = (jj == 0) | (jj == cur) | (jj == cur - 1)
    score = jnp.where(valid, jnp.where(forced, SEL_BIG, imp), -SEL_BIG)

    def rank_rows(n8):
        def branch():
            rows8 = [score[8 * v:8 * v + 8, :] for v in range(n8)]
            j8 = lax.broadcasted_iota(jnp.int32, (8, Q_TILE), 0)
            ranks = [jnp.zeros((8, Q_TILE), F32) for _ in rows8]
            for jp in range(8 * n8):
                row = score[jp:jp + 1, :]
                for v, blk in enumerate(rows8):
                    if 8 * v > jp:
                        ahead = jnp.where(row >= blk, 1.0, 0.0)
                    elif 8 * v + 7 <= jp:
                        ahead = jnp.where(row > blk, 1.0, 0.0)
                    else:
                        ahead = jnp.where(j8 + 8 * v > jp, jnp.where(row >= blk, 1.0, 0.0),
                                          jnp.where(row > blk, 1.0, 0.0))
                    ranks[v] = ranks[v] + ahead
            rank = jnp.concatenate(ranks, axis=0)
            sel_ref[0:8 * n8, :] = jnp.where(valid[0:8 * n8] & (rank < SEL_TOPK), 1.0, 0.0)
        return branch

    n8_needed = (qt * Q_TILE + Q_TILE - 1) // (8 * SEL_BLOCK) + 1
    lax.switch(n8_needed - 1, [(lambda: None) if n8 * 8 <= SEL_TOPK else rank_rows(n8)
                               for n8 in range(1, N_SEL // 8 + 1)])

    sub = SEL_TILE_KEYS // SEL_BLOCK

    def scores_into(kt, s_ref):
        member = sel_ref[pl.ds(kt * sub, sub), :]
        mrows = (member - 1.0) * (-NEG_BIG)
        qs_ref[MASK_LANE0:MASK_LANE0 + 16, :] = jnp.concatenate(
            [heads([mrows] * REP), jnp.zeros((16 - sub, wide), F32)], axis=0).astype(BF16)
        k_tile = ksel_ref[pl.ds(kt * SEL_TILE_BLOCKS, SEL_TILE_BLOCKS)].reshape(SEL_TILE_KEYS, LANES)
        s_ref[...] = _dot(k_tile, qs_ref[...])

    def absorb(kt, s_ref, carry, diagonal):
        m_run, acc = carry
        kb0 = kt * SEL_TILE_BLOCKS
        if diagonal:
            causal = dm_ref[qt % (SEL_TILE_KEYS // Q_TILE)]
        v_t = jnp.concatenate([vt_ref[kb0 + x, 0:VT_ROWS, :] for x in range(SEL_TILE_BLOCKS)], axis=1)
        ms_, accs = [], []
        for r in range(REP):
            sr = s_ref[:, r * Q_TILE:(r + 1) * Q_TILE]
            if diagonal:
                sr = sr + causal
            m_old = head(m_run, r)
            m_new = jnp.maximum(m_old, jnp.max(sr, axis=0, keepdims=True))
            p = jnp.exp2(sr - m_new).astype(BF16)
            accs.append(head(acc, r) * jnp.exp2(m_old - m_new) + _dot(v_t, p))
            ms_.append(m_new)
        return heads(ms_), heads(accs)

    carry = (jnp.full((1, wide), NEG_BIG, F32), jnp.zeros((VT_ROWS, wide), F32))
    n_full = (qt * Q_TILE) // SEL_TILE_KEYS

    n_sel_tiles = SEQ // SEL_TILE_KEYS
    tr = lax.broadcasted_iota(jnp.int32, (n_sel_tiles, N_SEL), 0)
    tc = lax.broadcasted_iota(jnp.int32, (n_sel_tiles, N_SEL), 1)
    tile_of_block = jnp.where(lax.shift_right_logical(tc, 3) == tr, 1.0, 0.0).astype(BF16)
    per_query = _dot(tile_of_block, sel_ref[...].astype(BF16))
    per_tile = _dot(per_query.astype(BF16), jnp.ones((Q_TILE, LANES), BF16))
    bit = lax.shift_left(1, lax.broadcasted_iota(jnp.int32, (n_sel_tiles, LANES), 0)).astype(F32)
    used_bits = jnp.sum(jnp.where(per_tile > 0.5, bit, 0.0), axis=0, keepdims=True)[0, 0].astype(jnp.int32)

    kb_first = qt * (Q_TILE // K_BLOCK) - WINDOW // K_BLOCK
    kbs = [jnp.maximum(kb_first + x, 0) for x in range(WIN_BLOCKS)]
    sw = _dot(jnp.concatenate([kwin_ref[kb] for kb in kbs], axis=0), q_win)

    n_visit = jnp.int32(0)
    for kt in range(n_sel_tiles):
        tiles_ref[n_visit] = kt
        take = (kt < n_full) & ((lax.shift_right_logical(used_bits, kt) & 1) == 1)
        n_visit = n_visit + jnp.where(take, 1, 0)
    tiles_ref[n_visit] = n_full

    scores_into(tiles_ref[0], sa_ref)
    wm = jnp.concatenate(
        [wm_ref[x * K_BLOCK:(x + 1) * K_BLOCK, :] + jnp.where(kb_first + x < 0, NEG_BIG, 0.0)
         for x in range(WIN_BLOCKS)], axis=0)
    pws = []
    for r in range(REP):
        s = head(sw, r) + wm
        pws.append(jnp.exp2(s - jnp.max(s, axis=0, keepdims=True)).astype(BF16))
    vw_t = jnp.concatenate([vt_ref[kb, VT_ROWS:2 * VT_ROWS, :] for kb in kbs], axis=1)
    acc_w = _dot(vw_t, heads(pws))
    o_w = acc_w[0:HEAD_DIM] / acc_w[HEAD_DIM:HEAD_DIM + 1]

    def tile_pair(i, carry):
        scores_into(tiles_ref[2 * i + 1], sb_ref)
        carry = absorb(tiles_ref[2 * i], sa_ref, carry, False)
        scores_into(tiles_ref[2 * i + 2], sa_ref)
        return absorb(tiles_ref[2 * i + 1], sb_ref, carry, False)

    carry = lax.fori_loop(0, n_visit // 2, tile_pair, carry)
    last = 2 * (n_visit // 2)

    def odd_tail(carry):
        scores_into(n_full, sb_ref)
        return absorb(n_full, sb_ref, absorb(tiles_ref[last], sa_ref, carry, False), True)

    def even_tail(carry):
        return absorb(n_full, sa_ref, carry, True)

    _, acc_s = lax.cond(n_visit % 2 == 1, odd_tail, even_tail, carry)
    o_s = acc_s[0:HEAD_DIM] / acc_s[HEAD_DIM:HEAD_DIM + 1]

    gates = jnp.concatenate([gt_ref[x] for x in range(Q_TILE // K_BLOCK)], axis=1)
    outs = []
    for r in range(REP):
        outs.append(gates[r:r + 1, :] * head(o_c, r)
                    + gates[REP + r:REP + r + 1, :] * head(o_s, r)
                    + gates[2 * REP + r:2 * REP + r + 1, :] * head(o_w, r))
    o_ref[...] = jnp.concatenate(outs, axis=0).T.astype(BF16)


def _attn(proj, ksel, kwin, vt, kc, vct, gt, sf, wm, dm, ov_t, wq_b, w_up, w_down):
    q0 = Z_COLS // (REP * HEAD_DIM)
    kvspec = lambda shape: pl.BlockSpec((None, None) + shape, lambda b, g, qt: (b, g) + (0,) * len(shape))
    const = lambda shape: pl.BlockSpec(shape, lambda b, g, qt: (0,) * len(shape))
    n_steps = BATCH * KV_GROUPS * N_QT
    assert 2 * (D_MODEL // CAST_UP_ROWS) == n_steps and D_FF // CAST_DOWN_ROWS <= n_steps
    step = lambda b, g, qt: (b * KV_GROUPS + g) * N_QT + qt
    wup_spec = pl.BlockSpec((CAST_UP_ROWS, D_FF), lambda b, g, qt: (step(b, g, qt) // 2, step(b, g, qt) % 2))
    wdn_spec = pl.BlockSpec((CAST_DOWN_ROWS, D_MODEL), lambda b, g, qt: (
        jnp.minimum(step(b, g, qt), D_FF // CAST_DOWN_ROWS - 1), 0))
    return pl.pallas_call(
        _attn_kernel,
        grid=(BATCH, KV_GROUPS, N_QT),
        in_specs=[
            pl.BlockSpec((Q_TILE, REP * HEAD_DIM), lambda b, g, qt: (b * N_QT + qt, q0 + g)),
            kvspec((N_KB, K_BLOCK, LANES)),
            kvspec((N_KB, K_BLOCK, LANES)),
            kvspec((N_KB, 2 * VT_ROWS, K_BLOCK)),
            kvspec((N_CMP_PAD, LANES)),
            kvspec((HEAD_DIM, N_CMP_PAD)),
            pl.BlockSpec((None, Q_TILE // K_BLOCK, GATE_ROWS_PER_GROUP, K_BLOCK), lambda b, g, qt: (b, qt, g, 0)),
            pl.BlockSpec((None, 8, REP * Q_TILE), lambda b, g, qt: (g, 0, 0)),
            const((WIN_KEYS, Q_TILE)),
            const((SEL_TILE_KEYS // Q_TILE, SEL_TILE_KEYS, Q_TILE)),
            const((N_SEL, N_CMP_PAD)),
            const((HEAD_DIM, Q_TILE)),
            wup_spec,
            wdn_spec,
        ],
        out_specs=[pl.BlockSpec((Q_TILE, REP * HEAD_DIM), lambda b, g, qt: (b * N_QT + qt, g)),
                   wup_spec, wdn_spec],
        out_shape=[jax.ShapeDtypeStruct((N_TOK, NSA_WIDTH), BF16),
                   jax.ShapeDtypeStruct((D_MODEL, 2 * D_FF), BF16),
                   jax.ShapeDtypeStruct((D_FF, D_MODEL), BF16)],
        scratch_shapes=[pltpu.VMEM((N_SEL, Q_TILE), F32), pltpu.VMEM((LANES, REP * Q_TILE), BF16),
                        pltpu.VMEM((SEL_TILE_KEYS, REP * Q_TILE), F32),
                        pltpu.VMEM((SEL_TILE_KEYS, REP * Q_TILE), F32),
                        pltpu.SMEM((SEQ // SEL_TILE_KEYS + 1,), jnp.int32)],
        compiler_params=_cparams("arbitrary", "arbitrary", "arbitrary"),
        name="attn",
    )(proj, ksel, kwin, vt, kc, vct, gt, sf, wm, dm, ov_t, wq_b, w_up, w_down)


def _spatial_gating(z, lnw, lnb, sw_ref, sbx):
    ge = jax.nn.gelu(z)
    u = ge[:, :GMLP_WIDTH]
    v = ge[:, GMLP_WIDTH:]
    mu = jnp.mean(v, axis=-1, keepdims=True)
    var = jnp.mean(jnp.square(v - mu), axis=-1, keepdims=True)
    vn = ((v - mu) * lax.rsqrt(var + LN_EPS) * lnw + lnb).astype(BF16)
    ti = lax.broadcasted_iota(jnp.int32, (GMLP_CHUNK, GMLP_CHUNK), 0)
    si = lax.broadcasted_iota(jnp.int32, (GMLP_CHUNK, GMLP_CHUNK), 1)
    ws = [jnp.where(ti >= si, sw_ref[gg], 0.0).astype(BF16) for gg in range(GMLP_GROUPS)]
    gd = GMLP_WIDTH // GMLP_GROUPS
    outs = []
    for c in range(z.shape[0] // GMLP_CHUNK):
        rows = slice(c * GMLP_CHUNK, (c + 1) * GMLP_CHUNK)
        mix = jnp.concatenate(
            [_dot(ws[gg], vn[rows, gg * gd:(gg + 1) * gd]) for gg in range(GMLP_GROUPS)], axis=1)
        outs.append((u[rows, :] * (mix + sbx)).astype(BF16))
    return jnp.concatenate(outs, axis=0)


def _oproj_kernel(a_ref, z_ref, x_ref, wa_ref, wb_ref, nw_ref, lnw_ref, lnb_ref, sw_ref, sbx_ref,
                  x1_ref, xn_ref):
    attn_part = _dot(a_ref[...], wa_ref[...])
    b = _spatial_gating(z_ref[...], lnw_ref[...], lnb_ref[...], sw_ref, sbx_ref[...])
    y = x_ref[...] + (attn_part + _dot(b, wb_ref[...]))
    x1_ref[...] = y
    ms = jnp.mean(y * y, axis=-1, keepdims=True)
    xn_ref[...] = (y * lax.rsqrt(ms + NORM_EPS) * nw_ref[...]).astype(BF16)


def _oproj(a, proj, x2, w_out, nw, lnw, lnb, sw, sbx):
    return pl.pallas_call(
        _oproj_kernel,
        grid=(N_TOK // OPROJ_TM,),
        in_specs=[
            pl.BlockSpec((OPROJ_TM, NSA_WIDTH), lambda i: (i, 0)),
            pl.BlockSpec((OPROJ_TM, Z_COLS), lambda i: (i, 0)),
            pl.BlockSpec((OPROJ_TM, D_MODEL), lambda i: (i, 0)),
            pl.BlockSpec((NSA_WIDTH, D_MODEL), lambda i: (0, 0)),
            pl.BlockSpec((GMLP_WIDTH, D_MODEL), lambda i: (1, 0)),
            pl.BlockSpec((1, D_MODEL), lambda i: (0, 0)),
            pl.BlockSpec((1, GMLP_WIDTH), lambda i: (0, 0)),
            pl.BlockSpec((1, GMLP_WIDTH), lambda i: (0, 0)),
            pl.BlockSpec((GMLP_GROUPS, GMLP_CHUNK, GMLP_CHUNK), lambda i: (0, 0, 0)),
            pl.BlockSpec((GMLP_CHUNK, GMLP_WIDTH), lambda i: (0, 0)),
        ],
        out_specs=[
            pl.BlockSpec((OPROJ_TM, D_MODEL), lambda i: (i, 0)),
            pl.BlockSpec((OPROJ_TM, D_MODEL), lambda i: (i, 0)),
        ],
        out_shape=[
            jax.ShapeDtypeStruct((N_TOK, D_MODEL), F32),
            jax.ShapeDtypeStruct((N_TOK, D_MODEL), BF16),
        ],
        compiler_params=_cparams("arbitrary"),
        name="oproj",
    )(a, proj, x2, w_out, w_out, nw, lnw, lnb, sw, sbx)


def _ffn_kernel(xn_ref, x1_ref, wg_ref, wu_ref, cwg_ref, cwu_ref, cbg_ref, cbu_ref, wd_ref,
                cwg_p_ref, cwu_p_ref, cbg_p_ref, cbu_p_ref, o_ref, h0_ref, h1_ref, carry_ref, wd_p_ref):
    i = pl.program_id(0)
    q = pl.program_id(1)
    nq = pl.num_programs(1)
    half = FFN_TF // 2

    @pl.when((i == 0) & (q == 0))
    def _():
        carry_ref[...] = jnp.zeros_like(carry_ref)
        h1_ref[...] = jnp.zeros_like(h1_ref)
        wd_p_ref[...] = jnp.zeros_like(wd_p_ref)

    @pl.when(q == 0)
    def _():
        o_ref[...] = x1_ref[...]

    seq_start = i % (SEQ // FFN_TM) == 0
    xn = xn_ref[...]

    def produce(c, h_ref):
        h_ref[0, 8:, :] = _dot(xn, wg_ref[:, c * half:(c + 1) * half])
        h_ref[1, 8:, :] = _dot(xn, wu_ref[:, c * half:(c + 1) * half])

    def conv(h_ref, part, tile, cw, cb):
        h_ref[part, 0:8, :] = jnp.where(seq_start, 0.0, carry_ref[tile, part])
        carry_ref[tile, part] = h_ref[part, FFN_TM:FFN_TM + 8, :]
        taps = [h_ref[part, 8 - k:8 - k + FFN_TM, :] for k in (2, 1, 0)]
        return cb + ((cw[0:1, :] * taps[0] + cw[1:2, :] * taps[1]) + cw[2:3, :] * taps[2])

    def gate(h_ref, tile, c, cwg, cwu, cbg, cbu, scale=None):
        cols = slice(c * half, (c + 1) * half)
        cg = conv(h_ref, 0, tile, cwg[:, cols], cbg[:, cols])
        cu = conv(h_ref, 1, tile, cwu[:, cols], cbu[:, cols])
        act = jax.nn.silu(cg) * cu
        if scale is not None:
            act = act * scale
        return act.astype(BF16)

    act_prev = gate(h1_ref, jnp.where(q == 0, 2 * nq - 1, 2 * q - 1), 1, cwg_p_ref, cwu_p_ref, cbg_p_ref,
                    cbu_p_ref, scale=jnp.where(q == 0, 0.0, 1.0))
    produce(0, h0_ref)
    down_prev = _dot(act_prev, wd_p_ref[...])
    act_cur = gate(h0_ref, 2 * q, 0, cwg_ref, cwu_ref, cbg_ref, cbu_ref)
    produce(1, h1_ref)
    o_ref[...] += down_prev + _dot(act_cur, wd_ref[0:half, :])
    wd_p_ref[...] = wd_ref[half:, :]

    @pl.when(q == nq - 1)
    def _():
        act_last = gate(h1_ref, 2 * q + 1, 1, cwg_ref, cwu_ref, cbg_ref, cbu_ref)
        o_ref[...] += _dot(act_last, wd_ref[half:, :])


def _ffn(xn, x1, w_up, conv_w, conv_b, w_down):
    nj = D_FF // FFN_TF
    prev = lambda q: jnp.maximum(q - 1, 0)
    return pl.pallas_call(
        _ffn_kernel,
        grid=(N_TOK // FFN_TM, nj),
        in_specs=[
            pl.BlockSpec((FFN_TM, D_MODEL), lambda i, q: (i, 0)),
            pl.BlockSpec((FFN_TM, D_MODEL), lambda i, q: (i, 0)),
            pl.BlockSpec((D_MODEL, FFN_TF), lambda i, q: (0, q)),
            pl.BlockSpec((D_MODEL, FFN_TF), lambda i, q: (0, nj + q)),
            pl.BlockSpec((CONV_WIDTH, FFN_TF), lambda i, q: (0, q)),
            pl.BlockSpec((CONV_WIDTH, FFN_TF), lambda i, q: (0, nj + q)),
            pl.BlockSpec((1, FFN_TF), lambda i, q: (0, q)),
            pl.BlockSpec((1, FFN_TF), lambda i, q: (0, nj + q)),
            pl.BlockSpec((FFN_TF, D_MODEL), lambda i, q: (q, 0)),
            pl.BlockSpec((CONV_WIDTH, FFN_TF), lambda i, q: (0, prev(q))),
            pl.BlockSpec((CONV_WIDTH, FFN_TF), lambda i, q: (0, nj + prev(q))),
            pl.BlockSpec((1, FFN_TF), lambda i, q: (0, prev(q))),
            pl.BlockSpec((1, FFN_TF), lambda i, q: (0, nj + prev(q))),
        ],
        out_specs=pl.BlockSpec((FFN_TM, D_MODEL), lambda i, q: (i, 0)),
        out_shape=jax.ShapeDtypeStruct((N_TOK, D_MODEL), F32),
        scratch_shapes=[pltpu.VMEM((2, FFN_TM + 8, FFN_TF // 2), F32), pltpu.VMEM((2, FFN_TM + 8, FFN_TF // 2), F32),
                        pltpu.VMEM((2 * nj, 2, 8, FFN_TF // 2), F32),
                        pltpu.VMEM((FFN_TF // 2, D_MODEL), BF16)],
        compiler_params=_cparams("arbitrary", "arbitrary"),
        name="ffn",
    )(xn, x1, w_up, w_up, conv_w, conv_w, conv_b, conv_b, w_down, conv_w, conv_w, conv_b, conv_b)


def _overlap_t():
    start = np.arange(N_CMP_PAD)[None, :] * CMP_STRIDE
    s0 = np.arange(N_SEL)[:, None] * SEL_BLOCK
    ov = (start < s0 + SEL_BLOCK) & (start + CMP_BLOCK > s0) & (np.arange(N_CMP_PAD)[None, :] < N_CMP_PAD - 1)
    return ov.astype(np.float32)


def _bf16_round(a):
    return np.asarray(a, np.float32).astype(BF16).astype(np.float32)


def _slope_features():
    sl = (np.power(2.0, -8.0 * np.arange(1, NSA_HEADS + 1) / NSA_HEADS).astype(np.float32)
          * np.float32(LOG2E)).astype(np.float32)
    s1 = _bf16_round(sl)
    s2 = _bf16_round(sl - s1)
    s3 = _bf16_round(sl - s1 - s2)
    rows = np.stack([s1, s2, s3, s1, s2, s3, -sl, np.zeros_like(sl)], axis=0)
    rows = rows.reshape(8, KV_GROUPS, REP).transpose(1, 0, 2)
    return np.repeat(rows, Q_TILE, axis=2).astype(np.float32)


def _window_mask():
    kl = np.arange(WIN_KEYS)[:, None]
    ql = np.arange(Q_TILE)[None, :]
    dist = ql + WINDOW - kl
    return np.where((dist >= 0) & (dist < WINDOW), 0.0, NEG_BIG).astype(np.float32)


def _diag_masks():
    kl = np.arange(SEL_TILE_KEYS)[None, :, None]
    ql = np.arange(Q_TILE)[None, None, :]
    off = np.arange(SEL_TILE_KEYS // Q_TILE)[:, None, None] * Q_TILE
    return np.where(kl > ql + off, NEG_BIG, 0.0).astype(np.float32)


W_IN_KV_END = NSA_WIDTH + 6 * KV_COLS
W_IN_Z0 = W_IN_KV_END + NSA_HEADS * N_BRANCH
W_IN_COLS = W_IN_Z0 + Z_COLS
W_IN_COLS_PAD = -(-W_IN_COLS // LANES) * LANES
WPREP_ROWS = 256


def _wprep_kernel(w_ref, o_ref, og_ref):
    rows = lambda r0, n: w_ref[r0:r0 + n, :]
    og_ref[...] = rows(W_IN_KV_END, LANES).T.astype(BF16)
    for c in range(Z_COLS // LANES):
        o_ref[:, c * LANES:(c + 1) * LANES] = rows(W_IN_Z0 + c * LANES, LANES).T.astype(BF16)
    for c in range(NSA_WIDTH // LANES):
        o_ref[:, Z_COLS + c * LANES:Z_COLS + (c + 1) * LANES] = rows(c * LANES, LANES).T.astype(BF16)
    piece = lambda s, g: rows(NSA_WIDTH + s * KV_COLS + g * HEAD_DIM, HEAD_DIM)
    for g in range(KV_GROUPS):
        for k, (s_lo, s_hi) in enumerate(((2, 4), (0, 1), (3, 5))):
            c0 = Z_COLS + NSA_WIDTH + (3 * g + k) * LANES
            o_ref[:, c0:c0 + LANES] = jnp.concatenate([piece(s_lo, g), piece(s_hi, g)], axis=0).T.astype(BF16)


def _wprep(w_in):
    return pl.pallas_call(
        _wprep_kernel,
        grid=(D_MODEL // WPREP_ROWS,),
        in_specs=[pl.BlockSpec((W_IN_COLS, WPREP_ROWS), lambda i: (0, i))],
        out_specs=[pl.BlockSpec((WPREP_ROWS, PROJ_COLS), lambda i: (i, 0)),
                   pl.BlockSpec((WPREP_ROWS, LANES), lambda i: (i, 0))],
        out_shape=[jax.ShapeDtypeStruct((D_MODEL, PROJ_COLS), BF16),
                   jax.ShapeDtypeStruct((D_MODEL, LANES), BF16)],
        compiler_params=_cparams("arbitrary"),
    )(w_in.T)


def _layout_w_in(w_in):
    assert w_in.shape == (D_MODEL, W_IN_COLS) and W_IN_Z0 % 8 == 0 and W_IN_Z0 - W_IN_KV_END <= LANES
    w_main, w_g = _wprep(w_in)
    w_g = w_g[:, :W_IN_Z0 - W_IN_KV_END]
    w_g = w_g.reshape(D_MODEL, KV_GROUPS, REP, N_BRANCH).transpose(0, 1, 3, 2)
    w_g = w_g.reshape(D_MODEL, KV_GROUPS, N_BRANCH * REP)
    w_g = jnp.pad(w_g, ((0, 0), (0, 0), (0, GATE_ROWS_PER_GROUP - N_BRANCH * REP)))
    w_g = jnp.pad(w_g.reshape(D_MODEL, KV_GROUPS * GATE_ROWS_PER_GROUP),
                  ((0, 0), (0, GATE_PAD - KV_GROUPS * GATE_ROWS_PER_GROUP)))
    return w_main, w_g


def _layout_compress(cmp_pos, cmp_w1, cmp_w2):
    pos2 = jnp.concatenate([cmp_pos[0], cmp_pos[1]], axis=1)
    w1 = cmp_w1.reshape(2, CMP_BLOCK, HEAD_DIM, CMP_HIDDEN)
    zero = jnp.zeros_like(w1[0])
    w1k = jnp.concatenate([w1[0], zero], axis=1)
    w1v = jnp.concatenate([zero, w1[1]], axis=1)
    w1p = jnp.concatenate([w1k, w1v], axis=2).astype(BF16)
    w1p = w1p.reshape(CMP_BLOCK // 2, 2 * LANES, 2 * CMP_HIDDEN)
    w2p = jnp.pad(cmp_w2, ((0, 0), (0, 0), (0, LANES - HEAD_DIM))).astype(BF16)
    return pos2, w1p, w2p


def kernel(x, attn_norm_w, w_in, q_norm_w, k_norm_w, cmp_pos, cmp_w1, cmp_w2, gmlp_ln_w, gmlp_ln_b,
           spatial_w, spatial_b, w_out, ffn_norm_w, w_up, conv_w, conv_b, w_down):
    x2 = x.reshape(N_TOK, D_MODEL)
    w_main, w_gate = _layout_w_in(w_in)
    proj, gl = _proj(x2, attn_norm_w.reshape(1, D_MODEL), w_main, w_gate)

    knw2 = jnp.concatenate([k_norm_w[1], k_norm_w[2]]).reshape(1, LANES)
    kcw = jnp.concatenate([k_norm_w[0], jnp.zeros((LANES - HEAD_DIM,), F32)]).reshape(1, LANES)
    pos2, w1p, w2p = _layout_compress(cmp_pos, cmp_w1, cmp_w2)
    ksel, kwin, vt, kc, vct, gt = _kprep(proj, gl, knw2, kcw, pos2, w1p, w2p)

    wq_b = jnp.broadcast_to(q_norm_w.reshape(HEAD_DIM, 1), (HEAD_DIM, Q_TILE))
    a, w_up_b, w_down_b = _attn(proj, ksel, kwin, vt, kc, vct, gt, jnp.asarray(_slope_features()),
                                jnp.asarray(_window_mask()), jnp.asarray(_diag_masks()),
                                jnp.asarray(_overlap_t(), dtype=BF16), wq_b, w_up, w_down)

    sbx = jnp.repeat(spatial_b.T, GMLP_WIDTH // GMLP_GROUPS, axis=1)
    x1, xn = _oproj(a, proj, x2, w_out.astype(BF16), ffn_norm_w.reshape(1, D_MODEL),
                    gmlp_ln_w.reshape(1, GMLP_WIDTH), gmlp_ln_b.reshape(1, GMLP_WIDTH), spatial_w, sbx)

    out = _ffn(xn, x1, w_up_b, conv_w, conv_b.reshape(1, 2 * D_FF), w_down_b)
    return out.reshape(BATCH, SEQ, D_MODEL)
```
